```python
import math
import jax, jax.numpy as jnp
from jax import lax
import numpy as np

D_MODEL = 1024
BATCH = 2
SEQ = 8192
DEPTH = 2

D_FF = 2816
NORM_EPS = 1e-6
NSA_HEADS = 8
NSA_KV_HEADS = 2
NSA_GROUP = NSA_HEADS // NSA_KV_HEADS
NSA_HEAD_DIM = 64
CMP_BLOCK = 32
CMP_STRIDE = 16
SEL_BLOCK = 64
SEL_TOPK = 16
WINDOW = 512
Q_BLOCK = 128
ROPE_THETA = 10000.0
FORCE_SCORE = 1e4
SSD_HEADS = 16
SSD_HEAD_DIM = 64
SSD_D_INNER = SSD_HEADS * SSD_HEAD_DIM
SSD_GROUPS = 2
SSD_STATE = 128
SSD_CONV = 4
SSD_CHUNK = 128
SSD_NORM_EPS = 1e-5
RWKV_HEAD_DIM = 64
RWKV_HEADS = D_MODEL // RWKV_HEAD_DIM
DECAY_LORA = 64
AAA_LORA = 64
GATE_LORA = 160
RWKV_GN_EPS = 64e-5

N_EVEN = (DEPTH + 1) // 2
N_ODD = DEPTH // 2

NSA_Q_W = NSA_HEADS * NSA_HEAD_DIM
NSA_KV_W = NSA_KV_HEADS * NSA_HEAD_DIM
SSD_XBC = SSD_D_INNER + 2 * SSD_GROUPS * SSD_STATE
IN_SPLITS = (NSA_Q_W, NSA_KV_W, NSA_KV_W, NSA_KV_W, NSA_KV_W, NSA_KV_W, NSA_KV_W,
             NSA_HEADS * 3, SSD_D_INNER, SSD_XBC, SSD_HEADS)
IN_WIDTH = sum(IN_SPLITS)
MIX_OUT_WIDTH = NSA_Q_W + SSD_D_INNER

kernel_name = 'hybrid_nsa_ssd_rwkv7_macaron'


def rmsnorm(x, g, eps=NORM_EPS):
    xf = x.astype(jnp.float32)
    y = xf * lax.rsqrt(jnp.mean(xf * xf, -1, keepdims=True) + eps)
    return (y * g.astype(jnp.float32)).astype(x.dtype)


def swiglu(x, w_gate, w_up, w_down):
    return (jax.nn.silu(x @ w_gate) * (x @ w_up)) @ w_down


def rope_tables(seq, dim):
    inv = ROPE_THETA ** (-jnp.arange(0, dim, 2, dtype=jnp.float32) / dim)
    ang = jnp.arange(seq, dtype=jnp.float32)[:, None] * inv[None, :]
    return jnp.cos(ang), jnp.sin(ang)


def apply_rope(x, cos, sin):
    shape = (1, x.shape[1]) + (1,) * (x.ndim - 3) + (cos.shape[-1],)
    c = cos.reshape(shape).astype(x.dtype)
    s = sin.reshape(shape).astype(x.dtype)
    x1, x2 = jnp.split(x, 2, axis=-1)
    return jnp.concatenate([x1 * c - x2 * s, x2 * c + x1 * s], -1)


def masked_softmax(s, mask):
    s = jnp.where(mask, s.astype(jnp.float32), -1e30)
    p = jnp.where(mask, jnp.exp(s - jnp.max(s, -1, keepdims=True)), 0.0)
    return p / jnp.maximum(jnp.sum(p, -1, keepdims=True), 1e-30)


def compress_blocks(k, pe, w1, w2):
    b, s, h, d = k.shape
    n_cmp = (s - CMP_BLOCK) // CMP_STRIDE + 1
    idx = jnp.arange(n_cmp)[:, None] * CMP_STRIDE + jnp.arange(CMP_BLOCK)[None, :]
    blk = k[:, idx] + pe[None, None, :, None, :]
    blk = jnp.moveaxis(blk, 3, 2).reshape(b, n_cmp, h, CMP_BLOCK * d)
    return jax.nn.silu(blk @ w1) @ w2


def nsa_attention(q, k_cmp, v_cmp, k_sel, v_sel, k_win, v_win, gates,
                  pe_k, w1_k, w2_k, pe_v, w1_v, w2_v):
    b, s, hkv, g, d = q.shape
    kc = compress_blocks(k_cmp, pe_k, w1_k, w2_k)
    vc = compress_blocks(v_cmp, pe_v, w1_v, w2_v)
    n_cmp = kc.shape[1]
    n_sel = s // SEL_BLOCK
    topk = min(SEL_TOPK, n_sel)
    cmp_start = jnp.arange(n_cmp) * CMP_STRIDE
    cmp_end = cmp_start + CMP_BLOCK - 1
    sel_start = jnp.arange(n_sel) * SEL_BLOCK
    overlap = ((cmp_start[:, None] < sel_start[None, :] + SEL_BLOCK)
               & (cmp_end[:, None] >= sel_start[None, :])).astype(jnp.float32)
    ks_blk = jnp.moveaxis(k_sel, 2, 1).reshape(b, hkv, n_sel, SEL_BLOCK, d)
    vs_blk = jnp.moveaxis(v_sel, 2, 1).reshape(b, hkv, n_sel, SEL_BLOCK, d)
    pad = ((0, 0), (WINDOW, 0), (0, 0), (0, 0))
    kw_pad = jnp.pad(k_win, pad)
    vw_pad = jnp.pad(v_win, pad)
    bi = jnp.arange(b)[:, None, None, None]
    hi = jnp.arange(hkv)[None, :, None, None]
    blk_ids = jnp.arange(n_sel)

    def block(qi):
        s0 = qi * Q_BLOCK
        qb = lax.dynamic_slice_in_dim(q, s0, Q_BLOCK, 1)
        gb = lax.dynamic_slice_in_dim(gates, s0, Q_BLOCK, 1)
        t = s0 + jnp.arange(Q_BLOCK)
        p_c = masked_softmax(jnp.einsum('bqhgd,bchd->bhgqc', qb, kc),
                             cmp_end[None, :] <= t[:, None])
        o_c = jnp.einsum('bhgqc,bchd->bqhgd', p_c.astype(vc.dtype), vc)
        imp = jnp.einsum('bhgqc,cj->bhqj', p_c, overlap)
        cur = t // SEL_BLOCK
        forced = ((blk_ids[None, :] == 0) | (blk_ids[None, :] == cur[:, None])
                  | (blk_ids[None, :] == cur[:, None] - 1))
        valid = sel_start[None, :] <= t[:, None]
        imp = jnp.where(valid, jnp.where(forced, FORCE_SCORE, imp), -jnp.inf)
        _, sel = lax.top_k(imp, topk)
        kg = ks_blk[bi, hi, sel].reshape(b, hkv, Q_BLOCK, topk * SEL_BLOCK, d)
        vg = vs_blk[bi, hi, sel].reshape(b, hkv, Q_BLOCK, topk * SEL_BLOCK, d)
        tok = (sel[..., None] * SEL_BLOCK + jnp.arange(SEL_BLOCK)).reshape(
            b, hkv, Q_BLOCK, topk * SEL_BLOCK)
        p_s = masked_softmax(jnp.einsum('bqhgd,bhqnd->bhgqn', qb, kg),
                             (tok <= t[:, None])[:, :, None])
        o_s = jnp.einsum('bhgqn,bhqnd->bqhgd', p_s.astype(vg.dtype), vg)
        kw = lax.dynamic_slice_in_dim(kw_pad, s0, Q_BLOCK + WINDOW, 1)
        vw = lax.dynamic_slice_in_dim(vw_pad, s0, Q_BLOCK + WINDOW, 1)
        kp = s0 - WINDOW + jnp.arange(Q_BLOCK + WINDOW)
        mask_w = ((kp[None, :] <= t[:, None]) & (kp[None, :] > t[:, None] - WINDOW)
                  & (kp[None, :] >= 0))
        p_w = masked_softmax(jnp.einsum('bqhgd,bkhd->bhgqk', qb, kw), mask_w)
        o_w = jnp.einsum('bhgqk,bkhd->bqhgd', p_w.astype(vw.dtype), vw)
        return gb[..., 0:1] * o_c + gb[..., 1:2] * o_s + gb[..., 2:3] * o_w

    out = lax.map(block, jnp.arange(s // Q_BLOCK))
    return jnp.moveaxis(out, 0, 1).reshape(b, s, hkv * g * d)


def causal_depthwise_conv(x, w, bias):
    y = lax.conv_general_dilated(x, w, window_strides=(1,), padding=[(SSD_CONV - 1, 0)],
                                 dimension_numbers=('NWC', 'WIO', 'NWC'),
                                 feature_group_count=x.shape[-1])
    return y + bias


def ssd_chunked(x, dt, a, bmat, cmat):
    b, s, h, p = x.shape
    g, n = bmat.shape[2], bmat.shape[3]
    r = h // g
    l = SSD_CHUNK
    c = s // l
    xd = (x * dt[..., None]).reshape(b, c, l, g, r, p)
    a_cs = jnp.cumsum(jnp.moveaxis((dt * a).reshape(b, c, l, g, r), 2, -1), axis=-1)
    bm = bmat.reshape(b, c, l, g, n)
    cm = cmat.reshape(b, c, l, g, n)
    causal = jnp.tril(jnp.ones((l, l), bool))
    seg = jnp.exp(jnp.where(causal, a_cs[..., :, None] - a_cs[..., None, :], -jnp.inf))
    cb = jnp.einsum('bclgn,bcsgn->bcgls', cm, bm)
    y_diag = jnp.einsum('bcgrls,bcsgrp->bclgrp', cb[:, :, :, None] * seg, xd)
    decay_states = jnp.exp(a_cs[..., -1:] - a_cs)
    states = jnp.einsum('bclgn,bcgrl,bclgrp->bcgrpn', bm, decay_states, xd)
    chunk_decay = jnp.exp(a_cs[..., -1])

    def step(carry, inp):
        st, dec = inp
        return carry * dec[..., None, None] + st, carry

    _, states_in = lax.scan(step, jnp.zeros_like(states[:, 0]),
                            (jnp.moveaxis(states, 1, 0), jnp.moveaxis(chunk_decay, 1, 0)))
    states_in = jnp.moveaxis(states_in, 0, 1)
    y_off = jnp.einsum('bclgn,bcgrpn,bcgrl->bclgrp', cm, states_in, jnp.exp(a_cs))
    return (y_diag + y_off).reshape(b, s, h, p)


def mamba2_branch(z, xbc, dt_raw, conv_w, conv_b, dt_bias, a_log, d_skip, norm_w):
    b, s, _ = z.shape
    f32 = jnp.float32
    xbc = jax.nn.silu(causal_depthwise_conv(xbc, conv_w, conv_b))
    xs, bm, cm = jnp.split(xbc, [SSD_D_INNER, SSD_D_INNER + SSD_GROUPS * SSD_STATE], -1)
    xs = xs.reshape(b, s, SSD_HEADS, SSD_HEAD_DIM).astype(f32)
    dt = jax.nn.softplus(dt_raw.astype(f32) + dt_bias.astype(f32))
    a = -jnp.exp(a_log.astype(f32))
    y = ssd_chunked(xs, dt, a, bm.reshape(b, s, SSD_GROUPS, SSD_STATE).astype(f32),
                    cm.reshape(b, s, SSD_GROUPS, SSD_STATE).astype(f32))
    y = y + xs * d_skip.astype(f32)[:, None]
    y = y.reshape(b, s, SSD_D_INNER) * jax.nn.silu(z.astype(f32))
    yg = y.reshape(b, s, SSD_GROUPS, SSD_D_INNER // SSD_GROUPS)
    yg = yg * lax.rsqrt(jnp.mean(yg * yg, -1, keepdims=True) + SSD_NORM_EPS)
    return (yg.reshape(b, s, SSD_D_INNER) * norm_w.astype(f32)).astype(z.dtype)


def nsa_ssd_mixer(h, cos, sin, w_in, pe_k, w1_k, w2_k, pe_v, w1_v, w2_v,
                  conv_w, conv_b, dt_bias, a_log, d_skip, ssd_norm_w, w_out):
    b, s, _ = h.shape
    proj = h @ w_in
    offs = np.cumsum(IN_SPLITS)[:-1].tolist()
    q, kc, vc, ks, vs, kw, vw, gl, z, xbc, dt = jnp.split(proj, offs, -1)
    kvshape = (b, s, NSA_KV_HEADS, NSA_HEAD_DIM)
    q = apply_rope(q.reshape(b, s, NSA_KV_HEADS, NSA_GROUP, NSA_HEAD_DIM), cos, sin) * (NSA_HEAD_DIM ** -0.5)
    kc = apply_rope(kc.reshape(kvshape), cos, sin)
    ks = apply_rope(ks.reshape(kvshape), cos, sin)
    kw = apply_rope(kw.reshape(kvshape), cos, sin)
    gates = jax.nn.sigmoid(gl).reshape(b, s, NSA_KV_HEADS, NSA_GROUP, 3)
    o_a = nsa_attention(q, kc, vc.reshape(kvshape), ks, vs.reshape(kvshape), kw, vw.reshape(kvshape),
                        gates, pe_k, w1_k, w2_k, pe_v, w1_v, w2_v)
    o_b = mamba2_branch(z, xbc, dt, conv_w, conv_b, dt_bias, a_log, d_skip, ssd_norm_w)
    return jnp.concatenate([o_a, o_b], -1) @ w_out


def rwkv7_time_mix(h, mu, w_r, w_k, w_v, w_o, w0, w1, w2, a0, a1, a2, g1, g2,
                   k_k, k_a, r_k, ln_g, ln_b):
    b, s, d = h.shape
    f32 = jnp.float32
    xx = jnp.pad(h, ((0, 0), (1, 0), (0, 0)))[:, :-1] - h
    xr, xw, xk, xv, xa, xg = [h + xx * mu[i] for i in range(6)]
    r = xr @ w_r
    w = -jax.nn.softplus(-(w0 + jnp.tanh(xw @ w1) @ w2)) - 0.5
    k = xk @ w_k
    v = xv @ w_v
    a = jax.nn.sigmoid(a0 + (xa @ a1) @ a2)
    g = jax.nn.sigmoid(xg @ g1) @ g2
    heads = lambda t: t.reshape(b, s, RWKV_HEADS, RWKV_HEAD_DIM).astype(f32)
    kk = heads(k * k_k)
    kk = kk / jnp.maximum(jnp.sqrt(jnp.sum(kk * kk, -1, keepdims=True)), 1e-12)
    k = k * (1 + (a - 1) * k_a)
    r_, k_, v_, a_ = heads(r), heads(k), heads(v), heads(a)
    decay = jnp.exp(-jnp.exp(heads(w)))

    def step(state, inp):
        r_t, d_t, k_t, v_t, kk_t, a_t = inp
        sa = jnp.einsum('bhij,bhj->bhi', state, -kk_t)
        state = (state * d_t[:, :, None, :] + sa[..., None] * (kk_t * a_t)[:, :, None, :]
                 + v_t[..., None] * k_t[:, :, None, :])
        return state, jnp.einsum('bhij,bhj->bhi', state, r_t)

    xs = tuple(jnp.moveaxis(t, 1, 0) for t in (r_, decay, k_, v_, kk, a_))
    _, y = lax.scan(step, jnp.zeros((b, RWKV_HEADS, RWKV_HEAD_DIM, RWKV_HEAD_DIM), f32), xs)
    y = jnp.moveaxis(y, 0, 1)
    mean = jnp.mean(y, -1, keepdims=True)
    var = jnp.mean(jnp.square(y - mean), -1, keepdims=True)
    y = ((y - mean) * lax.rsqrt(var + RWKV_GN_EPS)).reshape(b, s, d)
    y = y * ln_g.astype(f32) + ln_b.astype(f32)
    bonus = jnp.sum(r_ * k_ * r_k.astype(f32), -1, keepdims=True) * v_
    y = y + bonus.reshape(b, s, d)
    return (y * g.astype(f32)).astype(h.dtype) @ w_o


def setup_inputs(seed: int = 0) -> dict:
    key = jax.random.key(seed)
    ks = iter(jax.random.split(key, 48))
    nrm = lambda shape, scale: scale * jax.random.normal(next(ks), shape, jnp.float32)
    uni = lambda shape, lo, hi: jax.random.uniform(next(ks), shape, jnp.float32, lo, hi)
    D, F, E, O = D_MODEL, D_FF, N_EVEN, N_ODD
    L, KD = CMP_BLOCK, NSA_HEAD_DIM
    x = nrm((BATCH, SEQ, D), 1.0)
    norm_gains = 1.0 + nrm((DEPTH, 6, D), 0.05)
    ffn1_w_gate = nrm((DEPTH, D, F), D ** -0.5)
    ffn1_w_up = nrm((DEPTH, D, F), D ** -0.5)
    ffn1_w_down = nrm((DEPTH, F, D), F ** -0.5)
    ffn2_w_gate = nrm((DEPTH, D, F), D ** -0.5)
    ffn2_w_up = nrm((DEPTH, D, F), D ** -0.5)
    ffn2_w_down = nrm((DEPTH, F, D), F ** -0.5)
    ab_w_in = nrm((E, D, IN_WIDTH), D ** -0.5)
    a_cmp_pe_k = nrm((E, L, KD), 0.02)
    a_cmp_w1_k = nrm((E, L * KD, KD), (L * KD) ** -0.5)
    a_cmp_w2_k = nrm((E, KD, KD), KD ** -0.5)
    a_cmp_pe_v = nrm((E, L, KD), 0.02)
    a_cmp_w1_v = nrm((E, L * KD, KD), (L * KD) ** -0.5)
    a_cmp_w2_v = nrm((E, KD, KD), KD ** -0.5)
    b_conv_w = nrm((E, SSD_CONV, 1, SSD_XBC), SSD_CONV ** -0.5)
    b_conv_b = nrm((E, SSD_XBC), 0.02)
    dt0 = jnp.exp(uni((E, SSD_HEADS), math.log(1e-3), math.log(1e-1)))
    b_dt_bias = dt0 + jnp.log(-jnp.expm1(-dt0))
    b_a_log = jnp.log(uni((E, SSD_HEADS), 1.0, 16.0))
    b_d_skip = 1.0 + nrm((E, SSD_HEADS), 0.1)
    b_norm_w = 1.0 + nrm((E, SSD_D_INNER), 0.05)
    ab_w_out = nrm((E, MIX_OUT_WIDTH, D), MIX_OUT_WIDTH ** -0.5)
    c_mu = uni((O, 6, D), 0.0, 1.0)
    c_w_r = nrm((O, D, D), D ** -0.5)
    c_w_k = nrm((O, D, D), D ** -0.5)
    c_w_v = nrm((O, D, D), D ** -0.5)
    c_w_o = nrm((O, D, D), D ** -0.5)
    c_w0 = uni((O, D), -6.0, 1.0)
    c_w1 = nrm((O, D, DECAY_LORA), D ** -0.5)
    c_w2 = nrm((O, DECAY_LORA, D), 0.1 * DECAY_LORA ** -0.5)
    c_a0 = nrm((O, D), 0.1)
    c_a1 = nrm((O, D, AAA_LORA), D ** -0.5)
    c_a2 = nrm((O, AAA_LORA, D), 0.1 * AAA_LORA ** -0.5)
    c_g1 = nrm((O, D, GATE_LORA), D ** -0.5)
    c_g2 = nrm((O, GATE_LORA, D), GATE_LORA ** -0.5)
    c_k_k = 0.85 + nrm((O, D), 0.05)
    c_k_a = 1.0 + nrm((O, D), 0.05)
    c_r_k = nrm((O, RWKV_HEADS, RWKV_HEAD_DIM), 0.1)
    c_ln_g = 1.0 + nrm((O, D), 0.05)
    c_ln_b = nrm((O, D), 0.02)
    return {'x': x, 'norm_gains': norm_gains,
            'ffn1_w_gate': ffn1_w_gate, 'ffn1_w_up': ffn1_w_up, 'ffn1_w_down': ffn1_w_down,
            'ffn2_w_gate': ffn2_w_gate, 'ffn2_w_up': ffn2_w_up, 'ffn2_w_down': ffn2_w_down,
            'ab_w_in': ab_w_in, 'a_cmp_pe_k': a_cmp_pe_k, 'a_cmp_w1_k': a_cmp_w1_k,
            'a_cmp_w2_k': a_cmp_w2_k, 'a_cmp_pe_v': a_cmp_pe_v, 'a_cmp_w1_v': a_cmp_w1_v,
            'a_cmp_w2_v': a_cmp_w2_v, 'b_conv_w': b_conv_w, 'b_conv_b': b_conv_b,
            'b_dt_bias': b_dt_bias, 'b_a_log': b_a_log, 'b_d_skip': b_d_skip,
            'b_norm_w': b_norm_w, 'ab_w_out': ab_w_out,
            'c_mu': c_mu, 'c_w_r': c_w_r, 'c_w_k': c_w_k, 'c_w_v': c_w_v, 'c_w_o': c_w_o,
            'c_w0': c_w0, 'c_w1': c_w1, 'c_w2': c_w2, 'c_a0': c_a0, 'c_a1': c_a1, 'c_a2': c_a2,
            'c_g1': c_g1, 'c_g2': c_g2, 'c_k_k': c_k_k, 'c_k_a': c_k_a, 'c_r_k': c_r_k,
            'c_ln_g': c_ln_g, 'c_ln_b': c_ln_b}


def reference(x, norm_gains, ffn1_w_gate, ffn1_w_up, ffn1_w_down, ffn2_w_gate, ffn2_w_up,
              ffn2_w_down, ab_w_in, a_cmp_pe_k, a_cmp_w1_k, a_cmp_w2_k, a_cmp_pe_v, a_cmp_w1_v,
              a_cmp_w2_v, b_conv_w, b_conv_b, b_dt_bias, b_a_log, b_d_skip, b_norm_w, ab_w_out,
              c_mu, c_w_r, c_w_k, c_w_v, c_w_o, c_w0, c_w1, c_w2, c_a0, c_a1, c_a2, c_g1, c_g2,
              c_k_k, c_k_a, c_r_k, c_ln_g, c_ln_b):
    cos, sin = rope_tables(x.shape[1], NSA_HEAD_DIM)
    for layer in range(DEPTH):
        ng = norm_gains[layer]
        hdn = swiglu(rmsnorm(x, ng[0]), ffn1_w_gate[layer], ffn1_w_up[layer], ffn1_w_down[layer])
        x = x + 0.5 * rmsnorm(hdn, ng[1])
        hdn = rmsnorm(x, ng[2])
        i = layer // 2
        if layer % 2 == 0:
            hdn = nsa_ssd_mixer(hdn, cos, sin, ab_w_in[i], a_cmp_pe_k[i], a_cmp_w1_k[i],
                                a_cmp_w2_k[i], a_cmp_pe_v[i], a_cmp_w1_v[i], a_cmp_w2_v[i],
                                b_conv_w[i], b_conv_b[i], b_dt_bias[i], b_a_log[i], b_d_skip[i],
                                b_norm_w[i], ab_w_out[i])
        else:
            hdn = rwkv7_time_mix(hdn, c_mu[i], c_w_r[i], c_w_k[i], c_w_v[i], c_w_o[i], c_w0[i],
                                 c_w1[i], c_w2[i], c_a0[i], c_a1[i], c_a2[i], c_g1[i], c_g2[i],
                                 c_k_k[i], c_k_a[i], c_r_k[i], c_ln_g[i], c_ln_b[i])
        x = x + rmsnorm(hdn, ng[3])
        hdn = swiglu(rmsnorm(x, ng[4]), ffn2_w_gate[layer], ffn2_w_up[layer], ffn2_w_down[layer])
        x = x + 0.5 * rmsnorm(hdn, ng[5])
    return x
```

```python
import functools

import jax
import jax.numpy as jnp
import numpy as np
from jax import lax
from jax.experimental import pallas as pl
from jax.experimental.pallas import tpu as pltpu

F32 = jnp.float32
BF16 = jnp.bfloat16
HIGHEST = lax.Precision.HIGHEST

D_MODEL = 1024
D_FF = 2816
NORM_EPS = 1e-6
NSA_HEADS = 8
NSA_KV_HEADS = 2
NSA_GROUP = NSA_HEADS // NSA_KV_HEADS
NSA_HEAD_DIM = 64
CMP_BLOCK = 32
CMP_STRIDE = 16
SEL_BLOCK = 64
SEL_TOPK = 16
WINDOW = 512
ROPE_THETA = 10000.0
FORCE_SCORE = 1e4
SEL_LANES = 128
SSD_HEADS = 16
SSD_HEAD_DIM = 64
SSD_D_INNER = SSD_HEADS * SSD_HEAD_DIM
SSD_GROUPS = 2
SSD_STATE = 128
SSD_CONV = 4
SSD_CHUNK = 128
SSD_NORM_EPS = 1e-5
SSD_XBC = SSD_D_INNER + 2 * SSD_GROUPS * SSD_STATE
RWKV_HEAD_DIM = 64
RWKV_HEADS = D_MODEL // RWKV_HEAD_DIM
RWKV_GN_EPS = 64e-5

NSA_Q_W = NSA_HEADS * NSA_HEAD_DIM
NSA_KV_W = NSA_KV_HEADS * NSA_HEAD_DIM
IN_SPLITS = (NSA_Q_W, NSA_KV_W, NSA_KV_W, NSA_KV_W, NSA_KV_W, NSA_KV_W, NSA_KV_W,
             NSA_HEADS * 3, SSD_D_INNER, SSD_XBC, SSD_HEADS)
IN_WIDTH = sum(IN_SPLITS)

LANES = 128
SUBLANES = 8
VMEM_LIMIT_BYTES = 56 * 1024 * 1024

NEG_MASK = -1e30
NEG_UNSELECTED = -2.0 ** 30
NEG_TAKEN = -3e38


def _params(*sem):
    return pltpu.CompilerParams(dimension_semantics=sem, vmem_limit_bytes=VMEM_LIMIT_BYTES)


def _rms(x, g, eps):
    return x * lax.rsqrt(jnp.mean(x * x, -1, keepdims=True) + eps) * g


def _silu(x):
    return x * jax.nn.sigmoid(x)


def _softplus(x):
    return jnp.maximum(x, 0.0) + jnp.log1p(jnp.exp(-jnp.abs(x)))


def _split3(a):
    a1 = a.astype(BF16)
    r1 = a - a1.astype(F32)
    a2 = r1.astype(BF16)
    a3 = (r1 - a2.astype(F32)).astype(BF16)
    return a1, a2, a3


def _dot_x3(a, b):
    acc = None
    for piece in _split3(a):
        d = jnp.dot(piece, b, preferred_element_type=F32)
        acc = d if acc is None else acc + d
    return acc


def _dot_x3_left(b, a):
    acc = None
    for piece in _split3(a):
        d = jnp.dot(b, piece, preferred_element_type=F32)
        acc = d if acc is None else acc + d
    return acc


def _dot_nt(a, b, **kw):
    return lax.dot_general(a, b, (((1,), (1,)), ((), ())), preferred_element_type=F32, **kw)


def _ffn_kernel(x_ref, gi_ref, go_ref, wg_ref, wu_ref, wd_ref, o_ref, h_scr, acc_scr):
    f = pl.program_id(1)

    @pl.when(f == 0)
    def _():
        h_scr[...] = _rms(x_ref[...], gi_ref[...], NORM_EPS).astype(BF16)
        acc_scr[...] = jnp.zeros_like(acc_scr)

    h = h_scr[...]
    gate = jnp.dot(h, wg_ref[...], preferred_element_type=F32)
    up = jnp.dot(h, wu_ref[...], preferred_element_type=F32)
    act = (_silu(gate) * up).astype(BF16)
    acc_scr[...] += jnp.dot(act, wd_ref[...], preferred_element_type=F32)

    @pl.when(f == pl.num_programs(1) - 1)
    def _():
        o_ref[...] = x_ref[...] + 0.5 * _rms(acc_scr[...], go_ref[...], NORM_EPS)


def _ffn(x, g_in, g_out, w_gate, w_up, w_down, tm=512, tf=1408):
    t, d = x.shape
    f = w_gate.shape[1]
    return pl.pallas_call(
        _ffn_kernel,
        grid=(t // tm, f // tf),
        in_specs=[
            pl.BlockSpec((tm, d), lambda i, j: (i, 0)),
            pl.BlockSpec((1, d), lambda i, j: (0, 0)),
            pl.BlockSpec((1, d), lambda i, j: (0, 0)),
            pl.BlockSpec((d, tf), lambda i, j: (0, j)),
            pl.BlockSpec((d, tf), lambda i, j: (0, j)),
            pl.BlockSpec((tf, d), lambda i, j: (j, 0)),
        ],
        out_specs=pl.BlockSpec((tm, d), lambda i, j: (i, 0)),
        out_shape=jax.ShapeDtypeStruct((t, d), F32),
        scratch_shapes=[pltpu.VMEM((tm, d), BF16), pltpu.VMEM((tm, d), F32)],
        compiler_params=_params("parallel", "arbitrary"),
        name="ffn_half_step",
    )(x, g_in.reshape(1, d), g_out.reshape(1, d), w_gate.astype(BF16), w_up.astype(BF16),
      w_down.astype(BF16))


INPROJ_MISC_W = 256
INPROJ_KV_W = 6 * NSA_KV_W
INPROJ_WIDTH = NSA_Q_W + INPROJ_KV_W + SSD_D_INNER + SSD_XBC + INPROJ_MISC_W


def _swap_halves(x):
    w = x.shape[-1]
    lane = lax.broadcasted_iota(jnp.int32, x.shape, x.ndim - 1)
    low = (lane & (NSA_HEAD_DIM - 1)) < (NSA_HEAD_DIM // 2)
    return jnp.where(low, pltpu.roll(x, w - NSA_HEAD_DIM // 2, x.ndim - 1),
                     pltpu.roll(x, NSA_HEAD_DIM // 2, x.ndim - 1))


def _inproj_kernel(x_ref, g_ref, w_ref, cos_ref, sin_ref, q_ref, kv_ref, z_ref, xbc_ref, misc_ref):
    h = _rms(x_ref[...], g_ref[...], NORM_EPS).astype(BF16)
    proj = jnp.dot(h, w_ref[...], preferred_element_type=F32)
    cos = cos_ref[...]
    sin = sin_ref[...]
    o = 0
    q = proj[:, o:o + NSA_Q_W]
    cos_q = jnp.concatenate([cos] * (NSA_Q_W // LANES), axis=1)
    sin_q = jnp.concatenate([sin] * (NSA_Q_W // LANES), axis=1)
    q_ref[...] = (q * cos_q + _swap_halves(q) * sin_q) * (NSA_HEAD_DIM ** -0.5)
    o += NSA_Q_W
    for i in range(6):
        piece = proj[:, o:o + NSA_KV_W]
        if i % 2 == 0:
            piece = piece * cos + _swap_halves(piece) * sin
        kv_ref[:, i * NSA_KV_W:(i + 1) * NSA_KV_W] = piece
        o += NSA_KV_W
    z_ref[...] = proj[:, o:o + SSD_D_INNER]
    o += SSD_D_INNER
    xbc_ref[...] = proj[:, o:o + SSD_XBC]
    o += SSD_XBC
    misc_ref[...] = proj[:, o:o + INPROJ_MISC_W]


def _inproj(x, g, w_in, cos, sin, seq, tm=256):
    t, d = x.shape
    offs = np.cumsum(IN_SPLITS)[:-1].tolist()
    q, kc, vc, ks, vs, kw, vw, gl, z, xbc, dt = jnp.split(w_in, offs, -1)
    pad = jnp.zeros((d, INPROJ_MISC_W - gl.shape[1] - dt.shape[1]), w_in.dtype)
    w = jnp.concatenate([q, kc, vc, ks, vs, kw, vw, z, xbc, gl, dt, pad], -1).astype(BF16)
    assert w.shape[1] == INPROJ_WIDTH
    nseq = seq // tm
    row = lambda i: (i, 0)
    const = lambda i: (0, 0)
    widths = (NSA_Q_W, INPROJ_KV_W, SSD_D_INNER, SSD_XBC, INPROJ_MISC_W)
    return pl.pallas_call(
        _inproj_kernel,
        grid=(t // tm,),
        in_specs=[
            pl.BlockSpec((tm, d), row),
            pl.BlockSpec((1, d), const),
            pl.BlockSpec((d, INPROJ_WIDTH), const),
            pl.BlockSpec((tm, LANES), lambda i: (i % nseq, 0)),
            pl.BlockSpec((tm, LANES), lambda i: (i % nseq, 0)),
        ],
        out_specs=[pl.BlockSpec((tm, wd), row) for wd in widths],
        out_shape=[jax.ShapeDtypeStruct((t, wd), F32) for wd in widths],
        compiler_params=_params("parallel"),
        name="mixer0_in_proj",
    )(x, g.reshape(1, d), w, cos, sin)


def _rope_tables(seq):
    inv = ROPE_THETA ** (-jnp.arange(0, NSA_HEAD_DIM, 2, dtype=F32) / NSA_HEAD_DIM)
    ang = jnp.arange(seq, dtype=F32)[:, None] * inv[None, :]
    cos, sin = jnp.cos(ang), jnp.sin(ang)
    reps = LANES // NSA_HEAD_DIM
    cos_t = jnp.concatenate([cos, cos] * reps, -1)
    sin_t = jnp.concatenate([-sin, sin] * reps, -1)
    return cos_t, sin_t


def _compress_kernel(k_ref, pe_ref, w1_ref, w2_ref, o_ref):
    k16 = k_ref[0, 0]
    w1 = w1_ref[0]
    half = w1.shape[0] // 2
    first = jnp.dot(k16, w1[:half], precision=HIGHEST, preferred_element_type=F32)
    second = jnp.dot(k16, w1[half:], precision=HIGHEST, preferred_element_type=F32)
    bias = jnp.dot(pe_ref[0], w1, precision=HIGHEST, preferred_element_type=F32)[0:1]
    n = k16.shape[0]
    pre = first + pltpu.roll(second, n - 1, 0) + bias
    o_ref[0, 0] = jnp.dot(_silu(pre), w2_ref[0], precision=HIGHEST, preferred_element_type=F32)


def _compress(kv16, pe, w1, w2):
    two, bh, n, wd = kv16.shape
    d = w2.shape[-1]
    return pl.pallas_call(
        _compress_kernel,
        grid=(two, bh),
        in_specs=[
            pl.BlockSpec((1, 1, n, wd), lambda a, b: (a, b, 0, 0)),
            pl.BlockSpec((1, SUBLANES, pe.shape[-1]), lambda a, b: (a, 0, 0)),
            pl.BlockSpec((1,) + w1.shape[1:], lambda a, b: (a, 0, 0)),
            pl.BlockSpec((1, d, d), lambda a, b: (a, 0, 0)),
        ],
        out_specs=pl.BlockSpec((1, 1, n, d), lambda a, b: (a, b, 0, 0)),
        out_shape=jax.ShapeDtypeStruct((two, bh, n, d), F32),
        compiler_params=_params("parallel", "parallel"),
        name="nsa_compress",
    )(kv16, pe, w1, w2)


def _group_rows(q):
    return jnp.concatenate(
        [q[:, g * NSA_HEAD_DIM:(g + 1) * NSA_HEAD_DIM] for g in range(NSA_GROUP)], axis=0)


def _ungroup_rows(o, tq):
    return jnp.concatenate([o[g * tq:(g + 1) * tq] for g in range(NSA_GROUP)], axis=1)


def _nsa_select_kernel(q_ref, kc_ref, vc_ref, ov_ref, oc_ref, bias_ref, *, tq, topk):
    s0 = pl.program_id(2) * tq
    qg = _group_rows(q_ref[0])
    kc = kc_ref[0]
    s = _dot_nt(qg, kc, precision=HIGHEST)
    rows, ncmp = s.shape
    t = s0 + (lax.broadcasted_iota(jnp.int32, (rows, ncmp), 0) & (tq - 1))
    cmp_end = lax.broadcasted_iota(jnp.int32, (rows, ncmp), 1) * CMP_STRIDE + (CMP_BLOCK - 1)
    mask = cmp_end <= t
    s = jnp.where(mask, s, NEG_MASK)
    p = jnp.where(mask, jnp.exp(s - jnp.max(s, -1, keepdims=True)), 0.0)
    p = p / jnp.maximum(jnp.sum(p, -1, keepdims=True), 1e-30)
    o = jnp.dot(p.astype(BF16), vc_ref[0].astype(BF16), preferred_element_type=F32)
    oc_ref[0] = _ungroup_rows(o, tq)

    psum = p[0:tq]
    for g in range(1, NSA_GROUP):
        psum = psum + p[g * tq:(g + 1) * tq]
    imp = _dot_x3(psum, ov_ref[...])
    blk = lax.broadcasted_iota(jnp.int32, imp.shape, 1)
    tt = s0 + lax.broadcasted_iota(jnp.int32, imp.shape, 0)
    cur = lax.shift_right_logical(tt, SEL_BLOCK.bit_length() - 1)
    forced = (blk == 0) | (blk == cur) | (blk == cur - 1)
    valid = blk * SEL_BLOCK <= tt
    x = jnp.where(valid, jnp.where(forced, FORCE_SCORE, imp), NEG_MASK)
    blk_f = blk.astype(F32)
    sel = jnp.zeros(imp.shape, jnp.bool_)
    for _ in range(topk):
        m = jnp.max(x, -1, keepdims=True)
        idx = jnp.min(jnp.where(x == m, blk_f, float(SEL_LANES)), -1, keepdims=True)
        hit = blk_f == idx
        sel = sel | hit
        x = jnp.where(hit, NEG_TAKEN, x)
    bias_ref[0, 0] = jnp.where(sel, 0.0, NEG_UNSELECTED).astype(BF16)


def _nsa_select(q, kc, vc, overlap, batch, seq, tq=128):
    ncmp = kc.shape[1]
    gw = NSA_GROUP * NSA_HEAD_DIM
    topk = min(SEL_TOPK, seq // SEL_BLOCK)
    kern = functools.partial(_nsa_select_kernel, tq=tq, topk=topk)
    return pl.pallas_call(
        kern,
        grid=(batch, NSA_KV_HEADS, seq // tq),
        in_specs=[
            pl.BlockSpec((1, tq, gw), lambda b, h, i: (b, i, h)),
            pl.BlockSpec((1, ncmp, NSA_HEAD_DIM), lambda b, h, i: (b * NSA_KV_HEADS + h, 0, 0)),
            pl.BlockSpec((1, ncmp, NSA_HEAD_DIM), lambda b, h, i: (b * NSA_KV_HEADS + h, 0, 0)),
            pl.BlockSpec((ncmp, SEL_LANES), lambda b, h, i: (0, 0)),
        ],
        out_specs=[
            pl.BlockSpec((1, tq, gw), lambda b, h, i: (b, i, h)),
            pl.BlockSpec((1, 1, tq, SEL_LANES), lambda b, h, i: (b, h, i, 0)),
        ],
        out_shape=[
            jax.ShapeDtypeStruct((batch, seq, NSA_KV_HEADS * gw), F32),
            jax.ShapeDtypeStruct((batch, NSA_KV_HEADS, seq, SEL_LANES), BF16),
        ],
        compiler_params=_params("parallel", "parallel", "parallel"),
        name="nsa_compressed_select",
    )(q, kc, vc, overlap)


def _softmax_step(carry, s, v, mask):
    m, l, acc = carry
    if mask is not None:
        s = jnp.where(mask, s, NEG_MASK)
    m_new = jnp.maximum(m, jnp.max(s, -1, keepdims=True))
    alpha = jnp.exp(m - m_new)
    p = jnp.exp(s - m_new)
    if mask is not None:
        p = jnp.where(mask, p, 0.0)
    l = alpha * l + jnp.sum(p, -1, keepdims=True)
    acc = alpha * acc + jnp.dot(p.astype(BF16), v, preferred_element_type=F32)
    return m_new, l, acc


def _nsa_attend_kernel(q_ref, bias_ref, ka_ref, vs_ref, kw_ref, vw_ref, oc_ref, gl_ref, o_ref,
                       *, tq, tk):
    i = pl.program_id(2)
    s0 = i * tq
    rows = NSA_GROUP * tq
    qg = _group_rows(q_ref[0]).astype(BF16)
    bias = bias_ref[0, 0]
    qa = jnp.concatenate([jnp.concatenate([bias] * NSA_GROUP, axis=0), qg], axis=1)
    init = (jnp.full((rows, 1), NEG_MASK, F32), jnp.zeros((rows, 1), F32),
            jnp.zeros((rows, NSA_HEAD_DIM), F32))

    def sel_tile(kt, carry, causal):
        k = ka_ref[0, pl.ds(pl.multiple_of(kt * tk, tk), tk), :]
        v = vs_ref[0, pl.ds(pl.multiple_of(kt * tk, tk), tk), :]
        s = _dot_nt(qa, k)
        mask = None
        if causal:
            t = s0 + (lax.broadcasted_iota(jnp.int32, s.shape, 0) & (tq - 1))
            kp = kt * tk + lax.broadcasted_iota(jnp.int32, s.shape, 1)
            mask = kp <= t
        return _softmax_step(carry, s, v, mask)

    n_full = s0 // tk
    carry = lax.fori_loop(0, n_full, lambda kt, c: sel_tile(kt, c, False), init)
    _, l_s, acc_s = sel_tile(n_full, carry, True)
    o_s = acc_s / jnp.maximum(l_s, 1e-30)

    def win_tile(kt, carry):
        k = kw_ref[0, pl.ds(pl.multiple_of(kt * tq, tq), tq), :]
        v = vw_ref[0, pl.ds(pl.multiple_of(kt * tq, tq), tq), :]
        s = _dot_nt(qg, k)
        t = s0 + (lax.broadcasted_iota(jnp.int32, s.shape, 0) & (tq - 1))
        kp = kt * tq + lax.broadcasted_iota(jnp.int32, s.shape, 1)
        return _softmax_step(carry, s, v, (kp <= t) & (kp > t - WINDOW))

    first = jnp.maximum(i - WINDOW // tq, 0)
    _, l_w, acc_w = lax.fori_loop(first, i + 1, win_tile, init)
    o_w = acc_w / jnp.maximum(l_w, 1e-30)

    gates = jax.nn.sigmoid(gl_ref[0, 0])
    o_c = oc_ref[0]
    pieces = []
    for g in range(NSA_GROUP):
        r = slice(g * tq, (g + 1) * tq)
        pieces.append(gates[:, 3 * g:3 * g + 1] * o_c[:, g * NSA_HEAD_DIM:(g + 1) * NSA_HEAD_DIM]
                      + gates[:, 3 * g + 1:3 * g + 2] * o_s[r]
                      + gates[:, 3 * g + 2:3 * g + 3] * o_w[r])
    o_ref[0] = jnp.concatenate(pieces, axis=1)


def _nsa_attend(q, bias, kaug, vs, kw, vw, o_c, gl, batch, seq, tq=128, tk=256):
    gw = NSA_GROUP * NSA_HEAD_DIM
    tk = min(tk, seq)
    assert tk % tq == 0 and WINDOW % tq == 0
    kern = functools.partial(_nsa_attend_kernel, tq=tq, tk=tk)
    bh = lambda b, h, i: (b * NSA_KV_HEADS + h, 0, 0)
    return pl.pallas_call(
        kern,
        grid=(batch, NSA_KV_HEADS, seq // tq),
        in_specs=[
            pl.BlockSpec((1, tq, gw), lambda b, h, i: (b, i, h)),
            pl.BlockSpec((1, 1, tq, SEL_LANES), lambda b, h, i: (b, h, i, 0)),
            pl.BlockSpec((1, seq, SEL_LANES + NSA_HEAD_DIM), bh),
            pl.BlockSpec((1, seq, NSA_HEAD_DIM), bh),
            pl.BlockSpec((1, seq, NSA_HEAD_DIM), bh),
            pl.BlockSpec((1, seq, NSA_HEAD_DIM), bh),
            pl.BlockSpec((1, tq, gw), lambda b, h, i: (b, i, h)),
            pl.BlockSpec((1, 1, tq, NSA_GROUP * 3), lambda b, h, i: (b, h, i, 0)),
        ],
        out_specs=pl.BlockSpec((1, tq, gw), lambda b, h, i: (b, i, h)),
        out_shape=jax.ShapeDtypeStruct((batch, seq, NSA_KV_HEADS * gw), F32),
        compiler_params=_params("parallel", "parallel", "arbitrary"),
        name="nsa_selected_window",
    )(q, bias, kaug, vs, kw, vw, o_c, gl)


def _nsa(q, kv, misc, pe_k, w1_k, w2_k, pe_v, w1_v, w2_v, batch, seq):
    d = NSA_HEAD_DIM
    heads = lambda a: jnp.moveaxis(a.reshape(batch, seq, NSA_KV_HEADS, d), 2, 1)
    piece = lambda i: heads(kv[:, i * NSA_KV_W:(i + 1) * NSA_KV_W])
    k_cmp, v_cmp, k_sel, v_sel, k_win, v_win = [piece(i) for i in range(6)]
    bh = batch * NSA_KV_HEADS
    n16 = seq // CMP_STRIDE
    kv16 = jnp.stack([k_cmp, v_cmp]).reshape(2, bh, n16, CMP_STRIDE * d)
    pe = jnp.stack([pe_k, pe_v]).reshape(2, 1, CMP_BLOCK * d)
    pe = jnp.broadcast_to(pe, (2, SUBLANES, CMP_BLOCK * d))
    cmp = _compress(kv16, pe, jnp.stack([w1_k, w1_v]), jnp.stack([w2_k, w2_v]))
    kc, vc = cmp[0], cmp[1]

    n_sel = seq // SEL_BLOCK
    cmp_start = np.arange(n16) * CMP_STRIDE
    sel_start = np.arange(SEL_LANES) * SEL_BLOCK
    overlap = ((cmp_start[:, None] < sel_start[None, :] + SEL_BLOCK)
               & (cmp_start[:, None] + CMP_BLOCK - 1 >= sel_start[None, :])
               & (np.arange(SEL_LANES)[None, :] < n_sel)
               & (np.arange(n16)[:, None] < (seq - CMP_BLOCK) // CMP_STRIDE + 1))
    overlap = jnp.asarray(overlap, BF16)

    q3 = q.reshape(batch, seq, NSA_Q_W)
    o_c, bias = _nsa_select(q3, kc, vc, overlap, batch, seq)

    onehot = (np.arange(seq)[:, None] // SEL_BLOCK == np.arange(SEL_LANES)[None, :])
    onehot = jnp.broadcast_to(jnp.asarray(onehot, BF16), (bh, seq, SEL_LANES))
    flat = lambda a: a.reshape(bh, seq, d).astype(BF16)
    kaug = jnp.concatenate([onehot, flat(k_sel)], -1)
    gl = misc[:, :NSA_HEADS * 3].reshape(batch, seq, NSA_KV_HEADS, NSA_GROUP * 3)
    gl = jnp.moveaxis(gl, 2, 1)
    o = _nsa_attend(q3, bias, kaug, flat(v_sel), flat(k_win), flat(v_win), o_c, gl, batch, seq)
    return o.reshape(batch * seq, NSA_Q_W)


def _ssd_kernel(xbc_ref, halo_ref, z_ref, dt_ref, cw_ref, cb_ref, dtb_ref, alog_ref, dskip_ref,
                nw_ref, tril_ref, o_ref, state_scr, y_scr):
    c = pl.program_id(1)
    l = SSD_CHUNK

    @pl.when(c == 0)
    def _():
        state_scr[...] = jnp.zeros_like(state_scr)

    x = xbc_ref[0]
    halo = jnp.where(c == 0, 0.0, halo_ref[0])
    xx = jnp.concatenate([halo, x], axis=0)
    cw = cw_ref[...]
    conv = cb_ref[...]
    for k in range(SSD_CONV):
        off = SUBLANES - (SSD_CONV - 1) + k
        conv = conv + cw[k:k + 1] * xx[off:off + l]
    xbc = _silu(conv)
    xs = xbc[:, :SSD_D_INNER]
    gn = SSD_GROUPS * SSD_STATE
    bmat = xbc[:, SSD_D_INNER:SSD_D_INNER + gn]
    cmat = xbc[:, SSD_D_INNER + gn:]

    dt = _softplus(dt_ref[0] + dtb_ref[...])
    da = dt * (-jnp.exp(alog_ref[...]))
    a_cs = _dot_x3_left(tril_ref[...], da)
    a_cs_t = a_cs.T
    a_last = a_cs[l - 1:l]
    causal = (lax.broadcasted_iota(jnp.int32, (l, l), 0)
              >= lax.broadcasted_iota(jnp.int32, (l, l), 1))
    dskip = dskip_ref[...]

    heads_per_group = SSD_HEADS // SSD_GROUPS
    for g in range(SSD_GROUPS):
        bg = bmat[:, g * SSD_STATE:(g + 1) * SSD_STATE]
        cg = cmat[:, g * SSD_STATE:(g + 1) * SSD_STATE]
        bg16 = bg.astype(BF16)
        cg16 = cg.astype(BF16)
        cb = _dot_nt(cg16, bg16)
        bgt16 = bg.T.astype(BF16)
        for r in range(heads_per_group):
            h = g * heads_per_group + r
            hs = slice(h * SSD_HEAD_DIM, (h + 1) * SSD_HEAD_DIM)
            col = a_cs[:, h:h + 1]
            rowv = a_cs_t[h:h + 1, :]
            seg = jnp.where(causal, jnp.exp(col - rowv), 0.0)
            xd = xs[:, hs] * dt[:, h:h + 1]
            y = jnp.dot((cb * seg).astype(BF16), xd.astype(BF16), preferred_element_type=F32)
            st = state_scr[h]
            y = y + jnp.exp(col) * jnp.dot(cg16, st.astype(BF16), preferred_element_type=F32)
            decay = jnp.exp(a_last[:, h:h + 1] - col)
            new = jnp.dot(bgt16, (decay * xd).astype(BF16), preferred_element_type=F32)
            state_scr[h] = st * jnp.exp(a_last[:, h:h + 1]) + new
            y_scr[:, hs] = y + xs[:, hs] * dskip[:, h:h + 1]

    y = y_scr[...] * _silu(z_ref[0])
    gw = SSD_D_INNER // SSD_GROUPS
    outs = []
    for g in range(SSD_GROUPS):
        yg = y[:, g * gw:(g + 1) * gw]
        outs.append(yg * lax.rsqrt(jnp.mean(yg * yg, -1, keepdims=True) + SSD_NORM_EPS))
    o_ref[0] = jnp.concatenate(outs, axis=1) * nw_ref[...]


def _pad_lanes(v, width=LANES):
    v = v.reshape(1, -1).astype(F32)
    return jnp.pad(v, ((0, 0), (0, width - v.shape[1])))


def _ssd(z, xbc, misc, conv_w, conv_b, dt_bias, a_log, d_skip, norm_w, batch, seq):
    l = SSD_CHUNK
    nc = seq // l
    z3 = z.reshape(batch, seq, SSD_D_INNER)
    x3 = xbc.reshape(batch, seq, SSD_XBC)
    dt = misc[:, NSA_HEADS * 3:NSA_HEADS * 3 + SSD_HEADS]
    dt3 = jnp.pad(dt, ((0, 0), (0, LANES - SSD_HEADS))).reshape(batch, seq, LANES)
    tril = jnp.asarray(np.tril(np.ones((l, l))), BF16)
    hb = l // SUBLANES
    const = lambda b, c: (0, 0)
    return pl.pallas_call(
        _ssd_kernel,
        grid=(batch, nc),
        in_specs=[
            pl.BlockSpec((1, l, SSD_XBC), lambda b, c: (b, c, 0)),
            pl.BlockSpec((1, SUBLANES, SSD_XBC), lambda b, c: (b, jnp.maximum(c * hb - 1, 0), 0)),
            pl.BlockSpec((1, l, SSD_D_INNER), lambda b, c: (b, c, 0)),
            pl.BlockSpec((1, l, LANES), lambda b, c: (b, c, 0)),
            pl.BlockSpec((SSD_CONV, SSD_XBC), const),
            pl.BlockSpec((1, SSD_XBC), const),
            pl.BlockSpec((1, LANES), const),
            pl.BlockSpec((1, LANES), const),
            pl.BlockSpec((1, LANES), const),
            pl.BlockSpec((1, SSD_D_INNER), const),
            pl.BlockSpec((l, l), const),
        ],
        out_specs=pl.BlockSpec((1, l, SSD_D_INNER), lambda b, c: (b, c, 0)),
        out_shape=jax.ShapeDtypeStruct((batch, seq, SSD_D_INNER), F32),
        scratch_shapes=[pltpu.VMEM((SSD_HEADS, SSD_STATE, SSD_HEAD_DIM), F32),
                        pltpu.VMEM((l, SSD_D_INNER), F32)],
        compiler_params=_params("parallel", "arbitrary"),
        name="ssd_chunk_scan",
    )(x3, x3, z3, dt3, conv_w.reshape(SSD_CONV, SSD_XBC), conv_b.reshape(1, SSD_XBC),
      _pad_lanes(dt_bias), _pad_lanes(a_log), _pad_lanes(d_skip), norm_w.reshape(1, SSD_D_INNER),
      tril).reshape(batch * seq, SSD_D_INNER)


def _outproj_kernel(*refs, n_parts):
    x_ref, g_ref = refs[0], refs[1]
    parts = refs[2:2 + n_parts]
    ws = refs[2 + n_parts:2 + 2 * n_parts]
    o_ref = refs[2 + 2 * n_parts]
    acc = None
    for p_ref, w_ref in zip(parts, ws):
        d = jnp.dot(p_ref[...].astype(BF16), w_ref[...], preferred_element_type=F32)
        acc = d if acc is None else acc + d
    o_ref[...] = x_ref[...] + _rms(acc, g_ref[...], NORM_EPS)


def _outproj(x, g, parts, weights, tm=512):
    t, d = x.shape
    row = lambda i: (i, 0)
    const = lambda i: (0, 0)
    n = len(parts)
    return pl.pallas_call(
        functools.partial(_outproj_kernel, n_parts=n),
        grid=(t // tm,),
        in_specs=([pl.BlockSpec((tm, d), row), pl.BlockSpec((1, d), const)]
                  + [pl.BlockSpec((tm, p.shape[1]), row) for p in parts]
                  + [pl.BlockSpec(w.shape, const) for w in weights]),
        out_specs=pl.BlockSpec((tm, d), row),
        out_shape=jax.ShapeDtypeStruct((t, d), F32),
        compiler_params=_params("parallel"),
        name="mixer_out_proj",
    )(x, g.reshape(1, d), *parts, *[w.astype(BF16) for w in weights])


def _head_sum(x, seg, seg_t):
    return _dot_x3(_dot_x3(x, seg), seg_t)


def _rwkv_pre_kernel(x_ref, halo_ref, g_ref, mu_ref, wr_ref, wk_ref, wv_ref, w0_ref, w1_ref,
                     w2_ref, a0_ref, a1_ref, a2_ref, g1_ref, g2_ref, kk_ref, ka_ref, seg_ref,
                     segt_ref, r_out, d_out, k_out, v_out, kk_out, a_out, g_out, *, tiles_per_seq):
    i = pl.program_id(0)
    h = _rms(x_ref[...], g_ref[...], NORM_EPS)
    prev_row = _rms(halo_ref[...], g_ref[...], NORM_EPS)[SUBLANES - 1:SUBLANES]
    prev_row = jnp.where(i % tiles_per_seq == 0, 0.0, prev_row)
    rowid = lax.broadcasted_iota(jnp.int32, h.shape, 0)
    prev = jnp.where(rowid == 0, prev_row, pltpu.roll(h, 1, 0))
    xx = prev - h
    mu = mu_ref[...]
    mix = lambda j: (h + xx * mu[j:j + 1]).astype(BF16)
    dot = lambda a, w_ref: jnp.dot(a, w_ref[...], preferred_element_type=F32)
    r = dot(mix(0), wr_ref)
    w = -_softplus(-(w0_ref[...] + dot(jnp.tanh(dot(mix(1), w1_ref)).astype(BF16), w2_ref))) - 0.5
    k = dot(mix(2), wk_ref)
    v = dot(mix(3), wv_ref)
    a = jax.nn.sigmoid(a0_ref[...] + dot(dot(mix(4), a1_ref).astype(BF16), a2_ref))
    g = dot(jax.nn.sigmoid(dot(mix(5), g1_ref)).astype(BF16), g2_ref)
    kk = k * kk_ref[...]
    norm = jnp.sqrt(_head_sum(kk * kk, seg_ref[...], segt_ref[...]))
    kk = kk / jnp.maximum(norm, 1e-12)
    r_out[...] = r
    d_out[...] = jnp.exp(-jnp.exp(w))
    k_out[...] = k * (1.0 + (a - 1.0) * ka_ref[...])
    v_out[...] = v
    kk_out[...] = kk
    a_out[...] = a
    g_out[...] = g


def _pad_cols(w, width):
    return jnp.pad(w, ((0, 0), (0, width - w.shape[1])))


def _pad_rows(w, width):
    return jnp.pad(w, ((0, width - w.shape[0]), (0, 0)))


def _seg_matrices():
    seg = np.zeros((D_MODEL, LANES), np.float32)
    seg[np.arange(D_MODEL), np.arange(D_MODEL) // RWKV_HEAD_DIM] = 1.0
    return jnp.asarray(seg, BF16), jnp.asarray(seg.T, BF16)


def _rwkv_pre(x, g, mu, w_r, w_k, w_v, w0, w1, w2, a0, a1, a2, g1, g2, k_k, k_a, seq, tm=256):
    t, d = x.shape
    lora = lambda w: -(-w.shape[1] // LANES) * LANES
    w1p, w2p = _pad_cols(w1, lora(w1)), _pad_rows(w2, lora(w1))
    a1p, a2p = _pad_cols(a1, lora(a1)), _pad_rows(a2, lora(a1))
    g1p, g2p = _pad_cols(g1, lora(g1)), _pad_rows(g2, lora(g1))
    seg, seg_t = _seg_matrices()
    row = lambda i: (i, 0)
    const = lambda i: (0, 0)
    hb = tm // SUBLANES
    vec = lambda v: v.reshape(1, d)
    mats = [w.astype(BF16) for w in (w_r, w_k, w_v)]
    ins = [x, x, vec(g), mu, *mats, vec(w0), w1p.astype(BF16), w2p.astype(BF16), vec(a0),
           a1p.astype(BF16), a2p.astype(BF16), g1p.astype(BF16), g2p.astype(BF16), vec(k_k),
           vec(k_a), seg, seg_t]
    in_specs = [pl.BlockSpec((tm, d), row),
                pl.BlockSpec((SUBLANES, d), lambda i: (jnp.maximum(i * hb - 1, 0), 0))]
    in_specs += [pl.BlockSpec(a.shape, const) for a in ins[2:]]
    return pl.pallas_call(
        functools.partial(_rwkv_pre_kernel, tiles_per_seq=seq // tm),
        grid=(t // tm,),
        in_specs=in_specs,
        out_specs=[pl.BlockSpec((tm, d), row)] * 7,
        out_shape=[jax.ShapeDtypeStruct((t, d), F32)] * 7,
        compiler_params=_params("parallel"),
        name="rwkv7_projections",
    )(*ins)


def _rwkv_scan_kernel(r_ref, d_ref, k_ref, v_ref, kk_ref, a_ref, ones_ref, eye_ref, y_ref,
                      state_scr, *, steps):
    @pl.when(pl.program_id(1) == 0)
    def _():
        state_scr[...] = jnp.zeros_like(state_scr)

    ones = ones_ref[...]
    eye = eye_ref[...]
    nt = D_MODEL // LANES
    hd = RWKV_HEAD_DIM

    def rows(ref, t):
        return jnp.concatenate(
            [jnp.broadcast_to(ref[0, t, pl.ds(c, 1), :], (hd, LANES)) for c in range(nt)], axis=0)

    def step(t, _):
        r, dec, k, v, kk, a = [rows(ref, t) for ref in (r_ref, d_ref, k_ref, v_ref, kk_ref, a_ref)]
        s = state_scr[...]
        sa = _dot_x3(s * kk, ones)
        vcol = _dot_x3(eye * v, ones)
        s = s * dec - sa * (kk * a) + vcol * k
        state_scr[...] = s
        yb = _dot_x3(s * r, ones) * eye
        for c in range(nt):
            y_ref[0, t, pl.ds(c, 1), :] = jnp.sum(yb[c * hd:(c + 1) * hd], axis=0, keepdims=True)
        return 0

    lax.fori_loop(0, steps, step, 0)


def _rwkv_scan(r, dec, k, v, kk, a, batch, seq, steps=256):
    nt = D_MODEL // LANES
    steps = min(steps, seq)
    shape4 = lambda x: x.reshape(batch, seq, nt, LANES)
    ones = np.kron(np.eye(2), np.ones((RWKV_HEAD_DIM, RWKV_HEAD_DIM)))
    eye = np.tile(np.eye(RWKV_HEAD_DIM), (nt, 2))
    blk = pl.BlockSpec((1, steps, nt, LANES), lambda b, c: (b, c, 0, 0))
    const = lambda b, c: (0, 0)
    y = pl.pallas_call(
        functools.partial(_rwkv_scan_kernel, steps=steps),
        grid=(batch, seq // steps),
        in_specs=[blk] * 6 + [pl.BlockSpec((LANES, LANES), const),
                              pl.BlockSpec((nt * RWKV_HEAD_DIM, LANES), const)],
        out_specs=blk,
        out_shape=jax.ShapeDtypeStruct((batch, seq, nt, LANES), F32),
        scratch_shapes=[pltpu.VMEM((nt * RWKV_HEAD_DIM, LANES), F32)],
        compiler_params=_params("parallel", "arbitrary"),
        name="rwkv7_recurrence",
    )(*[shape4(x) for x in (r, dec, k, v, kk, a)], jnp.asarray(ones, BF16), jnp.asarray(eye, F32))
    return y.reshape(batch * seq, D_MODEL)


def _rwkv_post_kernel(x_ref, y_ref, r_ref, k_ref, v_ref, g_ref, lng_ref, lnb_ref, rk_ref, wo_ref,
                      gn_ref, seg_ref, segt_ref, o_ref):
    seg, seg_t = seg_ref[...], segt_ref[...]
    y = y_ref[...]
    inv = 1.0 / RWKV_HEAD_DIM
    mean = _head_sum(y, seg, seg_t) * inv
    yc = y - mean
    var = _head_sum(yc * yc, seg, seg_t) * inv
    yn = yc * lax.rsqrt(var + RWKV_GN_EPS) * lng_ref[...] + lnb_ref[...]
    bonus = _head_sum(r_ref[...] * k_ref[...] * rk_ref[...], seg, seg_t) * v_ref[...]
    out = ((yn + bonus) * g_ref[...]).astype(BF16)
    proj = jnp.dot(out, wo_ref[...], preferred_element_type=F32)
    o_ref[...] = x_ref[...] + _rms(proj, gn_ref[...], NORM_EPS)


def _rwkv_post(x, y, r, k, v, g, ln_g, ln_b, r_k, w_o, gn, tm=256):
    t, d = x.shape
    seg, seg_t = _seg_matrices()
    row = lambda i: (i, 0)
    const = lambda i: (0, 0)
    vec = lambda a: a.reshape(1, d)
    small = [vec(ln_g), vec(ln_b), vec(r_k), w_o.astype(BF16), vec(gn), seg, seg_t]
    return pl.pallas_call(
        _rwkv_post_kernel,
        grid=(t // tm,),
        in_specs=[pl.BlockSpec((tm, d), row)] * 6 + [pl.BlockSpec(a.shape, const) for a in small],
        out_specs=pl.BlockSpec((tm, d), row),
        out_shape=jax.ShapeDtypeStruct((t, d), F32),
        compiler_params=_params("parallel"),
        name="rwkv7_output",
    )(x, y, r, k, v, g, *small)


def _nsa_ssd_mixer(x, g_pre, g_post, cos, sin, w_in, pe_k, w1_k, w2_k, pe_v, w1_v, w2_v, conv_w,
                   conv_b, dt_bias, a_log, d_skip, norm_w, w_out, batch, seq):
    q, kv, z, xbc, misc = _inproj(x, g_pre, w_in, cos, sin, seq)
    o_a = _nsa(q, kv, misc, pe_k, w1_k, w2_k, pe_v, w1_v, w2_v, batch, seq)
    o_b = _ssd(z, xbc, misc, conv_w, conv_b, dt_bias, a_log, d_skip, norm_w, batch, seq)
    return _outproj(x, g_post, [o_a, o_b], [w_out[:NSA_Q_W], w_out[NSA_Q_W:]])


def _rwkv7_mixer(x, g_pre, g_post, mu, w_r, w_k, w_v, w_o, w0, w1, w2, a0, a1, a2, g1, g2, k_k,
                 k_a, r_k, ln_g, ln_b, batch, seq):
    r, dec, k, v, kk, a, g = _rwkv_pre(x, g_pre, mu, w_r, w_k, w_v, w0, w1, w2, a0, a1, a2, g1,
                                       g2, k_k, k_a, seq)
    y = _rwkv_scan(r, dec, k, v, kk, a, batch, seq)
    return _rwkv_post(x, y, r, k, v, g, ln_g, ln_b, r_k, w_o, g_post)


def kernel(x, norm_gains, ffn1_w_gate, ffn1_w_up, ffn1_w_down, ffn2_w_gate, ffn2_w_up, ffn2_w_down, ab_w_in, a_cmp_pe_k, a_cmp_w1_k, a_cmp_w2_k, a_cmp_pe_v, a_cmp_w1_v, a_cmp_w2_v, b_conv_w, b_conv_b, b_dt_bias, b_a_log, b_d_skip, b_norm_w, ab_w_out, c_mu, c_w_r, c_w_k, c_w_v, c_w_o, c_w0, c_w1, c_w2, c_a0, c_a1, c_a2, c_g1, c_g2, c_k_k, c_k_a, c_r_k, c_ln_g, c_ln_b):
    batch, seq, d = x.shape
    depth = norm_gains.shape[0]
    cos, sin = _rope_tables(seq)
    x = x.reshape(batch * seq, d)
    for layer in range(depth):
        ng = norm_gains[layer]
        x = _ffn(x, ng[0], ng[1], ffn1_w_gate[layer], ffn1_w_up[layer], ffn1_w_down[layer])
        i = layer // 2
        if layer % 2 == 0:
            x = _nsa_ssd_mixer(x, ng[2], ng[3], cos, sin, ab_w_in[i], a_cmp_pe_k[i], a_cmp_w1_k[i],
                               a_cmp_w2_k[i], a_cmp_pe_v[i], a_cmp_w1_v[i], a_cmp_w2_v[i],
                               b_conv_w[i], b_conv_b[i], b_dt_bias[i], b_a_log[i], b_d_skip[i],
                               b_norm_w[i], ab_w_out[i], batch, seq)
        else:
            x = _rwkv7_mixer(x, ng[2], ng[3], c_mu[i], c_w_r[i], c_w_k[i], c_w_v[i], c_w_o[i],
                             c_w0[i], c_w1[i], c_w2[i], c_a0[i], c_a1[i], c_a2[i], c_g1[i],
                             c_g2[i], c_k_k[i], c_k_a[i], c_r_k[i], c_ln_g[i], c_ln_b[i],
                             batch, seq)
        x = _ffn(x, ng[4], ng[5], ffn2_w_gate[layer], ffn2_w_up[layer], ffn2_w_down[layer])
    return x.reshape(batch, seq, d)
```

```python
import functools

import jax
import jax.numpy as jnp
import numpy as np
from jax import lax
from jax.experimental import pallas as pl
from jax.experimental.pallas import tpu as pltpu

F32 = jnp.float32
BF16 = jnp.bfloat16
HIGHEST = lax.Precision.HIGHEST

D_MODEL = 1024
D_FF = 2816
NORM_EPS = 1e-6
NSA_HEADS = 8
NSA_KV_HEADS = 2
NSA_GROUP = NSA_HEADS // NSA_KV_HEADS
NSA_HEAD_DIM = 64
CMP_BLOCK = 32
CMP_STRIDE = 16
SEL_BLOCK = 64
SEL_TOPK = 16
WINDOW = 512
ROPE_THETA = 10000.0
FORCE_SCORE = 1e4
SEL_LANES = 128
SSD_HEADS = 16
SSD_HEAD_DIM = 64
SSD_D_INNER = SSD_HEADS * SSD_HEAD_DIM
SSD_GROUPS = 2
SSD_STATE = 128
SSD_CONV = 4
SSD_CHUNK = 128
SSD_NORM_EPS = 1e-5
SSD_XBC = SSD_D_INNER + 2 * SSD_GROUPS * SSD_STATE
RWKV_HEAD_DIM = 64
RWKV_HEADS = D_MODEL // RWKV_HEAD_DIM
RWKV_GN_EPS = 64e-5

NSA_Q_W = NSA_HEADS * NSA_HEAD_DIM
NSA_KV_W = NSA_KV_HEADS * NSA_HEAD_DIM
IN_SPLITS = (NSA_Q_W, NSA_KV_W, NSA_KV_W, NSA_KV_W, NSA_KV_W, NSA_KV_W, NSA_KV_W,
             NSA_HEADS * 3, SSD_D_INNER, SSD_XBC, SSD_HEADS)
IN_WIDTH = sum(IN_SPLITS)

LANES = 128
SUBLANES = 8
VMEM_LIMIT_BYTES = 56 * 1024 * 1024

NEG_MASK = -1e30
NEG_UNSELECTED = -2.0 ** 30
NEG_TAKEN = -3e38


def _params(*sem):
    return pltpu.CompilerParams(dimension_semantics=sem, vmem_limit_bytes=VMEM_LIMIT_BYTES)


def _rms(x, g, eps):
    return x * lax.rsqrt(jnp.mean(x * x, -1, keepdims=True) + eps) * g


def _silu(x):
    return x * jax.nn.sigmoid(x)


def _softplus(x):
    return jnp.maximum(x, 0.0) + jnp.log1p(jnp.exp(-jnp.abs(x)))


def _split3(a):
    a1 = a.astype(BF16)
    r1 = a - a1.astype(F32)
    a2 = r1.astype(BF16)
    a3 = (r1 - a2.astype(F32)).astype(BF16)
    return a1, a2, a3


def _dot_x3(a, b):
    acc = None
    for piece in _split3(a):
        d = jnp.dot(piece, b, preferred_element_type=F32)
        acc = d if acc is None else acc + d
    return acc


def _dot_x3_left(b, a):
    acc = None
    for piece in _split3(a):
        d = jnp.dot(b, piece, preferred_element_type=F32)
        acc = d if acc is None else acc + d
    return acc


def _dot_nt(a, b, **kw):
    return lax.dot_general(a, b, (((1,), (1,)), ((), ())), preferred_element_type=F32, **kw)


def _ffn_kernel(x_ref, gi_ref, go_ref, wg_ref, wu_ref, wd_ref, o_ref, h_scr, acc_scr):
    f = pl.program_id(1)

    @pl.when(f == 0)
    def _():
        h_scr[...] = _rms(x_ref[...], gi_ref[...], NORM_EPS).astype(BF16)
        acc_scr[...] = jnp.zeros_like(acc_scr)

    h = h_scr[...]
    gate = jnp.dot(h, wg_ref[...], preferred_element_type=F32)
    up = jnp.dot(h, wu_ref[...], preferred_element_type=F32)
    act = (_silu(gate) * up).astype(BF16)
    acc_scr[...] += jnp.dot(act, wd_ref[...], preferred_element_type=F32)

    @pl.when(f == pl.num_programs(1) - 1)
    def _():
        o_ref[...] = x_ref[...] + 0.5 * _rms(acc_scr[...], go_ref[...], NORM_EPS)


def _ffn(x, g_in, g_out, w_gate, w_up, w_down, tm=512, tf=1408):
    t, d = x.shape
    f = w_gate.shape[1]
    return pl.pallas_call(
        _ffn_kernel,
        grid=(t // tm, f // tf),
        in_specs=[
            pl.BlockSpec((tm, d), lambda i, j: (i, 0)),
            pl.BlockSpec((1, d), lambda i, j: (0, 0)),
            pl.BlockSpec((1, d), lambda i, j: (0, 0)),
            pl.BlockSpec((d, tf), lambda i, j: (0, j)),
            pl.BlockSpec((d, tf), lambda i, j: (0, j)),
            pl.BlockSpec((tf, d), lambda i, j: (j, 0)),
        ],
        out_specs=pl.BlockSpec((tm, d), lambda i, j: (i, 0)),
        out_shape=jax.ShapeDtypeStruct((t, d), F32),
        scratch_shapes=[pltpu.VMEM((tm, d), BF16), pltpu.VMEM((tm, d), F32)],
        compiler_params=_params("parallel", "arbitrary"),
        name="ffn_half_step",
    )(x, g_in.reshape(1, d), g_out.reshape(1, d), w_gate.astype(BF16), w_up.astype(BF16),
      w_down.astype(BF16))


INPROJ_MISC_W = 256
INPROJ_KV_W = 6 * NSA_KV_W
INPROJ_WIDTH = NSA_Q_W + INPROJ_KV_W + SSD_D_INNER + SSD_XBC + INPROJ_MISC_W


def _swap_halves(x):
    w = x.shape[-1]
    lane = lax.broadcasted_iota(jnp.int32, x.shape, x.ndim - 1)
    low = (lane & (NSA_HEAD_DIM - 1)) < (NSA_HEAD_DIM // 2)
    return jnp.where(low, pltpu.roll(x, w - NSA_HEAD_DIM // 2, x.ndim - 1),
                     pltpu.roll(x, NSA_HEAD_DIM // 2, x.ndim - 1))


def _inproj_kernel(x_ref, g_ref, w_ref, cos_ref, sin_ref, q_ref, kv_ref, z_ref, xbc_ref, misc_ref):
    h = _rms(x_ref[...], g_ref[...], NORM_EPS).astype(BF16)
    proj = jnp.dot(h, w_ref[...], preferred_element_type=F32)
    cos = cos_ref[...]
    sin = sin_ref[...]
    o = 0
    q = proj[:, o:o + NSA_Q_W]
    cos_q = jnp.concatenate([cos] * (NSA_Q_W // LANES), axis=1)
    sin_q = jnp.concatenate([sin] * (NSA_Q_W // LANES), axis=1)
    q_ref[...] = (q * cos_q + _swap_halves(q) * sin_q) * (NSA_HEAD_DIM ** -0.5)
    o += NSA_Q_W
    for i in range(6):
        piece = proj[:, o:o + NSA_KV_W]
        if i % 2 == 0:
            piece = piece * cos + _swap_halves(piece) * sin
        kv_ref[:, i * NSA_KV_W:(i + 1) * NSA_KV_W] = piece
        o += NSA_KV_W
    z_ref[...] = proj[:, o:o + SSD_D_INNER]
    o += SSD_D_INNER
    xbc_ref[...] = proj[:, o:o + SSD_XBC]
    o += SSD_XBC
    misc_ref[...] = proj[:, o:o + INPROJ_MISC_W]


def _inproj(x, g, w_in, cos, sin, seq, tm=256):
    t, d = x.shape
    offs = np.cumsum(IN_SPLITS)[:-1].tolist()
    q, kc, vc, ks, vs, kw, vw, gl, z, xbc, dt = jnp.split(w_in, offs, -1)
    pad = jnp.zeros((d, INPROJ_MISC_W - gl.shape[1] - dt.shape[1]), w_in.dtype)
    w = jnp.concatenate([q, kc, vc, ks, vs, kw, vw, z, xbc, gl, dt, pad], -1).astype(BF16)
    assert w.shape[1] == INPROJ_WIDTH
    nseq = seq // tm
    row = lambda i: (i, 0)
    const = lambda i: (0, 0)
    widths = (NSA_Q_W, INPROJ_KV_W, SSD_D_INNER, SSD_XBC, INPROJ_MISC_W)
    return pl.pallas_call(
        _inproj_kernel,
        grid=(t // tm,),
        in_specs=[
            pl.BlockSpec((tm, d), row),
            pl.BlockSpec((1, d), const),
            pl.BlockSpec((d, INPROJ_WIDTH), const),
            pl.BlockSpec((tm, LANES), lambda i: (i % nseq, 0)),
            pl.BlockSpec((tm, LANES), lambda i: (i % nseq, 0)),
        ],
        out_specs=[pl.BlockSpec((tm, wd), row) for wd in widths],
        out_shape=[jax.ShapeDtypeStruct((t, wd), F32) for wd in widths],
        compiler_params=_params("parallel"),
        name="mixer0_in_proj",
    )(x, g.reshape(1, d), w, cos, sin)


def _rope_tables(seq):
    inv = ROPE_THETA ** (-jnp.arange(0, NSA_HEAD_DIM, 2, dtype=F32) / NSA_HEAD_DIM)
    ang = jnp.arange(seq, dtype=F32)[:, None] * inv[None, :]
    cos, sin = jnp.cos(ang), jnp.sin(ang)
    reps = LANES // NSA_HEAD_DIM
    cos_t = jnp.concatenate([cos, cos] * reps, -1)
    sin_t = jnp.concatenate([-sin, sin] * reps, -1)
    return cos_t, sin_t


def _compress_kernel(k_ref, pe_ref, w1_ref, w2_ref, o_ref):
    k16 = k_ref[0, 0]
    w1 = w1_ref[0]
    half = w1.shape[0] // 2
    first = jnp.dot(k16, w1[:half], precision=HIGHEST, preferred_element_type=F32)
    second = jnp.dot(k16, w1[half:], precision=HIGHEST, preferred_element_type=F32)
    bias = jnp.dot(pe_ref[0], w1, precision=HIGHEST, preferred_element_type=F32)[0:1]
    n = k16.shape[0]
    pre = first + pltpu.roll(second, n - 1, 0) + bias
    o_ref[0, 0] = jnp.dot(_silu(pre), w2_ref[0], precision=HIGHEST, preferred_element_type=F32)


def _compress(kv16, pe, w1, w2):
    two, bh, n, wd = kv16.shape
    d = w2.shape[-1]
    return pl.pallas_call(
        _compress_kernel,
        grid=(two, bh),
        in_specs=[
            pl.BlockSpec((1, 1, n, wd), lambda a, b: (a, b, 0, 0)),
            pl.BlockSpec((1, SUBLANES, pe.shape[-1]), lambda a, b: (a, 0, 0)),
            pl.BlockSpec((1,) + w1.shape[1:], lambda a, b: (a, 0, 0)),
            pl.BlockSpec((1, d, d), lambda a, b: (a, 0, 0)),
        ],
        out_specs=pl.BlockSpec((1, 1, n, d), lambda a, b: (a, b, 0, 0)),
        out_shape=jax.ShapeDtypeStruct((two, bh, n, d), F32),
        compiler_params=_params("parallel", "parallel"),
        name="nsa_compress",
    )(kv16, pe, w1, w2)


def _group_rows(q):
    return jnp.concatenate(
        [q[:, g * NSA_HEAD_DIM:(g + 1) * NSA_HEAD_DIM] for g in range(NSA_GROUP)], axis=0)


def _ungroup_rows(o, tq):
    return jnp.concatenate([o[g * tq:(g + 1) * tq] for g in range(NSA_GROUP)], axis=1)


def _nsa_select_kernel(q_ref, kc_ref, vc_ref, ov_ref, oc_ref, bias_ref, *, tq, topk):
    s0 = pl.program_id(2) * tq
    qg = _group_rows(q_ref[0])
    kc = kc_ref[0]
    s = _dot_nt(qg, kc, precision=HIGHEST)
    rows, ncmp = s.shape
    t = s0 + (lax.broadcasted_iota(jnp.int32, (rows, ncmp), 0) & (tq - 1))
    cmp_end = lax.broadcasted_iota(jnp.int32, (rows, ncmp), 1) * CMP_STRIDE + (CMP_BLOCK - 1)
    mask = cmp_end <= t
    s = jnp.where(mask, s, NEG_MASK)
    p = jnp.where(mask, jnp.exp(s - jnp.max(s, -1, keepdims=True)), 0.0)
    p = p / jnp.maximum(jnp.sum(p, -1, keepdims=True), 1e-30)
    o = jnp.dot(p.astype(BF16), vc_ref[0].astype(BF16), preferred_element_type=F32)
    oc_ref[0] = _ungroup_rows(o, tq)

    psum = p[0:tq]
    for g in range(1, NSA_GROUP):
        psum = psum + p[g * tq:(g + 1) * tq]
    imp = _dot_x3(psum, ov_ref[...])
    blk = lax.broadcasted_iota(jnp.int32, imp.shape, 1)
    tt = s0 + lax.broadcasted_iota(jnp.int32, imp.shape, 0)
    cur = lax.shift_right_logical(tt, SEL_BLOCK.bit_length() - 1)
    forced = (blk == 0) | (blk == cur) | (blk == cur - 1)
    valid = blk * SEL_BLOCK <= tt
    x = jnp.where(valid, jnp.where(forced, FORCE_SCORE, imp), NEG_MASK)
    blk_f = blk.astype(F32)
    sel = jnp.zeros(imp.shape, jnp.bool_)
    for _ in range(topk):
        m = jnp.max(x, -1, keepdims=True)
        idx = jnp.min(jnp.where(x == m, blk_f, float(SEL_LANES)), -1, keepdims=True)
        hit = blk_f == idx
        sel = sel | hit
        x = jnp.where(hit, NEG_TAKEN, x)
    bias_ref[0, 0] = jnp.where(sel, 0.0, NEG_UNSELECTED).astype(BF16)


def _nsa_select(q, kc, vc, overlap, batch, seq, tq=128):
    ncmp = kc.shape[1]
    gw = NSA_GROUP * NSA_HEAD_DIM
    topk = min(SEL_TOPK, seq // SEL_BLOCK)
    kern = functools.partial(_nsa_select_kernel, tq=tq, topk=topk)
    return pl.pallas_call(
        kern,
        grid=(batch, NSA_KV_HEADS, seq // tq),
        in_specs=[
            pl.BlockSpec((1, tq, gw), lambda b, h, i: (b, i, h)),
            pl.BlockSpec((1, ncmp, NSA_HEAD_DIM), lambda b, h, i: (b * NSA_KV_HEADS + h, 0, 0)),
            pl.BlockSpec((1, ncmp, NSA_HEAD_DIM), lambda b, h, i: (b * NSA_KV_HEADS + h, 0, 0)),
            pl.BlockSpec((ncmp, SEL_LANES), lambda b, h, i: (0, 0)),
        ],
        out_specs=[
            pl.BlockSpec((1, tq, gw), lambda b, h, i: (b, i, h)),
            pl.BlockSpec((1, 1, tq, SEL_LANES), lambda b, h, i: (b, h, i, 0)),
        ],
        out_shape=[
            jax.ShapeDtypeStruct((batch, seq, NSA_KV_HEADS * gw), F32),
            jax.ShapeDtypeStruct((batch, NSA_KV_HEADS, seq, SEL_LANES), BF16),
        ],
        compiler_params=_params("parallel", "parallel", "parallel"),
        name="nsa_compressed_select",
    )(q, kc, vc, overlap)


def _softmax_step(carry, s, v, mask):
    m, l, acc = carry
    if mask is not None:
        s = jnp.where(mask, s, NEG_MASK)
    m_new = jnp.maximum(m, jnp.max(s, -1, keepdims=True))
    alpha = jnp.exp(m - m_new)
    p = jnp.exp(s - m_new)
    if mask is not None:
        p = jnp.where(mask, p, 0.0)
    l = alpha * l + jnp.sum(p, -1, keepdims=True)
    acc = alpha * acc + jnp.dot(p.astype(BF16), v, preferred_element_type=F32)
    return m_new, l, acc


def _nsa_attend_kernel(q_ref, bias_ref, ka_ref, vs_ref, kw_ref, vw_ref, oc_ref, gl_ref, o_ref,
                       *, tq, tk):
    i = pl.program_id(2)
    s0 = i * tq
    rows = NSA_GROUP * tq
    qg = _group_rows(q_ref[0]).astype(BF16)
    bias = bias_ref[0, 0]
    qa = jnp.concatenate([jnp.concatenate([bias] * NSA_GROUP, axis=0), qg], axis=1)
    init = (jnp.full((rows, 1), NEG_MASK, F32), jnp.zeros((rows, 1), F32),
            jnp.zeros((rows, NSA_HEAD_DIM), F32))

    def sel_tile(kt, carry, causal):
        k = ka_ref[0, pl.ds(pl.multiple_of(kt * tk, tk), tk), :]
        v = vs_ref[0, pl.ds(pl.multiple_of(kt * tk, tk), tk), :]
        s = _dot_nt(qa, k)
        mask = None
        if causal:
            t = s0 + (lax.broadcasted_iota(jnp.int32, s.shape, 0) & (tq - 1))
            kp = kt * tk + lax.broadcasted_iota(jnp.int32, s.shape, 1)
            mask = kp <= t
        return _softmax_step(carry, s, v, mask)

    n_full = s0 // tk
    carry = lax.fori_loop(0, n_full, lambda kt, c: sel_tile(kt, c, False), init)
    _, l_s, acc_s = sel_tile(n_full, carry, True)
    o_s = acc_s / jnp.maximum(l_s, 1e-30)

    def win_tile(kt, carry):
        k = kw_ref[0, pl.ds(pl.multiple_of(kt * tq, tq), tq), :]
        v = vw_ref[0, pl.ds(pl.multiple_of(kt * tq, tq), tq), :]
        s = _dot_nt(qg, k)
        t = s0 + (lax.broadcasted_iota(jnp.int32, s.shape, 0) & (tq - 1))
        kp = kt * tq + lax.broadcasted_iota(jnp.int32, s.shape, 1)
        return _softmax_step(carry, s, v, (kp <= t) & (kp > t - WINDOW))

    first = jnp.maximum(i - WINDOW // tq, 0)
    _, l_w, acc_w = lax.fori_loop(first, i + 1, win_tile, init)
    o_w = acc_w / jnp.maximum(l_w, 1e-30)

    gates = jax.nn.sigmoid(gl_ref[0, 0])
    o_c = oc_ref[0]
    pieces = []
    for g in range(NSA_GROUP):
        r = slice(g * tq, (g + 1) * tq)
        pieces.append(gates[:, 3 * g:3 * g + 1] * o_c[:, g * NSA_HEAD_DIM:(g + 1) * NSA_HEAD_DIM]
                      + gates[:, 3 * g + 1:3 * g + 2] * o_s[r]
                      + gates[:, 3 * g + 2:3 * g + 3] * o_w[r])
    o_ref[0] = jnp.concatenate(pieces, axis=1)


def _nsa_attend(q, bias, kaug, vs, kw, vw, o_c, gl, batch, seq, tq=128, tk=256):
    gw = NSA_GROUP * NSA_HEAD_DIM
    tk = min(tk, seq)
    assert tk % tq == 0 and WINDOW % tq == 0
    kern = functools.partial(_nsa_attend_kernel, tq=tq, tk=tk)
    bh = lambda b, h, i: (b * NSA_KV_HEADS + h, 0, 0)
    return pl.pallas_call(
        kern,
        grid=(batch, NSA_KV_HEADS, seq // tq),
        in_specs=[
            pl.BlockSpec((1, tq, gw), lambda b, h, i: (b, i, h)),
            pl.BlockSpec((1, 1, tq, SEL_LANES), lambda b, h, i: (b, h, i, 0)),
            pl.BlockSpec((1, seq, SEL_LANES + NSA_HEAD_DIM), bh),
            pl.BlockSpec((1, seq, NSA_HEAD_DIM), bh),
            pl.BlockSpec((1, seq, NSA_HEAD_DIM), bh),
            pl.BlockSpec((1, seq, NSA_HEAD_DIM), bh),
            pl.BlockSpec((1, tq, gw), lambda b, h, i: (b, i, h)),
            pl.BlockSpec((1, 1, tq, NSA_GROUP * 3), lambda b, h, i: (b, h, i, 0)),
        ],
        out_specs=pl.BlockSpec((1, tq, gw), lambda b, h, i: (b, i, h)),
        out_shape=jax.ShapeDtypeStruct((batch, seq, NSA_KV_HEADS * gw), F32),
        compiler_params=_params("parallel", "parallel", "arbitrary"),
        name="nsa_selected_window",
    )(q, bias, kaug, vs, kw, vw, o_c, gl)


def _nsa(q, kv, misc, pe_k, w1_k, w2_k, pe_v, w1_v, w2_v, batch, seq):
    d = NSA_HEAD_DIM
    heads = lambda a: jnp.moveaxis(a.reshape(batch, seq, NSA_KV_HEADS, d), 2, 1)
    piece = lambda i: heads(kv[:, i * NSA_KV_W:(i + 1) * NSA_KV_W])
    k_cmp, v_cmp, k_sel, v_sel, k_win, v_win = [piece(i) for i in range(6)]
    bh = batch * NSA_KV_HEADS
    n16 = seq // CMP_STRIDE
    kv16 = jnp.stack([k_cmp, v_cmp]).reshape(2, bh, n16, CMP_STRIDE * d)
    pe = jnp.stack([pe_k, pe_v]).reshape(2, 1, CMP_BLOCK * d)
    pe = jnp.broadcast_to(pe, (2, SUBLANES, CMP_BLOCK * d))
    cmp = _compress(kv16, pe, jnp.stack([w1_k, w1_v]), jnp.stack([w2_k, w2_v]))
    kc, vc = cmp[0], cmp[1]

    n_sel = seq // SEL_BLOCK
    cmp_start = np.arange(n16) * CMP_STRIDE
    sel_start = np.arange(SEL_LANES) * SEL_BLOCK
    overlap = ((cmp_start[:, None] < sel_start[None, :] + SEL_BLOCK)
               & (cmp_start[:, None] + CMP_BLOCK - 1 >= sel_start[None, :])
               & (np.arange(SEL_LANES)[None, :] < n_sel)
               & (np.arange(n16)[:, None] < (seq - CMP_BLOCK) // CMP_STRIDE + 1))
    overlap = jnp.asarray(overlap, BF16)

    q3 = q.reshape(batch, seq, NSA_Q_W)
    o_c, bias = _nsa_select(q3, kc, vc, overlap, batch, seq)

    onehot = (np.arange(seq)[:, None] // SEL_BLOCK == np.arange(SEL_LANES)[None, :])
    onehot = jnp.broadcast_to(jnp.asarray(onehot, BF16), (bh, seq, SEL_LANES))
    flat = lambda a: a.reshape(bh, seq, d).astype(BF16)
    kaug = jnp.concatenate([onehot, flat(k_sel)], -1)
    gl = misc[:, :NSA_HEADS * 3].reshape(batch, seq, NSA_KV_HEADS, NSA_GROUP * 3)
    gl = jnp.moveaxis(gl, 2, 1)
    o = _nsa_attend(q3, bias, kaug, flat(v_sel), flat(k_win), flat(v_win), o_c, gl, batch, seq)
    return o.reshape(batch * seq, NSA_Q_W)


def _ssd_kernel(xbc_ref, halo_ref, z_ref, dt_ref, cw_ref, cb_ref, dtb_ref, alog_ref, dskip_ref,
                nw_ref, tril_ref, o_ref, state_scr, y_scr):
    c = pl.program_id(1)
    l = SSD_CHUNK

    @pl.when(c == 0)
    def _():
        state_scr[...] = jnp.zeros_like(state_scr)

    x = xbc_ref[0]
    halo = jnp.where(c == 0, 0.0, halo_ref[0])
    xx = jnp.concatenate([halo, x], axis=0)
    cw = cw_ref[...]
    conv = cb_ref[...]
    for k in range(SSD_CONV):
        off = SUBLANES - (SSD_CONV - 1) + k
        conv = conv + cw[k:k + 1] * xx[off:off + l]
    xbc = _silu(conv)
    xs = xbc[:, :SSD_D_INNER]
    gn = SSD_GROUPS * SSD_STATE
    bmat = xbc[:, SSD_D_INNER:SSD_D_INNER + gn]
    cmat = xbc[:, SSD_D_INNER + gn:]

    dt = _softplus(dt_ref[0] + dtb_ref[...])
    da = dt * (-jnp.exp(alog_ref[...]))
    a_cs = _dot_x3_left(tril_ref[...], da)
    a_cs_t = a_cs.T
    a_last = a_cs[l - 1:l]
    causal = (lax.broadcasted_iota(jnp.int32, (l, l), 0)
              >= lax.broadcasted_iota(jnp.int32, (l, l), 1))
    dskip = dskip_ref[...]

    heads_per_group = SSD_HEADS // SSD_GROUPS
    for g in range(SSD_GROUPS):
        bg = bmat[:, g * SSD_STATE:(g + 1) * SSD_STATE]
        cg = cmat[:, g * SSD_STATE:(g + 1) * SSD_STATE]
        bg16 = bg.astype(BF16)
        cg16 = cg.astype(BF16)
        cb = _dot_nt(cg16, bg16)
        bgt16 = bg.T.astype(BF16)
        for r in range(heads_per_group):
            h = g * heads_per_group + r
            hs = slice(h * SSD_HEAD_DIM, (h + 1) * SSD_HEAD_DIM)
            col = a_cs[:, h:h + 1]
            rowv = a_cs_t[h:h + 1, :]
            seg = jnp.where(causal, jnp.exp(col - rowv), 0.0)
            xd = xs[:, hs] * dt[:, h:h + 1]
            y = jnp.dot((cb * seg).astype(BF16), xd.astype(BF16), preferred_element_type=F32)
            st = state_scr[h]
            y = y + jnp.exp(col) * jnp.dot(cg16, st.astype(BF16), preferred_element_type=F32)
            decay = jnp.exp(a_last[:, h:h + 1] - col)
            new = jnp.dot(bgt16, (decay * xd).astype(BF16), preferred_element_type=F32)
            state_scr[h] = st * jnp.exp(a_last[:, h:h + 1]) + new
            y_scr[:, hs] = y + xs[:, hs] * dskip[:, h:h + 1]

    y = y_scr[...] * _silu(z_ref[0])
    gw = SSD_D_INNER // SSD_GROUPS
    outs = []
    for g in range(SSD_GROUPS):
        yg = y[:, g * gw:(g + 1) * gw]
        outs.append(yg * lax.rsqrt(jnp.mean(yg * yg, -1, keepdims=True) + SSD_NORM_EPS))
    o_ref[0] = jnp.concatenate(outs, axis=1) * nw_ref[...]


def _pad_lanes(v, width=LANES):
    v = v.reshape(1, -1).astype(F32)
    return jnp.pad(v, ((0, 0), (0, width - v.shape[1])))


def _ssd(z, xbc, misc, conv_w, conv_b, dt_bias, a_log, d_skip, norm_w, batch, seq):
    l = SSD_CHUNK
    nc = seq // l
    z3 = z.reshape(batch, seq, SSD_D_INNER)
    x3 = xbc.reshape(batch, seq, SSD_XBC)
    dt = misc[:, NSA_HEADS * 3:NSA_HEADS * 3 + SSD_HEADS]
    dt3 = jnp.pad(dt, ((0, 0), (0, LANES - SSD_HEADS))).reshape(batch, seq, LANES)
    tril = jnp.asarray(np.tril(np.ones((l, l))), BF16)
    hb = l // SUBLANES
    const = lambda b, c: (0, 0)
    return pl.pallas_call(
        _ssd_kernel,
        grid=(batch, nc),
        in_specs=[
            pl.BlockSpec((1, l, SSD_XBC), lambda b, c: (b, c, 0)),
            pl.BlockSpec((1, SUBLANES, SSD_XBC), lambda b, c: (b, jnp.maximum(c * hb - 1, 0), 0)),
            pl.BlockSpec((1, l, SSD_D_INNER), lambda b, c: (b, c, 0)),
            pl.BlockSpec((1, l, LANES), lambda b, c: (b, c, 0)),
            pl.BlockSpec((SSD_CONV, SSD_XBC), const),
            pl.BlockSpec((1, SSD_XBC), const),
            pl.BlockSpec((1, LANES), const),
            pl.BlockSpec((1, LANES), const),
            pl.BlockSpec((1, LANES), const),
            pl.BlockSpec((1, SSD_D_INNER), const),
            pl.BlockSpec((l, l), const),
        ],
        out_specs=pl.BlockSpec((1, l, SSD_D_INNER), lambda b, c: (b, c, 0)),
        out_shape=jax.ShapeDtypeStruct((batch, seq, SSD_D_INNER), F32),
        scratch_shapes=[pltpu.VMEM((SSD_HEADS, SSD_STATE, SSD_HEAD_DIM), F32),
                        pltpu.VMEM((l, SSD_D_INNER), F32)],
        compiler_params=_params("parallel", "arbitrary"),
        name="ssd_chunk_scan",
    )(x3, x3, z3, dt3, conv_w.reshape(SSD_CONV, SSD_XBC), conv_b.reshape(1, SSD_XBC),
      _pad_lanes(dt_bias), _pad_lanes(a_log), _pad_lanes(d_skip), norm_w.reshape(1, SSD_D_INNER),
      tril).reshape(batch * seq, SSD_D_INNER)


def _outproj_kernel(*refs, n_parts):
    x_ref, g_ref = refs[0], refs[1]
    parts = refs[2:2 + n_parts]
    ws = refs[2 + n_parts:2 + 2 * n_parts]
    o_ref = refs[2 + 2 * n_parts]
    acc = None
    for p_ref, w_ref in zip(parts, ws):
        d = jnp.dot(p_ref[...].astype(BF16), w_ref[...], preferred_element_type=F32)
        acc = d if acc is None else acc + d
    o_ref[...] = x_ref[...] + _rms(acc, g_ref[...], NORM_EPS)


def _outproj(x, g, parts, weights, tm=512):
    t, d = x.shape
    row = lambda i: (i, 0)
    const = lambda i: (0, 0)
    n = len(parts)
    return pl.pallas_call(
        functools.partial(_outproj_kernel, n_parts=n),
        grid=(t // tm,),
        in_specs=([pl.BlockSpec((tm, d), row), pl.BlockSpec((1, d), const)]
                  + [pl.BlockSpec((tm, p.shape[1]), row) for p in parts]
                  + [pl.BlockSpec(w.shape, const) for w in weights]),
        out_specs=pl.BlockSpec((tm, d), row),
        out_shape=jax.ShapeDtypeStruct((t, d), F32),
        compiler_params=_params("parallel"),
        name="mixer_out_proj",
    )(x, g.reshape(1, d), *parts, *[w.astype(BF16) for w in weights])


def _head_sum(x, seg, seg_t):
    return _dot_x3(_dot_x3(x, seg), seg_t)


def _rwkv_pre_kernel(x_ref, halo_ref, g_ref, mu_ref, wr_ref, wk_ref, wv_ref, w0_ref, w1_ref,
                     w2_ref, a0_ref, a1_ref, a2_ref, g1_ref, g2_ref, kk_ref, ka_ref, seg_ref,
                     segt_ref, r_out, ld_out, k_out, v_out, kk_out, g_out, bt_out, kt_out, ldt_out,
                     *, tiles_per_seq):
    i = pl.program_id(0)
    h = _rms(x_ref[...], g_ref[...], NORM_EPS)
    prev_row = _rms(halo_ref[...], g_ref[...], NORM_EPS)[SUBLANES - 1:SUBLANES]
    prev_row = jnp.where(i % tiles_per_seq == 0, 0.0, prev_row)
    rowid = lax.broadcasted_iota(jnp.int32, h.shape, 0)
    prev = jnp.where(rowid == 0, prev_row, pltpu.roll(h, 1, 0))
    xx = prev - h
    mu = mu_ref[...]
    mix = lambda j: (h + xx * mu[j:j + 1]).astype(BF16)
    dot = lambda a, w_ref: jnp.dot(a, w_ref[...], preferred_element_type=F32)
    r = dot(mix(0), wr_ref)
    w = -_softplus(-(w0_ref[...] + dot(jnp.tanh(dot(mix(1), w1_ref)).astype(BF16), w2_ref))) - 0.5
    k = dot(mix(2), wk_ref)
    v = dot(mix(3), wv_ref)
    a = jax.nn.sigmoid(a0_ref[...] + dot(dot(mix(4), a1_ref).astype(BF16), a2_ref))
    g = dot(jax.nn.sigmoid(dot(mix(5), g1_ref)).astype(BF16), g2_ref)
    kk = k * kk_ref[...]
    norm = jnp.sqrt(_head_sum(kk * kk, seg_ref[...], segt_ref[...]))
    kk = kk / jnp.maximum(norm, 1e-12)
    k = k * (1.0 + (a - 1.0) * ka_ref[...])
    log_decay = -jnp.exp(w)
    r_out[...] = r
    ld_out[...] = log_decay
    k_out[...] = k
    v_out[...] = v
    kk_out[...] = kk
    g_out[...] = g
    bt_out[0] = (kk * a).T
    kt_out[0] = k.T
    ldt_out[0] = log_decay.T


def _pad_cols(w, width):
    return jnp.pad(w, ((0, 0), (0, width - w.shape[1])))


def _pad_rows(w, width):
    return jnp.pad(w, ((0, width - w.shape[0]), (0, 0)))


def _seg_matrices():
    seg = np.zeros((D_MODEL, LANES), np.float32)
    seg[np.arange(D_MODEL), np.arange(D_MODEL) // RWKV_HEAD_DIM] = 1.0
    return jnp.asarray(seg, BF16), jnp.asarray(seg.T, BF16)


def _rwkv_pre(x, g, mu, w_r, w_k, w_v, w0, w1, w2, a0, a1, a2, g1, g2, k_k, k_a, seq, tm=256):
    t, d = x.shape
    lora = lambda w: -(-w.shape[1] // LANES) * LANES
    w1p, w2p = _pad_cols(w1, lora(w1)), _pad_rows(w2, lora(w1))
    a1p, a2p = _pad_cols(a1, lora(a1)), _pad_rows(a2, lora(a1))
    g1p, g2p = _pad_cols(g1, lora(g1)), _pad_rows(g2, lora(g1))
    seg, seg_t = _seg_matrices()
    row = lambda i: (i, 0)
    const = lambda i: (0, 0)
    hb = tm // SUBLANES
    vec = lambda v: v.reshape(1, d)
    mats = [w.astype(BF16) for w in (w_r, w_k, w_v)]
    ins = [x, x, vec(g), mu, *mats, vec(w0), w1p.astype(BF16), w2p.astype(BF16), vec(a0),
           a1p.astype(BF16), a2p.astype(BF16), g1p.astype(BF16), g2p.astype(BF16), vec(k_k),
           vec(k_a), seg, seg_t]
    in_specs = [pl.BlockSpec((tm, d), row),
                pl.BlockSpec((SUBLANES, d), lambda i: (jnp.maximum(i * hb - 1, 0), 0))]
    in_specs += [pl.BlockSpec(a.shape, const) for a in ins[2:]]
    tps = seq // tm
    col = pl.BlockSpec((1, d, tm), lambda i: (i // tps, 0, i % tps))
    return pl.pallas_call(
        functools.partial(_rwkv_pre_kernel, tiles_per_seq=tps),
        grid=(t // tm,),
        in_specs=in_specs,
        out_specs=[pl.BlockSpec((tm, d), row)] * 6 + [col] * 3,
        out_shape=([jax.ShapeDtypeStruct((t, d), F32)] * 6
                   + [jax.ShapeDtypeStruct((t // seq, d, seq), F32)] * 3),
        compiler_params=_params("parallel"),
        name="rwkv7_projections",
    )(*ins)


RWKV_CHUNK = 128


def _rwkv_chunk_kernel(r_ref, ld_ref, kk_ref, v_ref, bt_ref, kt_ref, ldt_ref, tril_ref, triu_ref,
                       y_ref, state_scr):
    @pl.when(pl.program_id(1) == 0)
    def _():
        state_scr[...] = jnp.zeros_like(state_scr)

    l = RWKV_CHUNK
    hd = RWKV_HEAD_DIM
    tril = tril_ref[...]
    ld = ld_ref[0]
    c_in = _dot_x3_left(tril, ld)
    a_bar = -kk_ref[0] * jnp.exp(c_in - ld)
    r_bar = r_ref[0] * jnp.exp(c_in)
    v = v_ref[0]
    ldt = ldt_ref[0]
    c_t = _dot_x3(ldt, triu_ref[...])
    scale_t = jnp.exp(-c_t)
    b_t = bt_ref[0] * scale_t
    k_t = kt_ref[0] * scale_t
    decay_col = jnp.exp(c_t[:, l - 1:l])

    row = lax.broadcasted_iota(jnp.int32, (l, l), 0)
    colx = lax.broadcasted_iota(jnp.int32, (l, l), 1)
    strict = row > colx
    incl = row >= colx
    lane = lax.broadcasted_iota(jnp.int32, (l, LANES), 1)
    first_head = lane < hd
    blockdiag = (lax.broadcasted_iota(jnp.int32, (LANES, LANES), 0) < hd) == (
        lax.broadcasted_iota(jnp.int32, (LANES, LANES), 1) < hd)
    mm = lambda a, b: jnp.dot(a.astype(BF16), b.astype(BF16), preferred_element_type=F32)

    for c in range(D_MODEL // LANES):
        ls = slice(c * LANES, (c + 1) * LANES)
        a_p, r_p, v_p = a_bar[:, ls], r_bar[:, ls], v[:, ls]
        bk_t = jnp.concatenate([b_t[ls], k_t[ls]], axis=1)
        h2 = state_scr[c]
        rhs = jnp.concatenate([bk_t, h2], axis=1)
        us, ys = [], []
        for hh in range(2):
            keep = first_head if hh == 0 else ~first_head
            x = jnp.concatenate([jnp.where(keep, a_p, 0.0), jnp.where(keep, r_p, 0.0)], axis=0)
            gh = mm(x, rhs)
            m_ab = jnp.where(strict, gh[:l, :l], 0.0)
            m_ak = jnp.where(strict, gh[:l, l:2 * l], 0.0)
            p_rb = jnp.where(incl, gh[l:, :l], 0.0)
            p_rk = jnp.where(incl, gh[l:, l:2 * l], 0.0)
            u = gh[:l, 2 * l:] + mm(m_ak, v_p)
            mp = m_ab
            n_factors = l.bit_length() - 1
            for f in range(n_factors):
                u = u + mm(mp, u)
                if f + 1 < n_factors:
                    mp = mm(mp, mp)
            us.append(u)
            ys.append(gh[l:, 2 * l:] + mm(jnp.concatenate([p_rb, p_rk], axis=1),
                                          jnp.concatenate([u, v_p], axis=0)))
        u_pair = jnp.where(first_head, us[0], us[1])
        y_ref[0, :, ls] = jnp.where(first_head, ys[0], ys[1])
        upd = h2 + mm(bk_t, jnp.concatenate([u_pair, v_p], axis=0))
        state_scr[c] = jnp.where(blockdiag, upd * decay_col[ls], 0.0)


def _rwkv_scan(r, ld, kk, v, bt, kt, ldt, batch, seq):
    l = RWKV_CHUNK
    d = D_MODEL
    rows = lambda x: x.reshape(batch, seq, d)
    rblk = pl.BlockSpec((1, l, d), lambda b, c: (b, c, 0))
    cblk = pl.BlockSpec((1, d, l), lambda b, c: (b, 0, c))
    tril = jnp.asarray(np.tril(np.ones((l, l))), BF16)
    y = pl.pallas_call(
        _rwkv_chunk_kernel,
        grid=(batch, seq // l),
        in_specs=[rblk] * 4 + [cblk] * 3 + [pl.BlockSpec((l, l), lambda b, c: (0, 0))] * 2,
        out_specs=rblk,
        out_shape=jax.ShapeDtypeStruct((batch, seq, d), F32),
        scratch_shapes=[pltpu.VMEM((d // LANES, LANES, LANES), F32)],
        compiler_params=_params("parallel", "arbitrary"),
        name="rwkv7_recurrence",
    )(rows(r), rows(ld), rows(kk), rows(v), bt, kt, ldt, tril, tril.T)
    return y.reshape(batch * seq, d)


def _rwkv_post_kernel(x_ref, y_ref, r_ref, k_ref, v_ref, g_ref, lng_ref, lnb_ref, rk_ref, wo_ref,
                      gn_ref, seg_ref, segt_ref, o_ref):
    seg, seg_t = seg_ref[...], segt_ref[...]
    y = y_ref[...]
    inv = 1.0 / RWKV_HEAD_DIM
    mean = _head_sum(y, seg, seg_t) * inv
    yc = y - mean
    var = _head_sum(yc * yc, seg, seg_t) * inv
    yn = yc * lax.rsqrt(var + RWKV_GN_EPS) * lng_ref[...] + lnb_ref[...]
    bonus = _head_sum(r_ref[...] * k_ref[...] * rk_ref[...], seg, seg_t) * v_ref[...]
    out = ((yn + bonus) * g_ref[...]).astype(BF16)
    proj = jnp.dot(out, wo_ref[...], preferred_element_type=F32)
    o_ref[...] = x_ref[...] + _rms(proj, gn_ref[...], NORM_EPS)


def _rwkv_post(x, y, r, k, v, g, ln_g, ln_b, r_k, w_o, gn, tm=256):
    t, d = x.shape
    seg, seg_t = _seg_matrices()
    row = lambda i: (i, 0)
    const = lambda i: (0, 0)
    vec = lambda a: a.reshape(1, d)
    small = [vec(ln_g), vec(ln_b), vec(r_k), w_o.astype(BF16), vec(gn), seg, seg_t]
    return pl.pallas_call(
        _rwkv_post_kernel,
        grid=(t // tm,),
        in_specs=[pl.BlockSpec((tm, d), row)] * 6 + [pl.BlockSpec(a.shape, const) for a in small],
        out_specs=pl.BlockSpec((tm, d), row),
        out_shape=jax.ShapeDtypeStruct((t, d), F32),
        compiler_params=_params("parallel"),
        name="rwkv7_output",
    )(x, y, r, k, v, g, *small)


def _nsa_ssd_mixer(x, g_pre, g_post, cos, sin, w_in, pe_k, w1_k, w2_k, pe_v, w1_v, w2_v, conv_w,
                   conv_b, dt_bias, a_log, d_skip, norm_w, w_out, batch, seq):
    q, kv, z, xbc, misc = _inproj(x, g_pre, w_in, cos, sin, seq)
    o_a = _nsa(q, kv, misc, pe_k, w1_k, w2_k, pe_v, w1_v, w2_v, batch, seq)
    o_b = _ssd(z, xbc, misc, conv_w, conv_b, dt_bias, a_log, d_skip, norm_w, batch, seq)
    return _outproj(x, g_post, [o_a, o_b], [w_out[:NSA_Q_W], w_out[NSA_Q_W:]])


def _rwkv7_mixer(x, g_pre, g_post, mu, w_r, w_k, w_v, w_o, w0, w1, w2, a0, a1, a2, g1, g2, k_k,
                 k_a, r_k, ln_g, ln_b, batch, seq):
    r, ld, k, v, kk, g, bt, kt, ldt = _rwkv_pre(x, g_pre, mu, w_r, w_k, w_v, w0, w1, w2, a0, a1,
                                                a2, g1, g2, k_k, k_a, seq)
    y = _rwkv_scan(r, ld, kk, v, bt, kt, ldt, batch, seq)
    return _rwkv_post(x, y, r, k, v, g, ln_g, ln_b, r_k, w_o, g_post)


def kernel(x, norm_gains, ffn1_w_gate, ffn1_w_up, ffn1_w_down, ffn2_w_gate, ffn2_w_up, ffn2_w_down, ab_w_in, a_cmp_pe_k, a_cmp_w1_k, a_cmp_w2_k, a_cmp_pe_v, a_cmp_w1_v, a_cmp_w2_v, b_conv_w, b_conv_b, b_dt_bias, b_a_log, b_d_skip, b_norm_w, ab_w_out, c_mu, c_w_r, c_w_k, c_w_v, c_w_o, c_w0, c_w1, c_w2, c_a0, c_a1, c_a2, c_g1, c_g2, c_k_k, c_k_a, c_r_k, c_ln_g, c_ln_b):
    batch, seq, d = x.shape
    depth = norm_gains.shape[0]
    cos, sin = _rope_tables(seq)
    x = x.reshape(batch * seq, d)
    for layer in range(depth):
        ng = norm_gains[layer]
        x = _ffn(x, ng[0], ng[1], ffn1_w_gate[layer], ffn1_w_up[layer], ffn1_w_down[layer])
        i = layer // 2
        if layer % 2 == 0:
            x = _nsa_ssd_mixer(x, ng[2], ng[3], cos, sin, ab_w_in[i], a_cmp_pe_k[i], a_cmp_w1_k[i],
                               a_cmp_w2_k[i], a_cmp_pe_v[i], a_cmp_w1_v[i], a_cmp_w2_v[i],
                               b_conv_w[i], b_conv_b[i], b_dt_bias[i], b_a_log[i], b_d_skip[i],
                               b_norm_w[i], ab_w_out[i], batch, seq)
        else:
            x = _rwkv7_mixer(x, ng[2], ng[3], c_mu[i], c_w_r[i], c_w_k[i], c_w_v[i], c_w_o[i],
                             c_w0[i], c_w1[i], c_w2[i], c_a0[i], c_a1[i], c_a2[i], c_g1[i],
                             c_g2[i], c_k_k[i], c_k_a[i], c_r_k[i], c_ln_g[i], c_ln_b[i],
                             batch, seq)
        x = _ffn(x, ng[4], ng[5], ffn2_w_gate[layer], ffn2_w_up[layer], ffn2_w_down[layer])
    return x.reshape(batch, seq, d)
```

```python
import functools

import jax
import jax.numpy as jnp
import numpy as np
from jax import lax
from jax.experimental import pallas as pl
from jax.experimental.pallas import tpu as pltpu

F32 = jnp.float32
BF16 = jnp.bfloat16
HIGHEST = lax.Precision.HIGHEST

D_MODEL = 1024
D_FF = 2816
NORM_EPS = 1e-6
NSA_HEADS = 8
NSA_KV_HEADS = 2
NSA_GROUP = NSA_HEADS // NSA_KV_HEADS
NSA_HEAD_DIM = 64
CMP_BLOCK = 32
CMP_STRIDE = 16
SEL_BLOCK = 64
SEL_TOPK = 16
WINDOW = 512
ROPE_THETA = 10000.0
FORCE_SCORE = 1e4
SEL_LANES = 128
SSD_HEADS = 16
SSD_HEAD_DIM = 64
SSD_D_INNER = SSD_HEADS * SSD_HEAD_DIM
SSD_GROUPS = 2
SSD_STATE = 128
SSD_CONV = 4
SSD_CHUNK = 128
SSD_NORM_EPS = 1e-5
SSD_XBC = SSD_D_INNER + 2 * SSD_GROUPS * SSD_STATE
RWKV_HEAD_DIM = 64
RWKV_HEADS = D_MODEL // RWKV_HEAD_DIM
RWKV_GN_EPS = 64e-5

NSA_Q_W = NSA_HEADS * NSA_HEAD_DIM
NSA_KV_W = NSA_KV_HEADS * NSA_HEAD_DIM
IN_SPLITS = (NSA_Q_W, NSA_KV_W, NSA_KV_W, NSA_KV_W, NSA_KV_W, NSA_KV_W, NSA_KV_W,
             NSA_HEADS * 3, SSD_D_INNER, SSD_XBC, SSD_HEADS)
IN_WIDTH = sum(IN_SPLITS)

LANES = 128
SUBLANES = 8
VMEM_LIMIT_BYTES = 56 * 1024 * 1024

NEG_MASK = -1e30
NEG_UNSELECTED = -2.0 ** 30
NEG_TAKEN = -3e38


def _params(*sem):
    return pltpu.CompilerParams(dimension_semantics=sem, vmem_limit_bytes=VMEM_LIMIT_BYTES)


def _rms(x, g, eps):
    return x * lax.rsqrt(jnp.mean(x * x, -1, keepdims=True) + eps) * g


def _silu(x):
    return x * jax.nn.sigmoid(x)


def _softplus(x):
    return jnp.maximum(x, 0.0) + jnp.log1p(jnp.exp(-jnp.abs(x)))


def _split3(a):
    a1 = a.astype(BF16)
    r1 = a - a1.astype(F32)
    a2 = r1.astype(BF16)
    a3 = (r1 - a2.astype(F32)).astype(BF16)
    return a1, a2, a3


def _dot_x3(a, b):
    acc = None
    for piece in _split3(a):
        d = jnp.dot(piece, b, preferred_element_type=F32)
        acc = d if acc is None else acc + d
    return acc


def _dot_x3_left(b, a):
    acc = None
    for piece in _split3(a):
        d = jnp.dot(b, piece, preferred_element_type=F32)
        acc = d if acc is None else acc + d
    return acc


def _dot_nt(a, b, **kw):
    return lax.dot_general(a, b, (((1,), (1,)), ((), ())), preferred_element_type=F32, **kw)


def _ffn_kernel(x_ref, gi_ref, go_ref, wg_ref, wu_ref, wd_ref, o_ref, h_scr, acc_scr):
    f = pl.program_id(1)

    @pl.when(f == 0)
    def _():
        h_scr[...] = _rms(x_ref[...], gi_ref[...], NORM_EPS).astype(BF16)
        acc_scr[...] = jnp.zeros_like(acc_scr)

    h = h_scr[...]
    gate = jnp.dot(h, wg_ref[...], preferred_element_type=F32)
    up = jnp.dot(h, wu_ref[...], preferred_element_type=F32)
    act = (_silu(gate) * up).astype(BF16)
    acc_scr[...] += jnp.dot(act, wd_ref[...], preferred_element_type=F32)

    @pl.when(f == pl.num_programs(1) - 1)
    def _():
        o_ref[...] = x_ref[...] + 0.5 * _rms(acc_scr[...], go_ref[...], NORM_EPS)


def _ffn(x, g_in, g_out, w_gate, w_up, w_down, tm=512, tf=1408):
    t, d = x.shape
    f = w_gate.shape[1]
    return pl.pallas_call(
        _ffn_kernel,
        grid=(t // tm, f // tf),
        in_specs=[
            pl.BlockSpec((tm, d), lambda i, j: (i, 0)),
            pl.BlockSpec((1, d), lambda i, j: (0, 0)),
            pl.BlockSpec((1, d), lambda i, j: (0, 0)),
            pl.BlockSpec((d, tf), lambda i, j: (0, j)),
            pl.BlockSpec((d, tf), lambda i, j: (0, j)),
            pl.BlockSpec((tf, d), lambda i, j: (j, 0)),
        ],
        out_specs=pl.BlockSpec((tm, d), lambda i, j: (i, 0)),
        out_shape=jax.ShapeDtypeStruct((t, d), F32),
        scratch_shapes=[pltpu.VMEM((tm, d), BF16), pltpu.VMEM((tm, d), F32)],
        compiler_params=_params("parallel", "arbitrary"),
        name="ffn_half_step",
    )(x, g_in.reshape(1, d), g_out.reshape(1, d), w_gate.astype(BF16), w_up.astype(BF16),
      w_down.astype(BF16))


INPROJ_MISC_W = 256
INPROJ_KV_W = 6 * NSA_KV_W
INPROJ_WIDTH = NSA_Q_W + INPROJ_KV_W + SSD_D_INNER + SSD_XBC + INPROJ_MISC_W


def _swap_halves(x):
    w = x.shape[-1]
    lane = lax.broadcasted_iota(jnp.int32, x.shape, x.ndim - 1)
    low = (lane & (NSA_HEAD_DIM - 1)) < (NSA_HEAD_DIM // 2)
    return jnp.where(low, pltpu.roll(x, w - NSA_HEAD_DIM // 2, x.ndim - 1),
                     pltpu.roll(x, NSA_HEAD_DIM // 2, x.ndim - 1))


def _inproj_kernel(x_ref, g_ref, w_ref, cos_ref, sin_ref, q_ref, kv_ref, z_ref, xbc_ref, misc_ref):
    h = _rms(x_ref[...], g_ref[...], NORM_EPS).astype(BF16)
    proj = jnp.dot(h, w_ref[...], preferred_element_type=F32)
    cos = cos_ref[...]
    sin = sin_ref[...]
    o = 0
    q = proj[:, o:o + NSA_Q_W]
    cos_q = jnp.concatenate([cos] * (NSA_Q_W // LANES), axis=1)
    sin_q = jnp.concatenate([sin] * (NSA_Q_W // LANES), axis=1)
    q_ref[...] = (q * cos_q + _swap_halves(q) * sin_q) * (NSA_HEAD_DIM ** -0.5)
    o += NSA_Q_W
    for i in range(6):
        piece = proj[:, o:o + NSA_KV_W]
        if i % 2 == 0:
            piece = piece * cos + _swap_halves(piece) * sin
        kv_ref[:, i * NSA_KV_W:(i + 1) * NSA_KV_W] = piece
        o += NSA_KV_W
    z_ref[...] = proj[:, o:o + SSD_D_INNER]
    o += SSD_D_INNER
    xbc_ref[...] = proj[:, o:o + SSD_XBC]
    o += SSD_XBC
    misc_ref[...] = proj[:, o:o + INPROJ_MISC_W]


def _inproj(x, g, w_in, cos, sin, seq, tm=256):
    t, d = x.shape
    offs = np.cumsum(IN_SPLITS)[:-1].tolist()
    q, kc, vc, ks, vs, kw, vw, gl, z, xbc, dt = jnp.split(w_in, offs, -1)
    pad = jnp.zeros((d, INPROJ_MISC_W - gl.shape[1] - dt.shape[1]), w_in.dtype)
    w = jnp.concatenate([q, kc, vc, ks, vs, kw, vw, z, xbc, gl, dt, pad], -1).astype(BF16)
    assert w.shape[1] == INPROJ_WIDTH
    nseq = seq // tm
    row = lambda i: (i, 0)
    const = lambda i: (0, 0)
    widths = (NSA_Q_W, INPROJ_KV_W, SSD_D_INNER, SSD_XBC, INPROJ_MISC_W)
    return pl.pallas_call(
        _inproj_kernel,
        grid=(t // tm,),
        in_specs=[
            pl.BlockSpec((tm, d), row),
            pl.BlockSpec((1, d), const),
            pl.BlockSpec((d, INPROJ_WIDTH), const),
            pl.BlockSpec((tm, LANES), lambda i: (i % nseq, 0)),
            pl.BlockSpec((tm, LANES), lambda i: (i % nseq, 0)),
        ],
        out_specs=[pl.BlockSpec((tm, wd), row) for wd in widths],
        out_shape=[jax.ShapeDtypeStruct((t, wd), F32) for wd in widths],
        compiler_params=_params("parallel"),
        name="mixer0_in_proj",
    )(x, g.reshape(1, d), w, cos, sin)


def _rope_tables(seq):
    inv = ROPE_THETA ** (-jnp.arange(0, NSA_HEAD_DIM, 2, dtype=F32) / NSA_HEAD_DIM)
    ang = jnp.arange(seq, dtype=F32)[:, None] * inv[None, :]
    cos, sin = jnp.cos(ang), jnp.sin(ang)
    reps = LANES // NSA_HEAD_DIM
    cos_t = jnp.concatenate([cos, cos] * reps, -1)
    sin_t = jnp.concatenate([-sin, sin] * reps, -1)
    return cos_t, sin_t


def _compress_kernel(k_ref, pe_ref, w1_ref, w2_ref, o_ref):
    k16 = k_ref[0, 0]
    w1 = w1_ref[0]
    half = w1.shape[0] // 2
    first = jnp.dot(k16, w1[:half], precision=HIGHEST, preferred_element_type=F32)
    second = jnp.dot(k16, w1[half:], precision=HIGHEST, preferred_element_type=F32)
    bias = jnp.dot(pe_ref[0], w1, precision=HIGHEST, preferred_element_type=F32)[0:1]
    n = k16.shape[0]
    pre = first + pltpu.roll(second, n - 1, 0) + bias
    o_ref[0, 0] = jnp.dot(_silu(pre), w2_ref[0], precision=HIGHEST, preferred_element_type=F32)


def _compress(kv16, pe, w1, w2):
    two, bh, n, wd = kv16.shape
    d = w2.shape[-1]
    return pl.pallas_call(
        _compress_kernel,
        grid=(two, bh),
        in_specs=[
            pl.BlockSpec((1, 1, n, wd), lambda a, b: (a, b, 0, 0)),
            pl.BlockSpec((1, SUBLANES, pe.shape[-1]), lambda a, b: (a, 0, 0)),
            pl.BlockSpec((1,) + w1.shape[1:], lambda a, b: (a, 0, 0)),
            pl.BlockSpec((1, d, d), lambda a, b: (a, 0, 0)),
        ],
        out_specs=pl.BlockSpec((1, 1, n, d), lambda a, b: (a, b, 0, 0)),
        out_shape=jax.ShapeDtypeStruct((two, bh, n, d), F32),
        compiler_params=_params("parallel", "parallel"),
        name="nsa_compress",
    )(kv16, pe, w1, w2)


def _group_rows(q):
    return jnp.concatenate(
        [q[:, g * NSA_HEAD_DIM:(g + 1) * NSA_HEAD_DIM] for g in range(NSA_GROUP)], axis=0)


def _ungroup_rows(o, tq):
    return jnp.concatenate([o[g * tq:(g + 1) * tq] for g in range(NSA_GROUP)], axis=1)


def _dot_nt_hi(a, b):
    a1 = a.astype(BF16)
    a2 = (a - a1.astype(F32)).astype(BF16)
    b1 = b.astype(BF16)
    b2 = (b - b1.astype(F32)).astype(BF16)
    return _dot_nt(a1, b1) + _dot_nt(a1, b2) + _dot_nt(a2, b1)


def _nsa_select_kernel(q_ref, kc_ref, vc_ref, ovt_ref, oc_ref, biast_ref, *, tq, topk):
    s0 = pl.program_id(2) * tq
    qg = _group_rows(q_ref[0])
    s = _dot_nt_hi(qg, kc_ref[0])
    rows, ncmp = s.shape
    t = s0 + (lax.broadcasted_iota(jnp.int32, (rows, ncmp), 0) & (tq - 1))
    cmp_end = lax.broadcasted_iota(jnp.int32, (rows, ncmp), 1) * CMP_STRIDE + (CMP_BLOCK - 1)
    mask = cmp_end <= t
    s = jnp.where(mask, s, NEG_MASK)
    p = jnp.where(mask, jnp.exp(s - jnp.max(s, -1, keepdims=True)), 0.0)
    p = p / jnp.maximum(jnp.sum(p, -1, keepdims=True), 1e-30)
    o = jnp.dot(p.astype(BF16), vc_ref[0].astype(BF16), preferred_element_type=F32)
    oc_ref[0] = _ungroup_rows(o, tq)

    psum = p[0:tq]
    for g in range(1, NSA_GROUP):
        psum = psum + p[g * tq:(g + 1) * tq]
    ovt = ovt_ref[...]
    imp = None
    for piece in _split3(psum):
        d = _dot_nt(ovt, piece)
        imp = d if imp is None else imp + d
    blk = lax.broadcasted_iota(jnp.int32, imp.shape, 0)
    tt = s0 + lax.broadcasted_iota(jnp.int32, imp.shape, 1)
    cur = lax.shift_right_logical(tt, SEL_BLOCK.bit_length() - 1)
    forced = (blk == 0) | (blk == cur) | (blk == cur - 1)
    valid = blk * SEL_BLOCK <= tt
    x = jnp.where(valid, jnp.where(forced, FORCE_SCORE, imp), NEG_MASK)
    blk_f = blk.astype(F32)
    sel = jnp.zeros(imp.shape, jnp.bool_)
    for _ in range(topk):
        m = jnp.max(x, 0, keepdims=True)
        idx = jnp.min(jnp.where(x == m, blk_f, float(SEL_LANES)), 0, keepdims=True)
        hit = blk_f == idx
        sel = sel | hit
        x = jnp.where(hit, NEG_TAKEN, x)
    biast_ref[0, 0] = jnp.where(sel, 0.0, NEG_UNSELECTED).astype(BF16)


def _nsa_select(q, kc, vc, overlap, batch, seq, tq=128):
    ncmp = kc.shape[1]
    gw = NSA_GROUP * NSA_HEAD_DIM
    topk = min(SEL_TOPK, seq // SEL_BLOCK)
    kern = functools.partial(_nsa_select_kernel, tq=tq, topk=topk)
    return pl.pallas_call(
        kern,
        grid=(batch, NSA_KV_HEADS, seq // tq),
        in_specs=[
            pl.BlockSpec((1, tq, gw), lambda b, h, i: (b, i, h)),
            pl.BlockSpec((1, ncmp, NSA_HEAD_DIM), lambda b, h, i: (b * NSA_KV_HEADS + h, 0, 0)),
            pl.BlockSpec((1, ncmp, NSA_HEAD_DIM), lambda b, h, i: (b * NSA_KV_HEADS + h, 0, 0)),
            pl.BlockSpec((SEL_LANES, ncmp), lambda b, h, i: (0, 0)),
        ],
        out_specs=[
            pl.BlockSpec((1, tq, gw), lambda b, h, i: (b, i, h)),
            pl.BlockSpec((1, 1, SEL_LANES, tq), lambda b, h, i: (b, h, 0, i)),
        ],
        out_shape=[
            jax.ShapeDtypeStruct((batch, seq, NSA_KV_HEADS * gw), F32),
            jax.ShapeDtypeStruct((batch, NSA_KV_HEADS, SEL_LANES, seq), BF16),
        ],
        compiler_params=_params("parallel", "parallel", "parallel"),
        name="nsa_compressed_select",
    )(q, kc, vc, overlap)


NSA_LANE_SPLIT = 2


def _softmax_step_t(carry, s, vt, mask):
    m, l, acc = carry
    if mask is not None:
        s = jnp.where(mask, s, NEG_MASK)
    m_new = jnp.maximum(m, jnp.max(s, 0, keepdims=True))
    alpha = jnp.exp(m - m_new)
    p = jnp.exp(s - m_new)
    if mask is not None:
        p = jnp.where(mask, p, 0.0)
    l = alpha * l + jnp.sum(p, 0, keepdims=True)
    acc = alpha * acc + jnp.dot(vt, p.astype(BF16), preferred_element_type=F32)
    return m_new, l, acc


def _nsa_attend_kernel(q_ref, biast_ref, ka_ref, vst_ref, kw_ref, vwt_ref, oc_ref, gl_ref, o_ref,
                       sa_scr, sb_scr, p_scr, *, tq, tk):
    i = pl.program_id(2)
    s0 = i * tq
    n = NSA_GROUP * tq
    half = n // NSA_LANE_SPLIT
    d = NSA_HEAD_DIM
    qt = q_ref[0].T
    qgt = jnp.concatenate([qt[g * d:(g + 1) * d] for g in range(NSA_GROUP)], axis=1)
    qgt = qgt.astype(BF16)
    qat = jnp.concatenate([jnp.concatenate([biast_ref[0, 0]] * NSA_GROUP, axis=1), qgt], axis=0)
    init = tuple((jnp.full((1, half), NEG_MASK, F32), jnp.zeros((1, half), F32),
                  jnp.zeros((d, half), F32)) for _ in range(NSA_LANE_SPLIT))

    def query_pos(shape):
        return s0 + (lax.broadcasted_iota(jnp.int32, shape, 1) & (tq - 1))

    halves = range(NSA_LANE_SPLIT)

    def scores(kt):
        k = ka_ref[0, pl.ds(pl.multiple_of(kt * tk, tk), tk), :]
        return tuple(jnp.dot(k, qat[:, hf * half:(hf + 1) * half], preferred_element_type=F32)
                     for hf in halves)

    def values(kt):
        vt = vst_ref[0, :, pl.ds(pl.multiple_of(kt * tk, tk), tk)]
        return tuple(jnp.dot(vt, p_scr[:, hf * half:(hf + 1) * half],
                             preferred_element_type=F32) for hf in halves)

    def sel_step(kt, stats, src_scr, dst_scr, causal):
        pv = values(jnp.maximum(kt - 1, 0))
        s_next = None if causal else scores(kt + 1)
        new_stats = []
        for hf in halves:
            cols = slice(hf * half, (hf + 1) * half)
            m, l, acc = stats[hf]
            s = src_scr[:, cols]
            if causal:
                kp = kt * tk + lax.broadcasted_iota(jnp.int32, s.shape, 0)
                mask = kp <= query_pos(s.shape)
                s = jnp.where(mask, s, NEG_MASK)
            m_new = jnp.maximum(m, jnp.max(s, 0, keepdims=True))
            alpha = jnp.exp(m - m_new)
            p = jnp.exp(s - m_new)
            if causal:
                p = jnp.where(mask, p, 0.0)
            new_stats.append((m_new, alpha * l + jnp.sum(p, 0, keepdims=True),
                              alpha * (acc + pv[hf])))
            p_scr[:, cols] = p.astype(BF16)
        if not causal:
            for hf in halves:
                dst_scr[:, hf * half:(hf + 1) * half] = s_next[hf]
        return tuple(new_stats)

    def by_parity(kt, stats, causal):
        return lax.cond((kt & 1) == 0,
                        lambda st: sel_step(kt, st, sa_scr, sb_scr, causal),
                        lambda st: sel_step(kt, st, sb_scr, sa_scr, causal), stats)

    n_full = s0 // tk
    first_scores = scores(0)
    for hf in halves:
        sa_scr[:, hf * half:(hf + 1) * half] = first_scores[hf]
    p_scr[...] = jnp.zeros_like(p_scr)
    stats = lax.fori_loop(0, n_full, lambda kt, c: by_parity(kt, c, False), init)
    stats = by_parity(n_full, stats, True)
    pv_last = values(n_full)
    o_s = jnp.concatenate([(stats[hf][2] + pv_last[hf]) / jnp.maximum(stats[hf][1], 1e-30)
                           for hf in halves], axis=1)

    span = WINDOW + tq
    start = pl.multiple_of(jnp.maximum(s0 - WINDOW, 0), tq)
    kwin = kw_ref[0, pl.ds(start, span), :]
    vwt = vwt_ref[0, :, pl.ds(start, span)]
    outs = []
    for hf in range(NSA_LANE_SPLIT):
        s = jnp.dot(kwin, qgt[:, hf * half:(hf + 1) * half], preferred_element_type=F32)
        kp = start + lax.broadcasted_iota(jnp.int32, s.shape, 0)
        t = query_pos(s.shape)
        _, l, acc = _softmax_step_t(init[hf], s, vwt, (kp <= t) & (kp > t - WINDOW))
        outs.append(acc / jnp.maximum(l, 1e-30))
    o_w = jnp.concatenate(outs, axis=1)

    def rows_layout(ot):
        return jnp.concatenate([ot[:, g * tq:(g + 1) * tq] for g in range(NSA_GROUP)], axis=0).T

    o_s = rows_layout(o_s)
    o_w = rows_layout(o_w)
    gates = jax.nn.sigmoid(gl_ref[0, 0])
    o_c = oc_ref[0]
    pieces = []
    for g in range(NSA_GROUP):
        c = slice(g * d, (g + 1) * d)
        pieces.append(gates[:, 3 * g:3 * g + 1] * o_c[:, c]
                      + gates[:, 3 * g + 1:3 * g + 2] * o_s[:, c]
                      + gates[:, 3 * g + 2:3 * g + 3] * o_w[:, c])
    o_ref[0] = jnp.concatenate(pieces, axis=1)


def _nsa_attend(q, biast, kaug, vst, kw, vwt, o_c, gl, batch, seq, tq=128, tk=512):
    gw = NSA_GROUP * NSA_HEAD_DIM
    tk = min(tk, seq)
    assert tk % tq == 0 and WINDOW % tq == 0 and seq >= WINDOW + tq
    kern = functools.partial(_nsa_attend_kernel, tq=tq, tk=tk)
    bh = lambda b, h, i: (b * NSA_KV_HEADS + h, 0, 0)
    return pl.pallas_call(
        kern,
        grid=(batch, NSA_KV_HEADS, seq // tq),
        in_specs=[
            pl.BlockSpec((1, tq, gw), lambda b, h, i: (b, i, h)),
            pl.BlockSpec((1, 1, SEL_LANES, tq), lambda b, h, i: (b, h, 0, i)),
            pl.BlockSpec((1, seq, SEL_LANES + NSA_HEAD_DIM), bh),
            pl.BlockSpec((1, NSA_HEAD_DIM, seq), bh),
            pl.BlockSpec((1, seq, NSA_HEAD_DIM), bh),
            pl.BlockSpec((1, NSA_HEAD_DIM, seq), bh),
            pl.BlockSpec((1, tq, gw), lambda b, h, i: (b, i, h)),
            pl.BlockSpec((1, 1, tq, NSA_GROUP * 3), lambda b, h, i: (b, h, i, 0)),
        ],
        out_specs=pl.BlockSpec((1, tq, gw), lambda b, h, i: (b, i, h)),
        out_shape=jax.ShapeDtypeStruct((batch, seq, NSA_KV_HEADS * gw), F32),
        scratch_shapes=[pltpu.VMEM((tk, NSA_GROUP * tq), F32),
                        pltpu.VMEM((tk, NSA_GROUP * tq), F32),
                        pltpu.VMEM((tk, NSA_GROUP * tq), BF16)],
        compiler_params=_params("parallel", "parallel", "arbitrary"),
        name="nsa_selected_window",
    )(q, biast, kaug, vst, kw, vwt, o_c, gl)


def _nsa(q, kv, misc, pe_k, w1_k, w2_k, pe_v, w1_v, w2_v, batch, seq):
    d = NSA_HEAD_DIM
    heads = lambda a: jnp.moveaxis(a.reshape(batch, seq, NSA_KV_HEADS, d), 2, 1)
    piece = lambda i: heads(kv[:, i * NSA_KV_W:(i + 1) * NSA_KV_W])
    k_cmp, v_cmp, k_sel, v_sel, k_win, v_win = [piece(i) for i in range(6)]
    bh = batch * NSA_KV_HEADS
    n16 = seq // CMP_STRIDE
    kv16 = jnp.stack([k_cmp, v_cmp]).reshape(2, bh, n16, CMP_STRIDE * d)
    pe = jnp.stack([pe_k, pe_v]).reshape(2, 1, CMP_BLOCK * d)
    pe = jnp.broadcast_to(pe, (2, SUBLANES, CMP_BLOCK * d))
    cmp = _compress(kv16, pe, jnp.stack([w1_k, w1_v]), jnp.stack([w2_k, w2_v]))
    kc, vc = cmp[0], cmp[1]

    n_sel = seq // SEL_BLOCK
    cmp_start = np.arange(n16) * CMP_STRIDE
    sel_start = np.arange(SEL_LANES) * SEL_BLOCK
    overlap = ((cmp_start[:, None] < sel_start[None, :] + SEL_BLOCK)
               & (cmp_start[:, None] + CMP_BLOCK - 1 >= sel_start[None, :])
               & (np.arange(SEL_LANES)[None, :] < n_sel)
               & (np.arange(n16)[:, None] < (seq - CMP_BLOCK) // CMP_STRIDE + 1))
    overlap_t = jnp.asarray(overlap.T, BF16)

    q3 = q.reshape(batch, seq, NSA_Q_W)
    o_c, biast = _nsa_select(q3, kc, vc, overlap_t, batch, seq)

    onehot = (np.arange(seq)[:, None] // SEL_BLOCK == np.arange(SEL_LANES)[None, :])
    onehot = jnp.broadcast_to(jnp.asarray(onehot, BF16), (bh, seq, SEL_LANES))
    flat = lambda a: a.reshape(bh, seq, d).astype(BF16)
    flat_t = lambda a: jnp.swapaxes(a.reshape(bh, seq, d), 1, 2).astype(BF16)
    kaug = jnp.concatenate([onehot, flat(k_sel)], -1)
    gl = misc[:, :NSA_HEADS * 3].reshape(batch, seq, NSA_KV_HEADS, NSA_GROUP * 3)
    gl = jnp.moveaxis(gl, 2, 1)
    o = _nsa_attend(q3, biast, kaug, flat_t(v_sel), flat(k_win), flat_t(v_win), o_c, gl, batch,
                    seq)
    return o.reshape(batch * seq, NSA_Q_W)


def _ssd_kernel(xbc_ref, halo_ref, z_ref, dt_ref, cw_ref, cb_ref, dtb_ref, alog_ref, dskip_ref,
                nw_ref, tril_ref, o_ref, state_scr, y_scr):
    c = pl.program_id(1)
    l = SSD_CHUNK

    @pl.when(c == 0)
    def _():
        state_scr[...] = jnp.zeros_like(state_scr)

    x = xbc_ref[0]
    halo = jnp.where(c == 0, 0.0, halo_ref[0])
    xx = jnp.concatenate([halo, x], axis=0)
    cw = cw_ref[...]
    conv = cb_ref[...]
    for k in range(SSD_CONV):
        off = SUBLANES - (SSD_CONV - 1) + k
        conv = conv + cw[k:k + 1] * xx[off:off + l]
    xbc = _silu(conv)
    xs = xbc[:, :SSD_D_INNER]
    gn = SSD_GROUPS * SSD_STATE
    bmat = xbc[:, SSD_D_INNER:SSD_D_INNER + gn]
    cmat = xbc[:, SSD_D_INNER + gn:]

    dt = _softplus(dt_ref[0] + dtb_ref[...])
    da = dt * (-jnp.exp(alog_ref[...]))
    a_cs = _dot_x3_left(tril_ref[...], da)
    a_cs_t = a_cs.T
    a_last = a_cs[l - 1:l]
    causal = (lax.broadcasted_iota(jnp.int32, (l, l), 0)
              >= lax.broadcasted_iota(jnp.int32, (l, l), 1))
    dskip = dskip_ref[...]

    heads_per_group = SSD_HEADS // SSD_GROUPS
    for g in range(SSD_GROUPS):
        bg = bmat[:, g * SSD_STATE:(g + 1) * SSD_STATE]
        cg = cmat[:, g * SSD_STATE:(g + 1) * SSD_STATE]
        bg16 = bg.astype(BF16)
        cg16 = cg.astype(BF16)
        cb = _dot_nt(cg16, bg16)
        bgt16 = bg.T.astype(BF16)
        for r in range(heads_per_group):
            h = g * heads_per_group + r
            hs = slice(h * SSD_HEAD_DIM, (h + 1) * SSD_HEAD_DIM)
            col = a_cs[:, h:h + 1]
            rowv = a_cs_t[h:h + 1, :]
            seg = jnp.where(causal, jnp.exp(col - rowv), 0.0)
            xd = xs[:, hs] * dt[:, h:h + 1]
            y = jnp.dot((cb * seg).astype(BF16), xd.astype(BF16), preferred_element_type=F32)
            st = state_scr[h]
            y = y + jnp.exp(col) * jnp.dot(cg16, st.astype(BF16), preferred_element_type=F32)
            decay = jnp.exp(a_last[:, h:h + 1] - col)
            new = jnp.dot(bgt16, (decay * xd).astype(BF16), preferred_element_type=F32)
            state_scr[h] = st * jnp.exp(a_last[:, h:h + 1]) + new
            y_scr[:, hs] = y + xs[:, hs] * dskip[:, h:h + 1]

    y = y_scr[...] * _silu(z_ref[0])
    gw = SSD_D_INNER // SSD_GROUPS
    outs = []
    for g in range(SSD_GROUPS):
        yg = y[:, g * gw:(g + 1) * gw]
        outs.append(yg * lax.rsqrt(jnp.mean(yg * yg, -1, keepdims=True) + SSD_NORM_EPS))
    o_ref[0] = jnp.concatenate(outs, axis=1) * nw_ref[...]


def _pad_lanes(v, width=LANES):
    v = v.reshape(1, -1).astype(F32)
    return jnp.pad(v, ((0, 0), (0, width - v.shape[1])))


def _ssd(z, xbc, misc, conv_w, conv_b, dt_bias, a_log, d_skip, norm_w, batch, seq):
    l = SSD_CHUNK
    nc = seq // l
    z3 = z.reshape(batch, seq, SSD_D_INNER)
    x3 = xbc.reshape(batch, seq, SSD_XBC)
    dt = misc[:, NSA_HEADS * 3:NSA_HEADS * 3 + SSD_HEADS]
    dt3 = jnp.pad(dt, ((0, 0), (0, LANES - SSD_HEADS))).reshape(batch, seq, LANES)
    tril = jnp.asarray(np.tril(np.ones((l, l))), BF16)
    hb = l // SUBLANES
    const = lambda b, c: (0, 0)
    return pl.pallas_call(
        _ssd_kernel,
        grid=(batch, nc),
        in_specs=[
            pl.BlockSpec((1, l, SSD_XBC), lambda b, c: (b, c, 0)),
            pl.BlockSpec((1, SUBLANES, SSD_XBC), lambda b, c: (b, jnp.maximum(c * hb - 1, 0), 0)),
            pl.BlockSpec((1, l, SSD_D_INNER), lambda b, c: (b, c, 0)),
            pl.BlockSpec((1, l, LANES), lambda b, c: (b, c, 0)),
            pl.BlockSpec((SSD_CONV, SSD_XBC), const),
            pl.BlockSpec((1, SSD_XBC), const),
            pl.BlockSpec((1, LANES), const),
            pl.BlockSpec((1, LANES), const),
            pl.BlockSpec((1, LANES), const),
            pl.BlockSpec((1, SSD_D_INNER), const),
            pl.BlockSpec((l, l), const),
        ],
        out_specs=pl.BlockSpec((1, l, SSD_D_INNER), lambda b, c: (b, c, 0)),
        out_shape=jax.ShapeDtypeStruct((batch, seq, SSD_D_INNER), F32),
        scratch_shapes=[pltpu.VMEM((SSD_HEADS, SSD_STATE, SSD_HEAD_DIM), F32),
                        pltpu.VMEM((l, SSD_D_INNER), F32)],
        compiler_params=_params("parallel", "arbitrary"),
        name="ssd_chunk_scan",
    )(x3, x3, z3, dt3, conv_w.reshape(SSD_CONV, SSD_XBC), conv_b.reshape(1, SSD_XBC),
      _pad_lanes(dt_bias), _pad_lanes(a_log), _pad_lanes(d_skip), norm_w.reshape(1, SSD_D_INNER),
      tril).reshape(batch * seq, SSD_D_INNER)


def _outproj_kernel(*refs, n_parts):
    x_ref, g_ref = refs[0], refs[1]
    parts = refs[2:2 + n_parts]
    ws = refs[2 + n_parts:2 + 2 * n_parts]
    o_ref = refs[2 + 2 * n_parts]
    acc = None
    for p_ref, w_ref in zip(parts, ws):
        d = jnp.dot(p_ref[...].astype(BF16), w_ref[...], preferred_element_type=F32)
        acc = d if acc is None else acc + d
    o_ref[...] = x_ref[...] + _rms(acc, g_ref[...], NORM_EPS)


def _outproj(x, g, parts, weights, tm=512):
    t, d = x.shape
    row = lambda i: (i, 0)
    const = lambda i: (0, 0)
    n = len(parts)
    return pl.pallas_call(
        functools.partial(_outproj_kernel, n_parts=n),
        grid=(t // tm,),
        in_specs=([pl.BlockSpec((tm, d), row), pl.BlockSpec((1, d), const)]
                  + [pl.BlockSpec((tm, p.shape[1]), row) for p in parts]
                  + [pl.BlockSpec(w.shape, const) for w in weights]),
        out_specs=pl.BlockSpec((tm, d), row),
        out_shape=jax.ShapeDtypeStruct((t, d), F32),
        compiler_params=_params("parallel"),
        name="mixer_out_proj",
    )(x, g.reshape(1, d), *parts, *[w.astype(BF16) for w in weights])


def _head_sum(x, seg, seg_t):
    return _dot_x3(_dot_x3(x, seg), seg_t)


def _rwkv_pre_kernel(x_ref, halo_ref, g_ref, mu_ref, wr_ref, wk_ref, wv_ref, w0_ref, w1_ref,
                     w2_ref, a0_ref, a1_ref, a2_ref, g1_ref, g2_ref, kk_ref, ka_ref, seg_ref,
                     segt_ref, r_out, ld_out, k_out, v_out, kk_out, g_out, bt_out, kt_out, ldt_out,
                     *, tiles_per_seq):
    i = pl.program_id(0)
    h = _rms(x_ref[...], g_ref[...], NORM_EPS)
    prev_row = _rms(halo_ref[...], g_ref[...], NORM_EPS)[SUBLANES - 1:SUBLANES]
    prev_row = jnp.where(i % tiles_per_seq == 0, 0.0, prev_row)
    rowid = lax.broadcasted_iota(jnp.int32, h.shape, 0)
    prev = jnp.where(rowid == 0, prev_row, pltpu.roll(h, 1, 0))
    xx = prev - h
    mu = mu_ref[...]
    mix = lambda j: (h + xx * mu[j:j + 1]).astype(BF16)
    dot = lambda a, w_ref: jnp.dot(a, w_ref[...], preferred_element_type=F32)
    r = dot(mix(0), wr_ref)
    w = -_softplus(-(w0_ref[...] + dot(jnp.tanh(dot(mix(1), w1_ref)).astype(BF16), w2_ref))) - 0.5
    k = dot(mix(2), wk_ref)
    v = dot(mix(3), wv_ref)
    a = jax.nn.sigmoid(a0_ref[...] + dot(dot(mix(4), a1_ref).astype(BF16), a2_ref))
    g = dot(jax.nn.sigmoid(dot(mix(5), g1_ref)).astype(BF16), g2_ref)
    kk = k * kk_ref[...]
    norm = jnp.sqrt(_head_sum(kk * kk, seg_ref[...], segt_ref[...]))
    kk = kk / jnp.maximum(norm, 1e-12)
    k = k * (1.0 + (a - 1.0) * ka_ref[...])
    log_decay = -jnp.exp(w)
    r_out[...] = r
    ld_out[...] = log_decay
    k_out[...] = k
    v_out[...] = v
    kk_out[...] = kk
    g_out[...] = g
    bt_out[0] = (kk * a).T
    kt_out[0] = k.T
    ldt_out[0] = log_decay.T


def _pad_cols(w, width):
    return jnp.pad(w, ((0, 0), (0, width - w.shape[1])))


def _pad_rows(w, width):
    return jnp.pad(w, ((0, width - w.shape[0]), (0, 0)))


def _seg_matrices():
    seg = np.zeros((D_MODEL, LANES), np.float32)
    seg[np.arange(D_MODEL), np.arange(D_MODEL) // RWKV_HEAD_DIM] = 1.0
    return jnp.asarray(seg, BF16), jnp.asarray(seg.T, BF16)


def _rwkv_pre(x, g, mu, w_r, w_k, w_v, w0, w1, w2, a0, a1, a2, g1, g2, k_k, k_a, seq, tm=256):
    t, d = x.shape
    lora = lambda w: -(-w.shape[1] // LANES) * LANES
    w1p, w2p = _pad_cols(w1, lora(w1)), _pad_rows(w2, lora(w1))
    a1p, a2p = _pad_cols(a1, lora(a1)), _pad_rows(a2, lora(a1))
    g1p, g2p = _pad_cols(g1, lora(g1)), _pad_rows(g2, lora(g1))
    seg, seg_t = _seg_matrices()
    row = lambda i: (i, 0)
    const = lambda i: (0, 0)
    hb = tm // SUBLANES
    vec = lambda v: v.reshape(1, d)
    mats = [w.astype(BF16) for w in (w_r, w_k, w_v)]
    ins = [x, x, vec(g), mu, *mats, vec(w0), w1p.astype(BF16), w2p.astype(BF16), vec(a0),
           a1p.astype(BF16), a2p.astype(BF16), g1p.astype(BF16), g2p.astype(BF16), vec(k_k),
           vec(k_a), seg, seg_t]
    in_specs = [pl.BlockSpec((tm, d), row),
                pl.BlockSpec((SUBLANES, d), lambda i: (jnp.maximum(i * hb - 1, 0), 0))]
    in_specs += [pl.BlockSpec(a.shape, const) for a in ins[2:]]
    tps = seq // tm
    col = pl.BlockSpec((1, d, tm), lambda i: (i // tps, 0, i % tps))
    return pl.pallas_call(
        functools.partial(_rwkv_pre_kernel, tiles_per_seq=tps),
        grid=(t // tm,),
        in_specs=in_specs,
        out_specs=[pl.BlockSpec((tm, d), row)] * 6 + [col] * 3,
        out_shape=([jax.ShapeDtypeStruct((t, d), F32)] * 6
                   + [jax.ShapeDtypeStruct((t // seq, d, seq), F32)] * 3),
        compiler_params=_params("parallel"),
        name="rwkv7_projections",
    )(*ins)


RWKV_CHUNK = 128


def _rwkv_chunk_kernel(r_ref, ld_ref, kk_ref, v_ref, bt_ref, kt_ref, ldt_ref, tril_ref, triu_ref,
                       y_ref, state_scr):
    @pl.when(pl.program_id(1) == 0)
    def _():
        state_scr[...] = jnp.zeros_like(state_scr)

    l = RWKV_CHUNK
    hd = RWKV_HEAD_DIM
    tril = tril_ref[...]
    ld = ld_ref[0]
    c_in = _dot_x3_left(tril, ld)
    a_bar = -kk_ref[0] * jnp.exp(c_in - ld)
    r_bar = r_ref[0] * jnp.exp(c_in)
    v = v_ref[0]
    ldt = ldt_ref[0]
    c_t = _dot_x3(ldt, triu_ref[...])
    scale_t = jnp.exp(-c_t)
    b_t = bt_ref[0] * scale_t
    k_t = kt_ref[0] * scale_t
    decay_col = jnp.exp(c_t[:, l - 1:l])

    row = lax.broadcasted_iota(jnp.int32, (l, l), 0)
    colx = lax.broadcasted_iota(jnp.int32, (l, l), 1)
    strict = row > colx
    incl = row >= colx
    lane = lax.broadcasted_iota(jnp.int32, (l, LANES), 1)
    first_head = lane < hd
    blockdiag = (lax.broadcasted_iota(jnp.int32, (LANES, LANES), 0) < hd) == (
        lax.broadcasted_iota(jnp.int32, (LANES, LANES), 1) < hd)
    mm = lambda a, b: jnp.dot(a, b, preferred_element_type=F32)
    b16 = lambda a: a.astype(BF16)

    pairs = range(D_MODEL // LANES)
    heads = [(c, hh) for c in pairs for hh in range(2)]
    lanes = {c: slice(c * LANES, (c + 1) * LANES) for c in pairs}
    v16 = {c: b16(v[:, lanes[c]]) for c in pairs}
    bk_t = {c: b16(jnp.concatenate([b_t[lanes[c]], k_t[lanes[c]]], axis=1)) for c in pairs}
    h2 = {c: state_scr[c] for c in pairs}
    gh = {}
    for c in pairs:
        rhs = jnp.concatenate([bk_t[c], b16(h2[c])], axis=1)
        a_p, r_p = a_bar[:, lanes[c]], r_bar[:, lanes[c]]
        for hh in range(2):
            keep = first_head if hh == 0 else ~first_head
            x = jnp.concatenate([jnp.where(keep, a_p, 0.0), jnp.where(keep, r_p, 0.0)], axis=0)
            gh[c, hh] = mm(b16(x), rhs)
    mp, u, p_r = {}, {}, {}
    for c, hh in heads:
        g = gh[c, hh]
        mp[c, hh] = b16(jnp.where(strict, g[:l, :l], 0.0))
        m_ak = b16(jnp.where(strict, g[:l, l:2 * l], 0.0))
        p_r[c, hh] = b16(jnp.concatenate([jnp.where(incl, g[l:, :l], 0.0),
                                          jnp.where(incl, g[l:, l:2 * l], 0.0)], axis=1))
        u[c, hh] = g[:l, 2 * l:] + mm(m_ak, v16[c])
    n_factors = l.bit_length() - 1
    for f in range(n_factors):
        du = {h: mm(mp[h], b16(u[h])) for h in heads}
        if f + 1 < n_factors:
            mp = {h: b16(mm(mp[h], mp[h])) for h in heads}
        u = {h: u[h] + du[h] for h in heads}
    ys = {h: gh[h][l:, 2 * l:] + mm(p_r[h], jnp.concatenate([b16(u[h]), v16[h[0]]], axis=0))
          for h in heads}
    for c in pairs:
        u_pair = jnp.where(first_head, u[c, 0], u[c, 1])
        y_ref[0, :, lanes[c]] = jnp.where(first_head, ys[c, 0], ys[c, 1])
        upd = h2[c] + mm(bk_t[c], jnp.concatenate([b16(u_pair), v16[c]], axis=0))
        state_scr[c] = jnp.where(blockdiag, upd * decay_col[lanes[c]], 0.0)


def _rwkv_scan(r, ld, kk, v, bt, kt, ldt, batch, seq):
    l = RWKV_CHUNK
    d = D_MODEL
    rows = lambda x: x.reshape(batch, seq, d)
    rblk = pl.BlockSpec((1, l, d), lambda b, c: (b, c, 0))
    cblk = pl.BlockSpec((1, d, l), lambda b, c: (b, 0, c))
    tril = jnp.asarray(np.tril(np.ones((l, l))), BF16)
    y = pl.pallas_call(
        _rwkv_chunk_kernel,
        grid=(batch, seq // l),
        in_specs=[rblk] * 4 + [cblk] * 3 + [pl.BlockSpec((l, l), lambda b, c: (0, 0))] * 2,
        out_specs=rblk,
        out_shape=jax.ShapeDtypeStruct((batch, seq, d), F32),
        scratch_shapes=[pltpu.VMEM((d // LANES, LANES, LANES), F32)],
        compiler_params=_params("parallel", "arbitrary"),
        name="rwkv7_recurrence",
    )(rows(r), rows(ld), rows(kk), rows(v), bt, kt, ldt, tril, tril.T)
    return y.reshape(batch * seq, d)


def _rwkv_post_kernel(x_ref, y_ref, r_ref, k_ref, v_ref, g_ref, lng_ref, lnb_ref, rk_ref, wo_ref,
                      gn_ref, seg_ref, segt_ref, o_ref):
    seg, seg_t = seg_ref[...], segt_ref[...]
    y = y_ref[...]
    inv = 1.0 / RWKV_HEAD_DIM
    mean = _head_sum(y, seg, seg_t) * inv
    yc = y - mean
    var = _head_sum(yc * yc, seg, seg_t) * inv
    yn = yc * lax.rsqrt(var + RWKV_GN_EPS) * lng_ref[...] + lnb_ref[...]
    bonus = _head_sum(r_ref[...] * k_ref[...] * rk_ref[...], seg, seg_t) * v_ref[...]
    out = ((yn + bonus) * g_ref[...]).astype(BF16)
    proj = jnp.dot(out, wo_ref[...], preferred_element_type=F32)
    o_ref[...] = x_ref[...] + _rms(proj, gn_ref[...], NORM_EPS)


def _rwkv_post(x, y, r, k, v, g, ln_g, ln_b, r_k, w_o, gn, tm=256):
    t, d = x.shape
    seg, seg_t = _seg_matrices()
    row = lambda i: (i, 0)
    const = lambda i: (0, 0)
    vec = lambda a: a.reshape(1, d)
    small = [vec(ln_g), vec(ln_b), vec(r_k), w_o.astype(BF16), vec(gn), seg, seg_t]
    return pl.pallas_call(
        _rwkv_post_kernel,
        grid=(t // tm,),
        in_specs=[pl.BlockSpec((tm, d), row)] * 6 + [pl.BlockSpec(a.shape, const) for a in small],
        out_specs=pl.BlockSpec((tm, d), row),
        out_shape=jax.ShapeDtypeStruct((t, d), F32),
        compiler_params=_params("parallel"),
        name="rwkv7_output",
    )(x, y, r, k, v, g, *small)


def _nsa_ssd_mixer(x, g_pre, g_post, cos, sin, w_in, pe_k, w1_k, w2_k, pe_v, w1_v, w2_v, conv_w,
                   conv_b, dt_bias, a_log, d_skip, norm_w, w_out, batch, seq):
    q, kv, z, xbc, misc = _inproj(x, g_pre, w_in, cos, sin, seq)
    o_a = _nsa(q, kv, misc, pe_k, w1_k, w2_k, pe_v, w1_v, w2_v, batch, seq)
    o_b = _ssd(z, xbc, misc, conv_w, conv_b, dt_bias, a_log, d_skip, norm_w, batch, seq)
    return _outproj(x, g_post, [o_a, o_b], [w_out[:NSA_Q_W], w_out[NSA_Q_W:]])


def _rwkv7_mixer(x, g_pre, g_post, mu, w_r, w_k, w_v, w_o, w0, w1, w2, a0, a1, a2, g1, g2, k_k,
                 k_a, r_k, ln_g, ln_b, batch, seq):
    r, ld, k, v, kk, g, bt, kt, ldt = _rwkv_pre(x, g_pre, mu, w_r, w_k, w_v, w0, w1, w2, a0, a1,
                                                a2, g1, g2, k_k, k_a, seq)
    y = _rwkv_scan(r, ld, kk, v, bt, kt, ldt, batch, seq)
    return _rwkv_post(x, y, r, k, v, g, ln_g, ln_b, r_k, w_o, g_post)


def kernel(x, norm_gains, ffn1_w_gate, ffn1_w_up, ffn1_w_down, ffn2_w_gate, ffn2_w_up, ffn2_w_down, ab_w_in, a_cmp_pe_k, a_cmp_w1_k, a_cmp_w2_k, a_cmp_pe_v, a_cmp_w1_v, a_cmp_w2_v, b_conv_w, b_conv_b, b_dt_bias, b_a_log, b_d_skip, b_norm_w, ab_w_out, c_mu, c_w_r, c_w_k, c_w_v, c_w_o, c_w0, c_w1, c_w2, c_a0, c_a1, c_a2, c_g1, c_g2, c_k_k, c_k_a, c_r_k, c_ln_g, c_ln_b):
    batch, seq, d = x.shape
    depth = norm_gains.shape[0]
    cos, sin = _rope_tables(seq)
    x = x.reshape(batch * seq, d)
    for layer in range(depth):
        ng = norm_gains[layer]
        x = _ffn(x, ng[0], ng[1], ffn1_w_gate[layer], ffn1_w_up[layer], ffn1_w_down[layer])
        i = layer // 2
        if layer % 2 == 0:
            x = _nsa_ssd_mixer(x, ng[2], ng[3], cos, sin, ab_w_in[i], a_cmp_pe_k[i], a_cmp_w1_k[i],
                               a_cmp_w2_k[i], a_cmp_pe_v[i], a_cmp_w1_v[i], a_cmp_w2_v[i],
                               b_conv_w[i], b_conv_b[i], b_dt_bias[i], b_a_log[i], b_d_skip[i],
                               b_norm_w[i], ab_w_out[i], batch, seq)
        else:
            x = _rwkv7_mixer(x, ng[2], ng[3], c_mu[i], c_w_r[i], c_w_k[i], c_w_v[i], c_w_o[i],
                             c_w0[i], c_w1[i], c_w2[i], c_a0[i], c_a1[i], c_a2[i], c_g1[i],
                             c_g2[i], c_k_k[i], c_k_a[i], c_r_k[i], c_ln_g[i], c_ln_b[i],
                             batch, seq)
        x = _ffn(x, ng[4], ng[5], ffn2_w_gate[layer], ffn2_w_up[layer], ffn2_w_down[layer])
    return x.reshape(batch, seq, d)
```

```python
import functools

import jax
import jax.numpy as jnp
import numpy as np
from jax import lax
from jax.experimental import pallas as pl
from jax.experimental.pallas import tpu as pltpu

F32 = jnp.float32
BF16 = jnp.bfloat16
HIGHEST = lax.Precision.HIGHEST

D_MODEL = 1024
D_FF = 2816
NORM_EPS = 1e-6
NSA_HEADS = 8
NSA_KV_HEADS = 2
NSA_GROUP = NSA_HEADS // NSA_KV_HEADS
NSA_HEAD_DIM = 64
CMP_BLOCK = 32
CMP_STRIDE = 16
SEL_BLOCK = 64
SEL_TOPK = 16
WINDOW = 512
ROPE_THETA = 10000.0
FORCE_SCORE = 1e4
SEL_LANES = 128
SSD_HEADS = 16
SSD_HEAD_DIM = 64
SSD_D_INNER = SSD_HEADS * SSD_HEAD_DIM
SSD_GROUPS = 2
SSD_STATE = 128
SSD_CONV = 4
SSD_CHUNK = 128
SSD_NORM_EPS = 1e-5
SSD_XBC = SSD_D_INNER + 2 * SSD_GROUPS * SSD_STATE
RWKV_HEAD_DIM = 64
RWKV_HEADS = D_MODEL // RWKV_HEAD_DIM
RWKV_GN_EPS = 64e-5

NSA_Q_W = NSA_HEADS * NSA_HEAD_DIM
NSA_KV_W = NSA_KV_HEADS * NSA_HEAD_DIM
IN_SPLITS = (NSA_Q_W, NSA_KV_W, NSA_KV_W, NSA_KV_W, NSA_KV_W, NSA_KV_W, NSA_KV_W,
             NSA_HEADS * 3, SSD_D_INNER, SSD_XBC, SSD_HEADS)
IN_WIDTH = sum(IN_SPLITS)

LANES = 128
SUBLANES = 8
VMEM_LIMIT_BYTES = 56 * 1024 * 1024

NEG_MASK = -1e30
NEG_UNSELECTED = -2.0 ** 30
NEG_TAKEN = -3e38


def _params(*sem):
    return pltpu.CompilerParams(dimension_semantics=sem, vmem_limit_bytes=VMEM_LIMIT_BYTES)


def _rms(x, g, eps):
    return x * lax.rsqrt(jnp.mean(x * x, -1, keepdims=True) + eps) * g


def _silu(x):
    return x * jax.nn.sigmoid(x)


def _softplus(x):
    return jnp.maximum(x, 0.0) + jnp.log1p(jnp.exp(-jnp.abs(x)))


def _split3(a):
    a1 = a.astype(BF16)
    r1 = a - a1.astype(F32)
    a2 = r1.astype(BF16)
    a3 = (r1 - a2.astype(F32)).astype(BF16)
    return a1, a2, a3


def _dot_x3(a, b):
    acc = None
    for piece in _split3(a):
        d = jnp.dot(piece, b, preferred_element_type=F32)
        acc = d if acc is None else acc + d
    return acc


def _dot_x3_left(b, a):
    acc = None
    for piece in _split3(a):
        d = jnp.dot(b, piece, preferred_element_type=F32)
        acc = d if acc is None else acc + d
    return acc


def _dot_nt(a, b, **kw):
    return lax.dot_general(a, b, (((1,), (1,)), ((), ())), preferred_element_type=F32, **kw)


MXU_TILE = 256


def _ffn_kernel(x_ref, gi_ref, go_ref, wg_ref, wu_ref, wd_ref, o_ref, *, chunks):
    x = x_ref[...]
    h = _rms(x, gi_ref[...], NORM_EPS).astype(BF16)
    acc = None
    for lo, hi in chunks:
        gate = jnp.dot(h, wg_ref[:, lo:hi], preferred_element_type=F32)
        up = jnp.dot(h, wu_ref[:, lo:hi], preferred_element_type=F32)
        act = (_silu(gate) * up).astype(BF16)
        part = jnp.dot(act, wd_ref[lo:hi, :], preferred_element_type=F32)
        acc = part if acc is None else acc + part
    o_ref[...] = x + 0.5 * _rms(acc, go_ref[...], NORM_EPS)


def _ffn(x, g_in, g_out, w_gate, w_up, w_down, tm=512, n_chunks=2):
    t, d = x.shape
    f = w_gate.shape[1]
    tiles = f // MXU_TILE
    assert tiles * MXU_TILE == f
    cuts = [MXU_TILE * ((tiles * c + n_chunks - 1) // n_chunks) for c in range(n_chunks + 1)]
    chunks = tuple(zip(cuts[:-1], cuts[1:]))
    resident = lambda shape: pl.BlockSpec(shape, lambda i: (0, 0), pipeline_mode=pl.Buffered(1))
    return pl.pallas_call(
        functools.partial(_ffn_kernel, chunks=chunks),
        grid=(t // tm,),
        in_specs=[
            pl.BlockSpec((tm, d), lambda i: (i, 0)),
            pl.BlockSpec((1, d), lambda i: (0, 0)),
            pl.BlockSpec((1, d), lambda i: (0, 0)),
            resident((d, f)),
            resident((d, f)),
            resident((f, d)),
        ],
        out_specs=pl.BlockSpec((tm, d), lambda i: (i, 0)),
        out_shape=jax.ShapeDtypeStruct((t, d), F32),
        compiler_params=_params("parallel"),
        name="ffn_half_step",
    )(x, g_in.reshape(1, d), g_out.reshape(1, d), w_gate.astype(BF16), w_up.astype(BF16),
      w_down.astype(BF16))


INPROJ_MISC_W = 256
INPROJ_KV_W = 6 * NSA_KV_W
INPROJ_WIDTH = NSA_Q_W + INPROJ_KV_W + SSD_D_INNER + SSD_XBC + INPROJ_MISC_W


def _swap_halves(x):
    w = x.shape[-1]
    lane = lax.broadcasted_iota(jnp.int32, x.shape, x.ndim - 1)
    low = (lane & (NSA_HEAD_DIM - 1)) < (NSA_HEAD_DIM // 2)
    return jnp.where(low, pltpu.roll(x, w - NSA_HEAD_DIM // 2, x.ndim - 1),
                     pltpu.roll(x, NSA_HEAD_DIM // 2, x.ndim - 1))


def _inproj_kernel(x_ref, g_ref, w_ref, cos_ref, sin_ref, q_ref, kv_ref, z_ref, xbc_ref, misc_ref):
    h = _rms(x_ref[...], g_ref[...], NORM_EPS).astype(BF16)
    proj = jnp.dot(h, w_ref[...], preferred_element_type=F32)
    cos = cos_ref[...]
    sin = sin_ref[...]
    o = 0
    q = proj[:, o:o + NSA_Q_W]
    cos_q = jnp.concatenate([cos] * (NSA_Q_W // LANES), axis=1)
    sin_q = jnp.concatenate([sin] * (NSA_Q_W // LANES), axis=1)
    q_ref[...] = (q * cos_q + _swap_halves(q) * sin_q) * (NSA_HEAD_DIM ** -0.5)
    o += NSA_Q_W
    for i in range(6):
        piece = proj[:, o:o + NSA_KV_W]
        if i % 2 == 0:
            piece = piece * cos + _swap_halves(piece) * sin
        kv_ref[:, i * NSA_KV_W:(i + 1) * NSA_KV_W] = piece
        o += NSA_KV_W
    z_ref[...] = proj[:, o:o + SSD_D_INNER]
    o += SSD_D_INNER
    xbc_ref[...] = proj[:, o:o + SSD_XBC]
    o += SSD_XBC
    misc_ref[...] = proj[:, o:o + INPROJ_MISC_W]


def _inproj(x, g, w_in, cos, sin, seq, tm=256):
    t, d = x.shape
    offs = np.cumsum(IN_SPLITS)[:-1].tolist()
    q, kc, vc, ks, vs, kw, vw, gl, z, xbc, dt = jnp.split(w_in, offs, -1)
    pad = jnp.zeros((d, INPROJ_MISC_W - gl.shape[1] - dt.shape[1]), w_in.dtype)
    w = jnp.concatenate([q, kc, vc, ks, vs, kw, vw, z, xbc, gl, dt, pad], -1).astype(BF16)
    assert w.shape[1] == INPROJ_WIDTH
    nseq = seq // tm
    row = lambda i: (i, 0)
    const = lambda i: (0, 0)
    widths = (NSA_Q_W, INPROJ_KV_W, SSD_D_INNER, SSD_XBC, INPROJ_MISC_W)
    return pl.pallas_call(
        _inproj_kernel,
        grid=(t // tm,),
        in_specs=[
            pl.BlockSpec((tm, d), row),
            pl.BlockSpec((1, d), const),
            pl.BlockSpec((d, INPROJ_WIDTH), const),
            pl.BlockSpec((tm, LANES), lambda i: (i % nseq, 0)),
            pl.BlockSpec((tm, LANES), lambda i: (i % nseq, 0)),
        ],
        out_specs=[pl.BlockSpec((tm, wd), row) for wd in widths],
        out_shape=[jax.ShapeDtypeStruct((t, wd), F32) for wd in widths],
        compiler_params=_params("parallel"),
        name="mixer0_in_proj",
    )(x, g.reshape(1, d), w, cos, sin)


def _rope_tables(seq):
    inv = ROPE_THETA ** (-jnp.arange(0, NSA_HEAD_DIM, 2, dtype=F32) / NSA_HEAD_DIM)
    ang = jnp.arange(seq, dtype=F32)[:, None] * inv[None, :]
    cos, sin = jnp.cos(ang), jnp.sin(ang)
    reps = LANES // NSA_HEAD_DIM
    cos_t = jnp.concatenate([cos, cos] * reps, -1)
    sin_t = jnp.concatenate([-sin, sin] * reps, -1)
    return cos_t, sin_t


def _compress_kernel(k_ref, pe_ref, w1_ref, w2_ref, o_ref):
    k16 = k_ref[0, 0]
    w1 = w1_ref[0]
    half = w1.shape[0] // 2
    first = jnp.dot(k16, w1[:half], precision=HIGHEST, preferred_element_type=F32)
    second = jnp.dot(k16, w1[half:], precision=HIGHEST, preferred_element_type=F32)
    bias = jnp.dot(pe_ref[0], w1, precision=HIGHEST, preferred_element_type=F32)[0:1]
    n = k16.shape[0]
    pre = first + pltpu.roll(second, n - 1, 0) + bias
    o_ref[0, 0] = jnp.dot(_silu(pre), w2_ref[0], precision=HIGHEST, preferred_element_type=F32)


def _compress(kv16, pe, w1, w2):
    two, bh, n, wd = kv16.shape
    d = w2.shape[-1]
    return pl.pallas_call(
        _compress_kernel,
        grid=(two, bh),
        in_specs=[
            pl.BlockSpec((1, 1, n, wd), lambda a, b: (a, b, 0, 0)),
            pl.BlockSpec((1, SUBLANES, pe.shape[-1]), lambda a, b: (a, 0, 0)),
            pl.BlockSpec((1,) + w1.shape[1:], lambda a, b: (a, 0, 0)),
            pl.BlockSpec((1, d, d), lambda a, b: (a, 0, 0)),
        ],
        out_specs=pl.BlockSpec((1, 1, n, d), lambda a, b: (a, b, 0, 0)),
        out_shape=jax.ShapeDtypeStruct((two, bh, n, d), F32),
        compiler_params=_params("parallel", "parallel"),
        name="nsa_compress",
    )(kv16, pe, w1, w2)


def _group_rows(q):
    return jnp.concatenate(
        [q[:, g * NSA_HEAD_DIM:(g + 1) * NSA_HEAD_DIM] for g in range(NSA_GROUP)], axis=0)


def _ungroup_rows(o, tq):
    return jnp.concatenate([o[g * tq:(g + 1) * tq] for g in range(NSA_GROUP)], axis=1)


def _dot_nt_hi(a, b):
    a1 = a.astype(BF16)
    a2 = (a - a1.astype(F32)).astype(BF16)
    b1 = b.astype(BF16)
    b2 = (b - b1.astype(F32)).astype(BF16)
    return _dot_nt(a1, b1) + _dot_nt(a1, b2) + _dot_nt(a2, b1)


def _nsa_select_kernel(q_ref, kc_ref, vc_ref, ovt_ref, oc_ref, biast_ref, *, tq, topk):
    s0 = pl.program_id(2) * tq
    qg = _group_rows(q_ref[0])
    s = _dot_nt_hi(qg, kc_ref[0])
    rows, ncmp = s.shape
    t = s0 + (lax.broadcasted_iota(jnp.int32, (rows, ncmp), 0) & (tq - 1))
    cmp_end = lax.broadcasted_iota(jnp.int32, (rows, ncmp), 1) * CMP_STRIDE + (CMP_BLOCK - 1)
    mask = cmp_end <= t
    s = jnp.where(mask, s, NEG_MASK)
    p = jnp.where(mask, jnp.exp(s - jnp.max(s, -1, keepdims=True)), 0.0)
    p = p / jnp.maximum(jnp.sum(p, -1, keepdims=True), 1e-30)
    o = jnp.dot(p.astype(BF16), vc_ref[0].astype(BF16), preferred_element_type=F32)
    oc_ref[0] = _ungroup_rows(o, tq)

    psum = p[0:tq]
    for g in range(1, NSA_GROUP):
        psum = psum + p[g * tq:(g + 1) * tq]
    ovt = ovt_ref[...]
    imp = None
    for piece in _split3(psum):
        d = _dot_nt(ovt, piece)
        imp = d if imp is None else imp + d
    blk = lax.broadcasted_iota(jnp.int32, imp.shape, 0)
    tt = s0 + lax.broadcasted_iota(jnp.int32, imp.shape, 1)
    cur = lax.shift_right_logical(tt, SEL_BLOCK.bit_length() - 1)
    forced = (blk == 0) | (blk == cur) | (blk == cur - 1)
    valid = blk * SEL_BLOCK <= tt
    x = jnp.where(valid, jnp.where(forced, FORCE_SCORE, imp), NEG_MASK)
    blk_f = blk.astype(F32)
    sel = jnp.zeros(imp.shape, jnp.bool_)
    for _ in range(topk):
        m = jnp.max(x, 0, keepdims=True)
        idx = jnp.min(jnp.where(x == m, blk_f, float(SEL_LANES)), 0, keepdims=True)
        hit = blk_f == idx
        sel = sel | hit
        x = jnp.where(hit, NEG_TAKEN, x)
    biast_ref[0, 0] = jnp.where(sel, 0.0, NEG_UNSELECTED).astype(BF16)


def _nsa_select(q, kc, vc, overlap, batch, seq, tq=128):
    ncmp = kc.shape[1]
    gw = NSA_GROUP * NSA_HEAD_DIM
    topk = min(SEL_TOPK, seq // SEL_BLOCK)
    kern = functools.partial(_nsa_select_kernel, tq=tq, topk=topk)
    return pl.pallas_call(
        kern,
        grid=(batch, NSA_KV_HEADS, seq // tq),
        in_specs=[
            pl.BlockSpec((1, tq, gw), lambda b, h, i: (b, i, h)),
            pl.BlockSpec((1, ncmp, NSA_HEAD_DIM), lambda b, h, i: (b * NSA_KV_HEADS + h, 0, 0)),
            pl.BlockSpec((1, ncmp, NSA_HEAD_DIM), lambda b, h, i: (b * NSA_KV_HEADS + h, 0, 0)),
            pl.BlockSpec((SEL_LANES, ncmp), lambda b, h, i: (0, 0)),
        ],
        out_specs=[
            pl.BlockSpec((1, tq, gw), lambda b, h, i: (b, i, h)),
            pl.BlockSpec((1, 1, SEL_LANES, tq), lambda b, h, i: (b, h, 0, i)),
        ],
        out_shape=[
            jax.ShapeDtypeStruct((batch, seq, NSA_KV_HEADS * gw), F32),
            jax.ShapeDtypeStruct((batch, NSA_KV_HEADS, SEL_LANES, seq), BF16),
        ],
        compiler_params=_params("parallel", "parallel", "parallel"),
        name="nsa_compressed_select",
    )(q, kc, vc, overlap)


NSA_LANE_SPLIT = 2


def _softmax_step_t(carry, s, vt, mask):
    m, l, acc = carry
    if mask is not None:
        s = jnp.where(mask, s, NEG_MASK)
    m_new = jnp.maximum(m, jnp.max(s, 0, keepdims=True))
    alpha = jnp.exp(m - m_new)
    p = jnp.exp(s - m_new)
    if mask is not None:
        p = jnp.where(mask, p, 0.0)
    l = alpha * l + jnp.sum(p, 0, keepdims=True)
    acc = alpha * acc + jnp.dot(vt, p.astype(BF16), preferred_element_type=F32)
    return m_new, l, acc


def _nsa_attend_kernel(q_ref, biast_ref, ka_ref, vst_ref, kw_ref, vwt_ref, oc_ref, gl_ref, o_ref,
                       sa_scr, sb_scr, p_scr, w_scr, *, tq, tk):
    i = pl.program_id(2)
    s0 = i * tq
    n = NSA_GROUP * tq
    half = n // NSA_LANE_SPLIT
    d = NSA_HEAD_DIM
    qt = q_ref[0].T
    qgt = jnp.concatenate([qt[g * d:(g + 1) * d] for g in range(NSA_GROUP)], axis=1)
    qgt = qgt.astype(BF16)
    qat = jnp.concatenate([jnp.concatenate([biast_ref[0, 0]] * NSA_GROUP, axis=1), qgt], axis=0)
    init = tuple((jnp.full((1, half), NEG_MASK, F32), jnp.zeros((1, half), F32),
                  jnp.zeros((d, half), F32)) for _ in range(NSA_LANE_SPLIT))

    def query_pos(shape):
        return s0 + (lax.broadcasted_iota(jnp.int32, shape, 1) & (tq - 1))

    halves = range(NSA_LANE_SPLIT)

    def scores(kt):
        k = ka_ref[0, pl.ds(pl.multiple_of(kt * tk, tk), tk), :]
        return tuple(jnp.dot(k, qat[:, hf * half:(hf + 1) * half], preferred_element_type=F32)
                     for hf in halves)

    def values(kt):
        vt = vst_ref[0, :, pl.ds(pl.multiple_of(kt * tk, tk), tk)]
        return tuple(jnp.dot(vt, p_scr[:, hf * half:(hf + 1) * half],
                             preferred_element_type=F32) for hf in halves)

    def sel_step(kt, stats, src_scr, dst_scr, causal):
        s_next = None if causal else scores(kt + 1)
        pv = values(jnp.maximum(kt - 1, 0))
        new_stats = []
        for hf in halves:
            cols = slice(hf * half, (hf + 1) * half)
            m, l, acc = stats[hf]
            s = src_scr[:, cols]
            if causal:
                kp = kt * tk + lax.broadcasted_iota(jnp.int32, s.shape, 0)
                mask = kp <= query_pos(s.shape)
                s = jnp.where(mask, s, NEG_MASK)
            m_new = jnp.maximum(m, jnp.max(s, 0, keepdims=True))
            alpha = jnp.exp(m - m_new)
            p = jnp.exp(s - m_new)
            if causal:
                p = jnp.where(mask, p, 0.0)
            new_stats.append((m_new, alpha * l + jnp.sum(p, 0, keepdims=True),
                              alpha * (acc + pv[hf])))
            p_scr[:, cols] = p.astype(BF16)
        if not causal:
            for hf in halves:
                dst_scr[:, hf * half:(hf + 1) * half] = s_next[hf]
        return tuple(new_stats)

    def by_parity(kt, stats, causal):
        return lax.cond((kt & 1) == 0,
                        lambda st: sel_step(kt, st, sa_scr, sb_scr, causal),
                        lambda st: sel_step(kt, st, sb_scr, sa_scr, causal), stats)

    n_full = s0 // tk
    first_scores = scores(0)
    span = WINDOW + tq
    start = pl.multiple_of(jnp.maximum(s0 - WINDOW, 0), tq)
    kwin = kw_ref[0, pl.ds(start, span), :]
    for hf in halves:
        cols = slice(hf * half, (hf + 1) * half)
        sa_scr[:, cols] = first_scores[hf]
        w_scr[:, cols] = jnp.dot(kwin, qgt[:, cols], preferred_element_type=F32)
    p_scr[...] = jnp.zeros_like(p_scr)
    stats = lax.fori_loop(0, n_full, lambda kt, c: by_parity(kt, c, False), init)
    stats = by_parity(n_full, stats, True)
    pv_last = values(n_full)

    vwt = vwt_ref[0, :, pl.ds(start, span)]
    outs = []
    for hf in halves:
        s = w_scr[:, hf * half:(hf + 1) * half]
        kp = start + lax.broadcasted_iota(jnp.int32, s.shape, 0)
        t = query_pos(s.shape)
        _, l, acc = _softmax_step_t(init[hf], s, vwt, (kp <= t) & (kp > t - WINDOW))
        outs.append(acc / jnp.maximum(l, 1e-30))
    o_w = jnp.concatenate(outs, axis=1)
    o_s = jnp.concatenate([(stats[hf][2] + pv_last[hf]) / jnp.maximum(stats[hf][1], 1e-30)
                           for hf in halves], axis=1)

    def rows_layout(ot):
        return jnp.concatenate([ot[:, g * tq:(g + 1) * tq] for g in range(NSA_GROUP)], axis=0).T

    o_s = rows_layout(o_s)
    o_w = rows_layout(o_w)
    gates = jax.nn.sigmoid(gl_ref[0, 0])
    o_c = oc_ref[0]
    pieces = []
    for g in range(NSA_GROUP):
        c = slice(g * d, (g + 1) * d)
        pieces.append(gates[:, 3 * g:3 * g + 1] * o_c[:, c]
                      + gates[:, 3 * g + 1:3 * g + 2] * o_s[:, c]
                      + gates[:, 3 * g + 2:3 * g + 3] * o_w[:, c])
    o_ref[0] = jnp.concatenate(pieces, axis=1)


def _nsa_attend(q, biast, kaug, vst, kw, vwt, o_c, gl, batch, seq, tq=128, tk=512):
    gw = NSA_GROUP * NSA_HEAD_DIM
    tk = min(tk, seq)
    assert tk % tq == 0 and WINDOW % tq == 0 and seq >= WINDOW + tq
    kern = functools.partial(_nsa_attend_kernel, tq=tq, tk=tk)
    bh = lambda b, h, i: (b * NSA_KV_HEADS + h, 0, 0)
    return pl.pallas_call(
        kern,
        grid=(batch, NSA_KV_HEADS, seq // tq),
        in_specs=[
            pl.BlockSpec((1, tq, gw), lambda b, h, i: (b, i, h)),
            pl.BlockSpec((1, 1, SEL_LANES, tq), lambda b, h, i: (b, h, 0, i)),
            pl.BlockSpec((1, seq, SEL_LANES + NSA_HEAD_DIM), bh),
            pl.BlockSpec((1, NSA_HEAD_DIM, seq), bh),
            pl.BlockSpec((1, seq, NSA_HEAD_DIM), bh),
            pl.BlockSpec((1, NSA_HEAD_DIM, seq), bh),
            pl.BlockSpec((1, tq, gw), lambda b, h, i: (b, i, h)),
            pl.BlockSpec((1, 1, tq, NSA_GROUP * 3), lambda b, h, i: (b, h, i, 0)),
        ],
        out_specs=pl.BlockSpec((1, tq, gw), lambda b, h, i: (b, i, h)),
        out_shape=jax.ShapeDtypeStruct((batch, seq, NSA_KV_HEADS * gw), F32),
        scratch_shapes=[pltpu.VMEM((tk, NSA_GROUP * tq), F32),
                        pltpu.VMEM((tk, NSA_GROUP * tq), F32),
                        pltpu.VMEM((tk, NSA_GROUP * tq), BF16),
                        pltpu.VMEM((WINDOW + tq, NSA_GROUP * tq), F32)],
        compiler_params=_params("parallel", "parallel", "arbitrary"),
        name="nsa_selected_window",
    )(q, biast, kaug, vst, kw, vwt, o_c, gl)


def _nsa(q, kv, misc, pe_k, w1_k, w2_k, pe_v, w1_v, w2_v, batch, seq):
    d = NSA_HEAD_DIM
    heads = lambda a: jnp.moveaxis(a.reshape(batch, seq, NSA_KV_HEADS, d), 2, 1)
    piece = lambda i: heads(kv[:, i * NSA_KV_W:(i + 1) * NSA_KV_W])
    k_cmp, v_cmp, k_sel, v_sel, k_win, v_win = [piece(i) for i in range(6)]
    bh = batch * NSA_KV_HEADS
    n16 = seq // CMP_STRIDE
    kv16 = jnp.stack([k_cmp, v_cmp]).reshape(2, bh, n16, CMP_STRIDE * d)
    pe = jnp.stack([pe_k, pe_v]).reshape(2, 1, CMP_BLOCK * d)
    pe = jnp.broadcast_to(pe, (2, SUBLANES, CMP_BLOCK * d))
    cmp = _compress(kv16, pe, jnp.stack([w1_k, w1_v]), jnp.stack([w2_k, w2_v]))
    kc, vc = cmp[0], cmp[1]

    n_sel = seq // SEL_BLOCK
    cmp_start = np.arange(n16) * CMP_STRIDE
    sel_start = np.arange(SEL_LANES) * SEL_BLOCK
    overlap = ((cmp_start[:, None] < sel_start[None, :] + SEL_BLOCK)
               & (cmp_start[:, None] + CMP_BLOCK - 1 >= sel_start[None, :])
               & (np.arange(SEL_LANES)[None, :] < n_sel)
               & (np.arange(n16)[:, None] < (seq - CMP_BLOCK) // CMP_STRIDE + 1))
    overlap_t = jnp.asarray(overlap.T, BF16)

    q3 = q.reshape(batch, seq, NSA_Q_W)
    o_c, biast = _nsa_select(q3, kc, vc, overlap_t, batch, seq)

    onehot = (np.arange(seq)[:, None] // SEL_BLOCK == np.arange(SEL_LANES)[None, :])
    onehot = jnp.broadcast_to(jnp.asarray(onehot, BF16), (bh, seq, SEL_LANES))
    flat = lambda a: a.reshape(bh, seq, d).astype(BF16)
    flat_t = lambda a: jnp.swapaxes(a.reshape(bh, seq, d), 1, 2).astype(BF16)
    kaug = jnp.concatenate([onehot, flat(k_sel)], -1)
    gl = misc[:, :NSA_HEADS * 3].reshape(batch, seq, NSA_KV_HEADS, NSA_GROUP * 3)
    gl = jnp.moveaxis(gl, 2, 1)
    o = _nsa_attend(q3, biast, kaug, flat_t(v_sel), flat(k_win), flat_t(v_win), o_c, gl, batch,
                    seq)
    return o.reshape(batch * seq, NSA_Q_W)


def _ssd_kernel(xbc_ref, halo_ref, z_ref, dt_ref, cw_ref, cb_ref, dtb_ref, alog_ref, dskip_ref,
                nw_ref, tril_ref, spread_ref, o_ref, state_scr, y_scr):
    c = pl.program_id(1)
    l = SSD_CHUNK

    @pl.when(c == 0)
    def _():
        state_scr[...] = jnp.zeros_like(state_scr)

    x = xbc_ref[0]
    halo = jnp.where(c == 0, 0.0, halo_ref[0])
    xx = jnp.concatenate([halo, x], axis=0)
    cw = cw_ref[...]
    conv = cb_ref[...]
    for k in range(SSD_CONV):
        off = SUBLANES - (SSD_CONV - 1) + k
        conv = conv + cw[k:k + 1] * xx[off:off + l]
    xbc = _silu(conv)
    xs = xbc[:, :SSD_D_INNER]
    gn = SSD_GROUPS * SSD_STATE
    bmat = xbc[:, SSD_D_INNER:SSD_D_INNER + gn]
    cmat = xbc[:, SSD_D_INNER + gn:]

    dt = _softplus(dt_ref[0] + dtb_ref[...])
    da = dt * (-jnp.exp(alog_ref[...]))
    a_cs = _dot_x3_left(tril_ref[...], da)
    a_cs_t = a_cs.T
    a_last = a_cs[l - 1:l]
    causal = (lax.broadcasted_iota(jnp.int32, (l, l), 0)
              >= lax.broadcasted_iota(jnp.int32, (l, l), 1))

    spread = spread_ref[...]
    dt_x = _dot_x3(dt, spread)
    grow_x = _dot_x3(jnp.exp(a_cs), spread)
    fade_x = _dot_x3(jnp.exp(a_last - a_cs), spread)
    chunk_x = _dot_x3(jnp.broadcast_to(jnp.exp(a_last), (SUBLANES, LANES)), spread)[0:1]
    xd = xs * dt_x
    xd16 = xd.astype(BF16)
    fxd16 = (xd * fade_x).astype(BF16)

    pairs = range(SSD_HEADS // 2)
    pairs_per_group = len(pairs) // SSD_GROUPS
    lanes = {c: slice(c * LANES, (c + 1) * LANES) for c in pairs}
    cb, y_off = {}, {}
    for g in range(SSD_GROUPS):
        bg = bmat[:, g * SSD_STATE:(g + 1) * SSD_STATE]
        cg16 = cmat[:, g * SSD_STATE:(g + 1) * SSD_STATE].astype(BF16)
        cb[g] = _dot_nt(cg16, bg.astype(BF16))
        bgt16 = bg.T.astype(BF16)
        for c in range(g * pairs_per_group, (g + 1) * pairs_per_group):
            st = state_scr[c]
            y_off[c] = jnp.dot(cg16, st.astype(BF16), preferred_element_type=F32)
            new = jnp.dot(bgt16, fxd16[:, lanes[c]], preferred_element_type=F32)
            state_scr[c] = st * chunk_x[:, lanes[c]] + new
    first_head = lax.broadcasted_iota(jnp.int32, (l, LANES), 1) < SSD_HEAD_DIM
    y_diag = {}
    for c in pairs:
        for hh in range(2):
            h = 2 * c + hh
            seg = jnp.where(causal, jnp.exp(a_cs[:, h:h + 1] - a_cs_t[h:h + 1, :]), 0.0)
            y_diag[h] = jnp.dot((cb[c // pairs_per_group] * seg).astype(BF16), xd16[:, lanes[c]],
                                preferred_element_type=F32)
    for c in pairs:
        y_scr[:, lanes[c]] = (jnp.where(first_head, y_diag[2 * c], y_diag[2 * c + 1])
                              + y_off[c] * grow_x[:, lanes[c]])

    y = (y_scr[...] + xs * dskip_ref[...]) * _silu(z_ref[0])
    gw = SSD_D_INNER // SSD_GROUPS
    outs = []
    for g in range(SSD_GROUPS):
        yg = y[:, g * gw:(g + 1) * gw]
        outs.append(yg * lax.rsqrt(jnp.mean(yg * yg, -1, keepdims=True) + SSD_NORM_EPS))
    o_ref[0] = jnp.concatenate(outs, axis=1) * nw_ref[...]


def _pad_lanes(v, width=LANES):
    v = v.reshape(1, -1).astype(F32)
    return jnp.pad(v, ((0, 0), (0, width - v.shape[1])))


def _ssd(z, xbc, misc, conv_w, conv_b, dt_bias, a_log, d_skip, norm_w, batch, seq):
    l = SSD_CHUNK
    nc = seq // l
    z3 = z.reshape(batch, seq, SSD_D_INNER)
    x3 = xbc.reshape(batch, seq, SSD_XBC)
    dt = misc[:, NSA_HEADS * 3:NSA_HEADS * 3 + SSD_HEADS]
    dt3 = jnp.pad(dt, ((0, 0), (0, LANES - SSD_HEADS))).reshape(batch, seq, LANES)
    tril = jnp.asarray(np.tril(np.ones((l, l))), BF16)
    spread = np.zeros((LANES, SSD_D_INNER), np.float32)
    spread[np.arange(SSD_D_INNER) // SSD_HEAD_DIM, np.arange(SSD_D_INNER)] = 1.0
    spread = jnp.asarray(spread, BF16)
    hb = l // SUBLANES
    const = lambda b, c: (0, 0)
    return pl.pallas_call(
        _ssd_kernel,
        grid=(batch, nc),
        in_specs=[
            pl.BlockSpec((1, l, SSD_XBC), lambda b, c: (b, c, 0)),
            pl.BlockSpec((1, SUBLANES, SSD_XBC), lambda b, c: (b, jnp.maximum(c * hb - 1, 0), 0)),
            pl.BlockSpec((1, l, SSD_D_INNER), lambda b, c: (b, c, 0)),
            pl.BlockSpec((1, l, LANES), lambda b, c: (b, c, 0)),
            pl.BlockSpec((SSD_CONV, SSD_XBC), const),
            pl.BlockSpec((1, SSD_XBC), const),
            pl.BlockSpec((1, LANES), const),
            pl.BlockSpec((1, LANES), const),
            pl.BlockSpec((1, SSD_D_INNER), const),
            pl.BlockSpec((1, SSD_D_INNER), const),
            pl.BlockSpec((l, l), const),
            pl.BlockSpec((LANES, SSD_D_INNER), const),
        ],
        out_specs=pl.BlockSpec((1, l, SSD_D_INNER), lambda b, c: (b, c, 0)),
        out_shape=jax.ShapeDtypeStruct((batch, seq, SSD_D_INNER), F32),
        scratch_shapes=[pltpu.VMEM((SSD_HEADS // 2, SSD_STATE, 2 * SSD_HEAD_DIM), F32),
                        pltpu.VMEM((l, SSD_D_INNER), F32)],
        compiler_params=_params("parallel", "arbitrary"),
        name="ssd_chunk_scan",
    )(x3, x3, z3, dt3, conv_w.reshape(SSD_CONV, SSD_XBC), conv_b.reshape(1, SSD_XBC),
      _pad_lanes(dt_bias), _pad_lanes(a_log),
      jnp.repeat(d_skip.astype(F32), SSD_HEAD_DIM).reshape(1, SSD_D_INNER),
      norm_w.reshape(1, SSD_D_INNER), tril, spread).reshape(batch * seq, SSD_D_INNER)


def _outproj_kernel(*refs, n_parts):
    x_ref, g_ref = refs[0], refs[1]
    parts = refs[2:2 + n_parts]
    ws = refs[2 + n_parts:2 + 2 * n_parts]
    o_ref = refs[2 + 2 * n_parts]
    acc = None
    for p_ref, w_ref in zip(parts, ws):
        d = jnp.dot(p_ref[...].astype(BF16), w_ref[...], preferred_element_type=F32)
        acc = d if acc is None else acc + d
    o_ref[...] = x_ref[...] + _rms(acc, g_ref[...], NORM_EPS)


def _outproj(x, g, parts, weights, tm=512):
    t, d = x.shape
    row = lambda i: (i, 0)
    const = lambda i: (0, 0)
    n = len(parts)
    return pl.pallas_call(
        functools.partial(_outproj_kernel, n_parts=n),
        grid=(t // tm,),
        in_specs=([pl.BlockSpec((tm, d), row), pl.BlockSpec((1, d), const)]
                  + [pl.BlockSpec((tm, p.shape[1]), row) for p in parts]
                  + [pl.BlockSpec(w.shape, const) for w in weights]),
        out_specs=pl.BlockSpec((tm, d), row),
        out_shape=jax.ShapeDtypeStruct((t, d), F32),
        compiler_params=_params("parallel"),
        name="mixer_out_proj",
    )(x, g.reshape(1, d), *parts, *[w.astype(BF16) for w in weights])


def _head_sum(x, seg, seg_t):
    return _dot_x3(_dot_x3(x, seg), seg_t)


def _rwkv_pre_kernel(x_ref, halo_ref, g_ref, mu_ref, wr_ref, wk_ref, wv_ref, w0_ref, w1_ref,
                     w2_ref, a0_ref, a1_ref, a2_ref, g1_ref, g2_ref, kk_ref, ka_ref, seg_ref,
                     segt_ref, r_out, ld_out, k_out, v_out, kk_out, g_out, bt_out, kt_out, ldt_out,
                     *, tiles_per_seq):
    i = pl.program_id(0)
    h = _rms(x_ref[...], g_ref[...], NORM_EPS)
    prev_row = _rms(halo_ref[...], g_ref[...], NORM_EPS)[SUBLANES - 1:SUBLANES]
    prev_row = jnp.where(i % tiles_per_seq == 0, 0.0, prev_row)
    rowid = lax.broadcasted_iota(jnp.int32, h.shape, 0)
    prev = jnp.where(rowid == 0, prev_row, pltpu.roll(h, 1, 0))
    xx = prev - h
    mu = mu_ref[...]
    mix = lambda j: (h + xx * mu[j:j + 1]).astype(BF16)
    dot = lambda a, w_ref: jnp.dot(a, w_ref[...], preferred_element_type=F32)
    r = dot(mix(0), wr_ref)
    w = -_softplus(-(w0_ref[...] + dot(jnp.tanh(dot(mix(1), w1_ref)).astype(BF16), w2_ref))) - 0.5
    k = dot(mix(2), wk_ref)
    v = dot(mix(3), wv_ref)
    a = jax.nn.sigmoid(a0_ref[...] + dot(dot(mix(4), a1_ref).astype(BF16), a2_ref))
    g = dot(jax.nn.sigmoid(dot(mix(5), g1_ref)).astype(BF16), g2_ref)
    kk = k * kk_ref[...]
    norm = jnp.sqrt(_head_sum(kk * kk, seg_ref[...], segt_ref[...]))
    kk = kk / jnp.maximum(norm, 1e-12)
    k = k * (1.0 + (a - 1.0) * ka_ref[...])
    log_decay = -jnp.exp(w)
    r_out[...] = r
    ld_out[...] = log_decay
    k_out[...] = k
    v_out[...] = v
    kk_out[...] = kk
    g_out[...] = g
    bt_out[0] = (kk * a).T
    kt_out[0] = k.T
    ldt_out[0] = log_decay.T


def _pad_cols(w, width):
    return jnp.pad(w, ((0, 0), (0, width - w.shape[1])))


def _pad_rows(w, width):
    return jnp.pad(w, ((0, width - w.shape[0]), (0, 0)))


def _seg_matrices():
    seg = np.zeros((D_MODEL, LANES), np.float32)
    seg[np.arange(D_MODEL), np.arange(D_MODEL) // RWKV_HEAD_DIM] = 1.0
    return jnp.asarray(seg, BF16), jnp.asarray(seg.T, BF16)


def _rwkv_pre(x, g, mu, w_r, w_k, w_v, w0, w1, w2, a0, a1, a2, g1, g2, k_k, k_a, seq, tm=256):
    t, d = x.shape
    lora = lambda w: -(-w.shape[1] // LANES) * LANES
    w1p, w2p = _pad_cols(w1, lora(w1)), _pad_rows(w2, lora(w1))
    a1p, a2p = _pad_cols(a1, lora(a1)), _pad_rows(a2, lora(a1))
    g1p, g2p = _pad_cols(g1, lora(g1)), _pad_rows(g2, lora(g1))
    seg, seg_t = _seg_matrices()
    row = lambda i: (i, 0)
    const = lambda i: (0, 0)
    hb = tm // SUBLANES
    vec = lambda v: v.reshape(1, d)
    mats = [w.astype(BF16) for w in (w_r, w_k, w_v)]
    ins = [x, x, vec(g), mu, *mats, vec(w0), w1p.astype(BF16), w2p.astype(BF16), vec(a0),
           a1p.astype(BF16), a2p.astype(BF16), g1p.astype(BF16), g2p.astype(BF16), vec(k_k),
           vec(k_a), seg, seg_t]
    in_specs = [pl.BlockSpec((tm, d), row),
                pl.BlockSpec((SUBLANES, d), lambda i: (jnp.maximum(i * hb - 1, 0), 0))]
    in_specs += [pl.BlockSpec(a.shape, const) for a in ins[2:]]
    tps = seq // tm
    col = pl.BlockSpec((1, d, tm), lambda i: (i // tps, 0, i % tps))
    return pl.pallas_call(
        functools.partial(_rwkv_pre_kernel, tiles_per_seq=tps),
        grid=(t // tm,),
        in_specs=in_specs,
        out_specs=[pl.BlockSpec((tm, d), row)] * 6 + [col] * 3,
        out_shape=([jax.ShapeDtypeStruct((t, d), F32)] * 6
                   + [jax.ShapeDtypeStruct((t // seq, d, seq), F32)] * 3),
        compiler_params=_params("parallel"),
        name="rwkv7_projections",
    )(*ins)


RWKV_CHUNK = 128


def _rwkv_chunk_kernel(r_ref, ld_ref, kk_ref, v_ref, bt_ref, kt_ref, ldt_ref, tril_ref, triu_ref,
                       y_ref, state_scr):
    @pl.when(pl.program_id(1) == 0)
    def _():
        state_scr[...] = jnp.zeros_like(state_scr)

    l = RWKV_CHUNK
    hd = RWKV_HEAD_DIM
    tril = tril_ref[...]
    ld = ld_ref[0]
    c_in = _dot_x3_left(tril, ld)
    a_bar = -kk_ref[0] * jnp.exp(c_in - ld)
    r_bar = r_ref[0] * jnp.exp(c_in)
    v = v_ref[0]
    ldt = ldt_ref[0]
    c_t = _dot_x3(ldt, triu_ref[...])
    scale_t = jnp.exp(-c_t)
    b_t = bt_ref[0] * scale_t
    k_t = kt_ref[0] * scale_t
    decay_col = jnp.exp(c_t[:, l - 1:l])

    row = lax.broadcasted_iota(jnp.int32, (l, l), 0)
    colx = lax.broadcasted_iota(jnp.int32, (l, l), 1)
    strict = row > colx
    incl = row >= colx
    lane = lax.broadcasted_iota(jnp.int32, (l, LANES), 1)
    first_head = lane < hd
    blockdiag = (lax.broadcasted_iota(jnp.int32, (LANES, LANES), 0) < hd) == (
        lax.broadcasted_iota(jnp.int32, (LANES, LANES), 1) < hd)
    mm = lambda a, b: jnp.dot(a, b, preferred_element_type=F32)
    b16 = lambda a: a.astype(BF16)

    pairs = range(D_MODEL // LANES)
    heads = [(c, hh) for c in pairs for hh in range(2)]
    lanes = {c: slice(c * LANES, (c + 1) * LANES) for c in pairs}
    v16 = {c: b16(v[:, lanes[c]]) for c in pairs}
    bk_t = {c: b16(jnp.concatenate([b_t[lanes[c]], k_t[lanes[c]]], axis=1)) for c in pairs}
    h2 = {c: state_scr[c] for c in pairs}
    gh = {}
    for c in pairs:
        rhs = jnp.concatenate([bk_t[c], b16(h2[c])], axis=1)
        a_p, r_p = a_bar[:, lanes[c]], r_bar[:, lanes[c]]
        for hh in range(2):
            keep = first_head if hh == 0 else ~first_head
            x = jnp.concatenate([jnp.where(keep, a_p, 0.0), jnp.where(keep, r_p, 0.0)], axis=0)
            gh[c, hh] = mm(b16(x), rhs)
    mp, u, p_r = {}, {}, {}
    for c, hh in heads:
        g = gh[c, hh]
        mp[c, hh] = b16(jnp.where(strict, g[:l, :l], 0.0))
        m_ak = b16(jnp.where(strict, g[:l, l:2 * l], 0.0))
        p_r[c, hh] = b16(jnp.concatenate([jnp.where(incl, g[l:, :l], 0.0),
                                          jnp.where(incl, g[l:, l:2 * l], 0.0)], axis=1))
        u[c, hh] = g[:l, 2 * l:] + mm(m_ak, v16[c])
    n_factors = l.bit_length() - 1
    for f in range(n_factors):
        du = {h: mm(mp[h], b16(u[h])) for h in heads}
        if f + 1 < n_factors:
            mp = {h: b16(mm(mp[h], mp[h])) for h in heads}
        u = {h: u[h] + du[h] for h in heads}
    ys = {h: gh[h][l:, 2 * l:] + mm(p_r[h], jnp.concatenate([b16(u[h]), v16[h[0]]], axis=0))
          for h in heads}
    for c in pairs:
        u_pair = jnp.where(first_head, u[c, 0], u[c, 1])
        y_ref[0, :, lanes[c]] = jnp.where(first_head, ys[c, 0], ys[c, 1])
        upd = h2[c] + mm(bk_t[c], jnp.concatenate([b16(u_pair), v16[c]], axis=0))
        state_scr[c] = jnp.where(blockdiag, upd * decay_col[lanes[c]], 0.0)


def _rwkv_scan(r, ld, kk, v, bt, kt, ldt, batch, seq):
    l = RWKV_CHUNK
    d = D_MODEL
    rows = lambda x: x.reshape(batch, seq, d)
    rblk = pl.BlockSpec((1, l, d), lambda b, c: (b, c, 0))
    cblk = pl.BlockSpec((1, d, l), lambda b, c: (b, 0, c))
    tril = jnp.asarray(np.tril(np.ones((l, l))), BF16)
    y = pl.pallas_call(
        _rwkv_chunk_kernel,
        grid=(batch, seq // l),
        in_specs=[rblk] * 4 + [cblk] * 3 + [pl.BlockSpec((l, l), lambda b, c: (0, 0))] * 2,
        out_specs=rblk,
        out_shape=jax.ShapeDtypeStruct((batch, seq, d), F32),
        scratch_shapes=[pltpu.VMEM((d // LANES, LANES, LANES), F32)],
        compiler_params=_params("parallel", "arbitrary"),
        name="rwkv7_recurrence",
    )(rows(r), rows(ld), rows(kk), rows(v), bt, kt, ldt, tril, tril.T)
    return y.reshape(batch * seq, d)


def _rwkv_post_kernel(x_ref, y_ref, r_ref, k_ref, v_ref, g_ref, lng_ref, lnb_ref, rk_ref, wo_ref,
                      gn_ref, seg_ref, segt_ref, o_ref):
    seg, seg_t = seg_ref[...], segt_ref[...]
    y = y_ref[...]
    inv = 1.0 / RWKV_HEAD_DIM
    mean = _head_sum(y, seg, seg_t) * inv
    yc = y - mean
    var = _head_sum(yc * yc, seg, seg_t) * inv
    yn = yc * lax.rsqrt(var + RWKV_GN_EPS) * lng_ref[...] + lnb_ref[...]
    bonus = _head_sum(r_ref[...] * k_ref[...] * rk_ref[...], seg, seg_t) * v_ref[...]
    out = ((yn + bonus) * g_ref[...]).astype(BF16)
    proj = jnp.dot(out, wo_ref[...], preferred_element_type=F32)
    o_ref[...] = x_ref[...] + _rms(proj, gn_ref[...], NORM_EPS)


def _rwkv_post(x, y, r, k, v, g, ln_g, ln_b, r_k, w_o, gn, tm=256):
    t, d = x.shape
    seg, seg_t = _seg_matrices()
    row = lambda i: (i, 0)
    const = lambda i: (0, 0)
    vec = lambda a: a.reshape(1, d)
    small = [vec(ln_g), vec(ln_b), vec(r_k), w_o.astype(BF16), vec(gn), seg, seg_t]
    return pl.pallas_call(
        _rwkv_post_kernel,
        grid=(t // tm,),
        in_specs=[pl.BlockSpec((tm, d), row)] * 6 + [pl.BlockSpec(a.shape, const) for a in small],
        out_specs=pl.BlockSpec((tm, d), row),
        out_shape=jax.ShapeDtypeStruct((t, d), F32),
        compiler_params=_params("parallel"),
        name="rwkv7_output",
    )(x, y, r, k, v, g, *small)


def _nsa_ssd_mixer(x, g_pre, g_post, cos, sin, w_in, pe_k, w1_k, w2_k, pe_v, w1_v, w2_v, conv_w,
                   conv_b, dt_bias, a_log, d_skip, norm_w, w_out, batch, seq):
    q, kv, z, xbc, misc = _inproj(x, g_pre, w_in, cos, sin, seq)
    o_a = _nsa(q, kv, misc, pe_k, w1_k, w2_k, pe_v, w1_v, w2_v, batch, seq)
    o_b = _ssd(z, xbc, misc, conv_w, conv_b, dt_bias, a_log, d_skip, norm_w, batch, seq)
    return _outproj(x, g_post, [o_a, o_b], [w_out[:NSA_Q_W], w_out[NSA_Q_W:]])


def _rwkv7_mixer(x, g_pre, g_post, mu, w_r, w_k, w_v, w_o, w0, w1, w2, a0, a1, a2, g1, g2, k_k,
                 k_a, r_k, ln_g, ln_b, batch, seq):
    r, ld, k, v, kk, g, bt, kt, ldt = _rwkv_pre(x, g_pre, mu, w_r, w_k, w_v, w0, w1, w2, a0, a1,
                                                a2, g1, g2, k_k, k_a, seq)
    y = _rwkv_scan(r, ld, kk, v, bt, kt, ldt, batch, seq)
    return _rwkv_post(x, y, r, k, v, g, ln_g, ln_b, r_k, w_o, g_post)


def kernel(x, norm_gains, ffn1_w_gate, ffn1_w_up, ffn1_w_down, ffn2_w_gate, ffn2_w_up, ffn2_w_down, ab_w_in, a_cmp_pe_k, a_cmp_w1_k, a_cmp_w2_k, a_cmp_pe_v, a_cmp_w1_v, a_cmp_w2_v, b_conv_w, b_conv_b, b_dt_bias, b_a_log, b_d_skip, b_norm_w, ab_w_out, c_mu, c_w_r, c_w_k, c_w_v, c_w_o, c_w0, c_w1, c_w2, c_a0, c_a1, c_a2, c_g1, c_g2, c_k_k, c_k_a, c_r_k, c_ln_g, c_ln_b):
    batch, seq, d = x.shape
    depth = norm_gains.shape[0]
    cos, sin = _rope_tables(seq)
    x = x.reshape(batch * seq, d)
    (ffn1_w_gate, ffn1_w_up, ffn1_w_down, ffn2_w_gate, ffn2_w_up, ffn2_w_down) = [
        w.astype(BF16) for w in (ffn1_w_gate, ffn1_w_up, ffn1_w_down, ffn2_w_gate, ffn2_w_up,
                                 ffn2_w_down)]
    for layer in range(depth):
        ng = norm_gains[layer]
        x = _ffn(x, ng[0], ng[1], ffn1_w_gate[layer], ffn1_w_up[layer], ffn1_w_down[layer])
        i = layer // 2
        if layer % 2 == 0:
            x = _nsa_ssd_mixer(x, ng[2], ng[3], cos, sin, ab_w_in[i], a_cmp_pe_k[i], a_cmp_w1_k[i],
                               a_cmp_w2_k[i], a_cmp_pe_v[i], a_cmp_w1_v[i], a_cmp_w2_v[i],
                               b_conv_w[i], b_conv_b[i], b_dt_bias[i], b_a_log[i], b_d_skip[i],
                               b_norm_w[i], ab_w_out[i], batch, seq)
        else:
            x = _rwkv7_mixer(x, ng[2], ng[3], c_mu[i], c_w_r[i], c_w_k[i], c_w_v[i], c_w_o[i],
                             c_w0[i], c_w1[i], c_w2[i], c_a0[i], c_a1[i], c_a2[i], c_g1[i],
                             c_g2[i], c_k_k[i], c_k_a[i], c_r_k[i], c_ln_g[i], c_ln_b[i],
                             batch, seq)
        x = _ffn(x, ng[4], ng[5], ffn2_w_gate[layer], ffn2_w_up[layer], ffn2_w_down[layer])
    return x.reshape(batch, seq, d)
```

```python
import functools

import jax
import jax.numpy as jnp
import numpy as np
from jax import lax
from jax.experimental import pallas as pl
from jax.experimental.pallas import tpu as pltpu

F32 = jnp.float32
BF16 = jnp.bfloat16
HIGHEST = lax.Precision.HIGHEST

D_MODEL = 1024
D_FF = 2816
NORM_EPS = 1e-6
NSA_HEADS = 8
NSA_KV_HEADS = 2
NSA_GROUP = NSA_HEADS // NSA_KV_HEADS
NSA_HEAD_DIM = 64
CMP_BLOCK = 32
CMP_STRIDE = 16
SEL_BLOCK = 64
SEL_TOPK = 16
WINDOW = 512
ROPE_THETA = 10000.0
FORCE_SCORE = 1e4
SEL_LANES = 128
SSD_HEADS = 16
SSD_HEAD_DIM = 64
SSD_D_INNER = SSD_HEADS * SSD_HEAD_DIM
SSD_GROUPS = 2
SSD_STATE = 128
SSD_CONV = 4
SSD_CHUNK = 128
SSD_NORM_EPS = 1e-5
SSD_XBC = SSD_D_INNER + 2 * SSD_GROUPS * SSD_STATE
RWKV_HEAD_DIM = 64
RWKV_HEADS = D_MODEL // RWKV_HEAD_DIM
RWKV_GN_EPS = 64e-5

NSA_Q_W = NSA_HEADS * NSA_HEAD_DIM
NSA_KV_W = NSA_KV_HEADS * NSA_HEAD_DIM
IN_SPLITS = (NSA_Q_W, NSA_KV_W, NSA_KV_W, NSA_KV_W, NSA_KV_W, NSA_KV_W, NSA_KV_W,
             NSA_HEADS * 3, SSD_D_INNER, SSD_XBC, SSD_HEADS)
IN_WIDTH = sum(IN_SPLITS)

LANES = 128
SUBLANES = 8
VMEM_LIMIT_BYTES = 56 * 1024 * 1024

NEG_MASK = -1e30
NEG_UNSELECTED = -2.0 ** 30
NEG_TAKEN = -3e38
LOG2_E = 1.4426950408889634


def _params(*sem):
    return pltpu.CompilerParams(dimension_semantics=sem, vmem_limit_bytes=VMEM_LIMIT_BYTES)


def _rms(x, g, eps):
    return x * lax.rsqrt(jnp.mean(x * x, -1, keepdims=True) + eps) * g


def _silu(x):
    return x * jax.nn.sigmoid(x)


def _softplus(x):
    return jnp.maximum(x, 0.0) + jnp.log1p(jnp.exp(-jnp.abs(x)))


def _split3(a):
    a1 = a.astype(BF16)
    r1 = a - a1.astype(F32)
    a2 = r1.astype(BF16)
    a3 = (r1 - a2.astype(F32)).astype(BF16)
    return a1, a2, a3


def _dot_x3(a, b):
    acc = None
    for piece in _split3(a):
        d = jnp.dot(piece, b, preferred_element_type=F32)
        acc = d if acc is None else acc + d
    return acc


def _dot_x3_left(b, a):
    acc = None
    for piece in _split3(a):
        d = jnp.dot(b, piece, preferred_element_type=F32)
        acc = d if acc is None else acc + d
    return acc


def _dot_nt(a, b, **kw):
    return lax.dot_general(a, b, (((1,), (1,)), ((), ())), preferred_element_type=F32, **kw)


MXU_TILE = 256


def _ffn_kernel(x_ref, gi_ref, go_ref, wg_ref, wu_ref, wd_ref, o_ref, *, chunks):
    x = x_ref[...]
    h = _rms(x, gi_ref[...], NORM_EPS).astype(BF16)
    acc = None
    for lo, hi in chunks:
        gate = jnp.dot(h, wg_ref[:, lo:hi], preferred_element_type=F32)
        up = jnp.dot(h, wu_ref[:, lo:hi], preferred_element_type=F32)
        act = (_silu(gate) * up).astype(BF16)
        part = jnp.dot(act, wd_ref[lo:hi, :], preferred_element_type=F32)
        acc = part if acc is None else acc + part
    o_ref[...] = x + 0.5 * _rms(acc, go_ref[...], NORM_EPS)


def _ffn(x, g_in, g_out, w_gate, w_up, w_down, tm=512, n_chunks=2):
    t, d = x.shape
    f = w_gate.shape[1]
    tiles = f // MXU_TILE
    assert tiles * MXU_TILE == f
    cuts = [MXU_TILE * ((tiles * c + n_chunks - 1) // n_chunks) for c in range(n_chunks + 1)]
    chunks = tuple(zip(cuts[:-1], cuts[1:]))
    resident = lambda shape: pl.BlockSpec(shape, lambda i: (0, 0), pipeline_mode=pl.Buffered(1))
    return pl.pallas_call(
        functools.partial(_ffn_kernel, chunks=chunks),
        grid=(t // tm,),
        in_specs=[
            pl.BlockSpec((tm, d), lambda i: (i, 0)),
            pl.BlockSpec((1, d), lambda i: (0, 0)),
            pl.BlockSpec((1, d), lambda i: (0, 0)),
            resident((d, f)),
            resident((d, f)),
            resident((f, d)),
        ],
        out_specs=pl.BlockSpec((tm, d), lambda i: (i, 0)),
        out_shape=jax.ShapeDtypeStruct((t, d), F32),
        compiler_params=_params("parallel"),
        name="ffn_half_step",
    )(x, g_in.reshape(1, d), g_out.reshape(1, d), w_gate.astype(BF16), w_up.astype(BF16),
      w_down.astype(BF16))


INPROJ_MISC_W = 256
INPROJ_KV_W = 6 * NSA_KV_W
INPROJ_WIDTH = NSA_Q_W + INPROJ_KV_W + SSD_D_INNER + SSD_XBC + INPROJ_MISC_W


def _swap_halves(x):
    w = x.shape[-1]
    lane = lax.broadcasted_iota(jnp.int32, x.shape, x.ndim - 1)
    low = (lane & (NSA_HEAD_DIM - 1)) < (NSA_HEAD_DIM // 2)
    return jnp.where(low, pltpu.roll(x, w - NSA_HEAD_DIM // 2, x.ndim - 1),
                     pltpu.roll(x, NSA_HEAD_DIM // 2, x.ndim - 1))


def _inproj_kernel(x_ref, g_ref, w_ref, cos_ref, sin_ref, q_ref, kv_ref, z_ref, xbc_ref, misc_ref):
    h = _rms(x_ref[...], g_ref[...], NORM_EPS).astype(BF16)
    proj = jnp.dot(h, w_ref[...], preferred_element_type=F32)
    cos = cos_ref[...]
    sin = sin_ref[...]
    o = 0
    q = proj[:, o:o + NSA_Q_W]
    cos_q = jnp.concatenate([cos] * (NSA_Q_W // LANES), axis=1)
    sin_q = jnp.concatenate([sin] * (NSA_Q_W // LANES), axis=1)
    q_ref[...] = (q * cos_q + _swap_halves(q) * sin_q) * (NSA_HEAD_DIM ** -0.5)
    o += NSA_Q_W
    for i in range(6):
        piece = proj[:, o:o + NSA_KV_W]
        if i % 2 == 0:
            piece = piece * cos + _swap_halves(piece) * sin
        kv_ref[:, i * NSA_KV_W:(i + 1) * NSA_KV_W] = piece
        o += NSA_KV_W
    z_ref[...] = proj[:, o:o + SSD_D_INNER]
    o += SSD_D_INNER
    xbc_ref[...] = proj[:, o:o + SSD_XBC]
    o += SSD_XBC
    misc_ref[...] = proj[:, o:o + INPROJ_MISC_W]


def _inproj(x, g, w_in, cos, sin, seq, tm=256):
    t, d = x.shape
    offs = np.cumsum(IN_SPLITS)[:-1].tolist()
    q, kc, vc, ks, vs, kw, vw, gl, z, xbc, dt = jnp.split(w_in, offs, -1)
    pad = jnp.zeros((d, INPROJ_MISC_W - gl.shape[1] - dt.shape[1]), w_in.dtype)
    w = jnp.concatenate([q, kc, vc, ks, vs, kw, vw, z, xbc, gl, dt, pad], -1).astype(BF16)
    assert w.shape[1] == INPROJ_WIDTH
    nseq = seq // tm
    row = lambda i: (i, 0)
    const = lambda i: (0, 0)
    widths = (NSA_Q_W, INPROJ_KV_W, SSD_D_INNER, SSD_XBC, INPROJ_MISC_W)
    return pl.pallas_call(
        _inproj_kernel,
        grid=(t // tm,),
        in_specs=[
            pl.BlockSpec((tm, d), row),
            pl.BlockSpec((1, d), const),
            pl.BlockSpec((d, INPROJ_WIDTH), const),
            pl.BlockSpec((tm, LANES), lambda i: (i % nseq, 0)),
            pl.BlockSpec((tm, LANES), lambda i: (i % nseq, 0)),
        ],
        out_specs=[pl.BlockSpec((tm, wd), row) for wd in widths],
        out_shape=[jax.ShapeDtypeStruct((t, wd), F32) for wd in widths],
        compiler_params=_params("parallel"),
        name="mixer0_in_proj",
    )(x, g.reshape(1, d), w, cos, sin)


def _rope_tables(seq):
    inv = ROPE_THETA ** (-jnp.arange(0, NSA_HEAD_DIM, 2, dtype=F32) / NSA_HEAD_DIM)
    ang = jnp.arange(seq, dtype=F32)[:, None] * inv[None, :]
    cos, sin = jnp.cos(ang), jnp.sin(ang)
    reps = LANES // NSA_HEAD_DIM
    cos_t = jnp.concatenate([cos, cos] * reps, -1)
    sin_t = jnp.concatenate([-sin, sin] * reps, -1)
    return cos_t, sin_t


def _compress_kernel(k_ref, pe_ref, w1_ref, w2_ref, o_ref):
    k16 = k_ref[0, 0]
    w1 = w1_ref[0]
    half = w1.shape[0] // 2
    first = jnp.dot(k16, w1[:half], precision=HIGHEST, preferred_element_type=F32)
    second = jnp.dot(k16, w1[half:], precision=HIGHEST, preferred_element_type=F32)
    bias = jnp.dot(pe_ref[0], w1, precision=HIGHEST, preferred_element_type=F32)[0:1]
    n = k16.shape[0]
    pre = first + pltpu.roll(second, n - 1, 0) + bias
    o_ref[0, 0] = jnp.dot(_silu(pre), w2_ref[0], precision=HIGHEST, preferred_element_type=F32)


def _compress(kv16, pe, w1, w2):
    two, bh, n, wd = kv16.shape
    d = w2.shape[-1]
    return pl.pallas_call(
        _compress_kernel,
        grid=(two, bh),
        in_specs=[
            pl.BlockSpec((1, 1, n, wd), lambda a, b: (a, b, 0, 0)),
            pl.BlockSpec((1, SUBLANES, pe.shape[-1]), lambda a, b: (a, 0, 0)),
            pl.BlockSpec((1,) + w1.shape[1:], lambda a, b: (a, 0, 0)),
            pl.BlockSpec((1, d, d), lambda a, b: (a, 0, 0)),
        ],
        out_specs=pl.BlockSpec((1, 1, n, d), lambda a, b: (a, b, 0, 0)),
        out_shape=jax.ShapeDtypeStruct((two, bh, n, d), F32),
        compiler_params=_params("parallel", "parallel"),
        name="nsa_compress",
    )(kv16, pe, w1, w2)


def _group_rows(q):
    return jnp.concatenate(
        [q[:, g * NSA_HEAD_DIM:(g + 1) * NSA_HEAD_DIM] for g in range(NSA_GROUP)], axis=0)


def _ungroup_rows(o, tq):
    return jnp.concatenate([o[g * tq:(g + 1) * tq] for g in range(NSA_GROUP)], axis=1)


def _dot_nt_hi(a, b):
    a1 = a.astype(BF16)
    a2 = (a - a1.astype(F32)).astype(BF16)
    b1 = b.astype(BF16)
    b2 = (b - b1.astype(F32)).astype(BF16)
    return _dot_nt(a1, b1) + _dot_nt(a1, b2) + _dot_nt(a2, b1)


def _nsa_select_kernel(q_ref, kc_ref, vc_ref, ovt_ref, oc_ref, biast_ref, *, tq, topk):
    s0 = pl.program_id(1) * tq
    gw = NSA_GROUP * NSA_HEAD_DIM
    heads = range(NSA_KV_HEADS)
    q = q_ref[0]
    s = [_dot_nt_hi(_group_rows(q[:, h * gw:(h + 1) * gw]), kc_ref[h]) for h in heads]
    rows, ncmp = s[0].shape
    t = s0 + (lax.broadcasted_iota(jnp.int32, (rows, ncmp), 0) & (tq - 1))
    cmp_end = lax.broadcasted_iota(jnp.int32, (rows, ncmp), 1) * CMP_STRIDE + (CMP_BLOCK - 1)
    mask = cmp_end <= t
    p = []
    for h in heads:
        sh = jnp.where(mask, s[h], NEG_MASK)
        ph = jnp.where(mask, jnp.exp(sh - jnp.max(sh, -1, keepdims=True)), 0.0)
        p.append(ph / jnp.maximum(jnp.sum(ph, -1, keepdims=True), 1e-30))
    for h in heads:
        o = jnp.dot(p[h].astype(BF16), vc_ref[h].astype(BF16), preferred_element_type=F32)
        oc_ref[0, :, h * gw:(h + 1) * gw] = _ungroup_rows(o, tq)

    ovt = ovt_ref[...]
    imp = []
    for h in heads:
        psum = p[h][0:tq]
        for g in range(1, NSA_GROUP):
            psum = psum + p[h][g * tq:(g + 1) * tq]
        acc = None
        for piece in _split3(psum):
            d = _dot_nt(ovt, piece)
            acc = d if acc is None else acc + d
        imp.append(acc)
    blk = lax.broadcasted_iota(jnp.int32, imp[0].shape, 0)
    tt = s0 + lax.broadcasted_iota(jnp.int32, imp[0].shape, 1)
    cur = lax.shift_right_logical(tt, SEL_BLOCK.bit_length() - 1)
    forced = (blk == 0) | (blk == cur) | (blk == cur - 1)
    valid = blk * SEL_BLOCK <= tt
    x = [jnp.where(valid, jnp.where(forced, FORCE_SCORE, imp[h]), NEG_MASK) for h in heads]
    blk_f = blk.astype(F32)
    sel = [jnp.zeros(blk.shape, jnp.bool_) for _ in heads]
    for _ in range(topk):
        for h in heads:
            m = jnp.max(x[h], 0, keepdims=True)
            idx = jnp.min(jnp.where(x[h] == m, blk_f, float(SEL_LANES)), 0, keepdims=True)
            hit = blk_f == idx
            sel[h] = sel[h] | hit
            x[h] = jnp.where(hit, NEG_TAKEN, x[h])
    for h in heads:
        biast_ref[0, h] = jnp.where(sel[h], 0.0, NEG_UNSELECTED).astype(BF16)


def _nsa_select(q, kc, vc, overlap, batch, seq, tq=128):
    ncmp = kc.shape[1]
    qw = NSA_KV_HEADS * NSA_GROUP * NSA_HEAD_DIM
    topk = min(SEL_TOPK, seq // SEL_BLOCK)
    kern = functools.partial(_nsa_select_kernel, tq=tq, topk=topk)
    return pl.pallas_call(
        kern,
        grid=(batch, seq // tq),
        in_specs=[
            pl.BlockSpec((1, tq, qw), lambda b, i: (b, i, 0)),
            pl.BlockSpec((NSA_KV_HEADS, ncmp, NSA_HEAD_DIM), lambda b, i: (b, 0, 0)),
            pl.BlockSpec((NSA_KV_HEADS, ncmp, NSA_HEAD_DIM), lambda b, i: (b, 0, 0)),
            pl.BlockSpec((SEL_LANES, ncmp), lambda b, i: (0, 0)),
        ],
        out_specs=[
            pl.BlockSpec((1, tq, qw), lambda b, i: (b, i, 0)),
            pl.BlockSpec((1, NSA_KV_HEADS, SEL_LANES, tq), lambda b, i: (b, 0, 0, i)),
        ],
        out_shape=[
            jax.ShapeDtypeStruct((batch, seq, qw), F32),
            jax.ShapeDtypeStruct((batch, NSA_KV_HEADS, SEL_LANES, seq), BF16),
        ],
        compiler_params=_params("parallel", "parallel"),
        name="nsa_compressed_select",
    )(q, kc, vc, overlap)


NSA_LANE_SPLIT = 2


def _nsa_attend_kernel(q_ref, biast_ref, ka_ref, vst_ref, kw_ref, vwt_ref, oc_ref, gl_ref, o_ref,
                       sa_scr, sb_scr, p_scr, w_scr, *, tq, tk):
    i = pl.program_id(2)
    s0 = i * tq
    n = NSA_GROUP * tq
    half = n // NSA_LANE_SPLIT
    d = NSA_HEAD_DIM
    qt = (q_ref[0] * LOG2_E).T
    qgt = jnp.concatenate([qt[g * d:(g + 1) * d] for g in range(NSA_GROUP)], axis=1)
    qgt = qgt.astype(BF16)
    qat = jnp.concatenate([jnp.concatenate([biast_ref[0, 0]] * NSA_GROUP, axis=1), qgt], axis=0)
    init = tuple((jnp.full((1, half), NEG_MASK, F32), jnp.zeros((1, half), F32),
                  jnp.zeros((d, half), F32)) for _ in range(NSA_LANE_SPLIT))

    def query_pos(shape):
        return s0 + (lax.broadcasted_iota(jnp.int32, shape, 1) & (tq - 1))

    halves = range(NSA_LANE_SPLIT)

    def scores(kt):
        k = ka_ref[0, pl.ds(pl.multiple_of(kt * tk, tk), tk), :]
        return tuple(jnp.dot(k, qat[:, hf * half:(hf + 1) * half], preferred_element_type=F32)
                     for hf in halves)

    def values(kt):
        vt = vst_ref[0, :, pl.ds(pl.multiple_of(kt * tk, tk), tk)]
        return tuple(jnp.dot(vt, p_scr[:, hf * half:(hf + 1) * half],
                             preferred_element_type=F32) for hf in halves)

    def sel_step(kt, stats, src_scr, dst_scr, causal):
        s_next = None if causal else scores(kt + 1)
        pv = values(jnp.maximum(kt - 1, 0))
        new_stats = []
        for hf in halves:
            cols = slice(hf * half, (hf + 1) * half)
            m, l, acc = stats[hf]
            s = src_scr[:, cols]
            if causal:
                kp = kt * tk + lax.broadcasted_iota(jnp.int32, s.shape, 0)
                mask = kp <= query_pos(s.shape)
                s = jnp.where(mask, s, NEG_MASK)
            m_new = jnp.maximum(m, jnp.max(s, 0, keepdims=True))
            alpha = jnp.exp2(m - m_new)
            p = jnp.exp2(s - m_new)
            if causal:
                p = jnp.where(mask, p, 0.0)
            new_stats.append((m_new, alpha * l + jnp.sum(p, 0, keepdims=True),
                              alpha * (acc + pv[hf])))
            p_scr[:, cols] = p.astype(BF16)
        if not causal:
            for hf in halves:
                dst_scr[:, hf * half:(hf + 1) * half] = s_next[hf]
        return tuple(new_stats)

    def by_parity(kt, stats, causal):
        return lax.cond((kt & 1) == 0,
                        lambda st: sel_step(kt, st, sa_scr, sb_scr, causal),
                        lambda st: sel_step(kt, st, sb_scr, sa_scr, causal), stats)

    n_full = s0 // tk
    first_scores = scores(0)
    span = WINDOW + tq
    start = pl.multiple_of(jnp.maximum(s0 - WINDOW, 0), tq)
    kwin = kw_ref[0, pl.ds(start, span), :]
    for hf in halves:
        cols = slice(hf * half, (hf + 1) * half)
        sa_scr[:, cols] = first_scores[hf]
        w_scr[:, cols] = jnp.dot(kwin, qgt[:, cols], preferred_element_type=F32)
    p_scr[...] = jnp.zeros_like(p_scr)
    stats = lax.fori_loop(0, n_full, lambda kt, c: by_parity(kt, c, False), init)
    stats = by_parity(n_full, stats, True)
    pv_last = values(n_full)

    vwt = vwt_ref[0, :, pl.ds(start, span)]
    p_w, l_w = [], []
    for hf in halves:
        s = w_scr[:, hf * half:(hf + 1) * half]
        kp = start + lax.broadcasted_iota(jnp.int32, s.shape, 0)
        t = query_pos(s.shape)
        mask = (kp <= t) & (kp > t - WINDOW)
        s = jnp.where(mask, s, NEG_MASK)
        p = jnp.where(mask, jnp.exp2(s - jnp.max(s, 0, keepdims=True)), 0.0)
        l_w.append(jnp.sum(p, 0, keepdims=True))
        p_w.append(p.astype(BF16))
    o_w = jnp.concatenate([jnp.dot(vwt, p_w[hf], preferred_element_type=F32)
                           / jnp.maximum(l_w[hf], 1e-30) for hf in halves], axis=1)
    o_s = jnp.concatenate([(stats[hf][2] + pv_last[hf]) / jnp.maximum(stats[hf][1], 1e-30)
                           for hf in halves], axis=1)

    def rows_layout(ot):
        return jnp.concatenate([ot[:, g * tq:(g + 1) * tq] for g in range(NSA_GROUP)], axis=0).T

    o_s = rows_layout(o_s)
    o_w = rows_layout(o_w)
    gates = jax.nn.sigmoid(gl_ref[0, 0])
    o_c = oc_ref[0]
    pieces = []
    for g in range(NSA_GROUP):
        c = slice(g * d, (g + 1) * d)
        pieces.append(gates[:, 3 * g:3 * g + 1] * o_c[:, c]
                      + gates[:, 3 * g + 1:3 * g + 2] * o_s[:, c]
                      + gates[:, 3 * g + 2:3 * g + 3] * o_w[:, c])
    o_ref[0] = jnp.concatenate(pieces, axis=1)


def _nsa_attend(q, biast, kaug, vst, kw, vwt, o_c, gl, batch, seq, tq=256, tk=512):
    gw = NSA_GROUP * NSA_HEAD_DIM
    tk = min(tk, seq)
    assert tk % tq == 0 and WINDOW % tq == 0 and seq >= WINDOW + tq
    kern = functools.partial(_nsa_attend_kernel, tq=tq, tk=tk)
    bh = lambda b, h, i: (b * NSA_KV_HEADS + h, 0, 0)
    return pl.pallas_call(
        kern,
        grid=(batch, NSA_KV_HEADS, seq // tq),
        in_specs=[
            pl.BlockSpec((1, tq, gw), lambda b, h, i: (b, i, h)),
            pl.BlockSpec((1, 1, SEL_LANES, tq), lambda b, h, i: (b, h, 0, i)),
            pl.BlockSpec((1, seq, SEL_LANES + NSA_HEAD_DIM), bh),
            pl.BlockSpec((1, NSA_HEAD_DIM, seq), bh),
            pl.BlockSpec((1, seq, NSA_HEAD_DIM), bh),
            pl.BlockSpec((1, NSA_HEAD_DIM, seq), bh),
            pl.BlockSpec((1, tq, gw), lambda b, h, i: (b, i, h)),
            pl.BlockSpec((1, 1, tq, NSA_GROUP * 3), lambda b, h, i: (b, h, i, 0)),
        ],
        out_specs=pl.BlockSpec((1, tq, gw), lambda b, h, i: (b, i, h)),
        out_shape=jax.ShapeDtypeStruct((batch, seq, NSA_KV_HEADS * gw), F32),
        scratch_shapes=[pltpu.VMEM((tk, NSA_GROUP * tq), F32),
                        pltpu.VMEM((tk, NSA_GROUP * tq), F32),
                        pltpu.VMEM((tk, NSA_GROUP * tq), BF16),
                        pltpu.VMEM((WINDOW + tq, NSA_GROUP * tq), F32)],
        compiler_params=_params("parallel", "parallel", "arbitrary"),
        name="nsa_selected_window",
    )(q, biast, kaug, vst, kw, vwt, o_c, gl)


def _nsa(q, kv, misc, pe_k, w1_k, w2_k, pe_v, w1_v, w2_v, batch, seq):
    d = NSA_HEAD_DIM
    heads = lambda a: jnp.moveaxis(a.reshape(batch, seq, NSA_KV_HEADS, d), 2, 1)
    piece = lambda i: heads(kv[:, i * NSA_KV_W:(i + 1) * NSA_KV_W])
    k_cmp, v_cmp, k_sel, v_sel, k_win, v_win = [piece(i) for i in range(6)]
    bh = batch * NSA_KV_HEADS
    n16 = seq // CMP_STRIDE
    kv16 = jnp.stack([k_cmp, v_cmp]).reshape(2, bh, n16, CMP_STRIDE * d)
    pe = jnp.stack([pe_k, pe_v]).reshape(2, 1, CMP_BLOCK * d)
    pe = jnp.broadcast_to(pe, (2, SUBLANES, CMP_BLOCK * d))
    cmp = _compress(kv16, pe, jnp.stack([w1_k, w1_v]), jnp.stack([w2_k, w2_v]))
    kc, vc = cmp[0], cmp[1]

    n_sel = seq // SEL_BLOCK
    cmp_start = np.arange(n16) * CMP_STRIDE
    sel_start = np.arange(SEL_LANES) * SEL_BLOCK
    overlap = ((cmp_start[:, None] < sel_start[None, :] + SEL_BLOCK)
               & (cmp_start[:, None] + CMP_BLOCK - 1 >= sel_start[None, :])
               & (np.arange(SEL_LANES)[None, :] < n_sel)
               & (np.arange(n16)[:, None] < (seq - CMP_BLOCK) // CMP_STRIDE + 1))
    overlap_t = jnp.asarray(overlap.T, BF16)

    q3 = q.reshape(batch, seq, NSA_Q_W)
    o_c, biast = _nsa_select(q3, kc, vc, overlap_t, batch, seq)

    onehot = (np.arange(seq)[:, None] // SEL_BLOCK == np.arange(SEL_LANES)[None, :])
    onehot = jnp.broadcast_to(jnp.asarray(onehot, BF16), (bh, seq, SEL_LANES))
    flat = lambda a: a.reshape(bh, seq, d).astype(BF16)
    flat_t = lambda a: jnp.swapaxes(a.reshape(bh, seq, d), 1, 2).astype(BF16)
    kaug = jnp.concatenate([onehot, flat(k_sel)], -1)
    gl = misc[:, :NSA_HEADS * 3].reshape(batch, seq, NSA_KV_HEADS, NSA_GROUP * 3)
    gl = jnp.moveaxis(gl, 2, 1)
    o = _nsa_attend(q3, biast, kaug, flat_t(v_sel), flat(k_win), flat_t(v_win), o_c, gl, batch,
                    seq)
    return o.reshape(batch * seq, NSA_Q_W)


def _ssd_kernel(xbc_ref, halo_ref, z_ref, dt_ref, cw_ref, cb_ref, dtb_ref, alog_ref, dskip_ref,
                nw_ref, tril_ref, spread_ref, o_ref, state_scr, y_scr):
    c = pl.program_id(1)
    l = SSD_CHUNK

    @pl.when(c == 0)
    def _():
        state_scr[...] = jnp.zeros_like(state_scr)

    x = xbc_ref[0]
    halo = jnp.where(c == 0, 0.0, halo_ref[0])
    xx = jnp.concatenate([halo, x], axis=0)
    cw = cw_ref[...]
    conv = cb_ref[...]
    for k in range(SSD_CONV):
        off = SUBLANES - (SSD_CONV - 1) + k
        conv = conv + cw[k:k + 1] * xx[off:off + l]
    xbc = _silu(conv)
    xs = xbc[:, :SSD_D_INNER]
    gn = SSD_GROUPS * SSD_STATE
    bmat = xbc[:, SSD_D_INNER:SSD_D_INNER + gn]
    cmat = xbc[:, SSD_D_INNER + gn:]

    dt = _softplus(dt_ref[0] + dtb_ref[...])
    da = dt * (-jnp.exp(alog_ref[...]))
    a_cs = _dot_x3_left(tril_ref[...], da)
    a_cs_t = a_cs.T
    a_last = a_cs[l - 1:l]
    causal = (lax.broadcasted_iota(jnp.int32, (l, l), 0)
              >= lax.broadcasted_iota(jnp.int32, (l, l), 1))

    spread = spread_ref[...]
    dt_x = _dot_x3(dt, spread)
    grow_x = _dot_x3(jnp.exp(a_cs), spread)
    fade_x = _dot_x3(jnp.exp(a_last - a_cs), spread)
    chunk_x = _dot_x3(jnp.broadcast_to(jnp.exp(a_last), (SUBLANES, LANES)), spread)[0:1]
    xd = xs * dt_x
    xd16 = xd.astype(BF16)
    fxd16 = (xd * fade_x).astype(BF16)

    pairs = range(SSD_HEADS // 2)
    pairs_per_group = len(pairs) // SSD_GROUPS
    lanes = {c: slice(c * LANES, (c + 1) * LANES) for c in pairs}
    cb, y_off = {}, {}
    for g in range(SSD_GROUPS):
        bg = bmat[:, g * SSD_STATE:(g + 1) * SSD_STATE]
        cg16 = cmat[:, g * SSD_STATE:(g + 1) * SSD_STATE].astype(BF16)
        cb[g] = _dot_nt(cg16, bg.astype(BF16))
        bgt16 = bg.T.astype(BF16)
        for c in range(g * pairs_per_group, (g + 1) * pairs_per_group):
            st = state_scr[c]
            y_off[c] = jnp.dot(cg16, st.astype(BF16), preferred_element_type=F32)
            new = jnp.dot(bgt16, fxd16[:, lanes[c]], preferred_element_type=F32)
            state_scr[c] = st * chunk_x[:, lanes[c]] + new
    first_head = lax.broadcasted_iota(jnp.int32, (l, LANES), 1) < SSD_HEAD_DIM
    y_diag = {}
    for c in pairs:
        for hh in range(2):
            h = 2 * c + hh
            seg = jnp.where(causal, jnp.exp(a_cs[:, h:h + 1] - a_cs_t[h:h + 1, :]), 0.0)
            y_diag[h] = jnp.dot((cb[c // pairs_per_group] * seg).astype(BF16), xd16[:, lanes[c]],
                                preferred_element_type=F32)
    for c in pairs:
        y_scr[:, lanes[c]] = (jnp.where(first_head, y_diag[2 * c], y_diag[2 * c + 1])
                              + y_off[c] * grow_x[:, lanes[c]])

    y = (y_scr[...] + xs * dskip_ref[...]) * _silu(z_ref[0])
    gw = SSD_D_INNER // SSD_GROUPS
    outs = []
    for g in range(SSD_GROUPS):
        yg = y[:, g * gw:(g + 1) * gw]
        outs.append(yg * lax.rsqrt(jnp.mean(yg * yg, -1, keepdims=True) + SSD_NORM_EPS))
    o_ref[0] = jnp.concatenate(outs, axis=1) * nw_ref[...]


def _pad_lanes(v, width=LANES):
    v = v.reshape(1, -1).astype(F32)
    return jnp.pad(v, ((0, 0), (0, width - v.shape[1])))


def _ssd(z, xbc, misc, conv_w, conv_b, dt_bias, a_log, d_skip, norm_w, batch, seq):
    l = SSD_CHUNK
    nc = seq // l
    z3 = z.reshape(batch, seq, SSD_D_INNER)
    x3 = xbc.reshape(batch, seq, SSD_XBC)
    dt = misc[:, NSA_HEADS * 3:NSA_HEADS * 3 + SSD_HEADS]
    dt3 = jnp.pad(dt, ((0, 0), (0, LANES - SSD_HEADS))).reshape(batch, seq, LANES)
    tril = jnp.asarray(np.tril(np.ones((l, l))), BF16)
    spread = np.zeros((LANES, SSD_D_INNER), np.float32)
    spread[np.arange(SSD_D_INNER) // SSD_HEAD_DIM, np.arange(SSD_D_INNER)] = 1.0
    spread = jnp.asarray(spread, BF16)
    hb = l // SUBLANES
    const = lambda b, c: (0, 0)
    return pl.pallas_call(
        _ssd_kernel,
        grid=(batch, nc),
        in_specs=[
            pl.BlockSpec((1, l, SSD_XBC), lambda b, c: (b, c, 0)),
            pl.BlockSpec((1, SUBLANES, SSD_XBC), lambda b, c: (b, jnp.maximum(c * hb - 1, 0), 0)),
            pl.BlockSpec((1, l, SSD_D_INNER), lambda b, c: (b, c, 0)),
            pl.BlockSpec((1, l, LANES), lambda b, c: (b, c, 0)),
            pl.BlockSpec((SSD_CONV, SSD_XBC), const),
            pl.BlockSpec((1, SSD_XBC), const),
            pl.BlockSpec((1, LANES), const),
            pl.BlockSpec((1, LANES), const),
            pl.BlockSpec((1, SSD_D_INNER), const),
            pl.BlockSpec((1, SSD_D_INNER), const),
            pl.BlockSpec((l, l), const),
            pl.BlockSpec((LANES, SSD_D_INNER), const),
        ],
        out_specs=pl.BlockSpec((1, l, SSD_D_INNER), lambda b, c: (b, c, 0)),
        out_shape=jax.ShapeDtypeStruct((batch, seq, SSD_D_INNER), F32),
        scratch_shapes=[pltpu.VMEM((SSD_HEADS // 2, SSD_STATE, 2 * SSD_HEAD_DIM), F32),
                        pltpu.VMEM((l, SSD_D_INNER), F32)],
        compiler_params=_params("parallel", "arbitrary"),
        name="ssd_chunk_scan",
    )(x3, x3, z3, dt3, conv_w.reshape(SSD_CONV, SSD_XBC), conv_b.reshape(1, SSD_XBC),
      _pad_lanes(dt_bias), _pad_lanes(a_log),
      jnp.repeat(d_skip.astype(F32), SSD_HEAD_DIM).reshape(1, SSD_D_INNER),
      norm_w.reshape(1, SSD_D_INNER), tril, spread).reshape(batch * seq, SSD_D_INNER)


def _outproj_kernel(*refs, n_parts):
    x_ref, g_ref = refs[0], refs[1]
    parts = refs[2:2 + n_parts]
    ws = refs[2 + n_parts:2 + 2 * n_parts]
    o_ref = refs[2 + 2 * n_parts]
    acc = None
    for p_ref, w_ref in zip(parts, ws):
        d = jnp.dot(p_ref[...].astype(BF16), w_ref[...], preferred_element_type=F32)
        acc = d if acc is None else acc + d
    o_ref[...] = x_ref[...] + _rms(acc, g_ref[...], NORM_EPS)


def _outproj(x, g, parts, weights, tm=512):
    t, d = x.shape
    row = lambda i: (i, 0)
    const = lambda i: (0, 0)
    n = len(parts)
    return pl.pallas_call(
        functools.partial(_outproj_kernel, n_parts=n),
        grid=(t // tm,),
        in_specs=([pl.BlockSpec((tm, d), row), pl.BlockSpec((1, d), const)]
                  + [pl.BlockSpec((tm, p.shape[1]), row) for p in parts]
                  + [pl.BlockSpec(w.shape, const) for w in weights]),
        out_specs=pl.BlockSpec((tm, d), row),
        out_shape=jax.ShapeDtypeStruct((t, d), F32),
        compiler_params=_params("parallel"),
        name="mixer_out_proj",
    )(x, g.reshape(1, d), *parts, *[w.astype(BF16) for w in weights])


def _dot_x2(a, b):
    a1 = a.astype(BF16)
    a2 = (a - a1.astype(F32)).astype(BF16)
    return (jnp.dot(a1, b, preferred_element_type=F32)
            + jnp.dot(a2, b, preferred_element_type=F32))


def _head_sum(x, seg, seg_t):
    return _dot_x2(_dot_x2(x, seg), seg_t)


def _rwkv_pre_kernel(x_ref, halo_ref, g_ref, mu_ref, wr_ref, wk_ref, wv_ref, w0_ref, w1_ref,
                     w2_ref, a0_ref, a1_ref, a2_ref, g1_ref, g2_ref, kk_ref, ka_ref, seg_ref,
                     segt_ref, r_out, ld_out, k_out, v_out, kk_out, g_out, bt_out, kt_out, ldt_out,
                     *, tiles_per_seq):
    i = pl.program_id(0)
    h = _rms(x_ref[...], g_ref[...], NORM_EPS)
    prev_row = _rms(halo_ref[...], g_ref[...], NORM_EPS)[SUBLANES - 1:SUBLANES]
    prev_row = jnp.where(i % tiles_per_seq == 0, 0.0, prev_row)
    rowid = lax.broadcasted_iota(jnp.int32, h.shape, 0)
    prev = jnp.where(rowid == 0, prev_row, pltpu.roll(h, 1, 0))
    xx = prev - h
    mu = mu_ref[...]
    mix = lambda j: (h + xx * mu[j:j + 1]).astype(BF16)
    dot = lambda a, w_ref: jnp.dot(a, w_ref[...], preferred_element_type=F32)
    r = dot(mix(0), wr_ref)
    w = -_softplus(-(w0_ref[...] + dot(jnp.tanh(dot(mix(1), w1_ref)).astype(BF16), w2_ref))) - 0.5
    k = dot(mix(2), wk_ref)
    v = dot(mix(3), wv_ref)
    a = jax.nn.sigmoid(a0_ref[...] + dot(dot(mix(4), a1_ref).astype(BF16), a2_ref))
    g = dot(jax.nn.sigmoid(dot(mix(5), g1_ref)).astype(BF16), g2_ref)
    kk = k * kk_ref[...]
    norm = jnp.sqrt(_head_sum(kk * kk, seg_ref[...], segt_ref[...]))
    kk = kk / jnp.maximum(norm, 1e-12)
    k = k * (1.0 + (a - 1.0) * ka_ref[...])
    log_decay = -jnp.exp(w)
    r_out[...] = r
    ld_out[...] = log_decay
    k_out[...] = k
    v_out[...] = v
    kk_out[...] = kk
    g_out[...] = g
    bt_out[0] = (kk * a).T
    kt_out[0] = k.T
    ldt_out[0] = log_decay.T


def _pad_cols(w, width):
    return jnp.pad(w, ((0, 0), (0, width - w.shape[1])))


def _pad_rows(w, width):
    return jnp.pad(w, ((0, width - w.shape[0]), (0, 0)))


def _seg_matrices():
    seg = np.zeros((D_MODEL, LANES), np.float32)
    seg[np.arange(D_MODEL), np.arange(D_MODEL) // RWKV_HEAD_DIM] = 1.0
    return jnp.asarray(seg, BF16), jnp.asarray(seg.T, BF16)


def _rwkv_pre(x, g, mu, w_r, w_k, w_v, w0, w1, w2, a0, a1, a2, g1, g2, k_k, k_a, seq, tm=256):
    t, d = x.shape
    lora = lambda w: -(-w.shape[1] // LANES) * LANES
    w1p, w2p = _pad_cols(w1, lora(w1)), _pad_rows(w2, lora(w1))
    a1p, a2p = _pad_cols(a1, lora(a1)), _pad_rows(a2, lora(a1))
    g1p, g2p = _pad_cols(g1, lora(g1)), _pad_rows(g2, lora(g1))
    seg, seg_t = _seg_matrices()
    row = lambda i: (i, 0)
    const = lambda i: (0, 0)
    hb = tm // SUBLANES
    vec = lambda v: v.reshape(1, d)
    mats = [w.astype(BF16) for w in (w_r, w_k, w_v)]
    ins = [x, x, vec(g), mu, *mats, vec(w0), w1p.astype(BF16), w2p.astype(BF16), vec(a0),
           a1p.astype(BF16), a2p.astype(BF16), g1p.astype(BF16), g2p.astype(BF16), vec(k_k),
           vec(k_a), seg, seg_t]
    in_specs = [pl.BlockSpec((tm, d), row),
                pl.BlockSpec((SUBLANES, d), lambda i: (jnp.maximum(i * hb - 1, 0), 0))]
    in_specs += [pl.BlockSpec(a.shape, const) for a in ins[2:]]
    tps = seq // tm
    col = pl.BlockSpec((1, d, tm), lambda i: (i // tps, 0, i % tps))
    return pl.pallas_call(
        functools.partial(_rwkv_pre_kernel, tiles_per_seq=tps),
        grid=(t // tm,),
        in_specs=in_specs,
        out_specs=[pl.BlockSpec((tm, d), row)] * 6 + [col] * 3,
        out_shape=([jax.ShapeDtypeStruct((t, d), F32)] * 6
                   + [jax.ShapeDtypeStruct((t // seq, d, seq), F32)] * 3),
        compiler_params=_params("parallel"),
        name="rwkv7_projections",
    )(*ins)


RWKV_CHUNK = 128


def _rwkv_chunk_kernel(r_ref, ld_ref, kk_ref, v_ref, bt_ref, kt_ref, ldt_ref, tril_ref, triu_ref,
                       y_ref, state_scr):
    @pl.when(pl.program_id(1) == 0)
    def _():
        state_scr[...] = jnp.zeros_like(state_scr)

    l = RWKV_CHUNK
    hd = RWKV_HEAD_DIM
    tril = tril_ref[...]
    ld = ld_ref[0]
    c_in = _dot_x3_left(tril, ld)
    a_bar = -kk_ref[0] * jnp.exp(c_in - ld)
    r_bar = r_ref[0] * jnp.exp(c_in)
    v = v_ref[0]
    ldt = ldt_ref[0]
    c_t = _dot_x3(ldt, triu_ref[...])
    scale_t = jnp.exp(-c_t)
    b_t = bt_ref[0] * scale_t
    k_t = kt_ref[0] * scale_t
    decay_col = jnp.exp(c_t[:, l - 1:l])

    row = lax.broadcasted_iota(jnp.int32, (l, l), 0)
    colx = lax.broadcasted_iota(jnp.int32, (l, l), 1)
    strict = row > colx
    incl = row >= colx
    lane = lax.broadcasted_iota(jnp.int32, (l, LANES), 1)
    first_head = lane < hd
    blockdiag = (lax.broadcasted_iota(jnp.int32, (LANES, LANES), 0) < hd) == (
        lax.broadcasted_iota(jnp.int32, (LANES, LANES), 1) < hd)
    mm = lambda a, b: jnp.dot(a, b, preferred_element_type=F32)
    b16 = lambda a: a.astype(BF16)

    pairs = range(D_MODEL // LANES)
    heads = [(c, hh) for c in pairs for hh in range(2)]
    lanes = {c: slice(c * LANES, (c + 1) * LANES) for c in pairs}
    v16 = {c: b16(v[:, lanes[c]]) for c in pairs}
    bk_t = {c: b16(jnp.concatenate([b_t[lanes[c]], k_t[lanes[c]]], axis=1)) for c in pairs}
    h2 = {c: state_scr[c] for c in pairs}
    gh = {}
    for c in pairs:
        rhs = jnp.concatenate([bk_t[c], b16(h2[c])], axis=1)
        a_p, r_p = a_bar[:, lanes[c]], r_bar[:, lanes[c]]
        for hh in range(2):
            keep = first_head if hh == 0 else ~first_head
            x = jnp.concatenate([jnp.where(keep, a_p, 0.0), jnp.where(keep, r_p, 0.0)], axis=0)
            gh[c, hh] = mm(b16(x), rhs)
    mp, u, p_r = {}, {}, {}
    for c, hh in heads:
        g = gh[c, hh]
        mp[c, hh] = b16(jnp.where(strict, g[:l, :l], 0.0))
        m_ak = b16(jnp.where(strict, g[:l, l:2 * l], 0.0))
        p_r[c, hh] = b16(jnp.concatenate([jnp.where(incl, g[l:, :l], 0.0),
                                          jnp.where(incl, g[l:, l:2 * l], 0.0)], axis=1))
        u[c, hh] = g[:l, 2 * l:] + mm(m_ak, v16[c])
    n_factors = l.bit_length() - 1
    for f in range(n_factors):
        du = {h: mm(mp[h], b16(u[h])) for h in heads}
        if f + 1 < n_factors:
            mp = {h: b16(mm(mp[h], mp[h])) for h in heads}
        u = {h: u[h] + du[h] for h in heads}
    ys = {h: gh[h][l:, 2 * l:] + mm(p_r[h], jnp.concatenate([b16(u[h]), v16[h[0]]], axis=0))
          for h in heads}
    for c in pairs:
        u_pair = jnp.where(first_head, u[c, 0], u[c, 1])
        y_ref[0, :, lanes[c]] = jnp.where(first_head, ys[c, 0], ys[c, 1])
        upd = h2[c] + mm(bk_t[c], jnp.concatenate([b16(u_pair), v16[c]], axis=0))
        state_scr[c] = jnp.where(blockdiag, upd * decay_col[lanes[c]], 0.0)


def _rwkv_scan(r, ld, kk, v, bt, kt, ldt, batch, seq):
    l = RWKV_CHUNK
    d = D_MODEL
    rows = lambda x: x.reshape(batch, seq, d)
    rblk = pl.BlockSpec((1, l, d), lambda b, c: (b, c, 0))
    cblk = pl.BlockSpec((1, d, l), lambda b, c: (b, 0, c))
    tril = jnp.asarray(np.tril(np.ones((l, l))), BF16)
    y = pl.pallas_call(
        _rwkv_chunk_kernel,
        grid=(batch, seq // l),
        in_specs=[rblk] * 4 + [cblk] * 3 + [pl.BlockSpec((l, l), lambda b, c: (0, 0))] * 2,
        out_specs=rblk,
        out_shape=jax.ShapeDtypeStruct((batch, seq, d), F32),
        scratch_shapes=[pltpu.VMEM((d // LANES, LANES, LANES), F32)],
        compiler_params=_params("parallel", "arbitrary"),
        name="rwkv7_recurrence",
    )(rows(r), rows(ld), rows(kk), rows(v), bt, kt, ldt, tril, tril.T)
    return y.reshape(batch * seq, d)


def _rwkv_post_kernel(x_ref, y_ref, r_ref, k_ref, v_ref, g_ref, lng_ref, lnb_ref, rk_ref, wo_ref,
                      gn_ref, seg_ref, segt_ref, o_ref):
    seg, seg_t = seg_ref[...], segt_ref[...]
    y = y_ref[...]
    inv = 1.0 / RWKV_HEAD_DIM
    mean = _head_sum(y, seg, seg_t) * inv
    yc = y - mean
    var = _head_sum(yc * yc, seg, seg_t) * inv
    yn = yc * lax.rsqrt(var + RWKV_GN_EPS) * lng_ref[...] + lnb_ref[...]
    bonus = _head_sum(r_ref[...] * k_ref[...] * rk_ref[...], seg, seg_t) * v_ref[...]
    out = ((yn + bonus) * g_ref[...]).astype(BF16)
    proj = jnp.dot(out, wo_ref[...], preferred_element_type=F32)
    o_ref[...] = x_ref[...] + _rms(proj, gn_ref[...], NORM_EPS)


def _rwkv_post(x, y, r, k, v, g, ln_g, ln_b, r_k, w_o, gn, tm=256):
    t, d = x.shape
    seg, seg_t = _seg_matrices()
    row = lambda i: (i, 0)
    const = lambda i: (0, 0)
    vec = lambda a: a.reshape(1, d)
    small = [vec(ln_g), vec(ln_b), vec(r_k), w_o.astype(BF16), vec(gn), seg, seg_t]
    return pl.pallas_call(
        _rwkv_post_kernel,
        grid=(t // tm,),
        in_specs=[pl.BlockSpec((tm, d), row)] * 6 + [pl.BlockSpec(a.shape, const) for a in small],
        out_specs=pl.BlockSpec((tm, d), row),
        out_shape=jax.ShapeDtypeStruct((t, d), F32),
        compiler_params=_params("parallel"),
        name="rwkv7_output",
    )(x, y, r, k, v, g, *small)


def _nsa_ssd_mixer(x, g_pre, g_post, cos, sin, w_in, pe_k, w1_k, w2_k, pe_v, w1_v, w2_v, conv_w,
                   conv_b, dt_bias, a_log, d_skip, norm_w, w_out, batch, seq):
    q, kv, z, xbc, misc = _inproj(x, g_pre, w_in, cos, sin, seq)
    o_a = _nsa(q, kv, misc, pe_k, w1_k, w2_k, pe_v, w1_v, w2_v, batch, seq)
    o_b = _ssd(z, xbc, misc, conv_w, conv_b, dt_bias, a_log, d_skip, norm_w, batch, seq)
    return _outproj(x, g_post, [o_a, o_b], [w_out[:NSA_Q_W], w_out[NSA_Q_W:]])


def _rwkv7_mixer(x, g_pre, g_post, mu, w_r, w_k, w_v, w_o, w0, w1, w2, a0, a1, a2, g1, g2, k_k,
                 k_a, r_k, ln_g, ln_b, batch, seq):
    r, ld, k, v, kk, g, bt, kt, ldt = _rwkv_pre(x, g_pre, mu, w_r, w_k, w_v, w0, w1, w2, a0, a1,
                                                a2, g1, g2, k_k, k_a, seq)
    y = _rwkv_scan(r, ld, kk, v, bt, kt, ldt, batch, seq)
    return _rwkv_post(x, y, r, k, v, g, ln_g, ln_b, r_k, w_o, g_post)


def kernel(x, norm_gains, ffn1_w_gate, ffn1_w_up, ffn1_w_down, ffn2_w_gate, ffn2_w_up, ffn2_w_down, ab_w_in, a_cmp_pe_k, a_cmp_w1_k, a_cmp_w2_k, a_cmp_pe_v, a_cmp_w1_v, a_cmp_w2_v, b_conv_w, b_conv_b, b_dt_bias, b_a_log, b_d_skip, b_norm_w, ab_w_out, c_mu, c_w_r, c_w_k, c_w_v, c_w_o, c_w0, c_w1, c_w2, c_a0, c_a1, c_a2, c_g1, c_g2, c_k_k, c_k_a, c_r_k, c_ln_g, c_ln_b):
    batch, seq, d = x.shape
    depth = norm_gains.shape[0]
    cos, sin = _rope_tables(seq)
    x = x.reshape(batch * seq, d)
    (ffn1_w_gate, ffn1_w_up, ffn1_w_down, ffn2_w_gate, ffn2_w_up, ffn2_w_down) = [
        w.astype(BF16) for w in (ffn1_w_gate, ffn1_w_up, ffn1_w_down, ffn2_w_gate, ffn2_w_up,
                                 ffn2_w_down)]
    for layer in range(depth):
        ng = norm_gains[layer]
        x = _ffn(x, ng[0], ng[1], ffn1_w_gate[layer], ffn1_w_up[layer], ffn1_w_down[layer])
        i = layer // 2
        if layer % 2 == 0:
            x = _nsa_ssd_mixer(x, ng[2], ng[3], cos, sin, ab_w_in[i], a_cmp_pe_k[i], a_cmp_w1_k[i],
                               a_cmp_w2_k[i], a_cmp_pe_v[i], a_cmp_w1_v[i], a_cmp_w2_v[i],
                               b_conv_w[i], b_conv_b[i], b_dt_bias[i], b_a_log[i], b_d_skip[i],
                               b_norm_w[i], ab_w_out[i], batch, seq)
        else:
            x = _rwkv7_mixer(x, ng[2], ng[3], c_mu[i], c_w_r[i], c_w_k[i], c_w_v[i], c_w_o[i],
                             c_w0[i], c_w1[i], c_w2[i], c_a0[i], c_a1[i], c_a2[i], c_g1[i],
                             c_g2[i], c_k_k[i], c_k_a[i], c_r_k[i], c_ln_g[i], c_ln_b[i],
                             batch, seq)
        x = _ffn(x, ng[4], ng[5], ffn2_w_gate[layer], ffn2_w_up[layer], ffn2_w_down[layer])
    return x.reshape(batch, seq, d)
```

```python
import functools

import jax
import jax.numpy as jnp
import numpy as np
from jax import lax
from jax.experimental import pallas as pl
from jax.experimental.pallas import tpu as pltpu

F32 = jnp.float32
BF16 = jnp.bfloat16
HIGHEST = lax.Precision.HIGHEST

D_MODEL = 1024
D_FF = 2816
NORM_EPS = 1e-6
NSA_HEADS = 8
NSA_KV_HEADS = 2
NSA_GROUP = NSA_HEADS // NSA_KV_HEADS
NSA_HEAD_DIM = 64
CMP_BLOCK = 32
CMP_STRIDE = 16
SEL_BLOCK = 64
SEL_TOPK = 16
WINDOW = 512
ROPE_THETA = 10000.0
FORCE_SCORE = 1e4
SEL_LANES = 128
SSD_HEADS = 16
SSD_HEAD_DIM = 64
SSD_D_INNER = SSD_HEADS * SSD_HEAD_DIM
SSD_GROUPS = 2
SSD_STATE = 128
SSD_CONV = 4
SSD_CHUNK = 128
SSD_NORM_EPS = 1e-5
SSD_XBC = SSD_D_INNER + 2 * SSD_GROUPS * SSD_STATE
RWKV_HEAD_DIM = 64
RWKV_HEADS = D_MODEL // RWKV_HEAD_DIM
RWKV_GN_EPS = 64e-5

NSA_Q_W = NSA_HEADS * NSA_HEAD_DIM
NSA_KV_W = NSA_KV_HEADS * NSA_HEAD_DIM
IN_SPLITS = (NSA_Q_W, NSA_KV_W, NSA_KV_W, NSA_KV_W, NSA_KV_W, NSA_KV_W, NSA_KV_W,
             NSA_HEADS * 3, SSD_D_INNER, SSD_XBC, SSD_HEADS)
IN_WIDTH = sum(IN_SPLITS)

LANES = 128
SUBLANES = 8
VMEM_LIMIT_BYTES = 56 * 1024 * 1024

NEG_MASK = -1e30
NEG_UNSELECTED = -2.0 ** 30
NEG_TAKEN = -3e38
LOG2_E = 1.4426950408889634


def _params(*sem):
    return pltpu.CompilerParams(dimension_semantics=sem, vmem_limit_bytes=VMEM_LIMIT_BYTES)


def _rms(x, g, eps):
    return x * lax.rsqrt(jnp.mean(x * x, -1, keepdims=True) + eps) * g


def _silu(x):
    return x * jax.nn.sigmoid(x)


def _softplus(x):
    return jnp.maximum(x, 0.0) + jnp.log1p(jnp.exp(-jnp.abs(x)))


def _split3(a):
    a1 = a.astype(BF16)
    r1 = a - a1.astype(F32)
    a2 = r1.astype(BF16)
    a3 = (r1 - a2.astype(F32)).astype(BF16)
    return a1, a2, a3


def _dot_x3(a, b):
    acc = None
    for piece in _split3(a):
        d = jnp.dot(piece, b, preferred_element_type=F32)
        acc = d if acc is None else acc + d
    return acc


def _dot_x3_left(b, a):
    acc = None
    for piece in _split3(a):
        d = jnp.dot(b, piece, preferred_element_type=F32)
        acc = d if acc is None else acc + d
    return acc


def _dot_nt(a, b, **kw):
    return lax.dot_general(a, b, (((1,), (1,)), ((), ())), preferred_element_type=F32, **kw)


MXU_TILE = 256


def _ffn_kernel(x_ref, gi_ref, go_ref, wg_ref, wu_ref, wd_ref, o_ref, *, chunks):
    x = x_ref[...]
    h = _rms(x, gi_ref[...], NORM_EPS).astype(BF16)
    acc = None
    for lo, hi in chunks:
        gate = jnp.dot(h, wg_ref[:, lo:hi], preferred_element_type=F32)
        up = jnp.dot(h, wu_ref[:, lo:hi], preferred_element_type=F32)
        act = (_silu(gate) * up).astype(BF16)
        part = jnp.dot(act, wd_ref[lo:hi, :], preferred_element_type=F32)
        acc = part if acc is None else acc + part
    o_ref[...] = x + 0.5 * _rms(acc, go_ref[...], NORM_EPS)


def _ffn(x, g_in, g_out, w_gate, w_up, w_down, tm=512, n_chunks=2):
    t, d = x.shape
    f = w_gate.shape[1]
    tiles = f // MXU_TILE
    assert tiles * MXU_TILE == f
    cuts = [MXU_TILE * ((tiles * c + n_chunks - 1) // n_chunks) for c in range(n_chunks + 1)]
    chunks = tuple(zip(cuts[:-1], cuts[1:]))
    resident = lambda shape: pl.BlockSpec(shape, lambda i: (0, 0), pipeline_mode=pl.Buffered(1))
    return pl.pallas_call(
        functools.partial(_ffn_kernel, chunks=chunks),
        grid=(t // tm,),
        in_specs=[
            pl.BlockSpec((tm, d), lambda i: (i, 0)),
            pl.BlockSpec((1, d), lambda i: (0, 0)),
            pl.BlockSpec((1, d), lambda i: (0, 0)),
            resident((d, f)),
            resident((d, f)),
            resident((f, d)),
        ],
        out_specs=pl.BlockSpec((tm, d), lambda i: (i, 0)),
        out_shape=jax.ShapeDtypeStruct((t, d), F32),
        compiler_params=_params("parallel"),
        name="ffn_half_step",
    )(x, g_in.reshape(1, d), g_out.reshape(1, d), w_gate.astype(BF16), w_up.astype(BF16),
      w_down.astype(BF16))


INPROJ_MISC_W = 256
INPROJ_KV_W = 6 * NSA_KV_W
INPROJ_WIDTH = NSA_Q_W + INPROJ_KV_W + SSD_D_INNER + SSD_XBC + INPROJ_MISC_W


def _swap_halves(x):
    w = x.shape[-1]
    lane = lax.broadcasted_iota(jnp.int32, x.shape, x.ndim - 1)
    low = (lane & (NSA_HEAD_DIM - 1)) < (NSA_HEAD_DIM // 2)
    return jnp.where(low, pltpu.roll(x, w - NSA_HEAD_DIM // 2, x.ndim - 1),
                     pltpu.roll(x, NSA_HEAD_DIM // 2, x.ndim - 1))


def _inproj_kernel(x_ref, g_ref, w_ref, cos_ref, sin_ref, q_ref, kvc_ref, kaug_ref, kwin_ref,
                   vst_ref, vwt_ref, z_ref, xbc_ref, misc_ref, *, tiles_per_seq):
    h = _rms(x_ref[...], g_ref[...], NORM_EPS).astype(BF16)
    proj = jnp.dot(h, w_ref[...], preferred_element_type=F32)
    tm = proj.shape[0]
    cos = cos_ref[...]
    sin = sin_ref[...]
    q = proj[:, :NSA_Q_W]
    cos_q = jnp.concatenate([cos] * (NSA_Q_W // LANES), axis=1)
    sin_q = jnp.concatenate([sin] * (NSA_Q_W // LANES), axis=1)
    q_ref[...] = (q * cos_q + _swap_halves(q) * sin_q) * (NSA_HEAD_DIM ** -0.5)
    piece = lambda i: proj[:, NSA_Q_W + i * NSA_KV_W:NSA_Q_W + (i + 1) * NSA_KV_W]
    rope = lambda p: p * cos + _swap_halves(p) * sin
    kvc_ref[:, :NSA_KV_W] = rope(piece(0))
    kvc_ref[:, NSA_KV_W:] = piece(1)
    k_sel, k_win = rope(piece(2)), rope(piece(4))
    pos = (pl.program_id(0) % tiles_per_seq) * tm + lax.broadcasted_iota(
        jnp.int32, (tm, SEL_LANES), 0)
    block_id = lax.shift_right_logical(pos, SEL_BLOCK.bit_length() - 1)
    onehot = jnp.where(lax.broadcasted_iota(jnp.int32, (tm, SEL_LANES), 1) == block_id,
                       1.0, 0.0).astype(BF16)
    vst = piece(3).T
    vwt = piece(5).T
    d = NSA_HEAD_DIM
    for hh in range(NSA_KV_HEADS):
        kaug_ref[0, hh] = jnp.concatenate([onehot, k_sel[:, hh * d:(hh + 1) * d].astype(BF16)],
                                          axis=1)
        kwin_ref[0, hh] = k_win[:, hh * d:(hh + 1) * d].astype(BF16)
        vst_ref[0, hh] = vst[hh * d:(hh + 1) * d].astype(BF16)
        vwt_ref[0, hh] = vwt[hh * d:(hh + 1) * d].astype(BF16)
    o = NSA_Q_W + INPROJ_KV_W
    z_ref[...] = proj[:, o:o + SSD_D_INNER]
    o += SSD_D_INNER
    xbc_ref[...] = proj[:, o:o + SSD_XBC]
    o += SSD_XBC
    misc_ref[...] = proj[:, o:o + INPROJ_MISC_W]


def _inproj(x, g, w_in, cos, sin, batch, seq, tm=256):
    t, d = x.shape
    offs = np.cumsum(IN_SPLITS)[:-1].tolist()
    q, kc, vc, ks, vs, kw, vw, gl, z, xbc, dt = jnp.split(w_in, offs, -1)
    pad = jnp.zeros((d, INPROJ_MISC_W - gl.shape[1] - dt.shape[1]), w_in.dtype)
    w = jnp.concatenate([q, kc, vc, ks, vs, kw, vw, z, xbc, gl, dt, pad], -1).astype(BF16)
    assert w.shape[1] == INPROJ_WIDTH
    nseq = seq // tm
    row = lambda i: (i, 0)
    const = lambda i: (0, 0)
    hd, hkv = NSA_HEAD_DIM, NSA_KV_HEADS
    by_head_rows = lambda wd: pl.BlockSpec((1, hkv, tm, wd), lambda i: (i // nseq, 0, i % nseq, 0))
    by_head_cols = pl.BlockSpec((1, hkv, hd, tm), lambda i: (i // nseq, 0, 0, i % nseq))
    flat = lambda wd: (pl.BlockSpec((tm, wd), row), jax.ShapeDtypeStruct((t, wd), F32))
    outs = [
        flat(NSA_Q_W),
        flat(2 * NSA_KV_W),
        (by_head_rows(SEL_LANES + hd), jax.ShapeDtypeStruct((batch, hkv, seq, SEL_LANES + hd), BF16)),
        (by_head_rows(hd), jax.ShapeDtypeStruct((batch, hkv, seq, hd), BF16)),
        (by_head_cols, jax.ShapeDtypeStruct((batch, hkv, hd, seq), BF16)),
        (by_head_cols, jax.ShapeDtypeStruct((batch, hkv, hd, seq), BF16)),
        flat(SSD_D_INNER),
        flat(SSD_XBC),
        flat(INPROJ_MISC_W),
    ]
    return pl.pallas_call(
        functools.partial(_inproj_kernel, tiles_per_seq=nseq),
        grid=(t // tm,),
        in_specs=[
            pl.BlockSpec((tm, d), row),
            pl.BlockSpec((1, d), const),
            pl.BlockSpec((d, INPROJ_WIDTH), const),
            pl.BlockSpec((tm, LANES), lambda i: (i % nseq, 0)),
            pl.BlockSpec((tm, LANES), lambda i: (i % nseq, 0)),
        ],
        out_specs=[o[0] for o in outs],
        out_shape=[o[1] for o in outs],
        compiler_params=_params("parallel"),
        name="mixer0_in_proj",
    )(x, g.reshape(1, d), w, cos, sin)


def _rope_tables(seq):
    inv = ROPE_THETA ** (-np.arange(0, NSA_HEAD_DIM, 2, dtype=np.float64) / NSA_HEAD_DIM)
    ang = np.arange(seq, dtype=np.float64)[:, None] * inv[None, :]
    cos, sin = np.cos(ang), np.sin(ang)
    reps = LANES // NSA_HEAD_DIM
    cos_t = np.concatenate([cos, cos] * reps, -1).astype(np.float32)
    sin_t = np.concatenate([-sin, sin] * reps, -1).astype(np.float32)
    return jnp.asarray(cos_t), jnp.asarray(sin_t)


def _compress_kernel(k_ref, pe_ref, w1_ref, w2_ref, o_ref):
    k16 = k_ref[0, 0]
    w1 = w1_ref[0]
    half = w1.shape[0] // 2
    first = jnp.dot(k16, w1[:half], precision=HIGHEST, preferred_element_type=F32)
    second = jnp.dot(k16, w1[half:], precision=HIGHEST, preferred_element_type=F32)
    bias = jnp.dot(pe_ref[0], w1, precision=HIGHEST, preferred_element_type=F32)[0:1]
    n = k16.shape[0]
    pre = first + pltpu.roll(second, n - 1, 0) + bias
    o_ref[0, 0] = jnp.dot(_silu(pre), w2_ref[0], precision=HIGHEST, preferred_element_type=F32)


def _compress(kv16, pe, w1, w2):
    two, bh, n, wd = kv16.shape
    d = w2.shape[-1]
    return pl.pallas_call(
        _compress_kernel,
        grid=(two, bh),
        in_specs=[
            pl.BlockSpec((1, 1, n, wd), lambda a, b: (a, b, 0, 0)),
            pl.BlockSpec((1, SUBLANES, pe.shape[-1]), lambda a, b: (a, 0, 0)),
            pl.BlockSpec((1,) + w1.shape[1:], lambda a, b: (a, 0, 0)),
            pl.BlockSpec((1, d, d), lambda a, b: (a, 0, 0)),
        ],
        out_specs=pl.BlockSpec((1, 1, n, d), lambda a, b: (a, b, 0, 0)),
        out_shape=jax.ShapeDtypeStruct((two, bh, n, d), F32),
        compiler_params=_params("parallel", "parallel"),
        name="nsa_compress",
    )(kv16, pe, w1, w2)


def _group_rows(q):
    return jnp.concatenate(
        [q[:, g * NSA_HEAD_DIM:(g + 1) * NSA_HEAD_DIM] for g in range(NSA_GROUP)], axis=0)


def _ungroup_rows(o, tq):
    return jnp.concatenate([o[g * tq:(g + 1) * tq] for g in range(NSA_GROUP)], axis=1)


def _dot_nt_hi(a, b):
    a1 = a.astype(BF16)
    a2 = (a - a1.astype(F32)).astype(BF16)
    b1 = b.astype(BF16)
    b2 = (b - b1.astype(F32)).astype(BF16)
    return _dot_nt(a1, b1) + _dot_nt(a1, b2) + _dot_nt(a2, b1)


def _nsa_select_kernel(q_ref, kc_ref, vc_ref, ovt_ref, oc_ref, biast_ref, *, tq, topk):
    s0 = pl.program_id(1) * tq
    gw = NSA_GROUP * NSA_HEAD_DIM
    heads = range(NSA_KV_HEADS)
    q = q_ref[0]
    s = [_dot_nt_hi(_group_rows(q[:, h * gw:(h + 1) * gw]), kc_ref[h]) for h in heads]
    rows, ncmp = s[0].shape
    t = s0 + (lax.broadcasted_iota(jnp.int32, (rows, ncmp), 0) & (tq - 1))
    cmp_end = lax.broadcasted_iota(jnp.int32, (rows, ncmp), 1) * CMP_STRIDE + (CMP_BLOCK - 1)
    mask = cmp_end <= t
    p = []
    for h in heads:
        sh = jnp.where(mask, s[h], NEG_MASK)
        ph = jnp.where(mask, jnp.exp(sh - jnp.max(sh, -1, keepdims=True)), 0.0)
        p.append(ph / jnp.maximum(jnp.sum(ph, -1, keepdims=True), 1e-30))
    for h in heads:
        o = jnp.dot(p[h].astype(BF16), vc_ref[h].astype(BF16), preferred_element_type=F32)
        oc_ref[0, :, h * gw:(h + 1) * gw] = _ungroup_rows(o, tq)

    ovt = ovt_ref[...]
    imp = []
    for h in heads:
        psum = p[h][0:tq]
        for g in range(1, NSA_GROUP):
            psum = psum + p[h][g * tq:(g + 1) * tq]
        acc = None
        for piece in _split3(psum):
            d = _dot_nt(ovt, piece)
            acc = d if acc is None else acc + d
        imp.append(acc)
    blk = lax.broadcasted_iota(jnp.int32, imp[0].shape, 0)
    tt = s0 + lax.broadcasted_iota(jnp.int32, imp[0].shape, 1)
    cur = lax.shift_right_logical(tt, SEL_BLOCK.bit_length() - 1)
    forced = (blk == 0) | (blk == cur) | (blk == cur - 1)
    valid = blk * SEL_BLOCK <= tt
    x = [jnp.where(valid, jnp.where(forced, FORCE_SCORE, imp[h]), NEG_MASK) for h in heads]
    blk_f = blk.astype(F32)
    sel = [jnp.zeros(blk.shape, jnp.bool_) for _ in heads]
    for _ in range(topk):
        for h in heads:
            m = jnp.max(x[h], 0, keepdims=True)
            idx = jnp.min(jnp.where(x[h] == m, blk_f, float(SEL_LANES)), 0, keepdims=True)
            hit = blk_f == idx
            sel[h] = sel[h] | hit
            x[h] = jnp.where(hit, NEG_TAKEN, x[h])
    for h in heads:
        biast_ref[0, h] = jnp.where(sel[h], 0.0, NEG_UNSELECTED).astype(BF16)


def _nsa_select(q, kc, vc, overlap, batch, seq, tq=256):
    ncmp = kc.shape[1]
    qw = NSA_KV_HEADS * NSA_GROUP * NSA_HEAD_DIM
    topk = min(SEL_TOPK, seq // SEL_BLOCK)
    kern = functools.partial(_nsa_select_kernel, tq=tq, topk=topk)
    return pl.pallas_call(
        kern,
        grid=(batch, seq // tq),
        in_specs=[
            pl.BlockSpec((1, tq, qw), lambda b, i: (b, i, 0)),
            pl.BlockSpec((NSA_KV_HEADS, ncmp, NSA_HEAD_DIM), lambda b, i: (b, 0, 0)),
            pl.BlockSpec((NSA_KV_HEADS, ncmp, NSA_HEAD_DIM), lambda b, i: (b, 0, 0)),
            pl.BlockSpec((SEL_LANES, ncmp), lambda b, i: (0, 0)),
        ],
        out_specs=[
            pl.BlockSpec((1, tq, qw), lambda b, i: (b, i, 0)),
            pl.BlockSpec((1, NSA_KV_HEADS, SEL_LANES, tq), lambda b, i: (b, 0, 0, i)),
        ],
        out_shape=[
            jax.ShapeDtypeStruct((batch, seq, qw), F32),
            jax.ShapeDtypeStruct((batch, NSA_KV_HEADS, SEL_LANES, seq), BF16),
        ],
        compiler_params=_params("parallel", "parallel"),
        name="nsa_compressed_select",
    )(q, kc, vc, overlap)


NSA_LANE_SPLIT = 2


def _nsa_attend_kernel(q_ref, biast_ref, ka_ref, vst_ref, kw_ref, vwt_ref, oc_ref, gl_ref, o_ref,
                       sa_scr, sb_scr, p_scr, w_scr, *, tq, tk):
    i = pl.program_id(2)
    s0 = i * tq
    n = NSA_GROUP * tq
    half = n // NSA_LANE_SPLIT
    d = NSA_HEAD_DIM
    qt = (q_ref[0] * LOG2_E).T
    qgt = jnp.concatenate([qt[g * d:(g + 1) * d] for g in range(NSA_GROUP)], axis=1)
    qgt = qgt.astype(BF16)
    qat = jnp.concatenate([jnp.concatenate([biast_ref[0, 0]] * NSA_GROUP, axis=1), qgt], axis=0)
    init = tuple((jnp.full((1, half), NEG_MASK, F32), jnp.zeros((1, half), F32),
                  jnp.zeros((d, half), F32)) for _ in range(NSA_LANE_SPLIT))

    def query_pos(shape):
        return s0 + (lax.broadcasted_iota(jnp.int32, shape, 1) & (tq - 1))

    halves = range(NSA_LANE_SPLIT)

    def scores(kt):
        k = ka_ref[0, pl.ds(pl.multiple_of(kt * tk, tk), tk), :]
        return tuple(jnp.dot(k, qat[:, hf * half:(hf + 1) * half], preferred_element_type=F32)
                     for hf in halves)

    def values(kt):
        vt = vst_ref[0, :, pl.ds(pl.multiple_of(kt * tk, tk), tk)]
        return tuple(jnp.dot(vt, p_scr[:, hf * half:(hf + 1) * half],
                             preferred_element_type=F32) for hf in halves)

    def sel_step(kt, stats, src_scr, dst_scr, causal):
        s_next = None if causal else scores(kt + 1)
        pv = values(jnp.maximum(kt - 1, 0))
        new_stats = []
        for hf in halves:
            cols = slice(hf * half, (hf + 1) * half)
            m, l, acc = stats[hf]
            s = src_scr[:, cols]
            if causal:
                kp = kt * tk + lax.broadcasted_iota(jnp.int32, s.shape, 0)
                mask = kp <= query_pos(s.shape)
                s = jnp.where(mask, s, NEG_MASK)
            m_new = jnp.maximum(m, jnp.max(s, 0, keepdims=True))
            alpha = jnp.exp2(m - m_new)
            p = jnp.exp2(s - m_new)
            if causal:
                p = jnp.where(mask, p, 0.0)
            new_stats.append((m_new, alpha * l + jnp.sum(p, 0, keepdims=True),
                              alpha * (acc + pv[hf])))
            p_scr[:, cols] = p.astype(BF16)
        if not causal:
            for hf in halves:
                dst_scr[:, hf * half:(hf + 1) * half] = s_next[hf]
        return tuple(new_stats)

    def by_parity(kt, stats, causal):
        return lax.cond((kt & 1) == 0,
                        lambda st: sel_step(kt, st, sa_scr, sb_scr, causal),
                        lambda st: sel_step(kt, st, sb_scr, sa_scr, causal), stats)

    n_full = s0 // tk
    first_scores = scores(0)
    span = WINDOW + tq
    start = pl.multiple_of(jnp.maximum(s0 - WINDOW, 0), tq)
    kwin = kw_ref[0, pl.ds(start, span), :]
    for hf in halves:
        cols = slice(hf * half, (hf + 1) * half)
        sa_scr[:, cols] = first_scores[hf]
        w_scr[:, cols] = jnp.dot(kwin, qgt[:, cols], preferred_element_type=F32)
    p_scr[...] = jnp.zeros_like(p_scr)
    stats = lax.fori_loop(0, n_full, lambda kt, c: by_parity(kt, c, False), init)
    stats = by_parity(n_full, stats, True)
    pv_last = values(n_full)

    vwt = vwt_ref[0, :, pl.ds(start, span)]
    p_w, l_w = [], []
    for hf in halves:
        s = w_scr[:, hf * half:(hf + 1) * half]
        kp = start + lax.broadcasted_iota(jnp.int32, s.shape, 0)
        t = query_pos(s.shape)
        mask = (kp <= t) & (kp > t - WINDOW)
        s = jnp.where(mask, s, NEG_MASK)
        p = jnp.where(mask, jnp.exp2(s - jnp.max(s, 0, keepdims=True)), 0.0)
        l_w.append(jnp.sum(p, 0, keepdims=True))
        p_w.append(p.astype(BF16))
    o_w = jnp.concatenate([jnp.dot(vwt, p_w[hf], preferred_element_type=F32)
                           / jnp.maximum(l_w[hf], 1e-30) for hf in halves], axis=1)
    o_s = jnp.concatenate([(stats[hf][2] + pv_last[hf]) / jnp.maximum(stats[hf][1], 1e-30)
                           for hf in halves], axis=1)

    def rows_layout(ot):
        return jnp.concatenate([ot[:, g * tq:(g + 1) * tq] for g in range(NSA_GROUP)], axis=0).T

    o_s = rows_layout(o_s)
    o_w = rows_layout(o_w)
    gates = jax.nn.sigmoid(gl_ref[0, 0])
    o_c = oc_ref[0]
    pieces = []
    for g in range(NSA_GROUP):
        c = slice(g * d, (g + 1) * d)
        pieces.append(gates[:, 3 * g:3 * g + 1] * o_c[:, c]
                      + gates[:, 3 * g + 1:3 * g + 2] * o_s[:, c]
                      + gates[:, 3 * g + 2:3 * g + 3] * o_w[:, c])
    o_ref[0] = jnp.concatenate(pieces, axis=1)


def _nsa_attend(q, biast, kaug, vst, kw, vwt, o_c, gl, batch, seq, tq=256, tk=512):
    gw = NSA_GROUP * NSA_HEAD_DIM
    tk = min(tk, seq)
    assert tk % tq == 0 and WINDOW % tq == 0 and seq >= WINDOW + tq
    kern = functools.partial(_nsa_attend_kernel, tq=tq, tk=tk)
    bh = lambda b, h, i: (b * NSA_KV_HEADS + h, 0, 0)
    return pl.pallas_call(
        kern,
        grid=(batch, NSA_KV_HEADS, seq // tq),
        in_specs=[
            pl.BlockSpec((1, tq, gw), lambda b, h, i: (b, i, h)),
            pl.BlockSpec((1, 1, SEL_LANES, tq), lambda b, h, i: (b, h, 0, i)),
            pl.BlockSpec((1, seq, SEL_LANES + NSA_HEAD_DIM), bh),
            pl.BlockSpec((1, NSA_HEAD_DIM, seq), bh),
            pl.BlockSpec((1, seq, NSA_HEAD_DIM), bh),
            pl.BlockSpec((1, NSA_HEAD_DIM, seq), bh),
            pl.BlockSpec((1, tq, gw), lambda b, h, i: (b, i, h)),
            pl.BlockSpec((1, 1, tq, NSA_GROUP * 3), lambda b, h, i: (b, h, i, 0)),
        ],
        out_specs=pl.BlockSpec((1, tq, gw), lambda b, h, i: (b, i, h)),
        out_shape=jax.ShapeDtypeStruct((batch, seq, NSA_KV_HEADS * gw), F32),
        scratch_shapes=[pltpu.VMEM((tk, NSA_GROUP * tq), F32),
                        pltpu.VMEM((tk, NSA_GROUP * tq), F32),
                        pltpu.VMEM((tk, NSA_GROUP * tq), BF16),
                        pltpu.VMEM((WINDOW + tq, NSA_GROUP * tq), F32)],
        compiler_params=_params("parallel", "parallel", "arbitrary"),
        name="nsa_selected_window",
    )(q, biast, kaug, vst, kw, vwt, o_c, gl)


def _nsa(q, kvc, kaug, kwin, vst, vwt, misc, pe_k, w1_k, w2_k, pe_v, w1_v, w2_v, batch, seq):
    d = NSA_HEAD_DIM
    bh = batch * NSA_KV_HEADS
    n16 = seq // CMP_STRIDE
    kv16 = jnp.transpose(kvc.reshape(batch, seq, 2, NSA_KV_HEADS, d), (2, 0, 3, 1, 4))
    kv16 = kv16.reshape(2, bh, n16, CMP_STRIDE * d)
    pe = jnp.stack([pe_k, pe_v]).reshape(2, 1, CMP_BLOCK * d)
    pe = jnp.broadcast_to(pe, (2, SUBLANES, CMP_BLOCK * d))
    cmp = _compress(kv16, pe, jnp.stack([w1_k, w1_v]), jnp.stack([w2_k, w2_v]))
    kc, vc = cmp[0], cmp[1]

    n_sel = seq // SEL_BLOCK
    cmp_start = np.arange(n16) * CMP_STRIDE
    sel_start = np.arange(SEL_LANES) * SEL_BLOCK
    overlap = ((cmp_start[:, None] < sel_start[None, :] + SEL_BLOCK)
               & (cmp_start[:, None] + CMP_BLOCK - 1 >= sel_start[None, :])
               & (np.arange(SEL_LANES)[None, :] < n_sel)
               & (np.arange(n16)[:, None] < (seq - CMP_BLOCK) // CMP_STRIDE + 1))
    overlap_t = jnp.asarray(overlap.T, BF16)

    q3 = q.reshape(batch, seq, NSA_Q_W)
    o_c, biast = _nsa_select(q3, kc, vc, overlap_t, batch, seq)

    gl = misc[:, :NSA_HEADS * 3].reshape(batch, seq, NSA_KV_HEADS, NSA_GROUP * 3)
    gl = jnp.moveaxis(gl, 2, 1)
    per_head = lambda a: a.reshape((bh,) + a.shape[2:])
    o = _nsa_attend(q3, biast, per_head(kaug), per_head(vst), per_head(kwin), per_head(vwt), o_c,
                    gl, batch, seq)
    return o.reshape(batch * seq, NSA_Q_W)


def _ssd_kernel(xbc_ref, halo_ref, z_ref, dt_ref, cw_ref, cb_ref, dtb_ref, alog_ref, dskip_ref,
                nw_ref, tril_ref, spread_ref, o_ref, state_scr, y_scr):
    c = pl.program_id(1)
    l = SSD_CHUNK

    @pl.when(c == 0)
    def _():
        state_scr[...] = jnp.zeros_like(state_scr)

    x = xbc_ref[0]
    halo = jnp.where(c == 0, 0.0, halo_ref[0])
    xx = jnp.concatenate([halo, x], axis=0)
    cw = cw_ref[...]
    conv = cb_ref[...]
    for k in range(SSD_CONV):
        off = SUBLANES - (SSD_CONV - 1) + k
        conv = conv + cw[k:k + 1] * xx[off:off + l]
    xbc = _silu(conv)
    xs = xbc[:, :SSD_D_INNER]
    gn = SSD_GROUPS * SSD_STATE
    bmat = xbc[:, SSD_D_INNER:SSD_D_INNER + gn]
    cmat = xbc[:, SSD_D_INNER + gn:]

    dt = _softplus(dt_ref[0] + dtb_ref[...])
    da = dt * (-jnp.exp(alog_ref[...]))
    a_cs = _dot_x3_left(tril_ref[...], da)
    a_cs_t = a_cs.T
    a_last = a_cs[l - 1:l]
    causal = (lax.broadcasted_iota(jnp.int32, (l, l), 0)
              >= lax.broadcasted_iota(jnp.int32, (l, l), 1))

    spread = spread_ref[...]
    dt_x = _dot_x3(dt, spread)
    grow_x = _dot_x3(jnp.exp(a_cs), spread)
    fade_x = _dot_x3(jnp.exp(a_last - a_cs), spread)
    chunk_x = _dot_x3(jnp.broadcast_to(jnp.exp(a_last), (SUBLANES, LANES)), spread)[0:1]
    xd = xs * dt_x
    xd16 = xd.astype(BF16)
    fxd16 = (xd * fade_x).astype(BF16)

    pairs = range(SSD_HEADS // 2)
    pairs_per_group = len(pairs) // SSD_GROUPS
    lanes = {c: slice(c * LANES, (c + 1) * LANES) for c in pairs}
    cb, y_off = {}, {}
    for g in range(SSD_GROUPS):
        bg = bmat[:, g * SSD_STATE:(g + 1) * SSD_STATE]
        cg16 = cmat[:, g * SSD_STATE:(g + 1) * SSD_STATE].astype(BF16)
        cb[g] = _dot_nt(cg16, bg.astype(BF16))
        bgt16 = bg.T.astype(BF16)
        for c in range(g * pairs_per_group, (g + 1) * pairs_per_group):
            st = state_scr[c]
            y_off[c] = jnp.dot(cg16, st.astype(BF16), preferred_element_type=F32)
            new = jnp.dot(bgt16, fxd16[:, lanes[c]], preferred_element_type=F32)
            state_scr[c] = st * chunk_x[:, lanes[c]] + new
    first_head = lax.broadcasted_iota(jnp.int32, (l, LANES), 1) < SSD_HEAD_DIM
    y_diag = {}
    for c in pairs:
        for hh in range(2):
            h = 2 * c + hh
            seg = jnp.where(causal, jnp.exp(a_cs[:, h:h + 1] - a_cs_t[h:h + 1, :]), 0.0)
            y_diag[h] = jnp.dot((cb[c // pairs_per_group] * seg).astype(BF16), xd16[:, lanes[c]],
                                preferred_element_type=F32)
    for c in pairs:
        y_scr[:, lanes[c]] = (jnp.where(first_head, y_diag[2 * c], y_diag[2 * c + 1])
                              + y_off[c] * grow_x[:, lanes[c]])

    y = (y_scr[...] + xs * dskip_ref[...]) * _silu(z_ref[0])
    gw = SSD_D_INNER // SSD_GROUPS
    outs = []
    for g in range(SSD_GROUPS):
        yg = y[:, g * gw:(g + 1) * gw]
        outs.append(yg * lax.rsqrt(jnp.mean(yg * yg, -1, keepdims=True) + SSD_NORM_EPS))
    o_ref[0] = jnp.concatenate(outs, axis=1) * nw_ref[...]


def _pad_lanes(v, width=LANES):
    v = v.reshape(1, -1).astype(F32)
    return jnp.pad(v, ((0, 0), (0, width - v.shape[1])))


def _ssd(z, xbc, misc, conv_w, conv_b, dt_bias, a_log, d_skip, norm_w, batch, seq):
    l = SSD_CHUNK
    nc = seq // l
    z3 = z.reshape(batch, seq, SSD_D_INNER)
    x3 = xbc.reshape(batch, seq, SSD_XBC)
    dt = misc[:, NSA_HEADS * 3:NSA_HEADS * 3 + SSD_HEADS]
    dt3 = jnp.pad(dt, ((0, 0), (0, LANES - SSD_HEADS))).reshape(batch, seq, LANES)
    tril = jnp.asarray(np.tril(np.ones((l, l))), BF16)
    spread = np.zeros((LANES, SSD_D_INNER), np.float32)
    spread[np.arange(SSD_D_INNER) // SSD_HEAD_DIM, np.arange(SSD_D_INNER)] = 1.0
    spread = jnp.asarray(spread, BF16)
    hb = l // SUBLANES
    const = lambda b, c: (0, 0)
    return pl.pallas_call(
        _ssd_kernel,
        grid=(batch, nc),
        in_specs=[
            pl.BlockSpec((1, l, SSD_XBC), lambda b, c: (b, c, 0)),
            pl.BlockSpec((1, SUBLANES, SSD_XBC), lambda b, c: (b, jnp.maximum(c * hb - 1, 0), 0)),
            pl.BlockSpec((1, l, SSD_D_INNER), lambda b, c: (b, c, 0)),
            pl.BlockSpec((1, l, LANES), lambda b, c: (b, c, 0)),
            pl.BlockSpec((SSD_CONV, SSD_XBC), const),
            pl.BlockSpec((1, SSD_XBC), const),
            pl.BlockSpec((1, LANES), const),
            pl.BlockSpec((1, LANES), const),
            pl.BlockSpec((1, SSD_D_INNER), const),
            pl.BlockSpec((1, SSD_D_INNER), const),
            pl.BlockSpec((l, l), const),
            pl.BlockSpec((LANES, SSD_D_INNER), const),
        ],
        out_specs=pl.BlockSpec((1, l, SSD_D_INNER), lambda b, c: (b, c, 0)),
        out_shape=jax.ShapeDtypeStruct((batch, seq, SSD_D_INNER), F32),
        scratch_shapes=[pltpu.VMEM((SSD_HEADS // 2, SSD_STATE, 2 * SSD_HEAD_DIM), F32),
                        pltpu.VMEM((l, SSD_D_INNER), F32)],
        compiler_params=_params("parallel", "arbitrary"),
        name="ssd_chunk_scan",
    )(x3, x3, z3, dt3, conv_w.reshape(SSD_CONV, SSD_XBC), conv_b.reshape(1, SSD_XBC),
      _pad_lanes(dt_bias), _pad_lanes(a_log),
      jnp.repeat(d_skip.astype(F32), SSD_HEAD_DIM).reshape(1, SSD_D_INNER),
      norm_w.reshape(1, SSD_D_INNER), tril, spread).reshape(batch * seq, SSD_D_INNER)


def _outproj_kernel(*refs, n_parts):
    x_ref, g_ref = refs[0], refs[1]
    parts = refs[2:2 + n_parts]
    ws = refs[2 + n_parts:2 + 2 * n_parts]
    o_ref = refs[2 + 2 * n_parts]
    acc = None
    for p_ref, w_ref in zip(parts, ws):
        d = jnp.dot(p_ref[...].astype(BF16), w_ref[...], preferred_element_type=F32)
        acc = d if acc is None else acc + d
    o_ref[...] = x_ref[...] + _rms(acc, g_ref[...], NORM_EPS)


def _outproj(x, g, parts, weights, tm=512):
    t, d = x.shape
    row = lambda i: (i, 0)
    const = lambda i: (0, 0)
    n = len(parts)
    return pl.pallas_call(
        functools.partial(_outproj_kernel, n_parts=n),
        grid=(t // tm,),
        in_specs=([pl.BlockSpec((tm, d), row), pl.BlockSpec((1, d), const)]
                  + [pl.BlockSpec((tm, p.shape[1]), row) for p in parts]
                  + [pl.BlockSpec(w.shape, const) for w in weights]),
        out_specs=pl.BlockSpec((tm, d), row),
        out_shape=jax.ShapeDtypeStruct((t, d), F32),
        compiler_params=_params("parallel"),
        name="mixer_out_proj",
    )(x, g.reshape(1, d), *parts, *[w.astype(BF16) for w in weights])


def _dot_x2(a, b):
    a1 = a.astype(BF16)
    a2 = (a - a1.astype(F32)).astype(BF16)
    return (jnp.dot(a1, b, preferred_element_type=F32)
            + jnp.dot(a2, b, preferred_element_type=F32))


def _head_sum(x, seg, seg_t):
    return _dot_x2(_dot_x2(x, seg), seg_t)


def _rwkv_pre_kernel(x_ref, halo_ref, g_ref, mu_ref, wr_ref, wk_ref, wv_ref, w0_ref, w1_ref,
                     w2_ref, a0_ref, a1_ref, a2_ref, g1_ref, g2_ref, kk_ref, ka_ref, seg_ref,
                     segt_ref, r_out, ld_out, k_out, v_out, kk_out, g_out, bt_out, kt_out, ldt_out,
                     *, tiles_per_seq):
    i = pl.program_id(0)
    h = _rms(x_ref[...], g_ref[...], NORM_EPS)
    prev_row = _rms(halo_ref[...], g_ref[...], NORM_EPS)[SUBLANES - 1:SUBLANES]
    prev_row = jnp.where(i % tiles_per_seq == 0, 0.0, prev_row)
    rowid = lax.broadcasted_iota(jnp.int32, h.shape, 0)
    prev = jnp.where(rowid == 0, prev_row, pltpu.roll(h, 1, 0))
    xx = prev - h
    mu = mu_ref[...]
    mix = lambda j: (h + xx * mu[j:j + 1]).astype(BF16)
    dot = lambda a, w_ref: jnp.dot(a, w_ref[...], preferred_element_type=F32)
    r = dot(mix(0), wr_ref)
    w = -_softplus(-(w0_ref[...] + dot(jnp.tanh(dot(mix(1), w1_ref)).astype(BF16), w2_ref))) - 0.5
    k = dot(mix(2), wk_ref)
    v = dot(mix(3), wv_ref)
    a = jax.nn.sigmoid(a0_ref[...] + dot(dot(mix(4), a1_ref).astype(BF16), a2_ref))
    g = dot(jax.nn.sigmoid(dot(mix(5), g1_ref)).astype(BF16), g2_ref)
    kk = k * kk_ref[...]
    norm = jnp.sqrt(_head_sum(kk * kk, seg_ref[...], segt_ref[...]))
    kk = kk / jnp.maximum(norm, 1e-12)
    k = k * (1.0 + (a - 1.0) * ka_ref[...])
    log_decay = -jnp.exp(w)
    r_out[...] = r
    ld_out[...] = log_decay
    k_out[...] = k
    v_out[...] = v
    kk_out[...] = kk
    g_out[...] = g
    bt_out[0] = (kk * a).T
    kt_out[0] = k.T
    ldt_out[0] = log_decay.T


def _pad_cols(w, width):
    return jnp.pad(w, ((0, 0), (0, width - w.shape[1])))


def _pad_rows(w, width):
    return jnp.pad(w, ((0, width - w.shape[0]), (0, 0)))


def _seg_matrices():
    seg = np.zeros((D_MODEL, LANES), np.float32)
    seg[np.arange(D_MODEL), np.arange(D_MODEL) // RWKV_HEAD_DIM] = 1.0
    return jnp.asarray(seg, BF16), jnp.asarray(seg.T, BF16)


def _rwkv_pre(x, g, mu, w_r, w_k, w_v, w0, w1, w2, a0, a1, a2, g1, g2, k_k, k_a, seq, tm=256):
    t, d = x.shape
    lora = lambda w: -(-w.shape[1] // LANES) * LANES
    w1p, w2p = _pad_cols(w1, lora(w1)), _pad_rows(w2, lora(w1))
    a1p, a2p = _pad_cols(a1, lora(a1)), _pad_rows(a2, lora(a1))
    g1p, g2p = _pad_cols(g1, lora(g1)), _pad_rows(g2, lora(g1))
    seg, seg_t = _seg_matrices()
    row = lambda i: (i, 0)
    const = lambda i: (0, 0)
    hb = tm // SUBLANES
    vec = lambda v: v.reshape(1, d)
    mats = [w.astype(BF16) for w in (w_r, w_k, w_v)]
    ins = [x, x, vec(g), mu, *mats, vec(w0), w1p.astype(BF16), w2p.astype(BF16), vec(a0),
           a1p.astype(BF16), a2p.astype(BF16), g1p.astype(BF16), g2p.astype(BF16), vec(k_k),
           vec(k_a), seg, seg_t]
    in_specs = [pl.BlockSpec((tm, d), row),
                pl.BlockSpec((SUBLANES, d), lambda i: (jnp.maximum(i * hb - 1, 0), 0))]
    in_specs += [pl.BlockSpec(a.shape, const) for a in ins[2:]]
    tps = seq // tm
    col = pl.BlockSpec((1, d, tm), lambda i: (i // tps, 0, i % tps))
    return pl.pallas_call(
        functools.partial(_rwkv_pre_kernel, tiles_per_seq=tps),
        grid=(t // tm,),
        in_specs=in_specs,
        out_specs=[pl.BlockSpec((tm, d), row)] * 6 + [col] * 3,
        out_shape=([jax.ShapeDtypeStruct((t, d), F32)] * 6
                   + [jax.ShapeDtypeStruct((t // seq, d, seq), F32)] * 3),
        compiler_params=_params("parallel"),
        name="rwkv7_projections",
    )(*ins)


RWKV_CHUNK = 128


def _rwkv_chunk_kernel(r_ref, ld_ref, kk_ref, v_ref, bt_ref, kt_ref, ldt_ref, tril_ref, triu_ref,
                       y_ref, state_scr):
    @pl.when(pl.program_id(1) == 0)
    def _():
        state_scr[...] = jnp.zeros_like(state_scr)

    l = RWKV_CHUNK
    hd = RWKV_HEAD_DIM
    tril = tril_ref[...]
    ld = ld_ref[0]
    c_in = _dot_x3_left(tril, ld)
    a_bar = -kk_ref[0] * jnp.exp(c_in - ld)
    r_bar = r_ref[0] * jnp.exp(c_in)
    v = v_ref[0]
    ldt = ldt_ref[0]
    c_t = _dot_x3(ldt, triu_ref[...])
    scale_t = jnp.exp(-c_t)
    b_t = bt_ref[0] * scale_t
    k_t = kt_ref[0] * scale_t
    decay_col = jnp.exp(c_t[:, l - 1:l])

    row = lax.broadcasted_iota(jnp.int32, (l, l), 0)
    colx = lax.broadcasted_iota(jnp.int32, (l, l), 1)
    strict = row > colx
    incl = row >= colx
    lane = lax.broadcasted_iota(jnp.int32, (l, LANES), 1)
    first_head = lane < hd
    blockdiag = (lax.broadcasted_iota(jnp.int32, (LANES, LANES), 0) < hd) == (
        lax.broadcasted_iota(jnp.int32, (LANES, LANES), 1) < hd)
    mm = lambda a, b: jnp.dot(a, b, preferred_element_type=F32)
    b16 = lambda a: a.astype(BF16)

    pairs = range(D_MODEL // LANES)
    heads = [(c, hh) for c in pairs for hh in range(2)]
    lanes = {c: slice(c * LANES, (c + 1) * LANES) for c in pairs}
    v16 = {c: b16(v[:, lanes[c]]) for c in pairs}
    bk_t = {c: b16(jnp.concatenate([b_t[lanes[c]], k_t[lanes[c]]], axis=1)) for c in pairs}
    h2 = {c: state_scr[c] for c in pairs}
    gh = {}
    for c in pairs:
        rhs = jnp.concatenate([bk_t[c], b16(h2[c])], axis=1)
        a_p, r_p = a_bar[:, lanes[c]], r_bar[:, lanes[c]]
        for hh in range(2):
            keep = first_head if hh == 0 else ~first_head
            x = jnp.concatenate([jnp.where(keep, a_p, 0.0), jnp.where(keep, r_p, 0.0)], axis=0)
            gh[c, hh] = mm(b16(x), rhs)
    mp, u, p_r = {}, {}, {}
    for c, hh in heads:
        g = gh[c, hh]
        mp[c, hh] = b16(jnp.where(strict, g[:l, :l], 0.0))
        m_ak = b16(jnp.where(strict, g[:l, l:2 * l], 0.0))
        p_r[c, hh] = b16(jnp.concatenate([jnp.where(incl, g[l:, :l], 0.0),
                                          jnp.where(incl, g[l:, l:2 * l], 0.0)], axis=1))
        u[c, hh] = g[:l, 2 * l:] + mm(m_ak, v16[c])
    n_factors = l.bit_length() - 1
    for f in range(n_factors):
        du = {h: mm(mp[h], b16(u[h])) for h in heads}
        if f + 1 < n_factors:
            mp = {h: b16(mm(mp[h], mp[h])) for h in heads}
        u = {h: u[h] + du[h] for h in heads}
    ys = {h: gh[h][l:, 2 * l:] + mm(p_r[h], jnp.concatenate([b16(u[h]), v16[h[0]]], axis=0))
          for h in heads}
    for c in pairs:
        u_pair = jnp.where(first_head, u[c, 0], u[c, 1])
        y_ref[0, :, lanes[c]] = jnp.where(first_head, ys[c, 0], ys[c, 1])
        upd = h2[c] + mm(bk_t[c], jnp.concatenate([b16(u_pair), v16[c]], axis=0))
        state_scr[c] = jnp.where(blockdiag, upd * decay_col[lanes[c]], 0.0)


def _rwkv_scan(r, ld, kk, v, bt, kt, ldt, batch, seq):
    l = RWKV_CHUNK
    d = D_MODEL
    rows = lambda x: x.reshape(batch, seq, d)
    rblk = pl.BlockSpec((1, l, d), lambda b, c: (b, c, 0))
    cblk = pl.BlockSpec((1, d, l), lambda b, c: (b, 0, c))
    tril = jnp.asarray(np.tril(np.ones((l, l))), BF16)
    y = pl.pallas_call(
        _rwkv_chunk_kernel,
        grid=(batch, seq // l),
        in_specs=[rblk] * 4 + [cblk] * 3 + [pl.BlockSpec((l, l), lambda b, c: (0, 0))] * 2,
        out_specs=rblk,
        out_shape=jax.ShapeDtypeStruct((batch, seq, d), F32),
        scratch_shapes=[pltpu.VMEM((d // LANES, LANES, LANES), F32)],
        compiler_params=_params("parallel", "arbitrary"),
        name="rwkv7_recurrence",
    )(rows(r), rows(ld), rows(kk), rows(v), bt, kt, ldt, tril, tril.T)
    return y.reshape(batch * seq, d)


def _rwkv_post_kernel(x_ref, y_ref, r_ref, k_ref, v_ref, g_ref, lng_ref, lnb_ref, rk_ref, wo_ref,
                      gn_ref, seg_ref, segt_ref, o_ref):
    seg, seg_t = seg_ref[...], segt_ref[...]
    y = y_ref[...]
    inv = 1.0 / RWKV_HEAD_DIM
    mean = _head_sum(y, seg, seg_t) * inv
    yc = y - mean
    var = _head_sum(yc * yc, seg, seg_t) * inv
    yn = yc * lax.rsqrt(var + RWKV_GN_EPS) * lng_ref[...] + lnb_ref[...]
    bonus = _head_sum(r_ref[...] * k_ref[...] * rk_ref[...], seg, seg_t) * v_ref[...]
    out = ((yn + bonus) * g_ref[...]).astype(BF16)
    proj = jnp.dot(out, wo_ref[...], preferred_element_type=F32)
    o_ref[...] = x_ref[...] + _rms(proj, gn_ref[...], NORM_EPS)


def _rwkv_post(x, y, r, k, v, g, ln_g, ln_b, r_k, w_o, gn, tm=256):
    t, d = x.shape
    seg, seg_t = _seg_matrices()
    row = lambda i: (i, 0)
    const = lambda i: (0, 0)
    vec = lambda a: a.reshape(1, d)
    small = [vec(ln_g), vec(ln_b), vec(r_k), w_o.astype(BF16), vec(gn), seg, seg_t]
    return pl.pallas_call(
        _rwkv_post_kernel,
        grid=(t // tm,),
        in_specs=[pl.BlockSpec((tm, d), row)] * 6 + [pl.BlockSpec(a.shape, const) for a in small],
        out_specs=pl.BlockSpec((tm, d), row),
        out_shape=jax.ShapeDtypeStruct((t, d), F32),
        compiler_params=_params("parallel"),
        name="rwkv7_output",
    )(x, y, r, k, v, g, *small)


def _nsa_ssd_mixer(x, g_pre, g_post, cos, sin, w_in, pe_k, w1_k, w2_k, pe_v, w1_v, w2_v, conv_w,
                   conv_b, dt_bias, a_log, d_skip, norm_w, w_out, batch, seq):
    q, kvc, kaug, kwin, vst, vwt, z, xbc, misc = _inproj(x, g_pre, w_in, cos, sin, batch, seq)
    o_a = _nsa(q, kvc, kaug, kwin, vst, vwt, misc, pe_k, w1_k, w2_k, pe_v, w1_v, w2_v, batch, seq)
    o_b = _ssd(z, xbc, misc, conv_w, conv_b, dt_bias, a_log, d_skip, norm_w, batch, seq)
    return _outproj(x, g_post, [o_a, o_b], [w_out[:NSA_Q_W], w_out[NSA_Q_W:]])


def _rwkv7_mixer(x, g_pre, g_post, mu, w_r, w_k, w_v, w_o, w0, w1, w2, a0, a1, a2, g1, g2, k_k,
                 k_a, r_k, ln_g, ln_b, batch, seq):
    r, ld, k, v, kk, g, bt, kt, ldt = _rwkv_pre(x, g_pre, mu, w_r, w_k, w_v, w0, w1, w2, a0, a1,
                                                a2, g1, g2, k_k, k_a, seq)
    y = _rwkv_scan(r, ld, kk, v, bt, kt, ldt, batch, seq)
    return _rwkv_post(x, y, r, k, v, g, ln_g, ln_b, r_k, w_o, g_post)


def kernel(x, norm_gains, ffn1_w_gate, ffn1_w_up, ffn1_w_down, ffn2_w_gate, ffn2_w_up, ffn2_w_down, ab_w_in, a_cmp_pe_k, a_cmp_w1_k, a_cmp_w2_k, a_cmp_pe_v, a_cmp_w1_v, a_cmp_w2_v, b_conv_w, b_conv_b, b_dt_bias, b_a_log, b_d_skip, b_norm_w, ab_w_out, c_mu, c_w_r, c_w_k, c_w_v, c_w_o, c_w0, c_w1, c_w2, c_a0, c_a1, c_a2, c_g1, c_g2, c_k_k, c_k_a, c_r_k, c_ln_g, c_ln_b):
    batch, seq, d = x.shape
    depth = norm_gains.shape[0]
    cos, sin = _rope_tables(seq)
    x = x.reshape(batch * seq, d)
    (ffn1_w_gate, ffn1_w_up, ffn1_w_down, ffn2_w_gate, ffn2_w_up, ffn2_w_down) = [
        w.astype(BF16) for w in (ffn1_w_gate, ffn1_w_up, ffn1_w_down, ffn2_w_gate, ffn2_w_up,
                                 ffn2_w_down)]
    for layer in range(depth):
        ng = norm_gains[layer]
        x = _ffn(x, ng[0], ng[1], ffn1_w_gate[layer], ffn1_w_up[layer], ffn1_w_down[layer])
        i = layer // 2
        if layer % 2 == 0:
            x = _nsa_ssd_mixer(x, ng[2], ng[3], cos, sin, ab_w_in[i], a_cmp_pe_k[i], a_cmp_w1_k[i],
                               a_cmp_w2_k[i], a_cmp_pe_v[i], a_cmp_w1_v[i], a_cmp_w2_v[i],
                               b_conv_w[i], b_conv_b[i], b_dt_bias[i], b_a_log[i], b_d_skip[i],
                               b_norm_w[i], ab_w_out[i], batch, seq)
        else:
            x = _rwkv7_mixer(x, ng[2], ng[3], c_mu[i], c_w_r[i], c_w_k[i], c_w_v[i], c_w_o[i],
                             c_w0[i], c_w1[i], c_w2[i], c_a0[i], c_a1[i], c_a2[i], c_g1[i],
                             c_g2[i], c_k_k[i], c_k_a[i], c_r_k[i], c_ln_g[i], c_ln_b[i],
                             batch, seq)
        x = _ffn(x, ng[4], ng[5], ffn2_w_gate[layer], ffn2_w_up[layer], ffn2_w_down[layer])
    return x.reshape(batch, seq, d)
```

```python
import functools

import jax
import jax.numpy as jnp
import numpy as np
from jax import lax
from jax.experimental import pallas as pl
from jax.experimental.pallas import tpu as pltpu

F32 = jnp.float32
BF16 = jnp.bfloat16
HIGHEST = lax.Precision.HIGHEST

D_MODEL = 1024
D_FF = 2816
NORM_EPS = 1e-6
NSA_HEADS = 8
NSA_KV_HEADS = 2
NSA_GROUP = NSA_HEADS // NSA_KV_HEADS
NSA_HEAD_DIM = 64
CMP_BLOCK = 32
CMP_STRIDE = 16
SEL_BLOCK = 64
SEL_TOPK = 16
WINDOW = 512
ROPE_THETA = 10000.0
FORCE_SCORE = 1e4
SEL_LANES = 128
SSD_HEADS = 16
SSD_HEAD_DIM = 64
SSD_D_INNER = SSD_HEADS * SSD_HEAD_DIM
SSD_GROUPS = 2
SSD_STATE = 128
SSD_CONV = 4
SSD_CHUNK = 128
SSD_NORM_EPS = 1e-5
SSD_XBC = SSD_D_INNER + 2 * SSD_GROUPS * SSD_STATE
RWKV_HEAD_DIM = 64
RWKV_HEADS = D_MODEL // RWKV_HEAD_DIM
RWKV_GN_EPS = 64e-5

NSA_Q_W = NSA_HEADS * NSA_HEAD_DIM
NSA_KV_W = NSA_KV_HEADS * NSA_HEAD_DIM
IN_SPLITS = (NSA_Q_W, NSA_KV_W, NSA_KV_W, NSA_KV_W, NSA_KV_W, NSA_KV_W, NSA_KV_W,
             NSA_HEADS * 3, SSD_D_INNER, SSD_XBC, SSD_HEADS)
IN_WIDTH = sum(IN_SPLITS)

LANES = 128
SUBLANES = 8
MXU_TILE = 256
VMEM_LIMIT_BYTES = 56 * 1024 * 1024

FFN_ROWS = 512
FFN_COLUMN_CHUNKS = 2
PROJ_ROWS = 256
NSA_QUERY_ROWS = 256
NSA_KEY_ROWS = 512

NEG_MASK = -1e30
NEG_UNSELECTED = -2.0 ** 30
NEG_TAKEN = -3e38
LOG2_E = 1.4426950408889634


def _params(*sem):
    return pltpu.CompilerParams(dimension_semantics=sem, vmem_limit_bytes=VMEM_LIMIT_BYTES)


def _rms(x, g, eps):
    return x * lax.rsqrt(jnp.mean(x * x, -1, keepdims=True) + eps) * g


def _silu(x):
    return x * jax.nn.sigmoid(x)


def _softplus(x):
    return jnp.maximum(x, 0.0) + jnp.log1p(jnp.exp(-jnp.abs(x)))


def _split3(a):
    a1 = a.astype(BF16)
    r1 = a - a1.astype(F32)
    a2 = r1.astype(BF16)
    a3 = (r1 - a2.astype(F32)).astype(BF16)
    return a1, a2, a3


def _dot_x3(a, b):
    acc = None
    for piece in _split3(a):
        d = jnp.dot(piece, b, preferred_element_type=F32)
        acc = d if acc is None else acc + d
    return acc


def _dot_x3_left(b, a):
    acc = None
    for piece in _split3(a):
        d = jnp.dot(b, piece, preferred_element_type=F32)
        acc = d if acc is None else acc + d
    return acc


def _dot_nt(a, b, **kw):
    return lax.dot_general(a, b, (((1,), (1,)), ((), ())), preferred_element_type=F32, **kw)


def _ffn_kernel(x_ref, gi_ref, go_ref, wg_ref, wu_ref, wd_ref, *rest, chunks, n_parts):
    o_ref = rest[-1]
    x = x_ref[...]
    if n_parts:
        proj = None
        for p_ref, w_ref in zip(rest[1:1 + n_parts], rest[1 + n_parts:1 + 2 * n_parts]):
            dd = jnp.dot(p_ref[...].astype(BF16), w_ref[...], preferred_element_type=F32)
            proj = dd if proj is None else proj + dd
        x = x + _rms(proj, rest[0][...], NORM_EPS)
    h = _rms(x, gi_ref[...], NORM_EPS).astype(BF16)
    acc = None
    for lo, hi in chunks:
        gate = jnp.dot(h, wg_ref[:, lo:hi], preferred_element_type=F32)
        up = jnp.dot(h, wu_ref[:, lo:hi], preferred_element_type=F32)
        act = (_silu(gate) * up).astype(BF16)
        part = jnp.dot(act, wd_ref[lo:hi, :], preferred_element_type=F32)
        acc = part if acc is None else acc + part
    o_ref[...] = x + 0.5 * _rms(acc, go_ref[...], NORM_EPS)


def _ffn(x, g_in, g_out, w_gate, w_up, w_down, mixer=None, tm=FFN_ROWS,
         n_chunks=FFN_COLUMN_CHUNKS):
    t, d = x.shape
    f = w_gate.shape[1]
    tiles = f // MXU_TILE
    assert tiles * MXU_TILE == f
    cuts = [MXU_TILE * ((tiles * c + n_chunks - 1) // n_chunks) for c in range(n_chunks + 1)]
    chunks = tuple(zip(cuts[:-1], cuts[1:]))
    row = lambda i: (i, 0)
    const = lambda i: (0, 0)
    resident = lambda shape: pl.BlockSpec(shape, const, pipeline_mode=pl.Buffered(1))
    operands = [x, g_in.reshape(1, d), g_out.reshape(1, d), w_gate.astype(BF16),
                w_up.astype(BF16), w_down.astype(BF16)]
    in_specs = [pl.BlockSpec((tm, d), row), pl.BlockSpec((1, d), const),
                pl.BlockSpec((1, d), const), resident((d, f)), resident((d, f)),
                resident((f, d))]
    n_parts = 0
    if mixer is not None:
        g_mixer, parts, weights = mixer
        n_parts = len(parts)
        operands += [g_mixer.reshape(1, d), *parts, *[w.astype(BF16) for w in weights]]
        in_specs += ([pl.BlockSpec((1, d), const)]
                     + [pl.BlockSpec((tm, p.shape[1]), row) for p in parts]
                     + [resident(w.shape) for w in weights])
    return pl.pallas_call(
        functools.partial(_ffn_kernel, chunks=chunks, n_parts=n_parts),
        grid=(t // tm,),
        in_specs=in_specs,
        out_specs=pl.BlockSpec((tm, d), row),
        out_shape=jax.ShapeDtypeStruct((t, d), F32),
        compiler_params=_params("parallel"),
        name="ffn_half_step",
    )(*operands)


INPROJ_MISC_W = 256
INPROJ_KV_W = 6 * NSA_KV_W
INPROJ_WIDTH = NSA_Q_W + INPROJ_KV_W + SSD_D_INNER + SSD_XBC + INPROJ_MISC_W


def _swap_halves(x):
    w = x.shape[-1]
    lane = lax.broadcasted_iota(jnp.int32, x.shape, x.ndim - 1)
    low = (lane & (NSA_HEAD_DIM - 1)) < (NSA_HEAD_DIM // 2)
    return jnp.where(low, pltpu.roll(x, w - NSA_HEAD_DIM // 2, x.ndim - 1),
                     pltpu.roll(x, NSA_HEAD_DIM // 2, x.ndim - 1))


def _inproj_kernel(x_ref, g_ref, w_ref, cos_ref, sin_ref, q_ref, kvc_ref, kaug_ref, kwin_ref,
                   vst_ref, vwt_ref, z_ref, xbc_ref, misc_ref, *, tiles_per_seq):
    h = _rms(x_ref[...], g_ref[...], NORM_EPS).astype(BF16)
    proj = jnp.dot(h, w_ref[...], preferred_element_type=F32)
    tm = proj.shape[0]
    cos = cos_ref[...]
    sin = sin_ref[...]
    q = proj[:, :NSA_Q_W]
    cos_q = jnp.concatenate([cos] * (NSA_Q_W // LANES), axis=1)
    sin_q = jnp.concatenate([sin] * (NSA_Q_W // LANES), axis=1)
    q_ref[...] = (q * cos_q + _swap_halves(q) * sin_q) * (NSA_HEAD_DIM ** -0.5)
    piece = lambda i: proj[:, NSA_Q_W + i * NSA_KV_W:NSA_Q_W + (i + 1) * NSA_KV_W]
    rope = lambda p: p * cos + _swap_halves(p) * sin
    kvc_ref[:, :NSA_KV_W] = rope(piece(0))
    kvc_ref[:, NSA_KV_W:] = piece(1)
    k_sel, k_win = rope(piece(2)), rope(piece(4))
    pos = (pl.program_id(0) % tiles_per_seq) * tm + lax.broadcasted_iota(
        jnp.int32, (tm, SEL_LANES), 0)
    block_id = lax.shift_right_logical(pos, SEL_BLOCK.bit_length() - 1)
    onehot = jnp.where(lax.broadcasted_iota(jnp.int32, (tm, SEL_LANES), 1) == block_id,
                       1.0, 0.0).astype(BF16)
    vst = piece(3).T
    vwt = piece(5).T
    d = NSA_HEAD_DIM
    for hh in range(NSA_KV_HEADS):
        kaug_ref[0, hh] = jnp.concatenate([onehot, k_sel[:, hh * d:(hh + 1) * d].astype(BF16)],
                                          axis=1)
        kwin_ref[0, hh] = k_win[:, hh * d:(hh + 1) * d].astype(BF16)
        vst_ref[0, hh] = vst[hh * d:(hh + 1) * d].astype(BF16)
        vwt_ref[0, hh] = vwt[hh * d:(hh + 1) * d].astype(BF16)
    o = NSA_Q_W + INPROJ_KV_W
    z_ref[...] = proj[:, o:o + SSD_D_INNER]
    o += SSD_D_INNER
    xbc_ref[...] = proj[:, o:o + SSD_XBC]
    o += SSD_XBC
    misc_ref[...] = proj[:, o:o + INPROJ_MISC_W]


def _inproj(x, g, w_in, cos, sin, batch, seq, tm=PROJ_ROWS):
    t, d = x.shape
    offs = np.cumsum(IN_SPLITS)[:-1].tolist()
    q, kc, vc, ks, vs, kw, vw, gl, z, xbc, dt = jnp.split(w_in, offs, -1)
    pad = jnp.zeros((d, INPROJ_MISC_W - gl.shape[1] - dt.shape[1]), w_in.dtype)
    w = jnp.concatenate([q, kc, vc, ks, vs, kw, vw, z, xbc, gl, dt, pad], -1).astype(BF16)
    assert w.shape[1] == INPROJ_WIDTH
    nseq = seq // tm
    row = lambda i: (i, 0)
    const = lambda i: (0, 0)
    hd, hkv = NSA_HEAD_DIM, NSA_KV_HEADS
    by_head_rows = lambda wd: pl.BlockSpec((1, hkv, tm, wd), lambda i: (i // nseq, 0, i % nseq, 0))
    by_head_cols = pl.BlockSpec((1, hkv, hd, tm), lambda i: (i // nseq, 0, 0, i % nseq))
    flat = lambda wd: (pl.BlockSpec((tm, wd), row), jax.ShapeDtypeStruct((t, wd), F32))
    outs = [
        flat(NSA_Q_W),
        flat(2 * NSA_KV_W),
        (by_head_rows(SEL_LANES + hd), jax.ShapeDtypeStruct((batch, hkv, seq, SEL_LANES + hd), BF16)),
        (by_head_rows(hd), jax.ShapeDtypeStruct((batch, hkv, seq, hd), BF16)),
        (by_head_cols, jax.ShapeDtypeStruct((batch, hkv, hd, seq), BF16)),
        (by_head_cols, jax.ShapeDtypeStruct((batch, hkv, hd, seq), BF16)),
        flat(SSD_D_INNER),
        flat(SSD_XBC),
        flat(INPROJ_MISC_W),
    ]
    return pl.pallas_call(
        functools.partial(_inproj_kernel, tiles_per_seq=nseq),
        grid=(t // tm,),
        in_specs=[
            pl.BlockSpec((tm, d), row),
            pl.BlockSpec((1, d), const),
            pl.BlockSpec((d, INPROJ_WIDTH), const),
            pl.BlockSpec((tm, LANES), lambda i: (i % nseq, 0)),
            pl.BlockSpec((tm, LANES), lambda i: (i % nseq, 0)),
        ],
        out_specs=[o[0] for o in outs],
        out_shape=[o[1] for o in outs],
        compiler_params=_params("parallel"),
        name="mixer0_in_proj",
    )(x, g.reshape(1, d), w, cos, sin)


def _rope_tables(seq):
    inv = ROPE_THETA ** (-np.arange(0, NSA_HEAD_DIM, 2, dtype=np.float64) / NSA_HEAD_DIM)
    ang = np.arange(seq, dtype=np.float64)[:, None] * inv[None, :]
    cos, sin = np.cos(ang), np.sin(ang)
    reps = LANES // NSA_HEAD_DIM
    cos_t = np.concatenate([cos, cos] * reps, -1).astype(np.float32)
    sin_t = np.concatenate([-sin, sin] * reps, -1).astype(np.float32)
    return jnp.asarray(cos_t), jnp.asarray(sin_t)


def _compress_kernel(k_ref, pe_ref, w1_ref, w2_ref, o_ref):
    k16 = k_ref[0, 0]
    w1 = w1_ref[0]
    half = w1.shape[0] // 2
    first = jnp.dot(k16, w1[:half], precision=HIGHEST, preferred_element_type=F32)
    second = jnp.dot(k16, w1[half:], precision=HIGHEST, preferred_element_type=F32)
    bias = jnp.dot(pe_ref[0], w1, precision=HIGHEST, preferred_element_type=F32)[0:1]
    n = k16.shape[0]
    pre = first + pltpu.roll(second, n - 1, 0) + bias
    o_ref[0, 0] = jnp.dot(_silu(pre), w2_ref[0], precision=HIGHEST, preferred_element_type=F32)


def _compress(kv16, pe, w1, w2):
    two, bh, n, wd = kv16.shape
    d = w2.shape[-1]
    return pl.pallas_call(
        _compress_kernel,
        grid=(two, bh),
        in_specs=[
            pl.BlockSpec((1, 1, n, wd), lambda a, b: (a, b, 0, 0)),
            pl.BlockSpec((1, SUBLANES, pe.shape[-1]), lambda a, b: (a, 0, 0)),
            pl.BlockSpec((1,) + w1.shape[1:], lambda a, b: (a, 0, 0)),
            pl.BlockSpec((1, d, d), lambda a, b: (a, 0, 0)),
        ],
        out_specs=pl.BlockSpec((1, 1, n, d), lambda a, b: (a, b, 0, 0)),
        out_shape=jax.ShapeDtypeStruct((two, bh, n, d), F32),
        compiler_params=_params("parallel", "parallel"),
        name="nsa_compress",
    )(kv16, pe, w1, w2)


def _group_rows(q):
    return jnp.concatenate(
        [q[:, g * NSA_HEAD_DIM:(g + 1) * NSA_HEAD_DIM] for g in range(NSA_GROUP)], axis=0)


def _ungroup_rows(o, tq):
    return jnp.concatenate([o[g * tq:(g + 1) * tq] for g in range(NSA_GROUP)], axis=1)


def _dot_nt_hi(a, b):
    a1 = a.astype(BF16)
    a2 = (a - a1.astype(F32)).astype(BF16)
    b1 = b.astype(BF16)
    b2 = (b - b1.astype(F32)).astype(BF16)
    return _dot_nt(a1, b1) + _dot_nt(a1, b2) + _dot_nt(a2, b1)


def _nsa_select_kernel(q_ref, kc_ref, vc_ref, ovt_ref, oc_ref, biast_ref, *, tq, topk):
    s0 = pl.program_id(1) * tq
    gw = NSA_GROUP * NSA_HEAD_DIM
    heads = range(NSA_KV_HEADS)
    q = q_ref[0]
    s = [_dot_nt_hi(_group_rows(q[:, h * gw:(h + 1) * gw]), kc_ref[h]) for h in heads]
    rows, ncmp = s[0].shape
    t = s0 + (lax.broadcasted_iota(jnp.int32, (rows, ncmp), 0) & (tq - 1))
    cmp_end = lax.broadcasted_iota(jnp.int32, (rows, ncmp), 1) * CMP_STRIDE + (CMP_BLOCK - 1)
    mask = cmp_end <= t
    p = []
    for h in heads:
        sh = jnp.where(mask, s[h], NEG_MASK)
        ph = jnp.where(mask, jnp.exp(sh - jnp.max(sh, -1, keepdims=True)), 0.0)
        p.append(ph / jnp.maximum(jnp.sum(ph, -1, keepdims=True), 1e-30))
    for h in heads:
        o = jnp.dot(p[h].astype(BF16), vc_ref[h].astype(BF16), preferred_element_type=F32)
        oc_ref[0, :, h * gw:(h + 1) * gw] = _ungroup_rows(o, tq)

    ovt = ovt_ref[...]
    imp = []
    for h in heads:
        psum = p[h][0:tq]
        for g in range(1, NSA_GROUP):
            psum = psum + p[h][g * tq:(g + 1) * tq]
        acc = None
        for piece in _split3(psum):
            d = _dot_nt(ovt, piece)
            acc = d if acc is None else acc + d
        imp.append(acc)
    blk = lax.broadcasted_iota(jnp.int32, imp[0].shape, 0)
    tt = s0 + lax.broadcasted_iota(jnp.int32, imp[0].shape, 1)
    cur = lax.shift_right_logical(tt, SEL_BLOCK.bit_length() - 1)
    forced = (blk == 0) | (blk == cur) | (blk == cur - 1)
    valid = blk * SEL_BLOCK <= tt
    x = [jnp.where(valid, jnp.where(forced, FORCE_SCORE, imp[h]), NEG_MASK) for h in heads]
    blk_f = blk.astype(F32)
    sel = [jnp.zeros(blk.shape, jnp.bool_) for _ in heads]
    for _ in range(topk):
        for h in heads:
            m = jnp.max(x[h], 0, keepdims=True)
            idx = jnp.min(jnp.where(x[h] == m, blk_f, float(SEL_LANES)), 0, keepdims=True)
            hit = blk_f == idx
            sel[h] = sel[h] | hit
            x[h] = jnp.where(hit, NEG_TAKEN, x[h])
    for h in heads:
        biast_ref[0, h] = jnp.where(sel[h], 0.0, NEG_UNSELECTED).astype(BF16)


def _nsa_select(q, kc, vc, overlap, batch, seq, tq=NSA_QUERY_ROWS):
    ncmp = kc.shape[1]
    qw = NSA_KV_HEADS * NSA_GROUP * NSA_HEAD_DIM
    topk = min(SEL_TOPK, seq // SEL_BLOCK)
    kern = functools.partial(_nsa_select_kernel, tq=tq, topk=topk)
    return pl.pallas_call(
        kern,
        grid=(batch, seq // tq),
        in_specs=[
            pl.BlockSpec((1, tq, qw), lambda b, i: (b, i, 0)),
            pl.BlockSpec((NSA_KV_HEADS, ncmp, NSA_HEAD_DIM), lambda b, i: (b, 0, 0)),
            pl.BlockSpec((NSA_KV_HEADS, ncmp, NSA_HEAD_DIM), lambda b, i: (b, 0, 0)),
            pl.BlockSpec((SEL_LANES, ncmp), lambda b, i: (0, 0)),
        ],
        out_specs=[
            pl.BlockSpec((1, tq, qw), lambda b, i: (b, i, 0)),
            pl.BlockSpec((1, NSA_KV_HEADS, SEL_LANES, tq), lambda b, i: (b, 0, 0, i)),
        ],
        out_shape=[
            jax.ShapeDtypeStruct((batch, seq, qw), F32),
            jax.ShapeDtypeStruct((batch, NSA_KV_HEADS, SEL_LANES, seq), BF16),
        ],
        compiler_params=_params("parallel", "parallel"),
        name="nsa_compressed_select",
    )(q, kc, vc, overlap)


NSA_LANE_SPLIT = 2


def _nsa_attend_kernel(q_ref, biast_ref, ka_ref, vst_ref, kw_ref, vwt_ref, oc_ref, gl_ref, o_ref,
                       sa_scr, sb_scr, p_scr, w_scr, *, tq, tk):
    i = pl.program_id(2)
    s0 = i * tq
    n = NSA_GROUP * tq
    half = n // NSA_LANE_SPLIT
    d = NSA_HEAD_DIM
    qt = (q_ref[0] * LOG2_E).T
    qgt = jnp.concatenate([qt[g * d:(g + 1) * d] for g in range(NSA_GROUP)], axis=1)
    qgt = qgt.astype(BF16)
    qat = jnp.concatenate([jnp.concatenate([biast_ref[0, 0]] * NSA_GROUP, axis=1), qgt], axis=0)
    init = tuple((jnp.full((1, half), NEG_MASK, F32), jnp.zeros((1, half), F32),
                  jnp.zeros((d, half), F32)) for _ in range(NSA_LANE_SPLIT))

    def query_pos(shape):
        return s0 + (lax.broadcasted_iota(jnp.int32, shape, 1) & (tq - 1))

    halves = range(NSA_LANE_SPLIT)

    def scores(kt):
        k = ka_ref[0, pl.ds(pl.multiple_of(kt * tk, tk), tk), :]
        return tuple(jnp.dot(k, qat[:, hf * half:(hf + 1) * half], preferred_element_type=F32)
                     for hf in halves)

    def values(kt):
        vt = vst_ref[0, :, pl.ds(pl.multiple_of(kt * tk, tk), tk)]
        return tuple(jnp.dot(vt, p_scr[:, hf * half:(hf + 1) * half],
                             preferred_element_type=F32) for hf in halves)

    span = WINDOW + tq
    start = pl.multiple_of(jnp.maximum(s0 - WINDOW, 0), tq)

    def window_scores():
        kwin = kw_ref[0, pl.ds(start, span), :]
        return tuple(jnp.dot(kwin, qgt[:, hf * half:(hf + 1) * half],
                             preferred_element_type=F32) for hf in halves)

    def sel_step(kt, stats, src_scr, dst_scr, causal):
        s_next = window_scores() if causal else scores(kt + 1)
        pv = values(jnp.maximum(kt - 1, 0))
        new_stats = []
        for hf in halves:
            cols = slice(hf * half, (hf + 1) * half)
            m, l, acc = stats[hf]
            s = src_scr[:, cols]
            if causal:
                kp = kt * tk + lax.broadcasted_iota(jnp.int32, s.shape, 0)
                mask = kp <= query_pos(s.shape)
                s = jnp.where(mask, s, NEG_MASK)
            m_new = jnp.maximum(m, jnp.max(s, 0, keepdims=True))
            alpha = jnp.exp2(m - m_new)
            p = jnp.exp2(s - m_new)
            if causal:
                p = jnp.where(mask, p, 0.0)
            new_stats.append((m_new, alpha * l + jnp.sum(p, 0, keepdims=True),
                              alpha * (acc + pv[hf])))
            p_scr[:, cols] = p.astype(BF16)
        for hf in halves:
            (w_scr if causal else dst_scr)[:, hf * half:(hf + 1) * half] = s_next[hf]
        return tuple(new_stats)

    def by_parity(kt, stats, causal):
        return lax.cond((kt & 1) == 0,
                        lambda st: sel_step(kt, st, sa_scr, sb_scr, causal),
                        lambda st: sel_step(kt, st, sb_scr, sa_scr, causal), stats)

    n_full = s0 // tk
    first_scores = scores(0)
    for hf in halves:
        sa_scr[:, hf * half:(hf + 1) * half] = first_scores[hf]
    p_scr[...] = jnp.zeros_like(p_scr)
    stats = lax.fori_loop(0, n_full, lambda kt, c: by_parity(kt, c, False), init)
    stats = by_parity(n_full, stats, True)
    pv_last = values(n_full)

    vwt = vwt_ref[0, :, pl.ds(start, span)]
    p_w, l_w = [], []
    for hf in halves:
        s = w_scr[:, hf * half:(hf + 1) * half]
        kp = start + lax.broadcasted_iota(jnp.int32, s.shape, 0)
        t = query_pos(s.shape)
        mask = (kp <= t) & (kp > t - WINDOW)
        s = jnp.where(mask, s, NEG_MASK)
        p = jnp.where(mask, jnp.exp2(s - jnp.max(s, 0, keepdims=True)), 0.0)
        l_w.append(jnp.sum(p, 0, keepdims=True))
        p_w.append(p.astype(BF16))
    o_w = jnp.concatenate([jnp.dot(vwt, p_w[hf], preferred_element_type=F32)
                           / jnp.maximum(l_w[hf], 1e-30) for hf in halves], axis=1)
    o_s = jnp.concatenate([(stats[hf][2] + pv_last[hf]) / jnp.maximum(stats[hf][1], 1e-30)
                           for hf in halves], axis=1)

    def rows_layout(ot):
        return jnp.concatenate([ot[:, g * tq:(g + 1) * tq] for g in range(NSA_GROUP)], axis=0).T

    o_s = rows_layout(o_s)
    o_w = rows_layout(o_w)
    gates = jax.nn.sigmoid(gl_ref[0, 0])
    o_c = oc_ref[0]
    pieces = []
    for g in range(NSA_GROUP):
        c = slice(g * d, (g + 1) * d)
        pieces.append(gates[:, 3 * g:3 * g + 1] * o_c[:, c]
                      + gates[:, 3 * g + 1:3 * g + 2] * o_s[:, c]
                      + gates[:, 3 * g + 2:3 * g + 3] * o_w[:, c])
    o_ref[0] = jnp.concatenate(pieces, axis=1)


def _nsa_attend(q, biast, kaug, vst, kw, vwt, o_c, gl, batch, seq, tq=NSA_QUERY_ROWS,
                tk=NSA_KEY_ROWS):
    gw = NSA_GROUP * NSA_HEAD_DIM
    tk = min(tk, seq)
    assert tk % tq == 0 and WINDOW % tq == 0 and seq >= WINDOW + tq
    kern = functools.partial(_nsa_attend_kernel, tq=tq, tk=tk)
    bh = lambda b, h, i: (b * NSA_KV_HEADS + h, 0, 0)
    return pl.pallas_call(
        kern,
        grid=(batch, NSA_KV_HEADS, seq // tq),
        in_specs=[
            pl.BlockSpec((1, tq, gw), lambda b, h, i: (b, i, h)),
            pl.BlockSpec((1, 1, SEL_LANES, tq), lambda b, h, i: (b, h, 0, i)),
            pl.BlockSpec((1, seq, SEL_LANES + NSA_HEAD_DIM), bh),
            pl.BlockSpec((1, NSA_HEAD_DIM, seq), bh),
            pl.BlockSpec((1, seq, NSA_HEAD_DIM), bh),
            pl.BlockSpec((1, NSA_HEAD_DIM, seq), bh),
            pl.BlockSpec((1, tq, gw), lambda b, h, i: (b, i, h)),
            pl.BlockSpec((1, 1, tq, NSA_GROUP * 3), lambda b, h, i: (b, h, i, 0)),
        ],
        out_specs=pl.BlockSpec((1, tq, gw), lambda b, h, i: (b, i, h)),
        out_shape=jax.ShapeDtypeStruct((batch, seq, NSA_KV_HEADS * gw), F32),
        scratch_shapes=[pltpu.VMEM((tk, NSA_GROUP * tq), F32),
                        pltpu.VMEM((tk, NSA_GROUP * tq), F32),
                        pltpu.VMEM((tk, NSA_GROUP * tq), BF16),
                        pltpu.VMEM((WINDOW + tq, NSA_GROUP * tq), F32)],
        compiler_params=_params("parallel", "parallel", "arbitrary"),
        name="nsa_selected_window",
    )(q, biast, kaug, vst, kw, vwt, o_c, gl)


def _nsa(q, kvc, kaug, kwin, vst, vwt, misc, pe_k, w1_k, w2_k, pe_v, w1_v, w2_v, batch, seq):
    d = NSA_HEAD_DIM
    bh = batch * NSA_KV_HEADS
    n16 = seq // CMP_STRIDE
    kv16 = jnp.transpose(kvc.reshape(batch, seq, 2, NSA_KV_HEADS, d), (2, 0, 3, 1, 4))
    kv16 = kv16.reshape(2, bh, n16, CMP_STRIDE * d)
    pe = jnp.stack([pe_k, pe_v]).reshape(2, 1, CMP_BLOCK * d)
    pe = jnp.broadcast_to(pe, (2, SUBLANES, CMP_BLOCK * d))
    cmp = _compress(kv16, pe, jnp.stack([w1_k, w1_v]), jnp.stack([w2_k, w2_v]))
    kc, vc = cmp[0], cmp[1]

    n_sel = seq // SEL_BLOCK
    cmp_start = np.arange(n16) * CMP_STRIDE
    sel_start = np.arange(SEL_LANES) * SEL_BLOCK
    overlap = ((cmp_start[:, None] < sel_start[None, :] + SEL_BLOCK)
               & (cmp_start[:, None] + CMP_BLOCK - 1 >= sel_start[None, :])
               & (np.arange(SEL_LANES)[None, :] < n_sel)
               & (np.arange(n16)[:, None] < (seq - CMP_BLOCK) // CMP_STRIDE + 1))
    overlap_t = jnp.asarray(overlap.T, BF16)

    q3 = q.reshape(batch, seq, NSA_Q_W)
    o_c, biast = _nsa_select(q3, kc, vc, overlap_t, batch, seq)

    gl = misc[:, :NSA_HEADS * 3].reshape(batch, seq, NSA_KV_HEADS, NSA_GROUP * 3)
    gl = jnp.moveaxis(gl, 2, 1)
    per_head = lambda a: a.reshape((bh,) + a.shape[2:])
    o = _nsa_attend(q3, biast, per_head(kaug), per_head(vst), per_head(kwin), per_head(vwt), o_c,
                    gl, batch, seq)
    return o.reshape(batch * seq, NSA_Q_W)


def _ssd_kernel(xbc_ref, halo_ref, z_ref, dt_ref, cw_ref, cb_ref, dtb_ref, alog_ref, dskip_ref,
                nw_ref, tril_ref, spread_ref, o_ref, state_scr, y_scr):
    c = pl.program_id(1)
    l = SSD_CHUNK

    @pl.when(c == 0)
    def _():
        state_scr[...] = jnp.zeros_like(state_scr)

    x = xbc_ref[0]
    halo = jnp.where(c == 0, 0.0, halo_ref[0])
    xx = jnp.concatenate([halo, x], axis=0)
    cw = cw_ref[...]
    conv = cb_ref[...]
    for k in range(SSD_CONV):
        off = SUBLANES - (SSD_CONV - 1) + k
        conv = conv + cw[k:k + 1] * xx[off:off + l]
    xbc = _silu(conv)
    xs = xbc[:, :SSD_D_INNER]
    gn = SSD_GROUPS * SSD_STATE
    bmat = xbc[:, SSD_D_INNER:SSD_D_INNER + gn]
    cmat = xbc[:, SSD_D_INNER + gn:]

    dt = _softplus(dt_ref[0] + dtb_ref[...])
    da = dt * (-jnp.exp(alog_ref[...]))
    a_cs = _dot_x3_left(tril_ref[...], da)
    a_cs_t = a_cs.T
    a_last = a_cs[l - 1:l]
    causal = (lax.broadcasted_iota(jnp.int32, (l, l), 0)
              >= lax.broadcasted_iota(jnp.int32, (l, l), 1))

    spread = spread_ref[...]
    dt_x = _dot_x3(dt, spread)
    grow_x = _dot_x3(jnp.exp(a_cs), spread)
    fade_x = _dot_x3(jnp.exp(a_last - a_cs), spread)
    chunk_x = _dot_x3(jnp.broadcast_to(jnp.exp(a_last), (SUBLANES, LANES)), spread)[0:1]
    xd = xs * dt_x
    xd16 = xd.astype(BF16)
    fxd16 = (xd * fade_x).astype(BF16)

    pairs = range(SSD_HEADS // 2)
    pairs_per_group = len(pairs) // SSD_GROUPS
    lanes = {c: slice(c * LANES, (c + 1) * LANES) for c in pairs}
    cb, y_off = {}, {}
    for g in range(SSD_GROUPS):
        bg = bmat[:, g * SSD_STATE:(g + 1) * SSD_STATE]
        cg16 = cmat[:, g * SSD_STATE:(g + 1) * SSD_STATE].astype(BF16)
        cb[g] = _dot_nt(cg16, bg.astype(BF16))
        bgt16 = bg.T.astype(BF16)
        for c in range(g * pairs_per_group, (g + 1) * pairs_per_group):
            st = state_scr[c]
            y_off[c] = jnp.dot(cg16, st.astype(BF16), preferred_element_type=F32)
            new = jnp.dot(bgt16, fxd16[:, lanes[c]], preferred_element_type=F32)
            state_scr[c] = st * chunk_x[:, lanes[c]] + new
    first_head = lax.broadcasted_iota(jnp.int32, (l, LANES), 1) < SSD_HEAD_DIM
    y_diag = {}
    for c in pairs:
        for hh in range(2):
            h = 2 * c + hh
            seg = jnp.where(causal, jnp.exp(a_cs[:, h:h + 1] - a_cs_t[h:h + 1, :]), 0.0)
            y_diag[h] = jnp.dot((cb[c // pairs_per_group] * seg).astype(BF16), xd16[:, lanes[c]],
                                preferred_element_type=F32)
    for c in pairs:
        y_scr[:, lanes[c]] = (jnp.where(first_head, y_diag[2 * c], y_diag[2 * c + 1])
                              + y_off[c] * grow_x[:, lanes[c]])

    y = (y_scr[...] + xs * dskip_ref[...]) * _silu(z_ref[0])
    gw = SSD_D_INNER // SSD_GROUPS
    outs = []
    for g in range(SSD_GROUPS):
        yg = y[:, g * gw:(g + 1) * gw]
        outs.append(yg * lax.rsqrt(jnp.mean(yg * yg, -1, keepdims=True) + SSD_NORM_EPS))
    o_ref[0] = jnp.concatenate(outs, axis=1) * nw_ref[...]


def _pad_lanes(v, width=LANES):
    v = v.reshape(1, -1).astype(F32)
    return jnp.pad(v, ((0, 0), (0, width - v.shape[1])))


def _ssd(z, xbc, misc, conv_w, conv_b, dt_bias, a_log, d_skip, norm_w, batch, seq):
    l = SSD_CHUNK
    nc = seq // l
    z3 = z.reshape(batch, seq, SSD_D_INNER)
    x3 = xbc.reshape(batch, seq, SSD_XBC)
    dt = misc[:, NSA_HEADS * 3:NSA_HEADS * 3 + SSD_HEADS]
    dt3 = jnp.pad(dt, ((0, 0), (0, LANES - SSD_HEADS))).reshape(batch, seq, LANES)
    tril = jnp.asarray(np.tril(np.ones((l, l))), BF16)
    spread = np.zeros((LANES, SSD_D_INNER), np.float32)
    spread[np.arange(SSD_D_INNER) // SSD_HEAD_DIM, np.arange(SSD_D_INNER)] = 1.0
    spread = jnp.asarray(spread, BF16)
    hb = l // SUBLANES
    const = lambda b, c: (0, 0)
    return pl.pallas_call(
        _ssd_kernel,
        grid=(batch, nc),
        in_specs=[
            pl.BlockSpec((1, l, SSD_XBC), lambda b, c: (b, c, 0)),
            pl.BlockSpec((1, SUBLANES, SSD_XBC), lambda b, c: (b, jnp.maximum(c * hb - 1, 0), 0)),
            pl.BlockSpec((1, l, SSD_D_INNER), lambda b, c: (b, c, 0)),
            pl.BlockSpec((1, l, LANES), lambda b, c: (b, c, 0)),
            pl.BlockSpec((SSD_CONV, SSD_XBC), const),
            pl.BlockSpec((1, SSD_XBC), const),
            pl.BlockSpec((1, LANES), const),
            pl.BlockSpec((1, LANES), const),
            pl.BlockSpec((1, SSD_D_INNER), const),
            pl.BlockSpec((1, SSD_D_INNER), const),
            pl.BlockSpec((l, l), const),
            pl.BlockSpec((LANES, SSD_D_INNER), const),
        ],
        out_specs=pl.BlockSpec((1, l, SSD_D_INNER), lambda b, c: (b, c, 0)),
        out_shape=jax.ShapeDtypeStruct((batch, seq, SSD_D_INNER), F32),
        scratch_shapes=[pltpu.VMEM((SSD_HEADS // 2, SSD_STATE, 2 * SSD_HEAD_DIM), F32),
                        pltpu.VMEM((l, SSD_D_INNER), F32)],
        compiler_params=_params("parallel", "arbitrary"),
        name="ssd_chunk_scan",
    )(x3, x3, z3, dt3, conv_w.reshape(SSD_CONV, SSD_XBC), conv_b.reshape(1, SSD_XBC),
      _pad_lanes(dt_bias), _pad_lanes(a_log),
      jnp.repeat(d_skip.astype(F32), SSD_HEAD_DIM).reshape(1, SSD_D_INNER),
      norm_w.reshape(1, SSD_D_INNER), tril, spread).reshape(batch * seq, SSD_D_INNER)


def _dot_x2(a, b):
    a1 = a.astype(BF16)
    a2 = (a - a1.astype(F32)).astype(BF16)
    return (jnp.dot(a1, b, preferred_element_type=F32)
            + jnp.dot(a2, b, preferred_element_type=F32))


def _head_sum(x, seg, seg_t):
    return _dot_x2(_dot_x2(x, seg), seg_t)


def _rwkv_pre_kernel(x_ref, halo_ref, g_ref, mu_ref, wr_ref, wk_ref, wv_ref, w0_ref, w1_ref,
                     w2_ref, a0_ref, a1_ref, a2_ref, g1_ref, g2_ref, kk_ref, ka_ref, seg_ref,
                     segt_ref, r_out, ld_out, k_out, v_out, kk_out, g_out, bt_out, kt_out, ldt_out,
                     *, tiles_per_seq):
    i = pl.program_id(0)
    h = _rms(x_ref[...], g_ref[...], NORM_EPS)
    prev_row = _rms(halo_ref[...], g_ref[...], NORM_EPS)[SUBLANES - 1:SUBLANES]
    prev_row = jnp.where(i % tiles_per_seq == 0, 0.0, prev_row)
    rowid = lax.broadcasted_iota(jnp.int32, h.shape, 0)
    prev = jnp.where(rowid == 0, prev_row, pltpu.roll(h, 1, 0))
    xx = prev - h
    mu = mu_ref[...]
    mix = lambda j: (h + xx * mu[j:j + 1]).astype(BF16)
    dot = lambda a, w_ref: jnp.dot(a, w_ref[...], preferred_element_type=F32)
    r = dot(mix(0), wr_ref)
    w = -_softplus(-(w0_ref[...] + dot(jnp.tanh(dot(mix(1), w1_ref)).astype(BF16), w2_ref))) - 0.5
    k = dot(mix(2), wk_ref)
    v = dot(mix(3), wv_ref)
    a = jax.nn.sigmoid(a0_ref[...] + dot(dot(mix(4), a1_ref).astype(BF16), a2_ref))
    g = dot(jax.nn.sigmoid(dot(mix(5), g1_ref)).astype(BF16), g2_ref)
    kk = k * kk_ref[...]
    norm = jnp.sqrt(_head_sum(kk * kk, seg_ref[...], segt_ref[...]))
    kk = kk / jnp.maximum(norm, 1e-12)
    k = k * (1.0 + (a - 1.0) * ka_ref[...])
    log_decay = -jnp.exp(w)
    r_out[...] = r
    ld_out[...] = log_decay
    k_out[...] = k
    v_out[...] = v
    kk_out[...] = kk
    g_out[...] = g
    bt_out[0] = (kk * a).T
    kt_out[0] = k.T
    ldt_out[0] = log_decay.T


def _pad_cols(w, width):
    return jnp.pad(w, ((0, 0), (0, width - w.shape[1])))


def _pad_rows(w, width):
    return jnp.pad(w, ((0, width - w.shape[0]), (0, 0)))


def _seg_matrices():
    seg = np.zeros((D_MODEL, LANES), np.float32)
    seg[np.arange(D_MODEL), np.arange(D_MODEL) // RWKV_HEAD_DIM] = 1.0
    return jnp.asarray(seg, BF16), jnp.asarray(seg.T, BF16)


def _rwkv_pre(x, g, mu, w_r, w_k, w_v, w0, w1, w2, a0, a1, a2, g1, g2, k_k, k_a, seq,
              tm=PROJ_ROWS):
    t, d = x.shape
    lora = lambda w: -(-w.shape[1] // LANES) * LANES
    w1p, w2p = _pad_cols(w1, lora(w1)), _pad_rows(w2, lora(w1))
    a1p, a2p = _pad_cols(a1, lora(a1)), _pad_rows(a2, lora(a1))
    g1p, g2p = _pad_cols(g1, lora(g1)), _pad_rows(g2, lora(g1))
    seg, seg_t = _seg_matrices()
    row = lambda i: (i, 0)
    const = lambda i: (0, 0)
    hb = tm // SUBLANES
    vec = lambda v: v.reshape(1, d)
    mats = [w.astype(BF16) for w in (w_r, w_k, w_v)]
    ins = [x, x, vec(g), mu, *mats, vec(w0), w1p.astype(BF16), w2p.astype(BF16), vec(a0),
           a1p.astype(BF16), a2p.astype(BF16), g1p.astype(BF16), g2p.astype(BF16), vec(k_k),
           vec(k_a), seg, seg_t]
    in_specs = [pl.BlockSpec((tm, d), row),
                pl.BlockSpec((SUBLANES, d), lambda i: (jnp.maximum(i * hb - 1, 0), 0))]
    in_specs += [pl.BlockSpec(a.shape, const) for a in ins[2:]]
    tps = seq // tm
    col = pl.BlockSpec((1, d, tm), lambda i: (i // tps, 0, i % tps))
    return pl.pallas_call(
        functools.partial(_rwkv_pre_kernel, tiles_per_seq=tps),
        grid=(t // tm,),
        in_specs=in_specs,
        out_specs=[pl.BlockSpec((tm, d), row)] * 6 + [col] * 3,
        out_shape=([jax.ShapeDtypeStruct((t, d), F32)] * 6
                   + [jax.ShapeDtypeStruct((t // seq, d, seq), F32)] * 3),
        compiler_params=_params("parallel"),
        name="rwkv7_projections",
    )(*ins)


RWKV_CHUNK = 128


def _rwkv_chunk_kernel(r_ref, ld_ref, kk_ref, v_ref, bt_ref, kt_ref, ldt_ref, tril_ref, triu_ref,
                       y_ref, state_scr):
    @pl.when(pl.program_id(1) == 0)
    def _():
        state_scr[...] = jnp.zeros_like(state_scr)

    l = RWKV_CHUNK
    hd = RWKV_HEAD_DIM
    tril = tril_ref[...]
    ld = ld_ref[0]
    c_in = _dot_x3_left(tril, ld)
    a_bar = -kk_ref[0] * jnp.exp(c_in - ld)
    r_bar = r_ref[0] * jnp.exp(c_in)
    v = v_ref[0]
    ldt = ldt_ref[0]
    c_t = _dot_x3(ldt, triu_ref[...])
    scale_t = jnp.exp(-c_t)
    b_t = bt_ref[0] * scale_t
    k_t = kt_ref[0] * scale_t
    decay_col = jnp.exp(c_t[:, l - 1:l])

    row = lax.broadcasted_iota(jnp.int32, (l, l), 0)
    colx = lax.broadcasted_iota(jnp.int32, (l, l), 1)
    strict = row > colx
    incl = row >= colx
    lane = lax.broadcasted_iota(jnp.int32, (l, LANES), 1)
    first_head = lane < hd
    blockdiag = (lax.broadcasted_iota(jnp.int32, (LANES, LANES), 0) < hd) == (
        lax.broadcasted_iota(jnp.int32, (LANES, LANES), 1) < hd)
    mm = lambda a, b: jnp.dot(a, b, preferred_element_type=F32)
    b16 = lambda a: a.astype(BF16)

    pairs = range(D_MODEL // LANES)
    heads = [(c, hh) for c in pairs for hh in range(2)]
    lanes = {c: slice(c * LANES, (c + 1) * LANES) for c in pairs}
    v16 = {c: b16(v[:, lanes[c]]) for c in pairs}
    bk_t = {c: b16(jnp.concatenate([b_t[lanes[c]], k_t[lanes[c]]], axis=1)) for c in pairs}
    h2 = {c: state_scr[c] for c in pairs}
    gh = {}
    for c in pairs:
        rhs = jnp.concatenate([bk_t[c], b16(h2[c])], axis=1)
        a_p, r_p = a_bar[:, lanes[c]], r_bar[:, lanes[c]]
        for hh in range(2):
            keep = first_head if hh == 0 else ~first_head
            x = jnp.concatenate([jnp.where(keep, a_p, 0.0), jnp.where(keep, r_p, 0.0)], axis=0)
            gh[c, hh] = mm(b16(x), rhs)
    mp, u, p_r = {}, {}, {}
    for c, hh in heads:
        g = gh[c, hh]
        mp[c, hh] = b16(jnp.where(strict, g[:l, :l], 0.0))
        m_ak = b16(jnp.where(strict, g[:l, l:2 * l], 0.0))
        p_r[c, hh] = b16(jnp.concatenate([jnp.where(incl, g[l:, :l], 0.0),
                                          jnp.where(incl, g[l:, l:2 * l], 0.0)], axis=1))
        u[c, hh] = g[:l, 2 * l:] + mm(m_ak, v16[c])
    n_factors = l.bit_length() - 1
    for f in range(n_factors):
        du = {h: mm(mp[h], b16(u[h])) for h in heads}
        if f + 1 < n_factors:
            mp = {h: b16(mm(mp[h], mp[h])) for h in heads}
        u = {h: u[h] + du[h] for h in heads}
    ys = {h: gh[h][l:, 2 * l:] + mm(p_r[h], jnp.concatenate([b16(u[h]), v16[h[0]]], axis=0))
          for h in heads}
    for c in pairs:
        u_pair = jnp.where(first_head, u[c, 0], u[c, 1])
        y_ref[0, :, lanes[c]] = jnp.where(first_head, ys[c, 0], ys[c, 1])
        upd = h2[c] + mm(bk_t[c], jnp.concatenate([b16(u_pair), v16[c]], axis=0))
        state_scr[c] = jnp.where(blockdiag, upd * decay_col[lanes[c]], 0.0)


def _rwkv_scan(r, ld, kk, v, bt, kt, ldt, batch, seq):
    l = RWKV_CHUNK
    d = D_MODEL
    rows = lambda x: x.reshape(batch, seq, d)
    rblk = pl.BlockSpec((1, l, d), lambda b, c: (b, c, 0))
    cblk = pl.BlockSpec((1, d, l), lambda b, c: (b, 0, c))
    tril = jnp.asarray(np.tril(np.ones((l, l))), BF16)
    y = pl.pallas_call(
        _rwkv_chunk_kernel,
        grid=(batch, seq // l),
        in_specs=[rblk] * 4 + [cblk] * 3 + [pl.BlockSpec((l, l), lambda b, c: (0, 0))] * 2,
        out_specs=rblk,
        out_shape=jax.ShapeDtypeStruct((batch, seq, d), F32),
        scratch_shapes=[pltpu.VMEM((d // LANES, LANES, LANES), F32)],
        compiler_params=_params("parallel", "arbitrary"),
        name="rwkv7_recurrence",
    )(rows(r), rows(ld), rows(kk), rows(v), bt, kt, ldt, tril, tril.T)
    return y.reshape(batch * seq, d)


def _rwkv_post_kernel(x_ref, y_ref, r_ref, k_ref, v_ref, g_ref, lng_ref, lnb_ref, rk_ref, wo_ref,
                      gn_ref, seg_ref, segt_ref, o_ref):
    seg, seg_t = seg_ref[...], segt_ref[...]
    y = y_ref[...]
    inv = 1.0 / RWKV_HEAD_DIM
    mean = _head_sum(y, seg, seg_t) * inv
    yc = y - mean
    var = _head_sum(yc * yc, seg, seg_t) * inv
    yn = yc * lax.rsqrt(var + RWKV_GN_EPS) * lng_ref[...] + lnb_ref[...]
    bonus = _head_sum(r_ref[...] * k_ref[...] * rk_ref[...], seg, seg_t) * v_ref[...]
    out = ((yn + bonus) * g_ref[...]).astype(BF16)
    proj = jnp.dot(out, wo_ref[...], preferred_element_type=F32)
    o_ref[...] = x_ref[...] + _rms(proj, gn_ref[...], NORM_EPS)


def _rwkv_post(x, y, r, k, v, g, ln_g, ln_b, r_k, w_o, gn, tm=PROJ_ROWS):
    t, d = x.shape
    seg, seg_t = _seg_matrices()
    row = lambda i: (i, 0)
    const = lambda i: (0, 0)
    vec = lambda a: a.reshape(1, d)
    small = [vec(ln_g), vec(ln_b), vec(r_k), w_o.astype(BF16), vec(gn), seg, seg_t]
    return pl.pallas_call(
        _rwkv_post_kernel,
        grid=(t // tm,),
        in_specs=[pl.BlockSpec((tm, d), row)] * 6 + [pl.BlockSpec(a.shape, const) for a in small],
        out_specs=pl.BlockSpec((tm, d), row),
        out_shape=jax.ShapeDtypeStruct((t, d), F32),
        compiler_params=_params("parallel"),
        name="rwkv7_output",
    )(x, y, r, k, v, g, *small)


def _nsa_ssd_mixer(x, g_pre, g_post, cos, sin, w_in, pe_k, w1_k, w2_k, pe_v, w1_v, w2_v, conv_w,
                   conv_b, dt_bias, a_log, d_skip, norm_w, w_out, batch, seq):
    q, kvc, kaug, kwin, vst, vwt, z, xbc, misc = _inproj(x, g_pre, w_in, cos, sin, batch, seq)
    o_a = _nsa(q, kvc, kaug, kwin, vst, vwt, misc, pe_k, w1_k, w2_k, pe_v, w1_v, w2_v, batch, seq)
    o_b = _ssd(z, xbc, misc, conv_w, conv_b, dt_bias, a_log, d_skip, norm_w, batch, seq)
    return g_post, [o_a, o_b], [w_out[:NSA_Q_W], w_out[NSA_Q_W:]]


def _rwkv7_mixer(x, g_pre, g_post, mu, w_r, w_k, w_v, w_o, w0, w1, w2, a0, a1, a2, g1, g2, k_k,
                 k_a, r_k, ln_g, ln_b, batch, seq):
    r, ld, k, v, kk, g, bt, kt, ldt = _rwkv_pre(x, g_pre, mu, w_r, w_k, w_v, w0, w1, w2, a0, a1,
                                                a2, g1, g2, k_k, k_a, seq)
    y = _rwkv_scan(r, ld, kk, v, bt, kt, ldt, batch, seq)
    return _rwkv_post(x, y, r, k, v, g, ln_g, ln_b, r_k, w_o, g_post)


def kernel(x, norm_gains, ffn1_w_gate, ffn1_w_up, ffn1_w_down, ffn2_w_gate, ffn2_w_up, ffn2_w_down, ab_w_in, a_cmp_pe_k, a_cmp_w1_k, a_cmp_w2_k, a_cmp_pe_v, a_cmp_w1_v, a_cmp_w2_v, b_conv_w, b_conv_b, b_dt_bias, b_a_log, b_d_skip, b_norm_w, ab_w_out, c_mu, c_w_r, c_w_k, c_w_v, c_w_o, c_w0, c_w1, c_w2, c_a0, c_a1, c_a2, c_g1, c_g2, c_k_k, c_k_a, c_r_k, c_ln_g, c_ln_b):
    batch, seq, d = x.shape
    depth = norm_gains.shape[0]
    cos, sin = _rope_tables(seq)
    x = x.reshape(batch * seq, d)
    (ffn1_w_gate, ffn1_w_up, ffn1_w_down, ffn2_w_gate, ffn2_w_up, ffn2_w_down) = [
        w.astype(BF16) for w in (ffn1_w_gate, ffn1_w_up, ffn1_w_down, ffn2_w_gate, ffn2_w_up,
                                 ffn2_w_down)]
    for layer in range(depth):
        ng = norm_gains[layer]
        x = _ffn(x, ng[0], ng[1], ffn1_w_gate[layer], ffn1_w_up[layer], ffn1_w_down[layer])
        i = layer // 2
        mixer = None
        if layer % 2 == 0:
            mixer = _nsa_ssd_mixer(x, ng[2], ng[3], cos, sin, ab_w_in[i], a_cmp_pe_k[i],
                                   a_cmp_w1_k[i], a_cmp_w2_k[i], a_cmp_pe_v[i], a_cmp_w1_v[i],
                                   a_cmp_w2_v[i], b_conv_w[i], b_conv_b[i], b_dt_bias[i],
                                   b_a_log[i], b_d_skip[i], b_norm_w[i], ab_w_out[i], batch, seq)
        else:
            x = _rwkv7_mixer(x, ng[2], ng[3], c_mu[i], c_w_r[i], c_w_k[i], c_w_v[i], c_w_o[i],
                             c_w0[i], c_w1[i], c_w2[i], c_a0[i], c_a1[i], c_a2[i], c_g1[i],
                             c_g2[i], c_k_k[i], c_k_a[i], c_r_k[i], c_ln_g[i], c_ln_b[i],
                             batch, seq)
        x = _ffn(x, ng[4], ng[5], ffn2_w_gate[layer], ffn2_w_up[layer], ffn2_w_down[layer],
                 mixer=mixer)
    return x.reshape(batch, seq, d)
```

```python
import functools

import jax
import jax.numpy as jnp
import numpy as np
from jax import lax
from jax.experimental import pallas as pl
from jax.experimental.pallas import tpu as pltpu

F32 = jnp.float32
BF16 = jnp.bfloat16
HIGHEST = lax.Precision.HIGHEST

D_MODEL = 1024
D_FF = 2816
NORM_EPS = 1e-6
NSA_HEADS = 8
NSA_KV_HEADS = 2
NSA_GROUP = NSA_HEADS // NSA_KV_HEADS
NSA_HEAD_DIM = 64
CMP_BLOCK = 32
CMP_STRIDE = 16
SEL_BLOCK = 64
SEL_TOPK = 16
WINDOW = 512
ROPE_THETA = 10000.0
FORCE_SCORE = 1e4
SEL_LANES = 128
SSD_HEADS = 16
SSD_HEAD_DIM = 64
SSD_D_INNER = SSD_HEADS * SSD_HEAD_DIM
SSD_GROUPS = 2
SSD_STATE = 128
SSD_CONV = 4
SSD_CHUNK = 128
SSD_NORM_EPS = 1e-5
SSD_XBC = SSD_D_INNER + 2 * SSD_GROUPS * SSD_STATE
RWKV_HEAD_DIM = 64
RWKV_HEADS = D_MODEL // RWKV_HEAD_DIM
RWKV_GN_EPS = 64e-5

NSA_Q_W = NSA_HEADS * NSA_HEAD_DIM
NSA_KV_W = NSA_KV_HEADS * NSA_HEAD_DIM
IN_SPLITS = (NSA_Q_W, NSA_KV_W, NSA_KV_W, NSA_KV_W, NSA_KV_W, NSA_KV_W, NSA_KV_W,
             NSA_HEADS * 3, SSD_D_INNER, SSD_XBC, SSD_HEADS)
IN_WIDTH = sum(IN_SPLITS)

LANES = 128
SUBLANES = 8
MXU_TILE = 256
VMEM_LIMIT_BYTES = 56 * 1024 * 1024

FFN_ROWS = 512
FFN_COLUMN_CHUNKS = 2
PROJ_ROWS = 256
NSA_QUERY_ROWS = 256
NSA_KEY_ROWS = 512

NEG_MASK = -1e30
NEG_UNSELECTED = -2.0 ** 30
NEG_TAKEN = -3e38
LOG2_E = 1.4426950408889634


def _params(*sem):
    return pltpu.CompilerParams(dimension_semantics=sem, vmem_limit_bytes=VMEM_LIMIT_BYTES)


def _rms(x, g, eps):
    return x * lax.rsqrt(jnp.mean(x * x, -1, keepdims=True) + eps) * g


def _silu(x):
    return x * jax.nn.sigmoid(x)


def _softplus(x):
    return jnp.maximum(x, 0.0) + jnp.log1p(jnp.exp(-jnp.abs(x)))


def _split3(a):
    a1 = a.astype(BF16)
    r1 = a - a1.astype(F32)
    a2 = r1.astype(BF16)
    a3 = (r1 - a2.astype(F32)).astype(BF16)
    return a1, a2, a3


def _dot_x3(a, b):
    acc = None
    for piece in _split3(a):
        d = jnp.dot(piece, b, preferred_element_type=F32)
        acc = d if acc is None else acc + d
    return acc


def _dot_x3_left(b, a):
    acc = None
    for piece in _split3(a):
        d = jnp.dot(b, piece, preferred_element_type=F32)
        acc = d if acc is None else acc + d
    return acc


def _dot_nt(a, b, **kw):
    return lax.dot_general(a, b, (((1,), (1,)), ((), ())), preferred_element_type=F32, **kw)


def _ffn_kernel(x_ref, gi_ref, go_ref, wg_ref, wu_ref, wd_ref, *rest, chunks, n_parts):
    o_ref = rest[-1]
    x = x_ref[...]
    if n_parts:
        proj = None
        for p_ref, w_ref in zip(rest[1:1 + n_parts], rest[1 + n_parts:1 + 2 * n_parts]):
            dd = jnp.dot(p_ref[...].astype(BF16), w_ref[...], preferred_element_type=F32)
            proj = dd if proj is None else proj + dd
        x = x + _rms(proj, rest[0][...], NORM_EPS)
    h = _rms(x, gi_ref[...], NORM_EPS).astype(BF16)
    acc = None
    for lo, hi in chunks:
        gate = jnp.dot(h, wg_ref[:, lo:hi], preferred_element_type=F32)
        up = jnp.dot(h, wu_ref[:, lo:hi], preferred_element_type=F32)
        act = (_silu(gate) * up).astype(BF16)
        part = jnp.dot(act, wd_ref[lo:hi, :], preferred_element_type=F32)
        acc = part if acc is None else acc + part
    o_ref[...] = x + 0.5 * _rms(acc, go_ref[...], NORM_EPS)


def _ffn(x, g_in, g_out, w_gate, w_up, w_down, mixer=None, tm=FFN_ROWS,
         n_chunks=FFN_COLUMN_CHUNKS):
    t, d = x.shape
    f = w_gate.shape[1]
    tiles = f // MXU_TILE
    assert tiles * MXU_TILE == f
    cuts = [MXU_TILE * ((tiles * c + n_chunks - 1) // n_chunks) for c in range(n_chunks + 1)]
    chunks = tuple(zip(cuts[:-1], cuts[1:]))
    row = lambda i: (i, 0)
    const = lambda i: (0, 0)
    resident = lambda shape: pl.BlockSpec(shape, const, pipeline_mode=pl.Buffered(1))
    operands = [x, g_in.reshape(1, d), g_out.reshape(1, d), w_gate.astype(BF16),
                w_up.astype(BF16), w_down.astype(BF16)]
    in_specs = [pl.BlockSpec((tm, d), row), pl.BlockSpec((1, d), const),
                pl.BlockSpec((1, d), const), resident((d, f)), resident((d, f)),
                resident((f, d))]
    n_parts = 0
    if mixer is not None:
        g_mixer, parts, weights = mixer
        n_parts = len(parts)
        operands += [g_mixer.reshape(1, d), *parts, *[w.astype(BF16) for w in weights]]
        in_specs += ([pl.BlockSpec((1, d), const)]
                     + [pl.BlockSpec((tm, p.shape[1]), row) for p in parts]
                     + [resident(w.shape) for w in weights])
    return pl.pallas_call(
        functools.partial(_ffn_kernel, chunks=chunks, n_parts=n_parts),
        grid=(t // tm,),
        in_specs=in_specs,
        out_specs=pl.BlockSpec((tm, d), row),
        out_shape=jax.ShapeDtypeStruct((t, d), F32),
        compiler_params=_params("parallel"),
        name="ffn_half_step",
    )(*operands)


INPROJ_MISC_W = 256
INPROJ_KV_W = 6 * NSA_KV_W
INPROJ_WIDTH = NSA_Q_W + INPROJ_KV_W + SSD_D_INNER + SSD_XBC + INPROJ_MISC_W


def _swap_halves(x):
    w = x.shape[-1]
    lane = lax.broadcasted_iota(jnp.int32, x.shape, x.ndim - 1)
    low = (lane & (NSA_HEAD_DIM - 1)) < (NSA_HEAD_DIM // 2)
    return jnp.where(low, pltpu.roll(x, w - NSA_HEAD_DIM // 2, x.ndim - 1),
                     pltpu.roll(x, NSA_HEAD_DIM // 2, x.ndim - 1))


def _inproj_kernel(x_ref, g_ref, w_ref, cos_ref, sin_ref, q_ref, kv16_ref, kaug_ref, kwin_ref,
                   vst_ref, vwt_ref, z_ref, xbc_ref, misc_ref, kvc_scr, *, tiles_per_seq):
    h = _rms(x_ref[...], g_ref[...], NORM_EPS).astype(BF16)
    proj = jnp.dot(h, w_ref[...], preferred_element_type=F32)
    tm = proj.shape[0]
    cos = cos_ref[...]
    sin = sin_ref[...]
    q = proj[:, :NSA_Q_W]
    cos_q = jnp.concatenate([cos] * (NSA_Q_W // LANES), axis=1)
    sin_q = jnp.concatenate([sin] * (NSA_Q_W // LANES), axis=1)
    q_ref[...] = (q * cos_q + _swap_halves(q) * sin_q) * (NSA_HEAD_DIM ** -0.5)
    piece = lambda i: proj[:, NSA_Q_W + i * NSA_KV_W:NSA_Q_W + (i + 1) * NSA_KV_W]
    rope = lambda p: p * cos + _swap_halves(p) * sin
    d = NSA_HEAD_DIM
    kvc_scr[0] = rope(piece(0))
    kvc_scr[1] = piece(1)
    groups = tm // CMP_STRIDE
    first = lax.broadcasted_iota(jnp.int32, (groups, LANES), 1) < d
    for kv in range(2):
        cols = [[] for _ in range(NSA_KV_HEADS)]
        for j in range(CMP_STRIDE // 2):
            even = kvc_scr[kv, pl.ds(2 * j, groups, stride=CMP_STRIDE), :]
            odd = kvc_scr[kv, pl.ds(2 * j + 1, groups, stride=CMP_STRIDE), :]
            cols[0].append(jnp.where(first, even, pltpu.roll(odd, d, 1)))
            cols[1].append(jnp.where(first, pltpu.roll(even, d, 1), odd))
        for hh in range(NSA_KV_HEADS):
            kv16_ref[kv, 0, hh] = jnp.concatenate(cols[hh], axis=1)
    k_sel, k_win = rope(piece(2)), rope(piece(4))
    pos = (pl.program_id(0) % tiles_per_seq) * tm + lax.broadcasted_iota(
        jnp.int32, (tm, SEL_LANES), 0)
    block_id = lax.shift_right_logical(pos, SEL_BLOCK.bit_length() - 1)
    onehot = jnp.where(lax.broadcasted_iota(jnp.int32, (tm, SEL_LANES), 1) == block_id,
                       1.0, 0.0).astype(BF16)
    vst = piece(3).T
    vwt = piece(5).T
    d = NSA_HEAD_DIM
    for hh in range(NSA_KV_HEADS):
        kaug_ref[0, hh] = jnp.concatenate([onehot, k_sel[:, hh * d:(hh + 1) * d].astype(BF16)],
                                          axis=1)
        kwin_ref[0, hh] = k_win[:, hh * d:(hh + 1) * d].astype(BF16)
        vst_ref[0, hh] = vst[hh * d:(hh + 1) * d].astype(BF16)
        vwt_ref[0, hh] = vwt[hh * d:(hh + 1) * d].astype(BF16)
    o = NSA_Q_W + INPROJ_KV_W
    z_ref[...] = proj[:, o:o + SSD_D_INNER]
    o += SSD_D_INNER
    xbc_ref[...] = proj[:, o:o + SSD_XBC]
    o += SSD_XBC
    misc_ref[...] = proj[:, o:o + INPROJ_MISC_W]


def _inproj(x, g, w_in, cos, sin, batch, seq, tm=PROJ_ROWS):
    t, d = x.shape
    offs = np.cumsum(IN_SPLITS)[:-1].tolist()
    q, kc, vc, ks, vs, kw, vw, gl, z, xbc, dt = jnp.split(w_in, offs, -1)
    pad = jnp.zeros((d, INPROJ_MISC_W - gl.shape[1] - dt.shape[1]), w_in.dtype)
    w = jnp.concatenate([q, kc, vc, ks, vs, kw, vw, z, xbc, gl, dt, pad], -1).astype(BF16)
    assert w.shape[1] == INPROJ_WIDTH
    nseq = seq // tm
    row = lambda i: (i, 0)
    const = lambda i: (0, 0)
    hd, hkv = NSA_HEAD_DIM, NSA_KV_HEADS
    by_head_rows = lambda wd: pl.BlockSpec((1, hkv, tm, wd), lambda i: (i // nseq, 0, i % nseq, 0))
    by_head_cols = pl.BlockSpec((1, hkv, hd, tm), lambda i: (i // nseq, 0, 0, i % nseq))
    flat = lambda wd: (pl.BlockSpec((tm, wd), row), jax.ShapeDtypeStruct((t, wd), F32))
    groups = tm // CMP_STRIDE
    kv16_spec = pl.BlockSpec((2, 1, hkv, groups, CMP_STRIDE * hd),
                             lambda i: (0, i // nseq, 0, i % nseq, 0))
    outs = [
        flat(NSA_Q_W),
        (kv16_spec, jax.ShapeDtypeStruct((2, batch, hkv, seq // CMP_STRIDE, CMP_STRIDE * hd), F32)),
        (by_head_rows(SEL_LANES + hd), jax.ShapeDtypeStruct((batch, hkv, seq, SEL_LANES + hd), BF16)),
        (by_head_rows(hd), jax.ShapeDtypeStruct((batch, hkv, seq, hd), BF16)),
        (by_head_cols, jax.ShapeDtypeStruct((batch, hkv, hd, seq), BF16)),
        (by_head_cols, jax.ShapeDtypeStruct((batch, hkv, hd, seq), BF16)),
        flat(SSD_D_INNER),
        flat(SSD_XBC),
        flat(INPROJ_MISC_W),
    ]
    return pl.pallas_call(
        functools.partial(_inproj_kernel, tiles_per_seq=nseq),
        grid=(t // tm,),
        in_specs=[
            pl.BlockSpec((tm, d), row),
            pl.BlockSpec((1, d), const),
            pl.BlockSpec((d, INPROJ_WIDTH), const),
            pl.BlockSpec((tm, LANES), lambda i: (i % nseq, 0)),
            pl.BlockSpec((tm, LANES), lambda i: (i % nseq, 0)),
        ],
        out_specs=[o[0] for o in outs],
        out_shape=[o[1] for o in outs],
        scratch_shapes=[pltpu.VMEM((2, tm, NSA_KV_W), F32)],
        compiler_params=_params("parallel"),
        name="mixer0_in_proj",
    )(x, g.reshape(1, d), w, cos, sin)


def _rope_tables(seq):
    inv = ROPE_THETA ** (-np.arange(0, NSA_HEAD_DIM, 2, dtype=np.float64) / NSA_HEAD_DIM)
    ang = np.arange(seq, dtype=np.float64)[:, None] * inv[None, :]
    cos, sin = np.cos(ang), np.sin(ang)
    reps = LANES // NSA_HEAD_DIM
    cos_t = np.concatenate([cos, cos] * reps, -1).astype(np.float32)
    sin_t = np.concatenate([-sin, sin] * reps, -1).astype(np.float32)
    return jnp.asarray(cos_t), jnp.asarray(sin_t)


def _compress_kernel(k_ref, pe_ref, w1_ref, w2_ref, o_ref):
    k16 = k_ref[0, 0]
    w1 = w1_ref[0]
    half = w1.shape[0] // 2
    first = jnp.dot(k16, w1[:half], precision=HIGHEST, preferred_element_type=F32)
    second = jnp.dot(k16, w1[half:], precision=HIGHEST, preferred_element_type=F32)
    bias = jnp.dot(pe_ref[0], w1, precision=HIGHEST, preferred_element_type=F32)[0:1]
    n = k16.shape[0]
    pre = first + pltpu.roll(second, n - 1, 0) + bias
    o_ref[0, 0] = jnp.dot(_silu(pre), w2_ref[0], precision=HIGHEST, preferred_element_type=F32)


def _compress(kv16, pe, w1, w2):
    two, bh, n, wd = kv16.shape
    d = w2.shape[-1]
    return pl.pallas_call(
        _compress_kernel,
        grid=(two, bh),
        in_specs=[
            pl.BlockSpec((1, 1, n, wd), lambda a, b: (a, b, 0, 0)),
            pl.BlockSpec((1, SUBLANES, pe.shape[-1]), lambda a, b: (a, 0, 0)),
            pl.BlockSpec((1,) + w1.shape[1:], lambda a, b: (a, 0, 0)),
            pl.BlockSpec((1, d, d), lambda a, b: (a, 0, 0)),
        ],
        out_specs=pl.BlockSpec((1, 1, n, d), lambda a, b: (a, b, 0, 0)),
        out_shape=jax.ShapeDtypeStruct((two, bh, n, d), F32),
        compiler_params=_params("parallel", "parallel"),
        name="nsa_compress",
    )(kv16, pe, w1, w2)


def _group_rows(q):
    return jnp.concatenate(
        [q[:, g * NSA_HEAD_DIM:(g + 1) * NSA_HEAD_DIM] for g in range(NSA_GROUP)], axis=0)


def _ungroup_rows(o, tq):
    return jnp.concatenate([o[g * tq:(g + 1) * tq] for g in range(NSA_GROUP)], axis=1)


def _dot_nt_hi(a, b):
    a1 = a.astype(BF16)
    a2 = (a - a1.astype(F32)).astype(BF16)
    b1 = b.astype(BF16)
    b2 = (b - b1.astype(F32)).astype(BF16)
    return _dot_nt(a1, b1) + _dot_nt(a1, b2) + _dot_nt(a2, b1)


def _nsa_select_kernel(q_ref, kc_ref, vc_ref, ovt_ref, oc_ref, biast_ref, *, tq, topk):
    s0 = pl.program_id(1) * tq
    gw = NSA_GROUP * NSA_HEAD_DIM
    heads = range(NSA_KV_HEADS)
    q = q_ref[0]
    s = [_dot_nt_hi(_group_rows(q[:, h * gw:(h + 1) * gw]), kc_ref[h]) for h in heads]
    rows, ncmp = s[0].shape
    t = s0 + (lax.broadcasted_iota(jnp.int32, (rows, ncmp), 0) & (tq - 1))
    cmp_end = lax.broadcasted_iota(jnp.int32, (rows, ncmp), 1) * CMP_STRIDE + (CMP_BLOCK - 1)
    mask = cmp_end <= t
    p = []
    for h in heads:
        sh = jnp.where(mask, s[h], NEG_MASK)
        ph = jnp.where(mask, jnp.exp(sh - jnp.max(sh, -1, keepdims=True)), 0.0)
        p.append(ph / jnp.maximum(jnp.sum(ph, -1, keepdims=True), 1e-30))
    for h in heads:
        o = jnp.dot(p[h].astype(BF16), vc_ref[h].astype(BF16), preferred_element_type=F32)
        oc_ref[0, :, h * gw:(h + 1) * gw] = _ungroup_rows(o, tq)

    ovt = ovt_ref[...]
    imp = []
    for h in heads:
        psum = p[h][0:tq]
        for g in range(1, NSA_GROUP):
            psum = psum + p[h][g * tq:(g + 1) * tq]
        acc = None
        for piece in _split3(psum):
            d = _dot_nt(ovt, piece)
            acc = d if acc is None else acc + d
        imp.append(acc)
    blk = lax.broadcasted_iota(jnp.int32, imp[0].shape, 0)
    tt = s0 + lax.broadcasted_iota(jnp.int32, imp[0].shape, 1)
    cur = lax.shift_right_logical(tt, SEL_BLOCK.bit_length() - 1)
    forced = (blk == 0) | (blk == cur) | (blk == cur - 1)
    valid = blk * SEL_BLOCK <= tt
    x = [jnp.where(valid, jnp.where(forced, FORCE_SCORE, imp[h]), NEG_MASK) for h in heads]
    blk_f = blk.astype(F32)
    sel = [jnp.zeros(blk.shape, jnp.bool_) for _ in heads]
    for _ in range(topk):
        for h in heads:
            m = jnp.max(x[h], 0, keepdims=True)
            idx = jnp.min(jnp.where(x[h] == m, blk_f, float(SEL_LANES)), 0, keepdims=True)
            hit = blk_f == idx
            sel[h] = sel[h] | hit
            x[h] = jnp.where(hit, NEG_TAKEN, x[h])
    for h in heads:
        biast_ref[0, h] = jnp.where(sel[h], 0.0, NEG_UNSELECTED).astype(BF16)


def _nsa_select(q, kc, vc, overlap, batch, seq, tq=NSA_QUERY_ROWS):
    ncmp = kc.shape[1]
    qw = NSA_KV_HEADS * NSA_GROUP * NSA_HEAD_DIM
    topk = min(SEL_TOPK, seq // SEL_BLOCK)
    kern = functools.partial(_nsa_select_kernel, tq=tq, topk=topk)
    return pl.pallas_call(
        kern,
        grid=(batch, seq // tq),
        in_specs=[
            pl.BlockSpec((1, tq, qw), lambda b, i: (b, i, 0)),
            pl.BlockSpec((NSA_KV_HEADS, ncmp, NSA_HEAD_DIM), lambda b, i: (b, 0, 0)),
            pl.BlockSpec((NSA_KV_HEADS, ncmp, NSA_HEAD_DIM), lambda b, i: (b, 0, 0)),
            pl.BlockSpec((SEL_LANES, ncmp), lambda b, i: (0, 0)),
        ],
        out_specs=[
            pl.BlockSpec((1, tq, qw), lambda b, i: (b, i, 0)),
            pl.BlockSpec((1, NSA_KV_HEADS, SEL_LANES, tq), lambda b, i: (b, 0, 0, i)),
        ],
        out_shape=[
            jax.ShapeDtypeStruct((batch, seq, qw), F32),
            jax.ShapeDtypeStruct((batch, NSA_KV_HEADS, SEL_LANES, seq), BF16),
        ],
        compiler_params=_params("parallel", "parallel"),
        name="nsa_compressed_select",
    )(q, kc, vc, overlap)


NSA_LANE_SPLIT = 2


def _nsa_attend_kernel(q_ref, biast_ref, ka_ref, vst_ref, kw_ref, vwt_ref, oc_ref, gl_ref, o_ref,
                       sa_scr, sb_scr, p_scr, w_scr, *, tq, tk):
    i = pl.program_id(2)
    s0 = i * tq
    n = NSA_GROUP * tq
    half = n // NSA_LANE_SPLIT
    d = NSA_HEAD_DIM
    qt = (q_ref[0] * LOG2_E).T
    qgt = jnp.concatenate([qt[g * d:(g + 1) * d] for g in range(NSA_GROUP)], axis=1)
    qgt = qgt.astype(BF16)
    qat = jnp.concatenate([jnp.concatenate([biast_ref[0, 0]] * NSA_GROUP, axis=1), qgt], axis=0)
    init = tuple((jnp.full((1, half), NEG_MASK, F32), jnp.zeros((1, half), F32),
                  jnp.zeros((d, half), F32)) for _ in range(NSA_LANE_SPLIT))

    def query_pos(shape):
        return s0 + (lax.broadcasted_iota(jnp.int32, shape, 1) & (tq - 1))

    halves = range(NSA_LANE_SPLIT)

    def scores(kt):
        k = ka_ref[0, pl.ds(pl.multiple_of(kt * tk, tk), tk), :]
        return tuple(jnp.dot(k, qat[:, hf * half:(hf + 1) * half], preferred_element_type=F32)
                     for hf in halves)

    def values(kt):
        vt = vst_ref[0, :, pl.ds(pl.multiple_of(kt * tk, tk), tk)]
        return tuple(jnp.dot(vt, p_scr[:, hf * half:(hf + 1) * half],
                             preferred_element_type=F32) for hf in halves)

    span = WINDOW + tq
    start = pl.multiple_of(jnp.maximum(s0 - WINDOW, 0), tq)

    def window_scores():
        kwin = kw_ref[0, pl.ds(start, span), :]
        return tuple(jnp.dot(kwin, qgt[:, hf * half:(hf + 1) * half],
                             preferred_element_type=F32) for hf in halves)

    def sel_step(kt, stats, src_scr, dst_scr, causal):
        s_next = window_scores() if causal else scores(kt + 1)
        pv = values(jnp.maximum(kt - 1, 0))
        new_stats = []
        for hf in halves:
            cols = slice(hf * half, (hf + 1) * half)
            m, l, acc = stats[hf]
            s = src_scr[:, cols]
            if causal:
                kp = kt * tk + lax.broadcasted_iota(jnp.int32, s.shape, 0)
                mask = kp <= query_pos(s.shape)
                s = jnp.where(mask, s, NEG_MASK)
            m_new = jnp.maximum(m, jnp.max(s, 0, keepdims=True))
            alpha = jnp.exp2(m - m_new)
            p = jnp.exp2(s - m_new)
            if causal:
                p = jnp.where(mask, p, 0.0)
            new_stats.append((m_new, alpha * l + jnp.sum(p, 0, keepdims=True),
                              alpha * (acc + pv[hf])))
            p_scr[:, cols] = p.astype(BF16)
        for hf in halves:
            (w_scr if causal else dst_scr)[:, hf * half:(hf + 1) * half] = s_next[hf]
        return tuple(new_stats)

    def by_parity(kt, stats, causal):
        return lax.cond((kt & 1) == 0,
                        lambda st: sel_step(kt, st, sa_scr, sb_scr, causal),
                        lambda st: sel_step(kt, st, sb_scr, sa_scr, causal), stats)

    n_full = s0 // tk
    first_scores = scores(0)
    for hf in halves:
        sa_scr[:, hf * half:(hf + 1) * half] = first_scores[hf]
    p_scr[...] = jnp.zeros_like(p_scr)
    stats = lax.fori_loop(0, n_full, lambda kt, c: by_parity(kt, c, False), init)
    stats = by_parity(n_full, stats, True)
    pv_last = values(n_full)

    vwt = vwt_ref[0, :, pl.ds(start, span)]
    p_w, l_w = [], []
    for hf in halves:
        s = w_scr[:, hf * half:(hf + 1) * half]
        kp = start + lax.broadcasted_iota(jnp.int32, s.shape, 0)
        t = query_pos(s.shape)
        mask = (kp <= t) & (kp > t - WINDOW)
        s = jnp.where(mask, s, NEG_MASK)
        p = jnp.where(mask, jnp.exp2(s - jnp.max(s, 0, keepdims=True)), 0.0)
        l_w.append(jnp.sum(p, 0, keepdims=True))
        p_w.append(p.astype(BF16))
    o_w = jnp.concatenate([jnp.dot(vwt, p_w[hf], preferred_element_type=F32)
                           / jnp.maximum(l_w[hf], 1e-30) for hf in halves], axis=1)
    o_s = jnp.concatenate([(stats[hf][2] + pv_last[hf]) / jnp.maximum(stats[hf][1], 1e-30)
                           for hf in halves], axis=1)

    def rows_layout(ot):
        return jnp.concatenate([ot[:, g * tq:(g + 1) * tq] for g in range(NSA_GROUP)], axis=0).T

    o_s = rows_layout(o_s)
    o_w = rows_layout(o_w)
    gates = jax.nn.sigmoid(gl_ref[0, 0])
    o_c = oc_ref[0]
    pieces = []
    for g in range(NSA_GROUP):
        c = slice(g * d, (g + 1) * d)
        pieces.append(gates[:, 3 * g:3 * g + 1] * o_c[:, c]
                      + gates[:, 3 * g + 1:3 * g + 2] * o_s[:, c]
                      + gates[:, 3 * g + 2:3 * g + 3] * o_w[:, c])
    o_ref[0] = jnp.concatenate(pieces, axis=1)


def _nsa_attend(q, biast, kaug, vst, kw, vwt, o_c, gl, batch, seq, tq=NSA_QUERY_ROWS,
                tk=NSA_KEY_ROWS):
    gw = NSA_GROUP * NSA_HEAD_DIM
    tk = min(tk, seq)
    assert tk % tq == 0 and WINDOW % tq == 0 and seq >= WINDOW + tq
    kern = functools.partial(_nsa_attend_kernel, tq=tq, tk=tk)
    bh = lambda b, h, i: (b * NSA_KV_HEADS + h, 0, 0)
    return pl.pallas_call(
        kern,
        grid=(batch, NSA_KV_HEADS, seq // tq),
        in_specs=[
            pl.BlockSpec((1, tq, gw), lambda b, h, i: (b, i, h)),
            pl.BlockSpec((1, 1, SEL_LANES, tq), lambda b, h, i: (b, h, 0, i)),
            pl.BlockSpec((1, seq, SEL_LANES + NSA_HEAD_DIM), bh),
            pl.BlockSpec((1, NSA_HEAD_DIM, seq), bh),
            pl.BlockSpec((1, seq, NSA_HEAD_DIM), bh),
            pl.BlockSpec((1, NSA_HEAD_DIM, seq), bh),
            pl.BlockSpec((1, tq, gw), lambda b, h, i: (b, i, h)),
            pl.BlockSpec((1, 1, tq, NSA_GROUP * 3), lambda b, h, i: (b, h, i, 0)),
        ],
        out_specs=pl.BlockSpec((1, tq, gw), lambda b, h, i: (b, i, h)),
        out_shape=jax.ShapeDtypeStruct((batch, seq, NSA_KV_HEADS * gw), F32),
        scratch_shapes=[pltpu.VMEM((tk, NSA_GROUP * tq), F32),
                        pltpu.VMEM((tk, NSA_GROUP * tq), F32),
                        pltpu.VMEM((tk, NSA_GROUP * tq), BF16),
                        pltpu.VMEM((WINDOW + tq, NSA_GROUP * tq), F32)],
        compiler_params=_params("parallel", "parallel", "arbitrary"),
        name="nsa_selected_window",
    )(q, biast, kaug, vst, kw, vwt, o_c, gl)


def _nsa(q, kv16, kaug, kwin, vst, vwt, misc, pe_k, w1_k, w2_k, pe_v, w1_v, w2_v, batch, seq):
    d = NSA_HEAD_DIM
    bh = batch * NSA_KV_HEADS
    n16 = seq // CMP_STRIDE
    kv16 = kv16.reshape(2, bh, n16, CMP_STRIDE * d)
    pe = jnp.stack([pe_k, pe_v]).reshape(2, 1, CMP_BLOCK * d)
    pe = jnp.broadcast_to(pe, (2, SUBLANES, CMP_BLOCK * d))
    cmp = _compress(kv16, pe, jnp.stack([w1_k, w1_v]), jnp.stack([w2_k, w2_v]))
    kc, vc = cmp[0], cmp[1]

    n_sel = seq // SEL_BLOCK
    cmp_start = np.arange(n16) * CMP_STRIDE
    sel_start = np.arange(SEL_LANES) * SEL_BLOCK
    overlap = ((cmp_start[:, None] < sel_start[None, :] + SEL_BLOCK)
               & (cmp_start[:, None] + CMP_BLOCK - 1 >= sel_start[None, :])
               & (np.arange(SEL_LANES)[None, :] < n_sel)
               & (np.arange(n16)[:, None] < (seq - CMP_BLOCK) // CMP_STRIDE + 1))
    overlap_t = jnp.asarray(overlap.T, BF16)

    q3 = q.reshape(batch, seq, NSA_Q_W)
    o_c, biast = _nsa_select(q3, kc, vc, overlap_t, batch, seq)

    gl = misc[:, :NSA_HEADS * 3].reshape(batch, seq, NSA_KV_HEADS, NSA_GROUP * 3)
    gl = jnp.moveaxis(gl, 2, 1)
    per_head = lambda a: a.reshape((bh,) + a.shape[2:])
    o = _nsa_attend(q3, biast, per_head(kaug), per_head(vst), per_head(kwin), per_head(vwt), o_c,
                    gl, batch, seq)
    return o.reshape(batch * seq, NSA_Q_W)


def _ssd_kernel(xbc_ref, halo_ref, z_ref, dt_ref, cw_ref, cb_ref, dtb_ref, alog_ref, dskip_ref,
                nw_ref, tril_ref, spread_ref, o_ref, state_scr, y_scr):
    c = pl.program_id(1)
    l = SSD_CHUNK

    @pl.when(c == 0)
    def _():
        state_scr[...] = jnp.zeros_like(state_scr)

    x = xbc_ref[0]
    halo = jnp.where(c == 0, 0.0, halo_ref[0])
    xx = jnp.concatenate([halo, x], axis=0)
    cw = cw_ref[...]
    conv = cb_ref[...]
    for k in range(SSD_CONV):
        off = SUBLANES - (SSD_CONV - 1) + k
        conv = conv + cw[k:k + 1] * xx[off:off + l]
    xbc = _silu(conv)
    xs = xbc[:, :SSD_D_INNER]
    gn = SSD_GROUPS * SSD_STATE
    bmat = xbc[:, SSD_D_INNER:SSD_D_INNER + gn]
    cmat = xbc[:, SSD_D_INNER + gn:]

    dt = _softplus(dt_ref[0] + dtb_ref[...])
    da = dt * (-jnp.exp(alog_ref[...]))
    a_cs = _dot_x3_left(tril_ref[...], da)
    a_cs_t = a_cs.T
    a_last = a_cs[l - 1:l]
    causal = (lax.broadcasted_iota(jnp.int32, (l, l), 0)
              >= lax.broadcasted_iota(jnp.int32, (l, l), 1))

    spread = spread_ref[...]
    dt_x = _dot_x3(dt, spread)
    grow_x = _dot_x3(jnp.exp(a_cs), spread)
    fade_x = _dot_x3(jnp.exp(a_last - a_cs), spread)
    chunk_x = _dot_x3(jnp.broadcast_to(jnp.exp(a_last), (SUBLANES, LANES)), spread)[0:1]
    xd = xs * dt_x
    xd16 = xd.astype(BF16)
    fxd16 = (xd * fade_x).astype(BF16)

    pairs = range(SSD_HEADS // 2)
    pairs_per_group = len(pairs) // SSD_GROUPS
    lanes = {c: slice(c * LANES, (c + 1) * LANES) for c in pairs}
    cb, y_off = {}, {}
    for g in range(SSD_GROUPS):
        bg = bmat[:, g * SSD_STATE:(g + 1) * SSD_STATE]
        cg16 = cmat[:, g * SSD_STATE:(g + 1) * SSD_STATE].astype(BF16)
        cb[g] = _dot_nt(cg16, bg.astype(BF16))
        bgt16 = bg.T.astype(BF16)
        for c in range(g * pairs_per_group, (g + 1) * pairs_per_group):
            st = state_scr[c]
            y_off[c] = jnp.dot(cg16, st.astype(BF16), preferred_element_type=F32)
            new = jnp.dot(bgt16, fxd16[:, lanes[c]], preferred_element_type=F32)
            state_scr[c] = st * chunk_x[:, lanes[c]] + new
    first_head = lax.broadcasted_iota(jnp.int32, (l, LANES), 1) < SSD_HEAD_DIM
    y_diag = {}
    for c in pairs:
        for hh in range(2):
            h = 2 * c + hh
            seg = jnp.where(causal, jnp.exp(a_cs[:, h:h + 1] - a_cs_t[h:h + 1, :]), 0.0)
            y_diag[h] = jnp.dot((cb[c // pairs_per_group] * seg).astype(BF16), xd16[:, lanes[c]],
                                preferred_element_type=F32)
    for c in pairs:
        y_scr[:, lanes[c]] = (jnp.where(first_head, y_diag[2 * c], y_diag[2 * c + 1])
                              + y_off[c] * grow_x[:, lanes[c]])

    y = (y_scr[...] + xs * dskip_ref[...]) * _silu(z_ref[0])
    gw = SSD_D_INNER // SSD_GROUPS
    outs = []
    for g in range(SSD_GROUPS):
        yg = y[:, g * gw:(g + 1) * gw]
        outs.append(yg * lax.rsqrt(jnp.mean(yg * yg, -1, keepdims=True) + SSD_NORM_EPS))
    o_ref[0] = jnp.concatenate(outs, axis=1) * nw_ref[...]


def _pad_lanes(v, width=LANES):
    v = v.reshape(1, -1).astype(F32)
    return jnp.pad(v, ((0, 0), (0, width - v.shape[1])))


def _ssd(z, xbc, misc, conv_w, conv_b, dt_bias, a_log, d_skip, norm_w, batch, seq):
    l = SSD_CHUNK
    nc = seq // l
    z3 = z.reshape(batch, seq, SSD_D_INNER)
    x3 = xbc.reshape(batch, seq, SSD_XBC)
    dt = misc[:, NSA_HEADS * 3:NSA_HEADS * 3 + SSD_HEADS]
    dt3 = jnp.pad(dt, ((0, 0), (0, LANES - SSD_HEADS))).reshape(batch, seq, LANES)
    tril = jnp.asarray(np.tril(np.ones((l, l))), BF16)
    spread = np.zeros((LANES, SSD_D_INNER), np.float32)
    spread[np.arange(SSD_D_INNER) // SSD_HEAD_DIM, np.arange(SSD_D_INNER)] = 1.0
    spread = jnp.asarray(spread, BF16)
    hb = l // SUBLANES
    const = lambda b, c: (0, 0)
    return pl.pallas_call(
        _ssd_kernel,
        grid=(batch, nc),
        in_specs=[
            pl.BlockSpec((1, l, SSD_XBC), lambda b, c: (b, c, 0)),
            pl.BlockSpec((1, SUBLANES, SSD_XBC), lambda b, c: (b, jnp.maximum(c * hb - 1, 0), 0)),
            pl.BlockSpec((1, l, SSD_D_INNER), lambda b, c: (b, c, 0)),
            pl.BlockSpec((1, l, LANES), lambda b, c: (b, c, 0)),
            pl.BlockSpec((SSD_CONV, SSD_XBC), const),
            pl.BlockSpec((1, SSD_XBC), const),
            pl.BlockSpec((1, LANES), const),
            pl.BlockSpec((1, LANES), const),
            pl.BlockSpec((1, SSD_D_INNER), const),
            pl.BlockSpec((1, SSD_D_INNER), const),
            pl.BlockSpec((l, l), const),
            pl.BlockSpec((LANES, SSD_D_INNER), const),
        ],
        out_specs=pl.BlockSpec((1, l, SSD_D_INNER), lambda b, c: (b, c, 0)),
        out_shape=jax.ShapeDtypeStruct((batch, seq, SSD_D_INNER), F32),
        scratch_shapes=[pltpu.VMEM((SSD_HEADS // 2, SSD_STATE, 2 * SSD_HEAD_DIM), F32),
                        pltpu.VMEM((l, SSD_D_INNER), F32)],
        compiler_params=_params("parallel", "arbitrary"),
        name="ssd_chunk_scan",
    )(x3, x3, z3, dt3, conv_w.reshape(SSD_CONV, SSD_XBC), conv_b.reshape(1, SSD_XBC),
      _pad_lanes(dt_bias), _pad_lanes(a_log),
      jnp.repeat(d_skip.astype(F32), SSD_HEAD_DIM).reshape(1, SSD_D_INNER),
      norm_w.reshape(1, SSD_D_INNER), tril, spread).reshape(batch * seq, SSD_D_INNER)


def _dot_x2(a, b):
    a1 = a.astype(BF16)
    a2 = (a - a1.astype(F32)).astype(BF16)
    return (jnp.dot(a1, b, preferred_element_type=F32)
            + jnp.dot(a2, b, preferred_element_type=F32))


def _head_sum(x, seg, seg_t):
    sums = _dot_x2(x, seg)
    return jnp.dot(sums.astype(BF16), seg_t, preferred_element_type=F32)


def _rwkv_pre_kernel(x_ref, halo_ref, g_ref, mu_ref, wr_ref, wk_ref, wv_ref, w0_ref, w1_ref,
                     w2_ref, a0_ref, a1_ref, a2_ref, g1_ref, g2_ref, kk_ref, ka_ref, seg_ref,
                     segt_ref, r_out, ld_out, k_out, v_out, kk_out, g_out, bt_out, kt_out, ldt_out,
                     *, tiles_per_seq):
    i = pl.program_id(0)
    h = _rms(x_ref[...], g_ref[...], NORM_EPS)
    prev_row = _rms(halo_ref[...], g_ref[...], NORM_EPS)[SUBLANES - 1:SUBLANES]
    prev_row = jnp.where(i % tiles_per_seq == 0, 0.0, prev_row)
    rowid = lax.broadcasted_iota(jnp.int32, h.shape, 0)
    prev = jnp.where(rowid == 0, prev_row, pltpu.roll(h, 1, 0))
    xx = prev - h
    mu = mu_ref[...]
    mix = lambda j: (h + xx * mu[j:j + 1]).astype(BF16)
    dot = lambda a, w_ref: jnp.dot(a, w_ref[...], preferred_element_type=F32)
    r = dot(mix(0), wr_ref)
    w = -_softplus(-(w0_ref[...] + dot(jnp.tanh(dot(mix(1), w1_ref)).astype(BF16), w2_ref))) - 0.5
    k = dot(mix(2), wk_ref)
    v = dot(mix(3), wv_ref)
    a = jax.nn.sigmoid(a0_ref[...] + dot(dot(mix(4), a1_ref).astype(BF16), a2_ref))
    g = dot(jax.nn.sigmoid(dot(mix(5), g1_ref)).astype(BF16), g2_ref)
    kk = k * kk_ref[...]
    norm = jnp.sqrt(_head_sum(kk * kk, seg_ref[...], segt_ref[...]))
    kk = kk / jnp.maximum(norm, 1e-12)
    k = k * (1.0 + (a - 1.0) * ka_ref[...])
    log_decay = -jnp.exp(w)
    r_out[...] = r
    ld_out[...] = log_decay
    k_out[...] = k
    v_out[...] = v
    kk_out[...] = kk
    g_out[...] = g
    bt_out[0] = (kk * a).T
    kt_out[0] = k.T
    ldt_out[0] = log_decay.T


def _pad_cols(w, width):
    return jnp.pad(w, ((0, 0), (0, width - w.shape[1])))


def _pad_rows(w, width):
    return jnp.pad(w, ((0, width - w.shape[0]), (0, 0)))


def _seg_matrices():
    seg = np.zeros((D_MODEL, LANES), np.float32)
    seg[np.arange(D_MODEL), np.arange(D_MODEL) // RWKV_HEAD_DIM] = 1.0
    return jnp.asarray(seg, BF16), jnp.asarray(seg.T, BF16)


def _rwkv_pre(x, g, mu, w_r, w_k, w_v, w0, w1, w2, a0, a1, a2, g1, g2, k_k, k_a, seq,
              tm=PROJ_ROWS):
    t, d = x.shape
    lora = lambda w: -(-w.shape[1] // LANES) * LANES
    w1p, w2p = _pad_cols(w1, lora(w1)), _pad_rows(w2, lora(w1))
    a1p, a2p = _pad_cols(a1, lora(a1)), _pad_rows(a2, lora(a1))
    g1p, g2p = _pad_cols(g1, lora(g1)), _pad_rows(g2, lora(g1))
    seg, seg_t = _seg_matrices()
    row = lambda i: (i, 0)
    const = lambda i: (0, 0)
    hb = tm // SUBLANES
    vec = lambda v: v.reshape(1, d)
    mats = [w.astype(BF16) for w in (w_r, w_k, w_v)]
    ins = [x, x, vec(g), mu, *mats, vec(w0), w1p.astype(BF16), w2p.astype(BF16), vec(a0),
           a1p.astype(BF16), a2p.astype(BF16), g1p.astype(BF16), g2p.astype(BF16), vec(k_k),
           vec(k_a), seg, seg_t]
    in_specs = [pl.BlockSpec((tm, d), row),
                pl.BlockSpec((SUBLANES, d), lambda i: (jnp.maximum(i * hb - 1, 0), 0))]
    in_specs += [pl.BlockSpec(a.shape, const) for a in ins[2:]]
    tps = seq // tm
    col = pl.BlockSpec((1, d, tm), lambda i: (i // tps, 0, i % tps))
    return pl.pallas_call(
        functools.partial(_rwkv_pre_kernel, tiles_per_seq=tps),
        grid=(t // tm,),
        in_specs=in_specs,
        out_specs=[pl.BlockSpec((tm, d), row)] * 6 + [col] * 3,
        out_shape=([jax.ShapeDtypeStruct((t, d), F32)] * 6
                   + [jax.ShapeDtypeStruct((t // seq, d, seq), F32)] * 3),
        compiler_params=_params("parallel"),
        name="rwkv7_projections",
    )(*ins)


RWKV_CHUNK = 128


def _rwkv_chunk_kernel(r_ref, ld_ref, kk_ref, v_ref, bt_ref, kt_ref, ldt_ref, tril_ref, triu_ref,
                       y_ref, state_scr):
    @pl.when(pl.program_id(1) == 0)
    def _():
        state_scr[...] = jnp.zeros_like(state_scr)

    l = RWKV_CHUNK
    hd = RWKV_HEAD_DIM
    tril = tril_ref[...]
    ld = ld_ref[0]
    c_in = _dot_x3_left(tril, ld)
    a_bar = -kk_ref[0] * jnp.exp(c_in - ld)
    r_bar = r_ref[0] * jnp.exp(c_in)
    v = v_ref[0]
    ldt = ldt_ref[0]
    c_t = _dot_x3(ldt, triu_ref[...])
    scale_t = jnp.exp(-c_t)
    b_t = bt_ref[0] * scale_t
    k_t = kt_ref[0] * scale_t
    decay_col = jnp.exp(c_t[:, l - 1:l])

    row = lax.broadcasted_iota(jnp.int32, (l, l), 0)
    colx = lax.broadcasted_iota(jnp.int32, (l, l), 1)
    strict = row > colx
    incl = row >= colx
    lane = lax.broadcasted_iota(jnp.int32, (l, LANES), 1)
    first_head = lane < hd
    blockdiag = (lax.broadcasted_iota(jnp.int32, (LANES, LANES), 0) < hd) == (
        lax.broadcasted_iota(jnp.int32, (LANES, LANES), 1) < hd)
    mm = lambda a, b: jnp.dot(a, b, preferred_element_type=F32)
    b16 = lambda a: a.astype(BF16)

    pairs = range(D_MODEL // LANES)
    heads = [(c, hh) for c in pairs for hh in range(2)]
    lanes = {c: slice(c * LANES, (c + 1) * LANES) for c in pairs}
    v16 = {c: b16(v[:, lanes[c]]) for c in pairs}
    bk_t = {c: b16(jnp.concatenate([b_t[lanes[c]], k_t[lanes[c]]], axis=1)) for c in pairs}
    h2 = {c: state_scr[c] for c in pairs}
    gh = {}
    for c in pairs:
        rhs = jnp.concatenate([bk_t[c], b16(h2[c])], axis=1)
        a_p, r_p = a_bar[:, lanes[c]], r_bar[:, lanes[c]]
        for hh in range(2):
            keep = first_head if hh == 0 else ~first_head
            x = jnp.concatenate([jnp.where(keep, a_p, 0.0), jnp.where(keep, r_p, 0.0)], axis=0)
            gh[c, hh] = mm(b16(x), rhs)
    mp, u, p_r = {}, {}, {}
    for c, hh in heads:
        g = gh[c, hh]
        mp[c, hh] = b16(jnp.where(strict, g[:l, :l], 0.0))
        m_ak = b16(jnp.where(strict, g[:l, l:2 * l], 0.0))
        p_r[c, hh] = b16(jnp.concatenate([jnp.where(incl, g[l:, :l], 0.0),
                                          jnp.where(incl, g[l:, l:2 * l], 0.0)], axis=1))
        u[c, hh] = g[:l, 2 * l:] + mm(m_ak, v16[c])
    n_factors = l.bit_length() - 1
    for f in range(n_factors):
        du = {h: mm(mp[h], b16(u[h])) for h in heads}
        if f + 1 < n_factors:
            mp = {h: b16(mm(mp[h], mp[h])) for h in heads}
        u = {h: u[h] + du[h] for h in heads}
    ys = {h: gh[h][l:, 2 * l:] + mm(p_r[h], jnp.concatenate([b16(u[h]), v16[h[0]]], axis=0))
          for h in heads}
    for c in pairs:
        u_pair = jnp.where(first_head, u[c, 0], u[c, 1])
        y_ref[0, :, lanes[c]] = jnp.where(first_head, ys[c, 0], ys[c, 1])
        upd = h2[c] + mm(bk_t[c], jnp.concatenate([b16(u_pair), v16[c]], axis=0))
        state_scr[c] = jnp.where(blockdiag, upd * decay_col[lanes[c]], 0.0)


def _rwkv_scan(r, ld, kk, v, bt, kt, ldt, batch, seq):
    l = RWKV_CHUNK
    d = D_MODEL
    rows = lambda x: x.reshape(batch, seq, d)
    rblk = pl.BlockSpec((1, l, d), lambda b, c: (b, c, 0))
    cblk = pl.BlockSpec((1, d, l), lambda b, c: (b, 0, c))
    tril = jnp.asarray(np.tril(np.ones((l, l))), BF16)
    y = pl.pallas_call(
        _rwkv_chunk_kernel,
        grid=(batch, seq // l),
        in_specs=[rblk] * 4 + [cblk] * 3 + [pl.BlockSpec((l, l), lambda b, c: (0, 0))] * 2,
        out_specs=rblk,
        out_shape=jax.ShapeDtypeStruct((batch, seq, d), F32),
        scratch_shapes=[pltpu.VMEM((d // LANES, LANES, LANES), F32)],
        compiler_params=_params("parallel", "arbitrary"),
        name="rwkv7_recurrence",
    )(rows(r), rows(ld), rows(kk), rows(v), bt, kt, ldt, tril, tril.T)
    return y.reshape(batch * seq, d)


def _rwkv_post_kernel(x_ref, y_ref, r_ref, k_ref, v_ref, g_ref, lng_ref, lnb_ref, rk_ref, wo_ref,
                      gn_ref, seg_ref, segt_ref, o_ref):
    seg, seg_t = seg_ref[...], segt_ref[...]
    y = y_ref[...]
    inv = 1.0 / RWKV_HEAD_DIM
    mean = _head_sum(y, seg, seg_t) * inv
    yc = y - mean
    var = _head_sum(yc * yc, seg, seg_t) * inv
    yn = yc * lax.rsqrt(var + RWKV_GN_EPS) * lng_ref[...] + lnb_ref[...]
    bonus = _head_sum(r_ref[...] * k_ref[...] * rk_ref[...], seg, seg_t) * v_ref[...]
    out = ((yn + bonus) * g_ref[...]).astype(BF16)
    proj = jnp.dot(out, wo_ref[...], preferred_element_type=F32)
    o_ref[...] = x_ref[...] + _rms(proj, gn_ref[...], NORM_EPS)


def _rwkv_post(x, y, r, k, v, g, ln_g, ln_b, r_k, w_o, gn, tm=PROJ_ROWS):
    t, d = x.shape
    seg, seg_t = _seg_matrices()
    row = lambda i: (i, 0)
    const = lambda i: (0, 0)
    vec = lambda a: a.reshape(1, d)
    small = [vec(ln_g), vec(ln_b), vec(r_k), w_o.astype(BF16), vec(gn), seg, seg_t]
    return pl.pallas_call(
        _rwkv_post_kernel,
        grid=(t // tm,),
        in_specs=[pl.BlockSpec((tm, d), row)] * 6 + [pl.BlockSpec(a.shape, const) for a in small],
        out_specs=pl.BlockSpec((tm, d), row),
        out_shape=jax.ShapeDtypeStruct((t, d), F32),
        compiler_params=_params("parallel"),
        name="rwkv7_output",
    )(x, y, r, k, v, g, *small)


def _nsa_ssd_mixer(x, g_pre, g_post, cos, sin, w_in, pe_k, w1_k, w2_k, pe_v, w1_v, w2_v, conv_w,
                   conv_b, dt_bias, a_log, d_skip, norm_w, w_out, batch, seq):
    q, kv16, kaug, kwin, vst, vwt, z, xbc, misc = _inproj(x, g_pre, w_in, cos, sin, batch, seq)
    o_a = _nsa(q, kv16, kaug, kwin, vst, vwt, misc, pe_k, w1_k, w2_k, pe_v, w1_v, w2_v, batch, seq)
    o_b = _ssd(z, xbc, misc, conv_w, conv_b, dt_bias, a_log, d_skip, norm_w, batch, seq)
    return g_post, [o_a, o_b], [w_out[:NSA_Q_W], w_out[NSA_Q_W:]]


def _rwkv7_mixer(x, g_pre, g_post, mu, w_r, w_k, w_v, w_o, w0, w1, w2, a0, a1, a2, g1, g2, k_k,
                 k_a, r_k, ln_g, ln_b, batch, seq):
    r, ld, k, v, kk, g, bt, kt, ldt = _rwkv_pre(x, g_pre, mu, w_r, w_k, w_v, w0, w1, w2, a0, a1,
                                                a2, g1, g2, k_k, k_a, seq)
    y = _rwkv_scan(r, ld, kk, v, bt, kt, ldt, batch, seq)
    return _rwkv_post(x, y, r, k, v, g, ln_g, ln_b, r_k, w_o, g_post)


def kernel(x, norm_gains, ffn1_w_gate, ffn1_w_up, ffn1_w_down, ffn2_w_gate, ffn2_w_up, ffn2_w_down, ab_w_in, a_cmp_pe_k, a_cmp_w1_k, a_cmp_w2_k, a_cmp_pe_v, a_cmp_w1_v, a_cmp_w2_v, b_conv_w, b_conv_b, b_dt_bias, b_a_log, b_d_skip, b_norm_w, ab_w_out, c_mu, c_w_r, c_w_k, c_w_v, c_w_o, c_w0, c_w1, c_w2, c_a0, c_a1, c_a2, c_g1, c_g2, c_k_k, c_k_a, c_r_k, c_ln_g, c_ln_b):
    batch, seq, d = x.shape
    depth = norm_gains.shape[0]
    cos, sin = _rope_tables(seq)
    x = x.reshape(batch * seq, d)
    (ffn1_w_gate, ffn1_w_up, ffn1_w_down, ffn2_w_gate, ffn2_w_up, ffn2_w_down) = [
        w.astype(BF16) for w in (ffn1_w_gate, ffn1_w_up, ffn1_w_down, ffn2_w_gate, ffn2_w_up,
                                 ffn2_w_down)]
    for layer in range(depth):
        ng = norm_gains[layer]
        x = _ffn(x, ng[0], ng[1], ffn1_w_gate[layer], ffn1_w_up[layer], ffn1_w_down[layer])
        i = layer // 2
        mixer = None
        if layer % 2 == 0:
            mixer = _nsa_ssd_mixer(x, ng[2], ng[3], cos, sin, ab_w_in[i], a_cmp_pe_k[i],
                                   a_cmp_w1_k[i], a_cmp_w2_k[i], a_cmp_pe_v[i], a_cmp_w1_v[i],
                                   a_cmp_w2_v[i], b_conv_w[i], b_conv_b[i], b_dt_bias[i],
                                   b_a_log[i], b_d_skip[i], b_norm_w[i], ab_w_out[i], batch, seq)
        else:
            x = _rwkv7_mixer(x, ng[2], ng[3], c_mu[i], c_w_r[i], c_w_k[i], c_w_v[i], c_w_o[i],
                             c_w0[i], c_w1[i], c_w2[i], c_a0[i], c_a1[i], c_a2[i], c_g1[i],
                             c_g2[i], c_k_k[i], c_k_a[i], c_r_k[i], c_ln_g[i], c_ln_b[i],
                             batch, seq)
        x = _ffn(x, ng[4], ng[5], ffn2_w_gate[layer], ffn2_w_up[layer], ffn2_w_down[layer],
                 mixer=mixer)
    return x.reshape(batch, seq, d)
```

```python
import functools

import jax
import jax.numpy as jnp
import numpy as np
from jax import lax
from jax.experimental import pallas as pl
from jax.experimental.pallas import tpu as pltpu

F32 = jnp.float32
BF16 = jnp.bfloat16
HIGHEST = lax.Precision.HIGHEST

D_MODEL = 1024
D_FF = 2816
NORM_EPS = 1e-6
NSA_HEADS = 8
NSA_KV_HEADS = 2
NSA_GROUP = NSA_HEADS // NSA_KV_HEADS
NSA_HEAD_DIM = 64
CMP_BLOCK = 32
CMP_STRIDE = 16
SEL_BLOCK = 64
SEL_TOPK = 16
WINDOW = 512
ROPE_THETA = 10000.0
FORCE_SCORE = 1e4
SEL_LANES = 128
SSD_HEADS = 16
SSD_HEAD_DIM = 64
SSD_D_INNER = SSD_HEADS * SSD_HEAD_DIM
SSD_GROUPS = 2
SSD_STATE = 128
SSD_CONV = 4
SSD_CHUNK = 128
SSD_NORM_EPS = 1e-5
SSD_XBC = SSD_D_INNER + 2 * SSD_GROUPS * SSD_STATE
RWKV_HEAD_DIM = 64
RWKV_HEADS = D_MODEL // RWKV_HEAD_DIM
RWKV_GN_EPS = 64e-5

NSA_Q_W = NSA_HEADS * NSA_HEAD_DIM
NSA_KV_W = NSA_KV_HEADS * NSA_HEAD_DIM
IN_SPLITS = (NSA_Q_W, NSA_KV_W, NSA_KV_W, NSA_KV_W, NSA_KV_W, NSA_KV_W, NSA_KV_W,
             NSA_HEADS * 3, SSD_D_INNER, SSD_XBC, SSD_HEADS)
IN_WIDTH = sum(IN_SPLITS)

LANES = 128
SUBLANES = 8
MXU_TILE = 256
VMEM_LIMIT_BYTES = 56 * 1024 * 1024

FFN_ROWS = 512
FFN_COLUMN_CHUNKS = 2
PROJ_ROWS = 256
NSA_QUERY_ROWS = 256
NSA_KEY_ROWS = 512

NEG_MASK = -1e30
NEG_UNSELECTED = -2.0 ** 30
NEG_TAKEN = -3e38
LOG2_E = 1.4426950408889634


def _params(*sem):
    return pltpu.CompilerParams(dimension_semantics=sem, vmem_limit_bytes=VMEM_LIMIT_BYTES)


def _rms(x, g, eps):
    return x * lax.rsqrt(jnp.mean(x * x, -1, keepdims=True) + eps) * g


def _silu(x):
    return x * jax.nn.sigmoid(x)


def _softplus(x):
    return jnp.maximum(x, 0.0) + jnp.log1p(jnp.exp(-jnp.abs(x)))


def _split3(a):
    a1 = a.astype(BF16)
    r1 = a - a1.astype(F32)
    a2 = r1.astype(BF16)
    a3 = (r1 - a2.astype(F32)).astype(BF16)
    return a1, a2, a3


def _dot_x3(a, b):
    acc = None
    for piece in _split3(a):
        d = jnp.dot(piece, b, preferred_element_type=F32)
        acc = d if acc is None else acc + d
    return acc


def _dot_x3_left(b, a):
    acc = None
    for piece in _split3(a):
        d = jnp.dot(b, piece, preferred_element_type=F32)
        acc = d if acc is None else acc + d
    return acc


def _dot_nt(a, b, **kw):
    return lax.dot_general(a, b, (((1,), (1,)), ((), ())), preferred_element_type=F32, **kw)


def _ffn_kernel(x_ref, gi_ref, go_ref, wg_ref, wu_ref, wd_ref, *rest, chunks, n_parts):
    o_ref = rest[-1]
    x = x_ref[...]
    if n_parts:
        proj = None
        for p_ref, w_ref in zip(rest[1:1 + n_parts], rest[1 + n_parts:1 + 2 * n_parts]):
            dd = jnp.dot(p_ref[...].astype(BF16), w_ref[...], preferred_element_type=F32)
            proj = dd if proj is None else proj + dd
        x = x + _rms(proj, rest[0][...], NORM_EPS)
    h = _rms(x, gi_ref[...], NORM_EPS).astype(BF16)
    acc = None
    for lo, hi in chunks:
        gate = jnp.dot(h, wg_ref[:, lo:hi], preferred_element_type=F32)
        up = jnp.dot(h, wu_ref[:, lo:hi], preferred_element_type=F32)
        act = (_silu(gate) * up).astype(BF16)
        part = jnp.dot(act, wd_ref[lo:hi, :], preferred_element_type=F32)
        acc = part if acc is None else acc + part
    o_ref[...] = x + 0.5 * _rms(acc, go_ref[...], NORM_EPS)


def _ffn(x, g_in, g_out, w_gate, w_up, w_down, layer, mixer=None, tm=FFN_ROWS,
         n_chunks=FFN_COLUMN_CHUNKS):
    t, d = x.shape
    f = w_gate.shape[2]
    tiles = f // MXU_TILE
    assert tiles * MXU_TILE == f
    cuts = [MXU_TILE * ((tiles * c + n_chunks - 1) // n_chunks) for c in range(n_chunks + 1)]
    chunks = tuple(zip(cuts[:-1], cuts[1:]))
    row = lambda i: (i, 0)
    const = lambda i: (0, 0)
    resident = lambda shape: pl.BlockSpec(shape, const, pipeline_mode=pl.Buffered(1))
    of_layer = lambda shape: pl.BlockSpec((None,) + shape, lambda i: (layer, 0, 0),
                                          pipeline_mode=pl.Buffered(1))
    operands = [x, g_in.reshape(1, d), g_out.reshape(1, d), w_gate, w_up, w_down]
    in_specs = [pl.BlockSpec((tm, d), row), pl.BlockSpec((1, d), const),
                pl.BlockSpec((1, d), const), of_layer((d, f)), of_layer((d, f)),
                of_layer((f, d))]
    n_parts = 0
    if mixer is not None:
        g_mixer, parts, weights = mixer
        n_parts = len(parts)
        operands += [g_mixer.reshape(1, d), *parts, *[w.astype(BF16) for w in weights]]
        in_specs += ([pl.BlockSpec((1, d), const)]
                     + [pl.BlockSpec((tm, p.shape[1]), row) for p in parts]
                     + [resident(w.shape) for w in weights])
    return pl.pallas_call(
        functools.partial(_ffn_kernel, chunks=chunks, n_parts=n_parts),
        grid=(t // tm,),
        in_specs=in_specs,
        out_specs=pl.BlockSpec((tm, d), row),
        out_shape=jax.ShapeDtypeStruct((t, d), F32),
        compiler_params=_params("parallel"),
        name="ffn_half_step",
    )(*operands)


INPROJ_MISC_W = 256
INPROJ_KV_W = 6 * NSA_KV_W
INPROJ_WIDTH = NSA_Q_W + INPROJ_KV_W + SSD_D_INNER + SSD_XBC + INPROJ_MISC_W


def _swap_halves(x):
    w = x.shape[-1]
    lane = lax.broadcasted_iota(jnp.int32, x.shape, x.ndim - 1)
    low = (lane & (NSA_HEAD_DIM - 1)) < (NSA_HEAD_DIM // 2)
    return jnp.where(low, pltpu.roll(x, w - NSA_HEAD_DIM // 2, x.ndim - 1),
                     pltpu.roll(x, NSA_HEAD_DIM // 2, x.ndim - 1))


def _inproj_kernel(x_ref, g_ref, w_ref, cos_ref, sin_ref, q_ref, kv16_ref, kaug_ref, kwin_ref,
                   vst_ref, vwt_ref, z_ref, xbc_ref, misc_ref, kvc_scr, *, tiles_per_seq):
    h = _rms(x_ref[...], g_ref[...], NORM_EPS).astype(BF16)
    proj = jnp.dot(h, w_ref[...], preferred_element_type=F32)
    tm = proj.shape[0]
    cos = cos_ref[...]
    sin = sin_ref[...]
    q = proj[:, :NSA_Q_W]
    cos_q = jnp.concatenate([cos] * (NSA_Q_W // LANES), axis=1)
    sin_q = jnp.concatenate([sin] * (NSA_Q_W // LANES), axis=1)
    q_ref[...] = (q * cos_q + _swap_halves(q) * sin_q) * (NSA_HEAD_DIM ** -0.5)
    piece = lambda i: proj[:, NSA_Q_W + i * NSA_KV_W:NSA_Q_W + (i + 1) * NSA_KV_W]
    rope = lambda p: p * cos + _swap_halves(p) * sin
    d = NSA_HEAD_DIM
    kvc_scr[0] = rope(piece(0))
    kvc_scr[1] = piece(1)
    groups = tm // CMP_STRIDE
    first = lax.broadcasted_iota(jnp.int32, (groups, LANES), 1) < d
    for kv in range(2):
        cols = [[] for _ in range(NSA_KV_HEADS)]
        for j in range(CMP_STRIDE // 2):
            even = kvc_scr[kv, pl.ds(2 * j, groups, stride=CMP_STRIDE), :]
            odd = kvc_scr[kv, pl.ds(2 * j + 1, groups, stride=CMP_STRIDE), :]
            cols[0].append(jnp.where(first, even, pltpu.roll(odd, d, 1)))
            cols[1].append(jnp.where(first, pltpu.roll(even, d, 1), odd))
        for hh in range(NSA_KV_HEADS):
            kv16_ref[kv, 0, hh] = jnp.concatenate(cols[hh], axis=1)
    k_sel, k_win = rope(piece(2)), rope(piece(4))
    pos = (pl.program_id(0) % tiles_per_seq) * tm + lax.broadcasted_iota(
        jnp.int32, (tm, SEL_LANES), 0)
    block_id = lax.shift_right_logical(pos, SEL_BLOCK.bit_length() - 1)
    onehot = jnp.where(lax.broadcasted_iota(jnp.int32, (tm, SEL_LANES), 1) == block_id,
                       1.0, 0.0).astype(BF16)
    vst = piece(3).T
    vwt = piece(5).T
    d = NSA_HEAD_DIM
    for hh in range(NSA_KV_HEADS):
        kaug_ref[0, hh] = jnp.concatenate([onehot, k_sel[:, hh * d:(hh + 1) * d].astype(BF16)],
                                          axis=1)
        kwin_ref[0, hh] = k_win[:, hh * d:(hh + 1) * d].astype(BF16)
        vst_ref[0, hh] = vst[hh * d:(hh + 1) * d].astype(BF16)
        vwt_ref[0, hh] = vwt[hh * d:(hh + 1) * d].astype(BF16)
    o = NSA_Q_W + INPROJ_KV_W
    z_ref[...] = proj[:, o:o + SSD_D_INNER]
    o += SSD_D_INNER
    xbc_ref[...] = proj[:, o:o + SSD_XBC]
    o += SSD_XBC
    misc_ref[...] = proj[:, o:o + INPROJ_MISC_W]


def _inproj(x, g, w_in, cos, sin, batch, seq, tm=PROJ_ROWS):
    t, d = x.shape
    offs = np.cumsum(IN_SPLITS)[:-1].tolist()
    q, kc, vc, ks, vs, kw, vw, gl, z, xbc, dt = jnp.split(w_in, offs, -1)
    pad = jnp.zeros((d, INPROJ_MISC_W - gl.shape[1] - dt.shape[1]), w_in.dtype)
    w = jnp.concatenate([q, kc, vc, ks, vs, kw, vw, z, xbc, gl, dt, pad], -1).astype(BF16)
    assert w.shape[1] == INPROJ_WIDTH
    nseq = seq // tm
    row = lambda i: (i, 0)
    const = lambda i: (0, 0)
    hd, hkv = NSA_HEAD_DIM, NSA_KV_HEADS
    by_head_rows = lambda wd: pl.BlockSpec((1, hkv, tm, wd), lambda i: (i // nseq, 0, i % nseq, 0))
    by_head_cols = pl.BlockSpec((1, hkv, hd, tm), lambda i: (i // nseq, 0, 0, i % nseq))
    flat = lambda wd: (pl.BlockSpec((tm, wd), row), jax.ShapeDtypeStruct((t, wd), F32))
    groups = tm // CMP_STRIDE
    kv16_spec = pl.BlockSpec((2, 1, hkv, groups, CMP_STRIDE * hd),
                             lambda i: (0, i // nseq, 0, i % nseq, 0))
    outs = [
        flat(NSA_Q_W),
        (kv16_spec, jax.ShapeDtypeStruct((2, batch, hkv, seq // CMP_STRIDE, CMP_STRIDE * hd), F32)),
        (by_head_rows(SEL_LANES + hd), jax.ShapeDtypeStruct((batch, hkv, seq, SEL_LANES + hd), BF16)),
        (by_head_rows(hd), jax.ShapeDtypeStruct((batch, hkv, seq, hd), BF16)),
        (by_head_cols, jax.ShapeDtypeStruct((batch, hkv, hd, seq), BF16)),
        (by_head_cols, jax.ShapeDtypeStruct((batch, hkv, hd, seq), BF16)),
        flat(SSD_D_INNER),
        flat(SSD_XBC),
        flat(INPROJ_MISC_W),
    ]
    return pl.pallas_call(
        functools.partial(_inproj_kernel, tiles_per_seq=nseq),
        grid=(t // tm,),
        in_specs=[
            pl.BlockSpec((tm, d), row),
            pl.BlockSpec((1, d), const),
            pl.BlockSpec((d, INPROJ_WIDTH), const),
            pl.BlockSpec((tm, LANES), lambda i: (i % nseq, 0)),
            pl.BlockSpec((tm, LANES), lambda i: (i % nseq, 0)),
        ],
        out_specs=[o[0] for o in outs],
        out_shape=[o[1] for o in outs],
        scratch_shapes=[pltpu.VMEM((2, tm, NSA_KV_W), F32)],
        compiler_params=_params("parallel"),
        name="mixer0_in_proj",
    )(x, g.reshape(1, d), w, cos, sin)


def _rope_tables(seq):
    inv = ROPE_THETA ** (-np.arange(0, NSA_HEAD_DIM, 2, dtype=np.float64) / NSA_HEAD_DIM)
    ang = np.arange(seq, dtype=np.float64)[:, None] * inv[None, :]
    cos, sin = np.cos(ang), np.sin(ang)
    reps = LANES // NSA_HEAD_DIM
    cos_t = np.concatenate([cos, cos] * reps, -1).astype(np.float32)
    sin_t = np.concatenate([-sin, sin] * reps, -1).astype(np.float32)
    return jnp.asarray(cos_t), jnp.asarray(sin_t)


def _compress_kernel(k_ref, pe_ref, w1_ref, w2_ref, o_ref):
    k16 = k_ref[0, 0]
    w1 = w1_ref[0]
    half = w1.shape[0] // 2
    first = jnp.dot(k16, w1[:half], precision=HIGHEST, preferred_element_type=F32)
    second = jnp.dot(k16, w1[half:], precision=HIGHEST, preferred_element_type=F32)
    bias = jnp.dot(pe_ref[0], w1, precision=HIGHEST, preferred_element_type=F32)[0:1]
    n = k16.shape[0]
    pre = first + pltpu.roll(second, n - 1, 0) + bias
    o_ref[0, 0] = jnp.dot(_silu(pre), w2_ref[0], precision=HIGHEST, preferred_element_type=F32)


def _compress(kv16, pe, w1, w2):
    two, bh, n, wd = kv16.shape
    d = w2.shape[-1]
    return pl.pallas_call(
        _compress_kernel,
        grid=(two, bh),
        in_specs=[
            pl.BlockSpec((1, 1, n, wd), lambda a, b: (a, b, 0, 0)),
            pl.BlockSpec((1, SUBLANES, pe.shape[-1]), lambda a, b: (a, 0, 0)),
            pl.BlockSpec((1,) + w1.shape[1:], lambda a, b: (a, 0, 0)),
            pl.BlockSpec((1, d, d), lambda a, b: (a, 0, 0)),
        ],
        out_specs=pl.BlockSpec((1, 1, n, d), lambda a, b: (a, b, 0, 0)),
        out_shape=jax.ShapeDtypeStruct((two, bh, n, d), F32),
        compiler_params=_params("parallel", "parallel"),
        name="nsa_compress",
    )(kv16, pe, w1, w2)


def _group_rows(q):
    return jnp.concatenate(
        [q[:, g * NSA_HEAD_DIM:(g + 1) * NSA_HEAD_DIM] for g in range(NSA_GROUP)], axis=0)


def _ungroup_rows(o, tq):
    return jnp.concatenate([o[g * tq:(g + 1) * tq] for g in range(NSA_GROUP)], axis=1)


def _dot_nt_hi(a, b):
    a1 = a.astype(BF16)
    a2 = (a - a1.astype(F32)).astype(BF16)
    b1 = b.astype(BF16)
    b2 = (b - b1.astype(F32)).astype(BF16)
    return _dot_nt(a1, b1) + _dot_nt(a1, b2) + _dot_nt(a2, b1)


def _nsa_select_kernel(q_ref, kc_ref, vc_ref, ovt_ref, oc_ref, biast_ref, *, tq, topk):
    s0 = pl.program_id(1) * tq
    gw = NSA_GROUP * NSA_HEAD_DIM
    heads = range(NSA_KV_HEADS)
    q = q_ref[0]
    s = [_dot_nt_hi(_group_rows(q[:, h * gw:(h + 1) * gw]), kc_ref[h]) for h in heads]
    rows, ncmp = s[0].shape
    t = s0 + (lax.broadcasted_iota(jnp.int32, (rows, ncmp), 0) & (tq - 1))
    cmp_end = lax.broadcasted_iota(jnp.int32, (rows, ncmp), 1) * CMP_STRIDE + (CMP_BLOCK - 1)
    mask = cmp_end <= t
    p = []
    for h in heads:
        sh = jnp.where(mask, s[h], NEG_MASK)
        ph = jnp.where(mask, jnp.exp(sh - jnp.max(sh, -1, keepdims=True)), 0.0)
        p.append(ph / jnp.maximum(jnp.sum(ph, -1, keepdims=True), 1e-30))
    for h in heads:
        o = jnp.dot(p[h].astype(BF16), vc_ref[h].astype(BF16), preferred_element_type=F32)
        oc_ref[0, :, h * gw:(h + 1) * gw] = _ungroup_rows(o, tq)

    ovt = ovt_ref[...]
    imp = []
    for h in heads:
        psum = p[h][0:tq]
        for g in range(1, NSA_GROUP):
            psum = psum + p[h][g * tq:(g + 1) * tq]
        acc = None
        for piece in _split3(psum):
            d = _dot_nt(ovt, piece)
            acc = d if acc is None else acc + d
        imp.append(acc)
    blk = lax.broadcasted_iota(jnp.int32, imp[0].shape, 0)
    tt = s0 + lax.broadcasted_iota(jnp.int32, imp[0].shape, 1)
    cur = lax.shift_right_logical(tt, SEL_BLOCK.bit_length() - 1)
    forced = (blk == 0) | (blk == cur) | (blk == cur - 1)
    valid = blk * SEL_BLOCK <= tt
    x = [jnp.where(valid, jnp.where(forced, FORCE_SCORE, imp[h]), NEG_MASK) for h in heads]
    blk_f = blk.astype(F32)
    sel = [jnp.zeros(blk.shape, jnp.bool_) for _ in heads]
    for _ in range(topk):
        for h in heads:
            m = jnp.max(x[h], 0, keepdims=True)
            idx = jnp.min(jnp.where(x[h] == m, blk_f, float(SEL_LANES)), 0, keepdims=True)
            hit = blk_f == idx
            sel[h] = sel[h] | hit
            x[h] = jnp.where(hit, NEG_TAKEN, x[h])
    for h in heads:
        biast_ref[0, h] = jnp.where(sel[h], 0.0, NEG_UNSELECTED).astype(BF16)


def _nsa_select(q, kc, vc, overlap, batch, seq, tq=NSA_QUERY_ROWS):
    ncmp = kc.shape[1]
    qw = NSA_KV_HEADS * NSA_GROUP * NSA_HEAD_DIM
    topk = min(SEL_TOPK, seq // SEL_BLOCK)
    kern = functools.partial(_nsa_select_kernel, tq=tq, topk=topk)
    return pl.pallas_call(
        kern,
        grid=(batch, seq // tq),
        in_specs=[
            pl.BlockSpec((1, tq, qw), lambda b, i: (b, i, 0)),
            pl.BlockSpec((NSA_KV_HEADS, ncmp, NSA_HEAD_DIM), lambda b, i: (b, 0, 0)),
            pl.BlockSpec((NSA_KV_HEADS, ncmp, NSA_HEAD_DIM), lambda b, i: (b, 0, 0)),
            pl.BlockSpec((SEL_LANES, ncmp), lambda b, i: (0, 0)),
        ],
        out_specs=[
            pl.BlockSpec((1, tq, qw), lambda b, i: (b, i, 0)),
            pl.BlockSpec((1, NSA_KV_HEADS, SEL_LANES, tq), lambda b, i: (b, 0, 0, i)),
        ],
        out_shape=[
            jax.ShapeDtypeStruct((batch, seq, qw), F32),
            jax.ShapeDtypeStruct((batch, NSA_KV_HEADS, SEL_LANES, seq), BF16),
        ],
        compiler_params=_params("parallel", "parallel"),
        name="nsa_compressed_select",
    )(q, kc, vc, overlap)


NSA_LANE_SPLIT = 2


def _nsa_attend_kernel(q_ref, biast_ref, ka_ref, vst_ref, kw_ref, vwt_ref, oc_ref, gl_ref, o_ref,
                       sa_scr, sb_scr, p_scr, w_scr, *, tq, tk):
    i = pl.program_id(2)
    s0 = i * tq
    n = NSA_GROUP * tq
    half = n // NSA_LANE_SPLIT
    d = NSA_HEAD_DIM
    qt = (q_ref[0] * LOG2_E).T
    qgt = jnp.concatenate([qt[g * d:(g + 1) * d] for g in range(NSA_GROUP)], axis=1)
    qgt = qgt.astype(BF16)
    qat = jnp.concatenate([jnp.concatenate([biast_ref[0, 0]] * NSA_GROUP, axis=1), qgt], axis=0)
    init = tuple((jnp.full((1, half), NEG_MASK, F32), jnp.zeros((1, half), F32),
                  jnp.zeros((d, half), F32)) for _ in range(NSA_LANE_SPLIT))

    def query_pos(shape):
        return s0 + (lax.broadcasted_iota(jnp.int32, shape, 1) & (tq - 1))

    halves = range(NSA_LANE_SPLIT)

    def scores(kt):
        k = ka_ref[0, pl.ds(pl.multiple_of(kt * tk, tk), tk), :]
        return tuple(jnp.dot(k, qat[:, hf * half:(hf + 1) * half], preferred_element_type=F32)
                     for hf in halves)

    def values(kt):
        vt = vst_ref[0, :, pl.ds(pl.multiple_of(kt * tk, tk), tk)]
        return tuple(jnp.dot(vt, p_scr[:, hf * half:(hf + 1) * half],
                             preferred_element_type=F32) for hf in halves)

    span = WINDOW + tq
    start = pl.multiple_of(jnp.maximum(s0 - WINDOW, 0), tq)

    def window_scores():
        kwin = kw_ref[0, pl.ds(start, span), :]
        return tuple(jnp.dot(kwin, qgt[:, hf * half:(hf + 1) * half],
                             preferred_element_type=F32) for hf in halves)

    def sel_step(kt, stats, src_scr, dst_scr, causal):
        s_next = window_scores() if causal else scores(kt + 1)
        pv = values(jnp.maximum(kt - 1, 0))
        new_stats = []
        for hf in halves:
            cols = slice(hf * half, (hf + 1) * half)
            m, l, acc = stats[hf]
            s = src_scr[:, cols]
            if causal:
                kp = kt * tk + lax.broadcasted_iota(jnp.int32, s.shape, 0)
                mask = kp <= query_pos(s.shape)
                s = jnp.where(mask, s, NEG_MASK)
            m_new = jnp.maximum(m, jnp.max(s, 0, keepdims=True))
            alpha = jnp.exp2(m - m_new)
            p = jnp.exp2(s - m_new)
            if causal:
                p = jnp.where(mask, p, 0.0)
            new_stats.append((m_new, alpha * l + jnp.sum(p, 0, keepdims=True),
                              alpha * (acc + pv[hf])))
            p_scr[:, cols] = p.astype(BF16)
        for hf in halves:
            (w_scr if causal else dst_scr)[:, hf * half:(hf + 1) * half] = s_next[hf]
        return tuple(new_stats)

    def by_parity(kt, stats, causal):
        return lax.cond((kt & 1) == 0,
                        lambda st: sel_step(kt, st, sa_scr, sb_scr, causal),
                        lambda st: sel_step(kt, st, sb_scr, sa_scr, causal), stats)

    n_full = s0 // tk
    first_scores = scores(0)
    for hf in halves:
        sa_scr[:, hf * half:(hf + 1) * half] = first_scores[hf]
    p_scr[...] = jnp.zeros_like(p_scr)
    stats = lax.fori_loop(0, n_full, lambda kt, c: by_parity(kt, c, False), init)
    stats = by_parity(n_full, stats, True)
    pv_last = values(n_full)

    vwt = vwt_ref[0, :, pl.ds(start, span)]
    p_w, l_w = [], []
    for hf in halves:
        s = w_scr[:, hf * half:(hf + 1) * half]
        kp = start + lax.broadcasted_iota(jnp.int32, s.shape, 0)
        t = query_pos(s.shape)
        mask = (kp <= t) & (kp > t - WINDOW)
        s = jnp.where(mask, s, NEG_MASK)
        p = jnp.where(mask, jnp.exp2(s - jnp.max(s, 0, keepdims=True)), 0.0)
        l_w.append(jnp.sum(p, 0, keepdims=True))
        p_w.append(p.astype(BF16))
    o_w = jnp.concatenate([jnp.dot(vwt, p_w[hf], preferred_element_type=F32)
                           / jnp.maximum(l_w[hf], 1e-30) for hf in halves], axis=1)
    o_s = jnp.concatenate([(stats[hf][2] + pv_last[hf]) / jnp.maximum(stats[hf][1], 1e-30)
                           for hf in halves], axis=1)

    def rows_layout(ot):
        return jnp.concatenate([ot[:, g * tq:(g + 1) * tq] for g in range(NSA_GROUP)], axis=0).T

    o_s = rows_layout(o_s)
    o_w = rows_layout(o_w)
    gates = jax.nn.sigmoid(gl_ref[0, 0])
    o_c = oc_ref[0]
    pieces = []
    for g in range(NSA_GROUP):
        c = slice(g * d, (g + 1) * d)
        pieces.append(gates[:, 3 * g:3 * g + 1] * o_c[:, c]
                      + gates[:, 3 * g + 1:3 * g + 2] * o_s[:, c]
                      + gates[:, 3 * g + 2:3 * g + 3] * o_w[:, c])
    o_ref[0] = jnp.concatenate(pieces, axis=1)


def _nsa_attend(q, biast, kaug, vst, kw, vwt, o_c, gl, batch, seq, tq=NSA_QUERY_ROWS,
                tk=NSA_KEY_ROWS):
    gw = NSA_GROUP * NSA_HEAD_DIM
    tk = min(tk, seq)
    assert tk % tq == 0 and WINDOW % tq == 0 and seq >= WINDOW + tq
    kern = functools.partial(_nsa_attend_kernel, tq=tq, tk=tk)
    bh = lambda b, h, i: (b * NSA_KV_HEADS + h, 0, 0)
    return pl.pallas_call(
        kern,
        grid=(batch, NSA_KV_HEADS, seq // tq),
        in_specs=[
            pl.BlockSpec((1, tq, gw), lambda b, h, i: (b, i, h)),
            pl.BlockSpec((1, 1, SEL_LANES, tq), lambda b, h, i: (b, h, 0, i)),
            pl.BlockSpec((1, seq, SEL_LANES + NSA_HEAD_DIM), bh),
            pl.BlockSpec((1, NSA_HEAD_DIM, seq), bh),
            pl.BlockSpec((1, seq, NSA_HEAD_DIM), bh),
            pl.BlockSpec((1, NSA_HEAD_DIM, seq), bh),
            pl.BlockSpec((1, tq, gw), lambda b, h, i: (b, i, h)),
            pl.BlockSpec((1, 1, tq, NSA_GROUP * 3), lambda b, h, i: (b, h, i, 0)),
        ],
        out_specs=pl.BlockSpec((1, tq, gw), lambda b, h, i: (b, i, h)),
        out_shape=jax.ShapeDtypeStruct((batch, seq, NSA_KV_HEADS * gw), F32),
        scratch_shapes=[pltpu.VMEM((tk, NSA_GROUP * tq), F32),
                        pltpu.VMEM((tk, NSA_GROUP * tq), F32),
                        pltpu.VMEM((tk, NSA_GROUP * tq), BF16),
                        pltpu.VMEM((WINDOW + tq, NSA_GROUP * tq), F32)],
        compiler_params=_params("parallel", "parallel", "arbitrary"),
        name="nsa_selected_window",
    )(q, biast, kaug, vst, kw, vwt, o_c, gl)


def _nsa(q, kv16, kaug, kwin, vst, vwt, misc, pe_k, w1_k, w2_k, pe_v, w1_v, w2_v, batch, seq):
    d = NSA_HEAD_DIM
    bh = batch * NSA_KV_HEADS
    n16 = seq // CMP_STRIDE
    kv16 = kv16.reshape(2, bh, n16, CMP_STRIDE * d)
    pe = jnp.stack([pe_k, pe_v]).reshape(2, 1, CMP_BLOCK * d)
    pe = jnp.broadcast_to(pe, (2, SUBLANES, CMP_BLOCK * d))
    cmp = _compress(kv16, pe, jnp.stack([w1_k, w1_v]), jnp.stack([w2_k, w2_v]))
    kc, vc = cmp[0], cmp[1]

    n_sel = seq // SEL_BLOCK
    cmp_start = np.arange(n16) * CMP_STRIDE
    sel_start = np.arange(SEL_LANES) * SEL_BLOCK
    overlap = ((cmp_start[:, None] < sel_start[None, :] + SEL_BLOCK)
               & (cmp_start[:, None] + CMP_BLOCK - 1 >= sel_start[None, :])
               & (np.arange(SEL_LANES)[None, :] < n_sel)
               & (np.arange(n16)[:, None] < (seq - CMP_BLOCK) // CMP_STRIDE + 1))
    overlap_t = jnp.asarray(overlap.T, BF16)

    q3 = q.reshape(batch, seq, NSA_Q_W)
    o_c, biast = _nsa_select(q3, kc, vc, overlap_t, batch, seq)

    gl = misc[:, :NSA_HEADS * 3].reshape(batch, seq, NSA_KV_HEADS, NSA_GROUP * 3)
    gl = jnp.moveaxis(gl, 2, 1)
    per_head = lambda a: a.reshape((bh,) + a.shape[2:])
    o = _nsa_attend(q3, biast, per_head(kaug), per_head(vst), per_head(kwin), per_head(vwt), o_c,
                    gl, batch, seq)
    return o.reshape(batch * seq, NSA_Q_W)


def _ssd_kernel(xbc_ref, halo_ref, z_ref, dt_ref, cw_ref, cb_ref, dtb_ref, alog_ref, dskip_ref,
                nw_ref, tril_ref, spread_ref, o_ref, state_scr, y_scr):
    c = pl.program_id(1)
    l = SSD_CHUNK

    @pl.when(c == 0)
    def _():
        state_scr[...] = jnp.zeros_like(state_scr)

    x = xbc_ref[0]
    halo = jnp.where(c == 0, 0.0, halo_ref[0])
    xx = jnp.concatenate([halo, x], axis=0)
    cw = cw_ref[...]
    conv = cb_ref[...]
    for k in range(SSD_CONV):
        off = SUBLANES - (SSD_CONV - 1) + k
        conv = conv + cw[k:k + 1] * xx[off:off + l]
    xbc = _silu(conv)
    xs = xbc[:, :SSD_D_INNER]
    gn = SSD_GROUPS * SSD_STATE
    bmat = xbc[:, SSD_D_INNER:SSD_D_INNER + gn]
    cmat = xbc[:, SSD_D_INNER + gn:]

    dt = _softplus(dt_ref[0] + dtb_ref[...])
    da = dt * (-jnp.exp(alog_ref[...]))
    a_cs = _dot_x3_left(tril_ref[...], da)
    a_cs_t = a_cs.T
    a_last = a_cs[l - 1:l]
    causal = (lax.broadcasted_iota(jnp.int32, (l, l), 0)
              >= lax.broadcasted_iota(jnp.int32, (l, l), 1))

    spread = spread_ref[...]
    dt_x = _dot_x3(dt, spread)
    grow_x = _dot_x3(jnp.exp(a_cs), spread)
    fade_x = _dot_x3(jnp.exp(a_last - a_cs), spread)
    chunk_x = _dot_x3(jnp.broadcast_to(jnp.exp(a_last), (SUBLANES, LANES)), spread)[0:1]
    xd = xs * dt_x
    xd16 = xd.astype(BF16)
    fxd16 = (xd * fade_x).astype(BF16)

    pairs = range(SSD_HEADS // 2)
    pairs_per_group = len(pairs) // SSD_GROUPS
    lanes = {c: slice(c * LANES, (c + 1) * LANES) for c in pairs}
    cb, y_off = {}, {}
    for g in range(SSD_GROUPS):
        bg = bmat[:, g * SSD_STATE:(g + 1) * SSD_STATE]
        cg16 = cmat[:, g * SSD_STATE:(g + 1) * SSD_STATE].astype(BF16)
        cb[g] = _dot_nt(cg16, bg.astype(BF16))
        bgt16 = bg.T.astype(BF16)
        for c in range(g * pairs_per_group, (g + 1) * pairs_per_group):
            st = state_scr[c]
            y_off[c] = jnp.dot(cg16, st.astype(BF16), preferred_element_type=F32)
            new = jnp.dot(bgt16, fxd16[:, lanes[c]], preferred_element_type=F32)
            state_scr[c] = st * chunk_x[:, lanes[c]] + new
    first_head = lax.broadcasted_iota(jnp.int32, (l, LANES), 1) < SSD_HEAD_DIM
    y_diag = {}
    for c in pairs:
        for hh in range(2):
            h = 2 * c + hh
            seg = jnp.where(causal, jnp.exp(a_cs[:, h:h + 1] - a_cs_t[h:h + 1, :]), 0.0)
            y_diag[h] = jnp.dot((cb[c // pairs_per_group] * seg).astype(BF16), xd16[:, lanes[c]],
                                preferred_element_type=F32)
    for c in pairs:
        y_scr[:, lanes[c]] = (jnp.where(first_head, y_diag[2 * c], y_diag[2 * c + 1])
                              + y_off[c] * grow_x[:, lanes[c]])

    y = (y_scr[...] + xs * dskip_ref[...]) * _silu(z_ref[0])
    gw = SSD_D_INNER // SSD_GROUPS
    outs = []
    for g in range(SSD_GROUPS):
        yg = y[:, g * gw:(g + 1) * gw]
        outs.append(yg * lax.rsqrt(jnp.mean(yg * yg, -1, keepdims=True) + SSD_NORM_EPS))
    o_ref[0] = jnp.concatenate(outs, axis=1) * nw_ref[...]


def _pad_lanes(v, width=LANES):
    v = v.reshape(1, -1).astype(F32)
    return jnp.pad(v, ((0, 0), (0, width - v.shape[1])))


def _ssd(z, xbc, misc, conv_w, conv_b, dt_bias, a_log, d_skip, norm_w, batch, seq):
    l = SSD_CHUNK
    nc = seq // l
    z3 = z.reshape(batch, seq, SSD_D_INNER)
    x3 = xbc.reshape(batch, seq, SSD_XBC)
    dt = misc[:, NSA_HEADS * 3:NSA_HEADS * 3 + SSD_HEADS]
    dt3 = jnp.pad(dt, ((0, 0), (0, LANES - SSD_HEADS))).reshape(batch, seq, LANES)
    tril = jnp.asarray(np.tril(np.ones((l, l))), BF16)
    spread = np.zeros((LANES, SSD_D_INNER), np.float32)
    spread[np.arange(SSD_D_INNER) // SSD_HEAD_DIM, np.arange(SSD_D_INNER)] = 1.0
    spread = jnp.asarray(spread, BF16)
    hb = l // SUBLANES
    const = lambda b, c: (0, 0)
    return pl.pallas_call(
        _ssd_kernel,
        grid=(batch, nc),
        in_specs=[
            pl.BlockSpec((1, l, SSD_XBC), lambda b, c: (b, c, 0)),
            pl.BlockSpec((1, SUBLANES, SSD_XBC), lambda b, c: (b, jnp.maximum(c * hb - 1, 0), 0)),
            pl.BlockSpec((1, l, SSD_D_INNER), lambda b, c: (b, c, 0)),
            pl.BlockSpec((1, l, LANES), lambda b, c: (b, c, 0)),
            pl.BlockSpec((SSD_CONV, SSD_XBC), const),
            pl.BlockSpec((1, SSD_XBC), const),
            pl.BlockSpec((1, LANES), const),
            pl.BlockSpec((1, LANES), const),
            pl.BlockSpec((1, SSD_D_INNER), const),
            pl.BlockSpec((1, SSD_D_INNER), const),
            pl.BlockSpec((l, l), const),
            pl.BlockSpec((LANES, SSD_D_INNER), const),
        ],
        out_specs=pl.BlockSpec((1, l, SSD_D_INNER), lambda b, c: (b, c, 0)),
        out_shape=jax.ShapeDtypeStruct((batch, seq, SSD_D_INNER), F32),
        scratch_shapes=[pltpu.VMEM((SSD_HEADS // 2, SSD_STATE, 2 * SSD_HEAD_DIM), F32),
                        pltpu.VMEM((l, SSD_D_INNER), F32)],
        compiler_params=_params("parallel", "arbitrary"),
        name="ssd_chunk_scan",
    )(x3, x3, z3, dt3, conv_w.reshape(SSD_CONV, SSD_XBC), conv_b.reshape(1, SSD_XBC),
      _pad_lanes(dt_bias), _pad_lanes(a_log),
      jnp.repeat(d_skip.astype(F32), SSD_HEAD_DIM).reshape(1, SSD_D_INNER),
      norm_w.reshape(1, SSD_D_INNER), tril, spread).reshape(batch * seq, SSD_D_INNER)


def _dot_x2(a, b):
    a1 = a.astype(BF16)
    a2 = (a - a1.astype(F32)).astype(BF16)
    return (jnp.dot(a1, b, preferred_element_type=F32)
            + jnp.dot(a2, b, preferred_element_type=F32))


def _head_sum(x, seg, seg_t):
    sums = _dot_x2(x, seg)
    return jnp.dot(sums.astype(BF16), seg_t, preferred_element_type=F32)


def _rwkv_pre_kernel(x_ref, halo_ref, g_ref, mu_ref, wr_ref, wk_ref, wv_ref, w0_ref, w1_ref,
                     w2_ref, a0_ref, a1_ref, a2_ref, g1_ref, g2_ref, kk_ref, ka_ref, seg_ref,
                     segt_ref, r_out, ld_out, k_out, v_out, kk_out, g_out, bt_out, kt_out, ldt_out,
                     *, tiles_per_seq):
    i = pl.program_id(0)
    h = _rms(x_ref[...], g_ref[...], NORM_EPS)
    prev_row = _rms(halo_ref[...], g_ref[...], NORM_EPS)[SUBLANES - 1:SUBLANES]
    prev_row = jnp.where(i % tiles_per_seq == 0, 0.0, prev_row)
    rowid = lax.broadcasted_iota(jnp.int32, h.shape, 0)
    prev = jnp.where(rowid == 0, prev_row, pltpu.roll(h, 1, 0))
    xx = prev - h
    mu = mu_ref[...]
    mix = lambda j: (h + xx * mu[j:j + 1]).astype(BF16)
    dot = lambda a, w_ref: jnp.dot(a, w_ref[...], preferred_element_type=F32)
    r = dot(mix(0), wr_ref)
    w = -_softplus(-(w0_ref[...] + dot(jnp.tanh(dot(mix(1), w1_ref)).astype(BF16), w2_ref))) - 0.5
    k = dot(mix(2), wk_ref)
    v = dot(mix(3), wv_ref)
    a = jax.nn.sigmoid(a0_ref[...] + dot(dot(mix(4), a1_ref).astype(BF16), a2_ref))
    g = dot(jax.nn.sigmoid(dot(mix(5), g1_ref)).astype(BF16), g2_ref)
    kk = k * kk_ref[...]
    norm = jnp.sqrt(_head_sum(kk * kk, seg_ref[...], segt_ref[...]))
    kk = kk / jnp.maximum(norm, 1e-12)
    k = k * (1.0 + (a - 1.0) * ka_ref[...])
    log_decay = -jnp.exp(w)
    r_out[...] = r
    ld_out[...] = log_decay
    k_out[...] = k
    v_out[...] = v
    kk_out[...] = kk
    g_out[...] = g
    bt_out[0] = (kk * a).T
    kt_out[0] = k.T
    ldt_out[0] = log_decay.T


def _pad_cols(w, width):
    return jnp.pad(w, ((0, 0), (0, width - w.shape[1])))


def _pad_rows(w, width):
    return jnp.pad(w, ((0, width - w.shape[0]), (0, 0)))


def _seg_matrices():
    seg = np.zeros((D_MODEL, LANES), np.float32)
    seg[np.arange(D_MODEL), np.arange(D_MODEL) // RWKV_HEAD_DIM] = 1.0
    return jnp.asarray(seg, BF16), jnp.asarray(seg.T, BF16)


def _rwkv_pre(x, g, mu, w_r, w_k, w_v, w0, w1, w2, a0, a1, a2, g1, g2, k_k, k_a, seq,
              tm=PROJ_ROWS):
    t, d = x.shape
    lora = lambda w: -(-w.shape[1] // LANES) * LANES
    w1p, w2p = _pad_cols(w1, lora(w1)), _pad_rows(w2, lora(w1))
    a1p, a2p = _pad_cols(a1, lora(a1)), _pad_rows(a2, lora(a1))
    g1p, g2p = _pad_cols(g1, lora(g1)), _pad_rows(g2, lora(g1))
    seg, seg_t = _seg_matrices()
    row = lambda i: (i, 0)
    const = lambda i: (0, 0)
    hb = tm // SUBLANES
    vec = lambda v: v.reshape(1, d)
    mats = [w.astype(BF16) for w in (w_r, w_k, w_v)]
    ins = [x, x, vec(g), mu, *mats, vec(w0), w1p.astype(BF16), w2p.astype(BF16), vec(a0),
           a1p.astype(BF16), a2p.astype(BF16), g1p.astype(BF16), g2p.astype(BF16), vec(k_k),
           vec(k_a), seg, seg_t]
    in_specs = [pl.BlockSpec((tm, d), row),
                pl.BlockSpec((SUBLANES, d), lambda i: (jnp.maximum(i * hb - 1, 0), 0))]
    in_specs += [pl.BlockSpec(a.shape, const) for a in ins[2:]]
    tps = seq // tm
    col = pl.BlockSpec((1, d, tm), lambda i: (i // tps, 0, i % tps))
    return pl.pallas_call(
        functools.partial(_rwkv_pre_kernel, tiles_per_seq=tps),
        grid=(t // tm,),
        in_specs=in_specs,
        out_specs=[pl.BlockSpec((tm, d), row)] * 6 + [col] * 3,
        out_shape=([jax.ShapeDtypeStruct((t, d), F32)] * 6
                   + [jax.ShapeDtypeStruct((t // seq, d, seq), F32)] * 3),
        compiler_params=_params("parallel"),
        name="rwkv7_projections",
    )(*ins)


RWKV_CHUNK = 128


def _rwkv_chunk_kernel(r_ref, ld_ref, kk_ref, v_ref, bt_ref, kt_ref, ldt_ref, tril_ref, triu_ref,
                       y_ref, state_scr):
    @pl.when(pl.program_id(1) == 0)
    def _():
        state_scr[...] = jnp.zeros_like(state_scr)

    l = RWKV_CHUNK
    hd = RWKV_HEAD_DIM
    tril = tril_ref[...]
    ld = ld_ref[0]
    c_in = _dot_x3_left(tril, ld)
    a_bar = -kk_ref[0] * jnp.exp(c_in - ld)
    r_bar = r_ref[0] * jnp.exp(c_in)
    v = v_ref[0]
    ldt = ldt_ref[0]
    c_t = _dot_x3(ldt, triu_ref[...])
    scale_t = jnp.exp(-c_t)
    b_t = bt_ref[0] * scale_t
    k_t = kt_ref[0] * scale_t
    decay_col = jnp.exp(c_t[:, l - 1:l])

    row = lax.broadcasted_iota(jnp.int32, (l, l), 0)
    colx = lax.broadcasted_iota(jnp.int32, (l, l), 1)
    strict = row > colx
    incl = row >= colx
    lane = lax.broadcasted_iota(jnp.int32, (l, LANES), 1)
    first_head = lane < hd
    blockdiag = (lax.broadcasted_iota(jnp.int32, (LANES, LANES), 0) < hd) == (
        lax.broadcasted_iota(jnp.int32, (LANES, LANES), 1) < hd)
    mm = lambda a, b: jnp.dot(a, b, preferred_element_type=F32)
    b16 = lambda a: a.astype(BF16)

    pairs = range(D_MODEL // LANES)
    heads = [(c, hh) for c in pairs for hh in range(2)]
    lanes = {c: slice(c * LANES, (c + 1) * LANES) for c in pairs}
    v16 = {c: b16(v[:, lanes[c]]) for c in pairs}
    bk_t = {c: b16(jnp.concatenate([b_t[lanes[c]], k_t[lanes[c]]], axis=1)) for c in pairs}
    h2 = {c: state_scr[c] for c in pairs}
    gh = {}
    for c in pairs:
        rhs = jnp.concatenate([bk_t[c], b16(h2[c])], axis=1)
        a_p, r_p = a_bar[:, lanes[c]], r_bar[:, lanes[c]]
        for hh in range(2):
            keep = first_head if hh == 0 else ~first_head
            x = jnp.concatenate([jnp.where(keep, a_p, 0.0), jnp.where(keep, r_p, 0.0)], axis=0)
            gh[c, hh] = mm(b16(x), rhs)
    mp, u, p_r = {}, {}, {}
    for c, hh in heads:
        g = gh[c, hh]
        mp[c, hh] = b16(jnp.where(strict, g[:l, :l], 0.0))
        m_ak = b16(jnp.where(strict, g[:l, l:2 * l], 0.0))
        p_r[c, hh] = b16(jnp.concatenate([jnp.where(incl, g[l:, :l], 0.0),
                                          jnp.where(incl, g[l:, l:2 * l], 0.0)], axis=1))
        u[c, hh] = g[:l, 2 * l:] + mm(m_ak, v16[c])
    n_factors = l.bit_length() - 1
    for f in range(n_factors):
        du = {h: mm(mp[h], b16(u[h])) for h in heads}
        if f + 1 < n_factors:
            mp = {h: b16(mm(mp[h], mp[h])) for h in heads}
        u = {h: u[h] + du[h] for h in heads}
    ys = {h: gh[h][l:, 2 * l:] + mm(p_r[h], jnp.concatenate([b16(u[h]), v16[h[0]]], axis=0))
          for h in heads}
    for c in pairs:
        u_pair = jnp.where(first_head, u[c, 0], u[c, 1])
        y_ref[0, :, lanes[c]] = jnp.where(first_head, ys[c, 0], ys[c, 1])
        upd = h2[c] + mm(bk_t[c], jnp.concatenate([b16(u_pair), v16[c]], axis=0))
        state_scr[c] = jnp.where(blockdiag, upd * decay_col[lanes[c]], 0.0)


def _rwkv_scan(r, ld, kk, v, bt, kt, ldt, batch, seq):
    l = RWKV_CHUNK
    d = D_MODEL
    rows = lambda x: x.reshape(batch, seq, d)
    rblk = pl.BlockSpec((1, l, d), lambda b, c: (b, c, 0))
    cblk = pl.BlockSpec((1, d, l), lambda b, c: (b, 0, c))
    tril = jnp.asarray(np.tril(np.ones((l, l))), BF16)
    y = pl.pallas_call(
        _rwkv_chunk_kernel,
        grid=(batch, seq // l),
        in_specs=[rblk] * 4 + [cblk] * 3 + [pl.BlockSpec((l, l), lambda b, c: (0, 0))] * 2,
        out_specs=rblk,
        out_shape=jax.ShapeDtypeStruct((batch, seq, d), F32),
        scratch_shapes=[pltpu.VMEM((d // LANES, LANES, LANES), F32)],
        compiler_params=_params("parallel", "arbitrary"),
        name="rwkv7_recurrence",
    )(rows(r), rows(ld), rows(kk), rows(v), bt, kt, ldt, tril, tril.T)
    return y.reshape(batch * seq, d)


def _rwkv_post_kernel(x_ref, y_ref, r_ref, k_ref, v_ref, g_ref, lng_ref, lnb_ref, rk_ref, wo_ref,
                      gn_ref, seg_ref, segt_ref, o_ref):
    seg, seg_t = seg_ref[...], segt_ref[...]
    y = y_ref[...]
    inv = 1.0 / RWKV_HEAD_DIM
    mean = _head_sum(y, seg, seg_t) * inv
    yc = y - mean
    var = _head_sum(yc * yc, seg, seg_t) * inv
    yn = yc * lax.rsqrt(var + RWKV_GN_EPS) * lng_ref[...] + lnb_ref[...]
    bonus = _head_sum(r_ref[...] * k_ref[...] * rk_ref[...], seg, seg_t) * v_ref[...]
    out = ((yn + bonus) * g_ref[...]).astype(BF16)
    proj = jnp.dot(out, wo_ref[...], preferred_element_type=F32)
    o_ref[...] = x_ref[...] + _rms(proj, gn_ref[...], NORM_EPS)


def _rwkv_post(x, y, r, k, v, g, ln_g, ln_b, r_k, w_o, gn, tm=PROJ_ROWS):
    t, d = x.shape
    seg, seg_t = _seg_matrices()
    row = lambda i: (i, 0)
    const = lambda i: (0, 0)
    vec = lambda a: a.reshape(1, d)
    small = [vec(ln_g), vec(ln_b), vec(r_k), w_o.astype(BF16), vec(gn), seg, seg_t]
    return pl.pallas_call(
        _rwkv_post_kernel,
        grid=(t // tm,),
        in_specs=[pl.BlockSpec((tm, d), row)] * 6 + [pl.BlockSpec(a.shape, const) for a in small],
        out_specs=pl.BlockSpec((tm, d), row),
        out_shape=jax.ShapeDtypeStruct((t, d), F32),
        compiler_params=_params("parallel"),
        name="rwkv7_output",
    )(x, y, r, k, v, g, *small)


def _nsa_ssd_mixer(x, g_pre, g_post, cos, sin, w_in, pe_k, w1_k, w2_k, pe_v, w1_v, w2_v, conv_w,
                   conv_b, dt_bias, a_log, d_skip, norm_w, w_out, batch, seq):
    q, kv16, kaug, kwin, vst, vwt, z, xbc, misc = _inproj(x, g_pre, w_in, cos, sin, batch, seq)
    o_a = _nsa(q, kv16, kaug, kwin, vst, vwt, misc, pe_k, w1_k, w2_k, pe_v, w1_v, w2_v, batch, seq)
    o_b = _ssd(z, xbc, misc, conv_w, conv_b, dt_bias, a_log, d_skip, norm_w, batch, seq)
    return g_post, [o_a, o_b], [w_out[:NSA_Q_W], w_out[NSA_Q_W:]]


def _rwkv7_mixer(x, g_pre, g_post, mu, w_r, w_k, w_v, w_o, w0, w1, w2, a0, a1, a2, g1, g2, k_k,
                 k_a, r_k, ln_g, ln_b, batch, seq):
    r, ld, k, v, kk, g, bt, kt, ldt = _rwkv_pre(x, g_pre, mu, w_r, w_k, w_v, w0, w1, w2, a0, a1,
                                                a2, g1, g2, k_k, k_a, seq)
    y = _rwkv_scan(r, ld, kk, v, bt, kt, ldt, batch, seq)
    return _rwkv_post(x, y, r, k, v, g, ln_g, ln_b, r_k, w_o, g_post)


def kernel(x, norm_gains, ffn1_w_gate, ffn1_w_up, ffn1_w_down, ffn2_w_gate, ffn2_w_up, ffn2_w_down, ab_w_in, a_cmp_pe_k, a_cmp_w1_k, a_cmp_w2_k, a_cmp_pe_v, a_cmp_w1_v, a_cmp_w2_v, b_conv_w, b_conv_b, b_dt_bias, b_a_log, b_d_skip, b_norm_w, ab_w_out, c_mu, c_w_r, c_w_k, c_w_v, c_w_o, c_w0, c_w1, c_w2, c_a0, c_a1, c_a2, c_g1, c_g2, c_k_k, c_k_a, c_r_k, c_ln_g, c_ln_b):
    batch, seq, d = x.shape
    depth = norm_gains.shape[0]
    cos, sin = _rope_tables(seq)
    x = x.reshape(batch * seq, d)
    (ffn1_w_gate, ffn1_w_up, ffn1_w_down, ffn2_w_gate, ffn2_w_up, ffn2_w_down) = [
        w.astype(BF16) for w in (ffn1_w_gate, ffn1_w_up, ffn1_w_down, ffn2_w_gate, ffn2_w_up,
                                 ffn2_w_down)]
    for layer in range(depth):
        ng = norm_gains[layer]
        x = _ffn(x, ng[0], ng[1], ffn1_w_gate, ffn1_w_up, ffn1_w_down, layer)
        i = layer // 2
        mixer = None
        if layer % 2 == 0:
            mixer = _nsa_ssd_mixer(x, ng[2], ng[3], cos, sin, ab_w_in[i], a_cmp_pe_k[i],
                                   a_cmp_w1_k[i], a_cmp_w2_k[i], a_cmp_pe_v[i], a_cmp_w1_v[i],
                                   a_cmp_w2_v[i], b_conv_w[i], b_conv_b[i], b_dt_bias[i],
                                   b_a_log[i], b_d_skip[i], b_norm_w[i], ab_w_out[i], batch, seq)
        else:
            x = _rwkv7_mixer(x, ng[2], ng[3], c_mu[i], c_w_r[i], c_w_k[i], c_w_v[i], c_w_o[i],
                             c_w0[i], c_w1[i], c_w2[i], c_a0[i], c_a1[i], c_a2[i], c_g1[i],
                             c_g2[i], c_k_k[i], c_k_a[i], c_r_k[i], c_ln_g[i], c_ln_b[i],
                             batch, seq)
        x = _ffn(x, ng[4], ng[5], ffn2_w_gate, ffn2_w_up, ffn2_w_down, layer, mixer=mixer)
    return x.reshape(batch, seq, d)
```

```python
import functools

import jax
import jax.numpy as jnp
import numpy as np
from jax import lax
from jax.experimental import pallas as pl
from jax.experimental.pallas import tpu as pltpu

F32 = jnp.float32
BF16 = jnp.bfloat16
HIGHEST = lax.Precision.HIGHEST

D_MODEL = 1024
D_FF = 2816
NORM_EPS = 1e-6
NSA_HEADS = 8
NSA_KV_HEADS = 2
NSA_GROUP = NSA_HEADS // NSA_KV_HEADS
NSA_HEAD_DIM = 64
CMP_BLOCK = 32
CMP_STRIDE = 16
SEL_BLOCK = 64
SEL_TOPK = 16
WINDOW = 512
ROPE_THETA = 10000.0
FORCE_SCORE = 1e4
SEL_LANES = 128
SSD_HEADS = 16
SSD_HEAD_DIM = 64
SSD_D_INNER = SSD_HEADS * SSD_HEAD_DIM
SSD_GROUPS = 2
SSD_STATE = 128
SSD_CONV = 4
SSD_CHUNK = 128
SSD_NORM_EPS = 1e-5
SSD_XBC = SSD_D_INNER + 2 * SSD_GROUPS * SSD_STATE
RWKV_HEAD_DIM = 64
RWKV_HEADS = D_MODEL // RWKV_HEAD_DIM
RWKV_GN_EPS = 64e-5

NSA_Q_W = NSA_HEADS * NSA_HEAD_DIM
NSA_KV_W = NSA_KV_HEADS * NSA_HEAD_DIM
IN_SPLITS = (NSA_Q_W, NSA_KV_W, NSA_KV_W, NSA_KV_W, NSA_KV_W, NSA_KV_W, NSA_KV_W,
             NSA_HEADS * 3, SSD_D_INNER, SSD_XBC, SSD_HEADS)
IN_WIDTH = sum(IN_SPLITS)

LANES = 128
SUBLANES = 8
MXU_TILE = 256
VMEM_LIMIT_BYTES = 56 * 1024 * 1024

FFN_ROWS = 512
FFN_COLUMN_CHUNKS = 2
PROJ_ROWS = 256
NSA_QUERY_ROWS = 256
NSA_KEY_ROWS = 512

NEG_MASK = -1e30
NEG_UNSELECTED = -2.0 ** 30
NEG_TAKEN = -3e38
LOG2_E = 1.4426950408889634


def _params(*sem):
    return pltpu.CompilerParams(dimension_semantics=sem, vmem_limit_bytes=VMEM_LIMIT_BYTES)


def _rms(x, g, eps):
    return x * lax.rsqrt(jnp.mean(x * x, -1, keepdims=True) + eps) * g


def _silu(x):
    return x * jax.nn.sigmoid(x)


def _softplus(x):
    return jnp.maximum(x, 0.0) + jnp.log1p(jnp.exp(-jnp.abs(x)))


def _split3(a):
    a1 = a.astype(BF16)
    r1 = a - a1.astype(F32)
    a2 = r1.astype(BF16)
    a3 = (r1 - a2.astype(F32)).astype(BF16)
    return a1, a2, a3


def _dot_x3(a, b):
    if a.shape[1] == LANES:
        return jnp.dot(jnp.concatenate(_split3(a), axis=1), jnp.concatenate([b, b, b], axis=0),
                       preferred_element_type=F32)
    acc = None
    for piece in _split3(a):
        d = jnp.dot(piece, b, preferred_element_type=F32)
        acc = d if acc is None else acc + d
    return acc


def _dot_x3_left(b, a):
    if a.shape[0] == LANES:
        return jnp.dot(jnp.concatenate([b, b, b], axis=1), jnp.concatenate(_split3(a), axis=0),
                       preferred_element_type=F32)
    acc = None
    for piece in _split3(a):
        d = jnp.dot(b, piece, preferred_element_type=F32)
        acc = d if acc is None else acc + d
    return acc


def _dot_nt(a, b, **kw):
    return lax.dot_general(a, b, (((1,), (1,)), ((), ())), preferred_element_type=F32, **kw)


def _ffn_kernel(x_ref, gi_ref, go_ref, wg_ref, wu_ref, wd_ref, *rest, chunks, n_parts):
    o_ref = rest[-1]
    x = x_ref[...]
    if n_parts:
        proj = None
        for p_ref, w_ref in zip(rest[1:1 + n_parts], rest[1 + n_parts:1 + 2 * n_parts]):
            dd = jnp.dot(p_ref[...].astype(BF16), w_ref[...], preferred_element_type=F32)
            proj = dd if proj is None else proj + dd
        x = x + _rms(proj, rest[0][...], NORM_EPS)
    h = _rms(x, gi_ref[...], NORM_EPS).astype(BF16)
    acc = None
    for lo, hi in chunks:
        gate = jnp.dot(h, wg_ref[:, lo:hi], preferred_element_type=F32)
        up = jnp.dot(h, wu_ref[:, lo:hi], preferred_element_type=F32)
        act = (_silu(gate) * up).astype(BF16)
        part = jnp.dot(act, wd_ref[lo:hi, :], preferred_element_type=F32)
        acc = part if acc is None else acc + part
    o_ref[...] = x + 0.5 * _rms(acc, go_ref[...], NORM_EPS)


def _ffn(x, g_in, g_out, w_gate, w_up, w_down, layer, mixer=None, tm=FFN_ROWS,
         n_chunks=FFN_COLUMN_CHUNKS):
    t, d = x.shape
    f = w_gate.shape[2]
    tiles = f // MXU_TILE
    assert tiles * MXU_TILE == f
    cuts = [MXU_TILE * ((tiles * c + n_chunks - 1) // n_chunks) for c in range(n_chunks + 1)]
    chunks = tuple(zip(cuts[:-1], cuts[1:]))
    row = lambda i: (i, 0)
    const = lambda i: (0, 0)
    resident = lambda shape: pl.BlockSpec(shape, const, pipeline_mode=pl.Buffered(1))
    of_layer = lambda shape: pl.BlockSpec((None,) + shape, lambda i: (layer, 0, 0),
                                          pipeline_mode=pl.Buffered(1))
    operands = [x, g_in.reshape(1, d), g_out.reshape(1, d), w_gate, w_up, w_down]
    in_specs = [pl.BlockSpec((tm, d), row), pl.BlockSpec((1, d), const),
                pl.BlockSpec((1, d), const), of_layer((d, f)), of_layer((d, f)),
                of_layer((f, d))]
    n_parts = 0
    if mixer is not None:
        g_mixer, parts, weights = mixer
        n_parts = len(parts)
        operands += [g_mixer.reshape(1, d), *parts, *[w.astype(BF16) for w in weights]]
        in_specs += ([pl.BlockSpec((1, d), const)]
                     + [pl.BlockSpec((tm, p.shape[1]), row) for p in parts]
                     + [resident(w.shape) for w in weights])
    return pl.pallas_call(
        functools.partial(_ffn_kernel, chunks=chunks, n_parts=n_parts),
        grid=(t // tm,),
        in_specs=in_specs,
        out_specs=pl.BlockSpec((tm, d), row),
        out_shape=jax.ShapeDtypeStruct((t, d), F32),
        compiler_params=_params("parallel"),
        name="ffn_half_step",
    )(*operands)


INPROJ_MISC_W = 256
INPROJ_KV_W = 6 * NSA_KV_W
INPROJ_WIDTH = NSA_Q_W + INPROJ_KV_W + SSD_D_INNER + SSD_XBC + INPROJ_MISC_W


def _swap_halves(x):
    w = x.shape[-1]
    lane = lax.broadcasted_iota(jnp.int32, x.shape, x.ndim - 1)
    low = (lane & (NSA_HEAD_DIM - 1)) < (NSA_HEAD_DIM // 2)
    return jnp.where(low, pltpu.roll(x, w - NSA_HEAD_DIM // 2, x.ndim - 1),
                     pltpu.roll(x, NSA_HEAD_DIM // 2, x.ndim - 1))


def _inproj_kernel(x_ref, g_ref, w_ref, cos_ref, sin_ref, q_ref, kv16_ref, kaug_ref, kwin_ref,
                   vst_ref, vwt_ref, z_ref, xbc_ref, misc_ref, kvc_scr, *, tiles_per_seq):
    h = _rms(x_ref[...], g_ref[...], NORM_EPS).astype(BF16)
    proj = jnp.dot(h, w_ref[...], preferred_element_type=F32)
    tm = proj.shape[0]
    cos = cos_ref[...]
    sin = sin_ref[...]
    q = proj[:, :NSA_Q_W]
    cos_q = jnp.concatenate([cos] * (NSA_Q_W // LANES), axis=1)
    sin_q = jnp.concatenate([sin] * (NSA_Q_W // LANES), axis=1)
    q_ref[...] = (q * cos_q + _swap_halves(q) * sin_q) * (NSA_HEAD_DIM ** -0.5)
    piece = lambda i: proj[:, NSA_Q_W + i * NSA_KV_W:NSA_Q_W + (i + 1) * NSA_KV_W]
    rope = lambda p: p * cos + _swap_halves(p) * sin
    d = NSA_HEAD_DIM
    kvc_scr[0] = rope(piece(0))
    kvc_scr[1] = piece(1)
    groups = tm // CMP_STRIDE
    first = lax.broadcasted_iota(jnp.int32, (groups, LANES), 1) < d
    for kv in range(2):
        cols = [[] for _ in range(NSA_KV_HEADS)]
        for j in range(CMP_STRIDE // 2):
            even = kvc_scr[kv, pl.ds(2 * j, groups, stride=CMP_STRIDE), :]
            odd = kvc_scr[kv, pl.ds(2 * j + 1, groups, stride=CMP_STRIDE), :]
            cols[0].append(jnp.where(first, even, pltpu.roll(odd, d, 1)))
            cols[1].append(jnp.where(first, pltpu.roll(even, d, 1), odd))
        for hh in range(NSA_KV_HEADS):
            kv16_ref[kv, 0, hh] = jnp.concatenate(cols[hh], axis=1)
    k_sel, k_win = rope(piece(2)), rope(piece(4))
    pos = (pl.program_id(0) % tiles_per_seq) * tm + lax.broadcasted_iota(
        jnp.int32, (tm, SEL_LANES), 0)
    block_id = lax.shift_right_logical(pos, SEL_BLOCK.bit_length() - 1)
    onehot = jnp.where(lax.broadcasted_iota(jnp.int32, (tm, SEL_LANES), 1) == block_id,
                       1.0, 0.0).astype(BF16)
    vst = piece(3).T
    vwt = piece(5).T
    d = NSA_HEAD_DIM
    for hh in range(NSA_KV_HEADS):
        kaug_ref[0, hh] = jnp.concatenate([onehot, k_sel[:, hh * d:(hh + 1) * d].astype(BF16)],
                                          axis=1)
        kwin_ref[0, hh] = k_win[:, hh * d:(hh + 1) * d].astype(BF16)
        vst_ref[0, hh] = vst[hh * d:(hh + 1) * d].astype(BF16)
        vwt_ref[0, hh] = vwt[hh * d:(hh + 1) * d].astype(BF16)
    o = NSA_Q_W + INPROJ_KV_W
    z_ref[...] = proj[:, o:o + SSD_D_INNER]
    o += SSD_D_INNER
    xbc_ref[...] = proj[:, o:o + SSD_XBC]
    o += SSD_XBC
    misc_ref[...] = proj[:, o:o + INPROJ_MISC_W]


def _inproj(x, g, w_in, cos, sin, batch, seq, tm=PROJ_ROWS):
    t, d = x.shape
    offs = np.cumsum(IN_SPLITS)[:-1].tolist()
    q, kc, vc, ks, vs, kw, vw, gl, z, xbc, dt = jnp.split(w_in, offs, -1)
    pad = jnp.zeros((d, INPROJ_MISC_W - gl.shape[1] - dt.shape[1]), w_in.dtype)
    w = jnp.concatenate([q, kc, vc, ks, vs, kw, vw, z, xbc, gl, dt, pad], -1).astype(BF16)
    assert w.shape[1] == INPROJ_WIDTH
    nseq = seq // tm
    row = lambda i: (i, 0)
    const = lambda i: (0, 0)
    hd, hkv = NSA_HEAD_DIM, NSA_KV_HEADS
    by_head_rows = lambda wd: pl.BlockSpec((1, hkv, tm, wd), lambda i: (i // nseq, 0, i % nseq, 0))
    by_head_cols = pl.BlockSpec((1, hkv, hd, tm), lambda i: (i // nseq, 0, 0, i % nseq))
    flat = lambda wd: (pl.BlockSpec((tm, wd), row), jax.ShapeDtypeStruct((t, wd), F32))
    groups = tm // CMP_STRIDE
    kv16_spec = pl.BlockSpec((2, 1, hkv, groups, CMP_STRIDE * hd),
                             lambda i: (0, i // nseq, 0, i % nseq, 0))
    outs = [
        flat(NSA_Q_W),
        (kv16_spec, jax.ShapeDtypeStruct((2, batch, hkv, seq // CMP_STRIDE, CMP_STRIDE * hd), F32)),
        (by_head_rows(SEL_LANES + hd), jax.ShapeDtypeStruct((batch, hkv, seq, SEL_LANES + hd), BF16)),
        (by_head_rows(hd), jax.ShapeDtypeStruct((batch, hkv, seq, hd), BF16)),
        (by_head_cols, jax.ShapeDtypeStruct((batch, hkv, hd, seq), BF16)),
        (by_head_cols, jax.ShapeDtypeStruct((batch, hkv, hd, seq), BF16)),
        flat(SSD_D_INNER),
        flat(SSD_XBC),
        flat(INPROJ_MISC_W),
    ]
    return pl.pallas_call(
        functools.partial(_inproj_kernel, tiles_per_seq=nseq),
        grid=(t // tm,),
        in_specs=[
            pl.BlockSpec((tm, d), row),
            pl.BlockSpec((1, d), const),
            pl.BlockSpec((d, INPROJ_WIDTH), const),
            pl.BlockSpec((tm, LANES), lambda i: (i % nseq, 0)),
            pl.BlockSpec((tm, LANES), lambda i: (i % nseq, 0)),
        ],
        out_specs=[o[0] for o in outs],
        out_shape=[o[1] for o in outs],
        scratch_shapes=[pltpu.VMEM((2, tm, NSA_KV_W), F32)],
        compiler_params=_params("parallel"),
        name="mixer0_in_proj",
    )(x, g.reshape(1, d), w, cos, sin)


def _rope_tables(seq):
    inv = ROPE_THETA ** (-np.arange(0, NSA_HEAD_DIM, 2, dtype=np.float64) / NSA_HEAD_DIM)
    ang = np.arange(seq, dtype=np.float64)[:, None] * inv[None, :]
    cos, sin = np.cos(ang), np.sin(ang)
    reps = LANES // NSA_HEAD_DIM
    cos_t = np.concatenate([cos, cos] * reps, -1).astype(np.float32)
    sin_t = np.concatenate([-sin, sin] * reps, -1).astype(np.float32)
    return jnp.asarray(cos_t), jnp.asarray(sin_t)


def _compress_kernel(k_ref, pe_ref, w1_ref, w2_ref, o_ref):
    k16 = k_ref[0, 0]
    w1 = w1_ref[0]
    half = w1.shape[0] // 2
    first = jnp.dot(k16, w1[:half], precision=HIGHEST, preferred_element_type=F32)
    second = jnp.dot(k16, w1[half:], precision=HIGHEST, preferred_element_type=F32)
    bias = jnp.dot(pe_ref[0], w1, precision=HIGHEST, preferred_element_type=F32)[0:1]
    n = k16.shape[0]
    pre = first + pltpu.roll(second, n - 1, 0) + bias
    o_ref[0, 0] = jnp.dot(_silu(pre), w2_ref[0], precision=HIGHEST, preferred_element_type=F32)


def _compress(kv16, pe, w1, w2):
    two, bh, n, wd = kv16.shape
    d = w2.shape[-1]
    return pl.pallas_call(
        _compress_kernel,
        grid=(two, bh),
        in_specs=[
            pl.BlockSpec((1, 1, n, wd), lambda a, b: (a, b, 0, 0)),
            pl.BlockSpec((1, SUBLANES, pe.shape[-1]), lambda a, b: (a, 0, 0)),
            pl.BlockSpec((1,) + w1.shape[1:], lambda a, b: (a, 0, 0)),
            pl.BlockSpec((1, d, d), lambda a, b: (a, 0, 0)),
        ],
        out_specs=pl.BlockSpec((1, 1, n, d), lambda a, b: (a, b, 0, 0)),
        out_shape=jax.ShapeDtypeStruct((two, bh, n, d), F32),
        compiler_params=_params("parallel", "parallel"),
        name="nsa_compress",
    )(kv16, pe, w1, w2)


def _group_rows(q):
    return jnp.concatenate(
        [q[:, g * NSA_HEAD_DIM:(g + 1) * NSA_HEAD_DIM] for g in range(NSA_GROUP)], axis=0)


def _ungroup_rows(o, tq):
    return jnp.concatenate([o[g * tq:(g + 1) * tq] for g in range(NSA_GROUP)], axis=1)


def _dot_nt_hi(a, b):
    a1 = a.astype(BF16)
    a2 = (a - a1.astype(F32)).astype(BF16)
    b1 = b.astype(BF16)
    b2 = (b - b1.astype(F32)).astype(BF16)
    if 3 * a.shape[1] <= MXU_TILE:
        return _dot_nt(jnp.concatenate([a1, a1, a2], axis=1),
                       jnp.concatenate([b1, b2, b1], axis=1))
    return _dot_nt(a1, b1) + _dot_nt(a1, b2) + _dot_nt(a2, b1)


def _nsa_select_kernel(q_ref, kc_ref, vc_ref, ovt_ref, oc_ref, biast_ref, *, tq, topk):
    s0 = pl.program_id(1) * tq
    gw = NSA_GROUP * NSA_HEAD_DIM
    heads = range(NSA_KV_HEADS)
    q = q_ref[0]
    s = [_dot_nt_hi(_group_rows(q[:, h * gw:(h + 1) * gw]), kc_ref[h]) for h in heads]
    rows, ncmp = s[0].shape
    t = s0 + (lax.broadcasted_iota(jnp.int32, (rows, ncmp), 0) & (tq - 1))
    cmp_end = lax.broadcasted_iota(jnp.int32, (rows, ncmp), 1) * CMP_STRIDE + (CMP_BLOCK - 1)
    mask = cmp_end <= t
    p = []
    for h in heads:
        sh = jnp.where(mask, s[h], NEG_MASK)
        ph = jnp.where(mask, jnp.exp(sh - jnp.max(sh, -1, keepdims=True)), 0.0)
        p.append(ph / jnp.maximum(jnp.sum(ph, -1, keepdims=True), 1e-30))
    for h in heads:
        o = jnp.dot(p[h].astype(BF16), vc_ref[h].astype(BF16), preferred_element_type=F32)
        oc_ref[0, :, h * gw:(h + 1) * gw] = _ungroup_rows(o, tq)

    ovt = ovt_ref[...]
    imp = []
    for h in heads:
        psum = p[h][0:tq]
        for g in range(1, NSA_GROUP):
            psum = psum + p[h][g * tq:(g + 1) * tq]
        acc = None
        for piece in _split3(psum):
            d = _dot_nt(ovt, piece)
            acc = d if acc is None else acc + d
        imp.append(acc)
    blk = lax.broadcasted_iota(jnp.int32, imp[0].shape, 0)
    tt = s0 + lax.broadcasted_iota(jnp.int32, imp[0].shape, 1)
    cur = lax.shift_right_logical(tt, SEL_BLOCK.bit_length() - 1)
    forced = (blk == 0) | (blk == cur) | (blk == cur - 1)
    valid = blk * SEL_BLOCK <= tt
    x = [jnp.where(valid, jnp.where(forced, FORCE_SCORE, imp[h]), NEG_MASK) for h in heads]
    blk_f = blk.astype(F32)
    sel = [jnp.zeros(blk.shape, jnp.bool_) for _ in heads]
    for _ in range(topk):
        for h in heads:
            m = jnp.max(x[h], 0, keepdims=True)
            idx = jnp.min(jnp.where(x[h] == m, blk_f, float(SEL_LANES)), 0, keepdims=True)
            hit = blk_f == idx
            sel[h] = sel[h] | hit
            x[h] = jnp.where(hit, NEG_TAKEN, x[h])
    for h in heads:
        biast_ref[0, h] = jnp.where(sel[h], 0.0, NEG_UNSELECTED).astype(BF16)


def _nsa_select(q, kc, vc, overlap, batch, seq, tq=NSA_QUERY_ROWS):
    ncmp = kc.shape[1]
    qw = NSA_KV_HEADS * NSA_GROUP * NSA_HEAD_DIM
    topk = min(SEL_TOPK, seq // SEL_BLOCK)
    kern = functools.partial(_nsa_select_kernel, tq=tq, topk=topk)
    return pl.pallas_call(
        kern,
        grid=(batch, seq // tq),
        in_specs=[
            pl.BlockSpec((1, tq, qw), lambda b, i: (b, i, 0)),
            pl.BlockSpec((NSA_KV_HEADS, ncmp, NSA_HEAD_DIM), lambda b, i: (b, 0, 0)),
            pl.BlockSpec((NSA_KV_HEADS, ncmp, NSA_HEAD_DIM), lambda b, i: (b, 0, 0)),
            pl.BlockSpec((SEL_LANES, ncmp), lambda b, i: (0, 0)),
        ],
        out_specs=[
            pl.BlockSpec((1, tq, qw), lambda b, i: (b, i, 0)),
            pl.BlockSpec((1, NSA_KV_HEADS, SEL_LANES, tq), lambda b, i: (b, 0, 0, i)),
        ],
        out_shape=[
            jax.ShapeDtypeStruct((batch, seq, qw), F32),
            jax.ShapeDtypeStruct((batch, NSA_KV_HEADS, SEL_LANES, seq), BF16),
        ],
        compiler_params=_params("parallel", "parallel"),
        name="nsa_compressed_select",
    )(q, kc, vc, overlap)


NSA_LANE_SPLIT = 2


def _nsa_attend_kernel(q_ref, biast_ref, ka_ref, vst_ref, kw_ref, vwt_ref, oc_ref, gl_ref, o_ref,
                       sa_scr, sb_scr, p_scr, w_scr, *, tq, tk):
    i = pl.program_id(2)
    s0 = i * tq
    n = NSA_GROUP * tq
    half = n // NSA_LANE_SPLIT
    d = NSA_HEAD_DIM
    qt = (q_ref[0] * LOG2_E).T
    qgt = jnp.concatenate([qt[g * d:(g + 1) * d] for g in range(NSA_GROUP)], axis=1)
    qgt = qgt.astype(BF16)
    qat = jnp.concatenate([jnp.concatenate([biast_ref[0, 0]] * NSA_GROUP, axis=1), qgt], axis=0)
    init = tuple((jnp.full((1, half), NEG_MASK, F32), jnp.zeros((1, half), F32),
                  jnp.zeros((d, half), F32)) for _ in range(NSA_LANE_SPLIT))

    def query_pos(shape):
        return s0 + (lax.broadcasted_iota(jnp.int32, shape, 1) & (tq - 1))

    halves = range(NSA_LANE_SPLIT)

    def scores(kt):
        k = ka_ref[0, pl.ds(pl.multiple_of(kt * tk, tk), tk), :]
        return tuple(jnp.dot(k, qat[:, hf * half:(hf + 1) * half], preferred_element_type=F32)
                     for hf in halves)

    def values(kt):
        vt = vst_ref[0, :, pl.ds(pl.multiple_of(kt * tk, tk), tk)]
        return tuple(jnp.dot(vt, p_scr[:, hf * half:(hf + 1) * half],
                             preferred_element_type=F32) for hf in halves)

    span = WINDOW + tq
    start = pl.multiple_of(jnp.maximum(s0 - WINDOW, 0), tq)

    def window_scores():
        kwin = kw_ref[0, pl.ds(start, span), :]
        return tuple(jnp.dot(kwin, qgt[:, hf * half:(hf + 1) * half],
                             preferred_element_type=F32) for hf in halves)

    def sel_step(kt, stats, src_scr, dst_scr, causal):
        s_next = window_scores() if causal else scores(kt + 1)
        pv = values(jnp.maximum(kt - 1, 0))
        new_stats = []
        for hf in halves:
            cols = slice(hf * half, (hf + 1) * half)
            m, l, acc = stats[hf]
            s = src_scr[:, cols]
            if causal:
                kp = kt * tk + lax.broadcasted_iota(jnp.int32, s.shape, 0)
                mask = kp <= query_pos(s.shape)
                s = jnp.where(mask, s, NEG_MASK)
            m_new = jnp.maximum(m, jnp.max(s, 0, keepdims=True))
            alpha = jnp.exp2(m - m_new)
            p = jnp.exp2(s - m_new)
            if causal:
                p = jnp.where(mask, p, 0.0)
            new_stats.append((m_new, alpha * l + jnp.sum(p, 0, keepdims=True),
                              alpha * (acc + pv[hf])))
            p_scr[:, cols] = p.astype(BF16)
        for hf in halves:
            (w_scr if causal else dst_scr)[:, hf * half:(hf + 1) * half] = s_next[hf]
        return tuple(new_stats)

    def by_parity(kt, stats, causal):
        return lax.cond((kt & 1) == 0,
                        lambda st: sel_step(kt, st, sa_scr, sb_scr, causal),
                        lambda st: sel_step(kt, st, sb_scr, sa_scr, causal), stats)

    n_full = s0 // tk
    first_scores = scores(0)
    for hf in halves:
        sa_scr[:, hf * half:(hf + 1) * half] = first_scores[hf]
    p_scr[...] = jnp.zeros_like(p_scr)
    stats = lax.fori_loop(0, n_full, lambda kt, c: by_parity(kt, c, False), init)
    stats = by_parity(n_full, stats, True)
    pv_last = values(n_full)

    vwt = vwt_ref[0, :, pl.ds(start, span)]
    p_w, l_w = [], []
    for hf in halves:
        s = w_scr[:, hf * half:(hf + 1) * half]
        kp = start + lax.broadcasted_iota(jnp.int32, s.shape, 0)
        t = query_pos(s.shape)
        mask = (kp <= t) & (kp > t - WINDOW)
        s = jnp.where(mask, s, NEG_MASK)
        p = jnp.where(mask, jnp.exp2(s - jnp.max(s, 0, keepdims=True)), 0.0)
        l_w.append(jnp.sum(p, 0, keepdims=True))
        p_w.append(p.astype(BF16))
    o_w = jnp.concatenate([jnp.dot(vwt, p_w[hf], preferred_element_type=F32)
                           / jnp.maximum(l_w[hf], 1e-30) for hf in halves], axis=1)
    o_s = jnp.concatenate([(stats[hf][2] + pv_last[hf]) / jnp.maximum(stats[hf][1], 1e-30)
                           for hf in halves], axis=1)

    def rows_layout(ot):
        return jnp.concatenate([ot[:, g * tq:(g + 1) * tq] for g in range(NSA_GROUP)], axis=0).T

    o_s = rows_layout(o_s)
    o_w = rows_layout(o_w)
    gates = jax.nn.sigmoid(gl_ref[0, 0])
    o_c = oc_ref[0]
    pieces = []
    for g in range(NSA_GROUP):
        c = slice(g * d, (g + 1) * d)
        pieces.append(gates[:, 3 * g:3 * g + 1] * o_c[:, c]
                      + gates[:, 3 * g + 1:3 * g + 2] * o_s[:, c]
                      + gates[:, 3 * g + 2:3 * g + 3] * o_w[:, c])
    o_ref[0] = jnp.concatenate(pieces, axis=1)


def _nsa_attend(q, biast, kaug, vst, kw, vwt, o_c, gl, batch, seq, tq=NSA_QUERY_ROWS,
                tk=NSA_KEY_ROWS):
    gw = NSA_GROUP * NSA_HEAD_DIM
    tk = min(tk, seq)
    assert tk % tq == 0 and WINDOW % tq == 0 and seq >= WINDOW + tq
    kern = functools.partial(_nsa_attend_kernel, tq=tq, tk=tk)
    bh = lambda b, h, i: (b * NSA_KV_HEADS + h, 0, 0)
    return pl.pallas_call(
        kern,
        grid=(batch, NSA_KV_HEADS, seq // tq),
        in_specs=[
            pl.BlockSpec((1, tq, gw), lambda b, h, i: (b, i, h)),
            pl.BlockSpec((1, 1, SEL_LANES, tq), lambda b, h, i: (b, h, 0, i)),
            pl.BlockSpec((1, seq, SEL_LANES + NSA_HEAD_DIM), bh),
            pl.BlockSpec((1, NSA_HEAD_DIM, seq), bh),
            pl.BlockSpec((1, seq, NSA_HEAD_DIM), bh),
            pl.BlockSpec((1, NSA_HEAD_DIM, seq), bh),
            pl.BlockSpec((1, tq, gw), lambda b, h, i: (b, i, h)),
            pl.BlockSpec((1, 1, tq, NSA_GROUP * 3), lambda b, h, i: (b, h, i, 0)),
        ],
        out_specs=pl.BlockSpec((1, tq, gw), lambda b, h, i: (b, i, h)),
        out_shape=jax.ShapeDtypeStruct((batch, seq, NSA_KV_HEADS * gw), F32),
        scratch_shapes=[pltpu.VMEM((tk, NSA_GROUP * tq), F32),
                        pltpu.VMEM((tk, NSA_GROUP * tq), F32),
                        pltpu.VMEM((tk, NSA_GROUP * tq), BF16),
                        pltpu.VMEM((WINDOW + tq, NSA_GROUP * tq), F32)],
        compiler_params=_params("parallel", "parallel", "arbitrary"),
        name="nsa_selected_window",
    )(q, biast, kaug, vst, kw, vwt, o_c, gl)


def _nsa(q, kv16, kaug, kwin, vst, vwt, misc, pe_k, w1_k, w2_k, pe_v, w1_v, w2_v, batch, seq):
    d = NSA_HEAD_DIM
    bh = batch * NSA_KV_HEADS
    n16 = seq // CMP_STRIDE
    kv16 = kv16.reshape(2, bh, n16, CMP_STRIDE * d)
    pe = jnp.stack([pe_k, pe_v]).reshape(2, 1, CMP_BLOCK * d)
    pe = jnp.broadcast_to(pe, (2, SUBLANES, CMP_BLOCK * d))
    cmp = _compress(kv16, pe, jnp.stack([w1_k, w1_v]), jnp.stack([w2_k, w2_v]))
    kc, vc = cmp[0], cmp[1]

    n_sel = seq // SEL_BLOCK
    cmp_start = np.arange(n16) * CMP_STRIDE
    sel_start = np.arange(SEL_LANES) * SEL_BLOCK
    overlap = ((cmp_start[:, None] < sel_start[None, :] + SEL_BLOCK)
               & (cmp_start[:, None] + CMP_BLOCK - 1 >= sel_start[None, :])
               & (np.arange(SEL_LANES)[None, :] < n_sel)
               & (np.arange(n16)[:, None] < (seq - CMP_BLOCK) // CMP_STRIDE + 1))
    overlap_t = jnp.asarray(overlap.T, BF16)

    q3 = q.reshape(batch, seq, NSA_Q_W)
    o_c, biast = _nsa_select(q3, kc, vc, overlap_t, batch, seq)

    gl = misc[:, :NSA_HEADS * 3].reshape(batch, seq, NSA_KV_HEADS, NSA_GROUP * 3)
    gl = jnp.moveaxis(gl, 2, 1)
    per_head = lambda a: a.reshape((bh,) + a.shape[2:])
    o = _nsa_attend(q3, biast, per_head(kaug), per_head(vst), per_head(kwin), per_head(vwt), o_c,
                    gl, batch, seq)
    return o.reshape(batch * seq, NSA_Q_W)


def _ssd_kernel(xbc_ref, halo_ref, z_ref, dt_ref, cw_ref, cb_ref, dtb_ref, alog_ref, dskip_ref,
                nw_ref, tril_ref, spread_ref, o_ref, state_scr, y_scr):
    c = pl.program_id(1)
    l = SSD_CHUNK

    @pl.when(c == 0)
    def _():
        state_scr[...] = jnp.zeros_like(state_scr)

    x = xbc_ref[0]
    halo = jnp.where(c == 0, 0.0, halo_ref[0])
    xx = jnp.concatenate([halo, x], axis=0)
    cw = cw_ref[...]
    conv = cb_ref[...]
    for k in range(SSD_CONV):
        off = SUBLANES - (SSD_CONV - 1) + k
        conv = conv + cw[k:k + 1] * xx[off:off + l]
    xbc = _silu(conv)
    xs = xbc[:, :SSD_D_INNER]
    gn = SSD_GROUPS * SSD_STATE
    bmat = xbc[:, SSD_D_INNER:SSD_D_INNER + gn]
    cmat = xbc[:, SSD_D_INNER + gn:]

    dt = _softplus(dt_ref[0] + dtb_ref[...])
    da = dt * (-jnp.exp(alog_ref[...]))
    a_cs = _dot_x3_left(tril_ref[...], da)
    a_cs_t = a_cs.T
    a_last = a_cs[l - 1:l]
    causal = (lax.broadcasted_iota(jnp.int32, (l, l), 0)
              >= lax.broadcasted_iota(jnp.int32, (l, l), 1))

    spread = spread_ref[...]
    dt_x = _dot_x3(dt, spread)
    grow_x = _dot_x3(jnp.exp(a_cs), spread)
    fade_x = _dot_x3(jnp.exp(a_last - a_cs), spread)
    chunk_x = _dot_x3(jnp.broadcast_to(jnp.exp(a_last), (SUBLANES, LANES)), spread)[0:1]
    xd = xs * dt_x
    xd16 = xd.astype(BF16)
    fxd16 = (xd * fade_x).astype(BF16)

    pairs = range(SSD_HEADS // 2)
    pairs_per_group = len(pairs) // SSD_GROUPS
    lanes = {c: slice(c * LANES, (c + 1) * LANES) for c in pairs}
    cb, y_off = {}, {}
    for g in range(SSD_GROUPS):
        bg = bmat[:, g * SSD_STATE:(g + 1) * SSD_STATE]
        cg16 = cmat[:, g * SSD_STATE:(g + 1) * SSD_STATE].astype(BF16)
        cb[g] = _dot_nt(cg16, bg.astype(BF16))
        bgt16 = bg.T.astype(BF16)
        for c in range(g * pairs_per_group, (g + 1) * pairs_per_group):
            st = state_scr[c]
            y_off[c] = jnp.dot(cg16, st.astype(BF16), preferred_element_type=F32)
            new = jnp.dot(bgt16, fxd16[:, lanes[c]], preferred_element_type=F32)
            state_scr[c] = st * chunk_x[:, lanes[c]] + new
    first_head = lax.broadcasted_iota(jnp.int32, (l, LANES), 1) < SSD_HEAD_DIM
    y_diag = {}
    for c in pairs:
        for hh in range(2):
            h = 2 * c + hh
            seg = jnp.where(causal, jnp.exp(a_cs[:, h:h + 1] - a_cs_t[h:h + 1, :]), 0.0)
            y_diag[h] = jnp.dot((cb[c // pairs_per_group] * seg).astype(BF16), xd16[:, lanes[c]],
                                preferred_element_type=F32)
    for c in pairs:
        y_scr[:, lanes[c]] = (jnp.where(first_head, y_diag[2 * c], y_diag[2 * c + 1])
                              + y_off[c] * grow_x[:, lanes[c]])

    y = (y_scr[...] + xs * dskip_ref[...]) * _silu(z_ref[0])
    gw = SSD_D_INNER // SSD_GROUPS
    outs = []
    for g in range(SSD_GROUPS):
        yg = y[:, g * gw:(g + 1) * gw]
        outs.append(yg * lax.rsqrt(jnp.mean(yg * yg, -1, keepdims=True) + SSD_NORM_EPS))
    o_ref[0] = jnp.concatenate(outs, axis=1) * nw_ref[...]


def _pad_lanes(v, width=LANES):
    v = v.reshape(1, -1).astype(F32)
    return jnp.pad(v, ((0, 0), (0, width - v.shape[1])))


def _ssd(z, xbc, misc, conv_w, conv_b, dt_bias, a_log, d_skip, norm_w, batch, seq):
    l = SSD_CHUNK
    nc = seq // l
    z3 = z.reshape(batch, seq, SSD_D_INNER)
    x3 = xbc.reshape(batch, seq, SSD_XBC)
    dt = misc[:, NSA_HEADS * 3:NSA_HEADS * 3 + SSD_HEADS]
    dt3 = jnp.pad(dt, ((0, 0), (0, LANES - SSD_HEADS))).reshape(batch, seq, LANES)
    tril = jnp.asarray(np.tril(np.ones((l, l))), BF16)
    spread = np.zeros((LANES, SSD_D_INNER), np.float32)
    spread[np.arange(SSD_D_INNER) // SSD_HEAD_DIM, np.arange(SSD_D_INNER)] = 1.0
    spread = jnp.asarray(spread, BF16)
    hb = l // SUBLANES
    const = lambda b, c: (0, 0)
    return pl.pallas_call(
        _ssd_kernel,
        grid=(batch, nc),
        in_specs=[
            pl.BlockSpec((1, l, SSD_XBC), lambda b, c: (b, c, 0)),
            pl.BlockSpec((1, SUBLANES, SSD_XBC), lambda b, c: (b, jnp.maximum(c * hb - 1, 0), 0)),
            pl.BlockSpec((1, l, SSD_D_INNER), lambda b, c: (b, c, 0)),
            pl.BlockSpec((1, l, LANES), lambda b, c: (b, c, 0)),
            pl.BlockSpec((SSD_CONV, SSD_XBC), const),
            pl.BlockSpec((1, SSD_XBC), const),
            pl.BlockSpec((1, LANES), const),
            pl.BlockSpec((1, LANES), const),
            pl.BlockSpec((1, SSD_D_INNER), const),
            pl.BlockSpec((1, SSD_D_INNER), const),
            pl.BlockSpec((l, l), const),
            pl.BlockSpec((LANES, SSD_D_INNER), const),
        ],
        out_specs=pl.BlockSpec((1, l, SSD_D_INNER), lambda b, c: (b, c, 0)),
        out_shape=jax.ShapeDtypeStruct((batch, seq, SSD_D_INNER), F32),
        scratch_shapes=[pltpu.VMEM((SSD_HEADS // 2, SSD_STATE, 2 * SSD_HEAD_DIM), F32),
                        pltpu.VMEM((l, SSD_D_INNER), F32)],
        compiler_params=_params("parallel", "arbitrary"),
        name="ssd_chunk_scan",
    )(x3, x3, z3, dt3, conv_w.reshape(SSD_CONV, SSD_XBC), conv_b.reshape(1, SSD_XBC),
      _pad_lanes(dt_bias), _pad_lanes(a_log),
      jnp.repeat(d_skip.astype(F32), SSD_HEAD_DIM).reshape(1, SSD_D_INNER),
      norm_w.reshape(1, SSD_D_INNER), tril, spread).reshape(batch * seq, SSD_D_INNER)


def _dot_x2(a, b):
    a1 = a.astype(BF16)
    a2 = (a - a1.astype(F32)).astype(BF16)
    return (jnp.dot(a1, b, preferred_element_type=F32)
            + jnp.dot(a2, b, preferred_element_type=F32))


def _head_sum(x, seg, seg_t):
    sums = _dot_x2(x, seg)
    return jnp.dot(sums.astype(BF16), seg_t, preferred_element_type=F32)


def _rwkv_pre_kernel(x_ref, halo_ref, g_ref, mu_ref, wr_ref, wk_ref, wv_ref, w0_ref, w1_ref,
                     w2_ref, a0_ref, a1_ref, a2_ref, g1_ref, g2_ref, kk_ref, ka_ref, seg_ref,
                     segt_ref, r_out, ld_out, k_out, v_out, kk_out, g_out, bt_out, kt_out, ldt_out,
                     *, tiles_per_seq):
    i = pl.program_id(0)
    h = _rms(x_ref[...], g_ref[...], NORM_EPS)
    prev_row = _rms(halo_ref[...], g_ref[...], NORM_EPS)[SUBLANES - 1:SUBLANES]
    prev_row = jnp.where(i % tiles_per_seq == 0, 0.0, prev_row)
    rowid = lax.broadcasted_iota(jnp.int32, h.shape, 0)
    prev = jnp.where(rowid == 0, prev_row, pltpu.roll(h, 1, 0))
    xx = prev - h
    mu = mu_ref[...]
    mix = lambda j: (h + xx * mu[j:j + 1]).astype(BF16)
    dot = lambda a, w_ref: jnp.dot(a, w_ref[...], preferred_element_type=F32)
    r = dot(mix(0), wr_ref)
    w = -_softplus(-(w0_ref[...] + dot(jnp.tanh(dot(mix(1), w1_ref)).astype(BF16), w2_ref))) - 0.5
    k = dot(mix(2), wk_ref)
    v = dot(mix(3), wv_ref)
    a = jax.nn.sigmoid(a0_ref[...] + dot(dot(mix(4), a1_ref).astype(BF16), a2_ref))
    g = dot(jax.nn.sigmoid(dot(mix(5), g1_ref)).astype(BF16), g2_ref)
    kk = k * kk_ref[...]
    norm = jnp.sqrt(_head_sum(kk * kk, seg_ref[...], segt_ref[...]))
    kk = kk / jnp.maximum(norm, 1e-12)
    k = k * (1.0 + (a - 1.0) * ka_ref[...])
    log_decay = -jnp.exp(w)
    r_out[...] = r
    ld_out[...] = log_decay
    k_out[...] = k
    v_out[...] = v
    kk_out[...] = kk
    g_out[...] = g
    bt_out[0] = (kk * a).T
    kt_out[0] = k.T
    ldt_out[0] = log_decay.T


def _pad_cols(w, width):
    return jnp.pad(w, ((0, 0), (0, width - w.shape[1])))


def _pad_rows(w, width):
    return jnp.pad(w, ((0, width - w.shape[0]), (0, 0)))


def _seg_matrices():
    seg = np.zeros((D_MODEL, LANES), np.float32)
    seg[np.arange(D_MODEL), np.arange(D_MODEL) // RWKV_HEAD_DIM] = 1.0
    return jnp.asarray(seg, BF16), jnp.asarray(seg.T, BF16)


def _rwkv_pre(x, g, mu, w_r, w_k, w_v, w0, w1, w2, a0, a1, a2, g1, g2, k_k, k_a, seq,
              tm=PROJ_ROWS):
    t, d = x.shape
    lora = lambda w: -(-w.shape[1] // LANES) * LANES
    w1p, w2p = _pad_cols(w1, lora(w1)), _pad_rows(w2, lora(w1))
    a1p, a2p = _pad_cols(a1, lora(a1)), _pad_rows(a2, lora(a1))
    g1p, g2p = _pad_cols(g1, lora(g1)), _pad_rows(g2, lora(g1))
    seg, seg_t = _seg_matrices()
    row = lambda i: (i, 0)
    const = lambda i: (0, 0)
    hb = tm // SUBLANES
    vec = lambda v: v.reshape(1, d)
    mats = [w.astype(BF16) for w in (w_r, w_k, w_v)]
    ins = [x, x, vec(g), mu, *mats, vec(w0), w1p.astype(BF16), w2p.astype(BF16), vec(a0),
           a1p.astype(BF16), a2p.astype(BF16), g1p.astype(BF16), g2p.astype(BF16), vec(k_k),
           vec(k_a), seg, seg_t]
    in_specs = [pl.BlockSpec((tm, d), row),
                pl.BlockSpec((SUBLANES, d), lambda i: (jnp.maximum(i * hb - 1, 0), 0))]
    in_specs += [pl.BlockSpec(a.shape, const) for a in ins[2:]]
    tps = seq // tm
    col = pl.BlockSpec((1, d, tm), lambda i: (i // tps, 0, i % tps))
    return pl.pallas_call(
        functools.partial(_rwkv_pre_kernel, tiles_per_seq=tps),
        grid=(t // tm,),
        in_specs=in_specs,
        out_specs=[pl.BlockSpec((tm, d), row)] * 6 + [col] * 3,
        out_shape=([jax.ShapeDtypeStruct((t, d), F32)] * 6
                   + [jax.ShapeDtypeStruct((t // seq, d, seq), F32)] * 3),
        compiler_params=_params("parallel"),
        name="rwkv7_projections",
    )(*ins)


RWKV_CHUNK = 128


def _rwkv_chunk_kernel(r_ref, ld_ref, kk_ref, v_ref, bt_ref, kt_ref, ldt_ref, tril_ref, triu_ref,
                       y_ref, state_scr):
    @pl.when(pl.program_id(1) == 0)
    def _():
        state_scr[...] = jnp.zeros_like(state_scr)

    l = RWKV_CHUNK
    hd = RWKV_HEAD_DIM
    tril = tril_ref[...]
    ld = ld_ref[0]
    c_in = _dot_x3_left(tril, ld)
    a_bar = -kk_ref[0] * jnp.exp(c_in - ld)
    r_bar = r_ref[0] * jnp.exp(c_in)
    v = v_ref[0]
    ldt = ldt_ref[0]
    c_t = _dot_x3(ldt, triu_ref[...])
    scale_t = jnp.exp(-c_t)
    b_t = bt_ref[0] * scale_t
    k_t = kt_ref[0] * scale_t
    decay_col = jnp.exp(c_t[:, l - 1:l])

    row = lax.broadcasted_iota(jnp.int32, (l, l), 0)
    colx = lax.broadcasted_iota(jnp.int32, (l, l), 1)
    strict = row > colx
    incl = row >= colx
    lane = lax.broadcasted_iota(jnp.int32, (l, LANES), 1)
    first_head = lane < hd
    blockdiag = (lax.broadcasted_iota(jnp.int32, (LANES, LANES), 0) < hd) == (
        lax.broadcasted_iota(jnp.int32, (LANES, LANES), 1) < hd)
    mm = lambda a, b: jnp.dot(a, b, preferred_element_type=F32)
    b16 = lambda a: a.astype(BF16)

    pairs = range(D_MODEL // LANES)
    heads = [(c, hh) for c in pairs for hh in range(2)]
    lanes = {c: slice(c * LANES, (c + 1) * LANES) for c in pairs}
    v16 = {c: b16(v[:, lanes[c]]) for c in pairs}
    bk_t = {c: b16(jnp.concatenate([b_t[lanes[c]], k_t[lanes[c]]], axis=1)) for c in pairs}
    h2 = {c: state_scr[c] for c in pairs}
    gh = {}
    for c in pairs:
        rhs = jnp.concatenate([bk_t[c], b16(h2[c])], axis=1)
        a_p, r_p = a_bar[:, lanes[c]], r_bar[:, lanes[c]]
        for hh in range(2):
            keep = first_head if hh == 0 else ~first_head
            x = jnp.concatenate([jnp.where(keep, a_p, 0.0), jnp.where(keep, r_p, 0.0)], axis=0)
            gh[c, hh] = mm(b16(x), rhs)
    mp, u, p_r = {}, {}, {}
    for c, hh in heads:
        g = gh[c, hh]
        mp[c, hh] = b16(jnp.where(strict, g[:l, :l], 0.0))
        m_ak = b16(jnp.where(strict, g[:l, l:2 * l], 0.0))
        p_r[c, hh] = b16(jnp.concatenate([jnp.where(incl, g[l:, :l], 0.0),
                                          jnp.where(incl, g[l:, l:2 * l], 0.0)], axis=1))
        u[c, hh] = g[:l, 2 * l:] + mm(m_ak, v16[c])
    n_factors = l.bit_length() - 1
    for f in range(n_factors):
        du = {h: mm(mp[h], b16(u[h])) for h in heads}
        if f + 1 < n_factors:
            mp = {h: b16(mm(mp[h], mp[h])) for h in heads}
        u = {h: u[h] + du[h] for h in heads}
    ys = {h: gh[h][l:, 2 * l:] + mm(p_r[h], jnp.concatenate([b16(u[h]), v16[h[0]]], axis=0))
          for h in heads}
    for c in pairs:
        u_pair = jnp.where(first_head, u[c, 0], u[c, 1])
        y_ref[0, :, lanes[c]] = jnp.where(first_head, ys[c, 0], ys[c, 1])
        upd = h2[c] + mm(bk_t[c], jnp.concatenate([b16(u_pair), v16[c]], axis=0))
        state_scr[c] = jnp.where(blockdiag, upd * decay_col[lanes[c]], 0.0)


def _rwkv_scan(r, ld, kk, v, bt, kt, ldt, batch, seq):
    l = RWKV_CHUNK
    d = D_MODEL
    rows = lambda x: x.reshape(batch, seq, d)
    rblk = pl.BlockSpec((1, l, d), lambda b, c: (b, c, 0))
    cblk = pl.BlockSpec((1, d, l), lambda b, c: (b, 0, c))
    tril = jnp.asarray(np.tril(np.ones((l, l))), BF16)
    y = pl.pallas_call(
        _rwkv_chunk_kernel,
        grid=(batch, seq // l),
        in_specs=[rblk] * 4 + [cblk] * 3 + [pl.BlockSpec((l, l), lambda b, c: (0, 0))] * 2,
        out_specs=rblk,
        out_shape=jax.ShapeDtypeStruct((batch, seq, d), F32),
        scratch_shapes=[pltpu.VMEM((d // LANES, LANES, LANES), F32)],
        compiler_params=_params("parallel", "arbitrary"),
        name="rwkv7_recurrence",
    )(rows(r), rows(ld), rows(kk), rows(v), bt, kt, ldt, tril, tril.T)
    return y.reshape(batch * seq, d)


def _rwkv_post_kernel(x_ref, y_ref, r_ref, k_ref, v_ref, g_ref, lng_ref, lnb_ref, rk_ref, wo_ref,
                      gn_ref, seg_ref, segt_ref, o_ref):
    seg, seg_t = seg_ref[...], segt_ref[...]
    y = y_ref[...]
    inv = 1.0 / RWKV_HEAD_DIM
    mean = _head_sum(y, seg, seg_t) * inv
    yc = y - mean
    var = _head_sum(yc * yc, seg, seg_t) * inv
    yn = yc * lax.rsqrt(var + RWKV_GN_EPS) * lng_ref[...] + lnb_ref[...]
    bonus = _head_sum(r_ref[...] * k_ref[...] * rk_ref[...], seg, seg_t) * v_ref[...]
    out = ((yn + bonus) * g_ref[...]).astype(BF16)
    proj = jnp.dot(out, wo_ref[...], preferred_element_type=F32)
    o_ref[...] = x_ref[...] + _rms(proj, gn_ref[...], NORM_EPS)


def _rwkv_post(x, y, r, k, v, g, ln_g, ln_b, r_k, w_o, gn, tm=PROJ_ROWS):
    t, d = x.shape
    seg, seg_t = _seg_matrices()
    row = lambda i: (i, 0)
    const = lambda i: (0, 0)
    vec = lambda a: a.reshape(1, d)
    small = [vec(ln_g), vec(ln_b), vec(r_k), w_o.astype(BF16), vec(gn), seg, seg_t]
    return pl.pallas_call(
        _rwkv_post_kernel,
        grid=(t // tm,),
        in_specs=[pl.BlockSpec((tm, d), row)] * 6 + [pl.BlockSpec(a.shape, const) for a in small],
        out_specs=pl.BlockSpec((tm, d), row),
        out_shape=jax.ShapeDtypeStruct((t, d), F32),
        compiler_params=_params("parallel"),
        name="rwkv7_output",
    )(x, y, r, k, v, g, *small)


def _nsa_ssd_mixer(x, g_pre, g_post, cos, sin, w_in, pe_k, w1_k, w2_k, pe_v, w1_v, w2_v, conv_w,
                   conv_b, dt_bias, a_log, d_skip, norm_w, w_out, batch, seq):
    q, kv16, kaug, kwin, vst, vwt, z, xbc, misc = _inproj(x, g_pre, w_in, cos, sin, batch, seq)
    o_a = _nsa(q, kv16, kaug, kwin, vst, vwt, misc, pe_k, w1_k, w2_k, pe_v, w1_v, w2_v, batch, seq)
    o_b = _ssd(z, xbc, misc, conv_w, conv_b, dt_bias, a_log, d_skip, norm_w, batch, seq)
    return g_post, [o_a, o_b], [w_out[:NSA_Q_W], w_out[NSA_Q_W:]]


def _rwkv7_mixer(x, g_pre, g_post, mu, w_r, w_k, w_v, w_o, w0, w1, w2, a0, a1, a2, g1, g2, k_k,
                 k_a, r_k, ln_g, ln_b, batch, seq):
    r, ld, k, v, kk, g, bt, kt, ldt = _rwkv_pre(x, g_pre, mu, w_r, w_k, w_v, w0, w1, w2, a0, a1,
                                                a2, g1, g2, k_k, k_a, seq)
    y = _rwkv_scan(r, ld, kk, v, bt, kt, ldt, batch, seq)
    return _rwkv_post(x, y, r, k, v, g, ln_g, ln_b, r_k, w_o, g_post)


def kernel(x, norm_gains, ffn1_w_gate, ffn1_w_up, ffn1_w_down, ffn2_w_gate, ffn2_w_up, ffn2_w_down, ab_w_in, a_cmp_pe_k, a_cmp_w1_k, a_cmp_w2_k, a_cmp_pe_v, a_cmp_w1_v, a_cmp_w2_v, b_conv_w, b_conv_b, b_dt_bias, b_a_log, b_d_skip, b_norm_w, ab_w_out, c_mu, c_w_r, c_w_k, c_w_v, c_w_o, c_w0, c_w1, c_w2, c_a0, c_a1, c_a2, c_g1, c_g2, c_k_k, c_k_a, c_r_k, c_ln_g, c_ln_b):
    batch, seq, d = x.shape
    depth = norm_gains.shape[0]
    cos, sin = _rope_tables(seq)
    x = x.reshape(batch * seq, d)
    (ffn1_w_gate, ffn1_w_up, ffn1_w_down, ffn2_w_gate, ffn2_w_up, ffn2_w_down) = [
        w.astype(BF16) for w in (ffn1_w_gate, ffn1_w_up, ffn1_w_down, ffn2_w_gate, ffn2_w_up,
                                 ffn2_w_down)]
    for layer in range(depth):
        ng = norm_gains[layer]
        x = _ffn(x, ng[0], ng[1], ffn1_w_gate, ffn1_w_up, ffn1_w_down, layer)
        i = layer // 2
        mixer = None
        if layer % 2 == 0:
            mixer = _nsa_ssd_mixer(x, ng[2], ng[3], cos, sin, ab_w_in[i], a_cmp_pe_k[i],
                                   a_cmp_w1_k[i], a_cmp_w2_k[i], a_cmp_pe_v[i], a_cmp_w1_v[i],
                                   a_cmp_w2_v[i], b_conv_w[i], b_conv_b[i], b_dt_bias[i],
                                   b_a_log[i], b_d_skip[i], b_norm_w[i], ab_w_out[i], batch, seq)
        else:
            x = _rwkv7_mixer(x, ng[2], ng[3], c_mu[i], c_w_r[i], c_w_k[i], c_w_v[i], c_w_o[i],
                             c_w0[i], c_w1[i], c_w2[i], c_a0[i], c_a1[i], c_a2[i], c_g1[i],
                             c_g2[i], c_k_k[i], c_k_a[i], c_r_k[i], c_ln_g[i], c_ln_b[i],
                             batch, seq)
        x = _ffn(x, ng[4], ng[5], ffn2_w_gate, ffn2_w_up, ffn2_w_down, layer, mixer=mixer)
    return x.reshape(batch, seq, d)
```

```python
import functools

import jax
import jax.numpy as jnp
import numpy as np
from jax import lax
from jax.experimental import pallas as pl
from jax.experimental.pallas import tpu as pltpu

F32 = jnp.float32
BF16 = jnp.bfloat16
HIGHEST = lax.Precision.HIGHEST

D_MODEL = 1024
D_FF = 2816
NORM_EPS = 1e-6
NSA_HEADS = 8
NSA_KV_HEADS = 2
NSA_GROUP = NSA_HEADS // NSA_KV_HEADS
NSA_HEAD_DIM = 64
CMP_BLOCK = 32
CMP_STRIDE = 16
SEL_BLOCK = 64
SEL_TOPK = 16
WINDOW = 512
ROPE_THETA = 10000.0
FORCE_SCORE = 1e4
SEL_LANES = 128
SSD_HEADS = 16
SSD_HEAD_DIM = 64
SSD_D_INNER = SSD_HEADS * SSD_HEAD_DIM
SSD_GROUPS = 2
SSD_STATE = 128
SSD_CONV = 4
SSD_CHUNK = 128
SSD_NORM_EPS = 1e-5
SSD_XBC = SSD_D_INNER + 2 * SSD_GROUPS * SSD_STATE
RWKV_HEAD_DIM = 64
RWKV_HEADS = D_MODEL // RWKV_HEAD_DIM
RWKV_GN_EPS = 64e-5

NSA_Q_W = NSA_HEADS * NSA_HEAD_DIM
NSA_KV_W = NSA_KV_HEADS * NSA_HEAD_DIM
IN_SPLITS = (NSA_Q_W, NSA_KV_W, NSA_KV_W, NSA_KV_W, NSA_KV_W, NSA_KV_W, NSA_KV_W,
             NSA_HEADS * 3, SSD_D_INNER, SSD_XBC, SSD_HEADS)
IN_WIDTH = sum(IN_SPLITS)

LANES = 128
SUBLANES = 8
MXU_TILE = 256
VMEM_LIMIT_BYTES = 56 * 1024 * 1024

FFN_ROWS = 512
FFN_COLUMN_CHUNKS = 2
PROJ_ROWS = 256
NSA_QUERY_ROWS = 256
NSA_KEY_ROWS = 512

NEG_MASK = -1e30
NEG_UNSELECTED = -2.0 ** 30
NEG_TAKEN = -3e38
LOG2_E = 1.4426950408889634


def _params(*sem):
    return pltpu.CompilerParams(dimension_semantics=sem, vmem_limit_bytes=VMEM_LIMIT_BYTES)


def _rms(x, g, eps):
    return x * lax.rsqrt(jnp.mean(x * x, -1, keepdims=True) + eps) * g


def _silu(x):
    return x * jax.nn.sigmoid(x)


def _softplus(x):
    return jnp.maximum(x, 0.0) + jnp.log1p(jnp.exp(-jnp.abs(x)))


def _split3(a):
    a1 = a.astype(BF16)
    r1 = a - a1.astype(F32)
    a2 = r1.astype(BF16)
    a3 = (r1 - a2.astype(F32)).astype(BF16)
    return a1, a2, a3


def _dot_x3(a, b):
    if a.shape[1] == LANES:
        return jnp.dot(jnp.concatenate(_split3(a), axis=1), jnp.concatenate([b, b, b], axis=0),
                       preferred_element_type=F32)
    acc = None
    for piece in _split3(a):
        d = jnp.dot(piece, b, preferred_element_type=F32)
        acc = d if acc is None else acc + d
    return acc


def _dot_x3_left(b, a):
    if a.shape[0] == LANES:
        return jnp.dot(jnp.concatenate([b, b, b], axis=1), jnp.concatenate(_split3(a), axis=0),
                       preferred_element_type=F32)
    acc = None
    for piece in _split3(a):
        d = jnp.dot(b, piece, preferred_element_type=F32)
        acc = d if acc is None else acc + d
    return acc


def _dot_nt(a, b, **kw):
    return lax.dot_general(a, b, (((1,), (1,)), ((), ())), preferred_element_type=F32, **kw)


def _ffn_kernel(x_ref, gi_ref, go_ref, wg_ref, wu_ref, wd_ref, *rest, chunks, n_parts):
    o_ref = rest[-1]
    x = x_ref[...]
    if n_parts:
        proj = None
        for p_ref, w_ref in zip(rest[1:1 + n_parts], rest[1 + n_parts:1 + 2 * n_parts]):
            dd = jnp.dot(p_ref[...].astype(BF16), w_ref[...], preferred_element_type=F32)
            proj = dd if proj is None else proj + dd
        x = x + _rms(proj, rest[0][...], NORM_EPS)
    h = _rms(x, gi_ref[...], NORM_EPS).astype(BF16)
    acc = None
    for lo, hi in chunks:
        gate = jnp.dot(h, wg_ref[:, lo:hi], preferred_element_type=F32)
        up = jnp.dot(h, wu_ref[:, lo:hi], preferred_element_type=F32)
        act = (_silu(gate) * up).astype(BF16)
        part = jnp.dot(act, wd_ref[lo:hi, :], preferred_element_type=F32)
        acc = part if acc is None else acc + part
    o_ref[...] = x + 0.5 * _rms(acc, go_ref[...], NORM_EPS)


def _ffn(x, g_in, g_out, w_gate, w_up, w_down, layer, mixer=None, tm=FFN_ROWS,
         n_chunks=FFN_COLUMN_CHUNKS):
    t, d = x.shape
    f = w_gate.shape[2]
    tiles = f // MXU_TILE
    assert tiles * MXU_TILE == f
    cuts = [MXU_TILE * ((tiles * c + n_chunks - 1) // n_chunks) for c in range(n_chunks + 1)]
    chunks = tuple(zip(cuts[:-1], cuts[1:]))
    row = lambda i: (i, 0)
    const = lambda i: (0, 0)
    resident = lambda shape: pl.BlockSpec(shape, const, pipeline_mode=pl.Buffered(1))
    of_layer = lambda shape: pl.BlockSpec((None,) + shape, lambda i: (layer, 0, 0),
                                          pipeline_mode=pl.Buffered(1))
    operands = [x, g_in.reshape(1, d), g_out.reshape(1, d), w_gate, w_up, w_down]
    in_specs = [pl.BlockSpec((tm, d), row), pl.BlockSpec((1, d), const),
                pl.BlockSpec((1, d), const), of_layer((d, f)), of_layer((d, f)),
                of_layer((f, d))]
    n_parts = 0
    if mixer is not None:
        g_mixer, parts, weights = mixer
        n_parts = len(parts)
        operands += [g_mixer.reshape(1, d), *parts, *[w.astype(BF16) for w in weights]]
        in_specs += ([pl.BlockSpec((1, d), const)]
                     + [pl.BlockSpec((tm, p.shape[1]), row) for p in parts]
                     + [resident(w.shape) for w in weights])
    return pl.pallas_call(
        functools.partial(_ffn_kernel, chunks=chunks, n_parts=n_parts),
        grid=(t // tm,),
        in_specs=in_specs,
        out_specs=pl.BlockSpec((tm, d), row),
        out_shape=jax.ShapeDtypeStruct((t, d), F32),
        compiler_params=_params("parallel"),
        name="ffn_half_step",
    )(*operands)


INPROJ_MISC_W = 256
INPROJ_KV_W = 6 * NSA_KV_W
INPROJ_WIDTH = NSA_Q_W + INPROJ_KV_W + SSD_D_INNER + SSD_XBC + INPROJ_MISC_W


def _swap_halves(x):
    w = x.shape[-1]
    lane = lax.broadcasted_iota(jnp.int32, x.shape, x.ndim - 1)
    low = (lane & (NSA_HEAD_DIM - 1)) < (NSA_HEAD_DIM // 2)
    return jnp.where(low, pltpu.roll(x, w - NSA_HEAD_DIM // 2, x.ndim - 1),
                     pltpu.roll(x, NSA_HEAD_DIM // 2, x.ndim - 1))


def _inproj_kernel(x_ref, g_ref, w_ref, cos_ref, sin_ref, q_ref, kv16_ref, kaug_ref, kwin_ref,
                   vst_ref, vwt_ref, z_ref, xbc_ref, misc_ref, kvc_scr, *, tiles_per_seq):
    h = _rms(x_ref[...], g_ref[...], NORM_EPS).astype(BF16)
    proj = jnp.dot(h, w_ref[...], preferred_element_type=F32)
    tm = proj.shape[0]
    cos = cos_ref[...]
    sin = sin_ref[...]
    q = proj[:, :NSA_Q_W]
    cos_q = jnp.concatenate([cos] * (NSA_Q_W // LANES), axis=1)
    sin_q = jnp.concatenate([sin] * (NSA_Q_W // LANES), axis=1)
    q_ref[...] = (q * cos_q + _swap_halves(q) * sin_q) * (NSA_HEAD_DIM ** -0.5)
    piece = lambda i: proj[:, NSA_Q_W + i * NSA_KV_W:NSA_Q_W + (i + 1) * NSA_KV_W]
    rope = lambda p: p * cos + _swap_halves(p) * sin
    d = NSA_HEAD_DIM
    kvc_scr[0] = rope(piece(0))
    kvc_scr[1] = piece(1)
    groups = tm // CMP_STRIDE
    first = lax.broadcasted_iota(jnp.int32, (groups, LANES), 1) < d
    for kv in range(2):
        cols = [[] for _ in range(NSA_KV_HEADS)]
        for j in range(CMP_STRIDE // 2):
            even = kvc_scr[kv, pl.ds(2 * j, groups, stride=CMP_STRIDE), :]
            odd = kvc_scr[kv, pl.ds(2 * j + 1, groups, stride=CMP_STRIDE), :]
            cols[0].append(jnp.where(first, even, pltpu.roll(odd, d, 1)))
            cols[1].append(jnp.where(first, pltpu.roll(even, d, 1), odd))
        for hh in range(NSA_KV_HEADS):
            kv16_ref[kv, 0, hh] = jnp.concatenate(cols[hh], axis=1)
    k_sel, k_win = rope(piece(2)), rope(piece(4))
    pos = (pl.program_id(0) % tiles_per_seq) * tm + lax.broadcasted_iota(
        jnp.int32, (tm, SEL_LANES), 0)
    block_id = lax.shift_right_logical(pos, SEL_BLOCK.bit_length() - 1)
    onehot = jnp.where(lax.broadcasted_iota(jnp.int32, (tm, SEL_LANES), 1) == block_id,
                       1.0, 0.0).astype(BF16)
    vst = piece(3).T
    vwt = piece(5).T
    d = NSA_HEAD_DIM
    for hh in range(NSA_KV_HEADS):
        kaug_ref[0, hh] = jnp.concatenate([onehot, k_sel[:, hh * d:(hh + 1) * d].astype(BF16)],
                                          axis=1)
        kwin_ref[0, hh] = k_win[:, hh * d:(hh + 1) * d].astype(BF16)
        vst_ref[0, hh] = vst[hh * d:(hh + 1) * d].astype(BF16)
        vwt_ref[0, hh] = vwt[hh * d:(hh + 1) * d].astype(BF16)
    o = NSA_Q_W + INPROJ_KV_W
    z_ref[...] = proj[:, o:o + SSD_D_INNER]
    o += SSD_D_INNER
    xbc_ref[...] = proj[:, o:o + SSD_XBC]
    o += SSD_XBC
    misc_ref[...] = proj[:, o:o + INPROJ_MISC_W]


def _inproj(x, g, w_in, cos, sin, batch, seq, tm=PROJ_ROWS):
    t, d = x.shape
    offs = np.cumsum(IN_SPLITS)[:-1].tolist()
    q, kc, vc, ks, vs, kw, vw, gl, z, xbc, dt = jnp.split(w_in, offs, -1)
    pad = jnp.zeros((d, INPROJ_MISC_W - gl.shape[1] - dt.shape[1]), w_in.dtype)
    w = jnp.concatenate([q, kc, vc, ks, vs, kw, vw, z, xbc, gl, dt, pad], -1).astype(BF16)
    assert w.shape[1] == INPROJ_WIDTH
    nseq = seq // tm
    row = lambda i: (i, 0)
    const = lambda i: (0, 0)
    hd, hkv = NSA_HEAD_DIM, NSA_KV_HEADS
    by_head_rows = lambda wd: pl.BlockSpec((1, hkv, tm, wd), lambda i: (i // nseq, 0, i % nseq, 0))
    by_head_cols = pl.BlockSpec((1, hkv, hd, tm), lambda i: (i // nseq, 0, 0, i % nseq))
    flat = lambda wd: (pl.BlockSpec((tm, wd), row), jax.ShapeDtypeStruct((t, wd), F32))
    groups = tm // CMP_STRIDE
    kv16_spec = pl.BlockSpec((2, 1, hkv, groups, CMP_STRIDE * hd),
                             lambda i: (0, i // nseq, 0, i % nseq, 0))
    outs = [
        flat(NSA_Q_W),
        (kv16_spec, jax.ShapeDtypeStruct((2, batch, hkv, seq // CMP_STRIDE, CMP_STRIDE * hd), F32)),
        (by_head_rows(SEL_LANES + hd), jax.ShapeDtypeStruct((batch, hkv, seq, SEL_LANES + hd), BF16)),
        (by_head_rows(hd), jax.ShapeDtypeStruct((batch, hkv, seq, hd), BF16)),
        (by_head_cols, jax.ShapeDtypeStruct((batch, hkv, hd, seq), BF16)),
        (by_head_cols, jax.ShapeDtypeStruct((batch, hkv, hd, seq), BF16)),
        flat(SSD_D_INNER),
        flat(SSD_XBC),
        flat(INPROJ_MISC_W),
    ]
    return pl.pallas_call(
        functools.partial(_inproj_kernel, tiles_per_seq=nseq),
        grid=(t // tm,),
        in_specs=[
            pl.BlockSpec((tm, d), row),
            pl.BlockSpec((1, d), const),
            pl.BlockSpec((d, INPROJ_WIDTH), const),
            pl.BlockSpec((tm, LANES), lambda i: (i % nseq, 0)),
            pl.BlockSpec((tm, LANES), lambda i: (i % nseq, 0)),
        ],
        out_specs=[o[0] for o in outs],
        out_shape=[o[1] for o in outs],
        scratch_shapes=[pltpu.VMEM((2, tm, NSA_KV_W), F32)],
        compiler_params=_params("parallel"),
        name="mixer0_in_proj",
    )(x, g.reshape(1, d), w, cos, sin)


def _rope_tables(seq):
    inv = ROPE_THETA ** (-np.arange(0, NSA_HEAD_DIM, 2, dtype=np.float64) / NSA_HEAD_DIM)
    ang = np.arange(seq, dtype=np.float64)[:, None] * inv[None, :]
    cos, sin = np.cos(ang), np.sin(ang)
    reps = LANES // NSA_HEAD_DIM
    cos_t = np.concatenate([cos, cos] * reps, -1).astype(np.float32)
    sin_t = np.concatenate([-sin, sin] * reps, -1).astype(np.float32)
    return jnp.asarray(cos_t), jnp.asarray(sin_t)


def _compress_kernel(k_ref, pe_ref, w1_ref, w2_ref, o_ref):
    k16 = k_ref[0, 0]
    w1 = w1_ref[0]
    half = w1.shape[0] // 2
    first = jnp.dot(k16, w1[:half], precision=HIGHEST, preferred_element_type=F32)
    second = jnp.dot(k16, w1[half:], precision=HIGHEST, preferred_element_type=F32)
    bias = jnp.dot(pe_ref[0], w1, precision=HIGHEST, preferred_element_type=F32)[0:1]
    n = k16.shape[0]
    pre = first + pltpu.roll(second, n - 1, 0) + bias
    o_ref[0, 0] = jnp.dot(_silu(pre), w2_ref[0], precision=HIGHEST, preferred_element_type=F32)


def _compress(kv16, pe, w1, w2):
    two, bh, n, wd = kv16.shape
    d = w2.shape[-1]
    return pl.pallas_call(
        _compress_kernel,
        grid=(two, bh),
        in_specs=[
            pl.BlockSpec((1, 1, n, wd), lambda a, b: (a, b, 0, 0)),
            pl.BlockSpec((1, SUBLANES, pe.shape[-1]), lambda a, b: (a, 0, 0)),
            pl.BlockSpec((1,) + w1.shape[1:], lambda a, b: (a, 0, 0)),
            pl.BlockSpec((1, d, d), lambda a, b: (a, 0, 0)),
        ],
        out_specs=pl.BlockSpec((1, 1, n, d), lambda a, b: (a, b, 0, 0)),
        out_shape=jax.ShapeDtypeStruct((two, bh, n, d), F32),
        compiler_params=_params("parallel", "parallel"),
        name="nsa_compress",
    )(kv16, pe, w1, w2)


def _group_rows(q):
    return jnp.concatenate(
        [q[:, g * NSA_HEAD_DIM:(g + 1) * NSA_HEAD_DIM] for g in range(NSA_GROUP)], axis=0)


def _ungroup_rows(o, tq):
    return jnp.concatenate([o[g * tq:(g + 1) * tq] for g in range(NSA_GROUP)], axis=1)


def _dot_nt_hi(a, b):
    a1 = a.astype(BF16)
    a2 = (a - a1.astype(F32)).astype(BF16)
    b1 = b.astype(BF16)
    b2 = (b - b1.astype(F32)).astype(BF16)
    if 3 * a.shape[1] <= MXU_TILE:
        return _dot_nt(jnp.concatenate([a1, a1, a2], axis=1),
                       jnp.concatenate([b1, b2, b1], axis=1))
    return _dot_nt(a1, b1) + _dot_nt(a1, b2) + _dot_nt(a2, b1)


def _nsa_select_kernel(q_ref, kc_ref, vc_ref, ovt_ref, oc_ref, biast_ref, *, tq, topk):
    s0 = pl.program_id(1) * tq
    gw = NSA_GROUP * NSA_HEAD_DIM
    heads = range(NSA_KV_HEADS)
    q = q_ref[0]
    s = [_dot_nt_hi(_group_rows(q[:, h * gw:(h + 1) * gw]), kc_ref[h]) for h in heads]
    rows, ncmp = s[0].shape
    t = s0 + (lax.broadcasted_iota(jnp.int32, (rows, ncmp), 0) & (tq - 1))
    cmp_end = lax.broadcasted_iota(jnp.int32, (rows, ncmp), 1) * CMP_STRIDE + (CMP_BLOCK - 1)
    mask = cmp_end <= t
    p = []
    for h in heads:
        sh = jnp.where(mask, s[h], NEG_MASK)
        ph = jnp.where(mask, jnp.exp(sh - jnp.max(sh, -1, keepdims=True)), 0.0)
        p.append(ph / jnp.maximum(jnp.sum(ph, -1, keepdims=True), 1e-30))
    for h in heads:
        o = jnp.dot(p[h].astype(BF16), vc_ref[h].astype(BF16), preferred_element_type=F32)
        oc_ref[0, :, h * gw:(h + 1) * gw] = _ungroup_rows(o, tq)

    ovt = ovt_ref[...]
    imp = []
    for h in heads:
        psum = p[h][0:tq]
        for g in range(1, NSA_GROUP):
            psum = psum + p[h][g * tq:(g + 1) * tq]
        acc = None
        for piece in _split3(psum):
            d = _dot_nt(ovt, piece)
            acc = d if acc is None else acc + d
        imp.append(acc)
    blk = lax.broadcasted_iota(jnp.int32, imp[0].shape, 0)
    tt = s0 + lax.broadcasted_iota(jnp.int32, imp[0].shape, 1)
    cur = lax.shift_right_logical(tt, SEL_BLOCK.bit_length() - 1)
    forced = (blk == 0) | (blk == cur) | (blk == cur - 1)
    valid = blk * SEL_BLOCK <= tt
    x = [jnp.where(valid, jnp.where(forced, FORCE_SCORE, imp[h]), NEG_MASK) for h in heads]
    blk_f = blk.astype(F32)
    sel = [jnp.zeros(blk.shape, jnp.bool_) for _ in heads]
    for _ in range(topk):
        for h in heads:
            m = jnp.max(x[h], 0, keepdims=True)
            idx = jnp.min(jnp.where(x[h] == m, blk_f, float(SEL_LANES)), 0, keepdims=True)
            hit = blk_f == idx
            sel[h] = sel[h] | hit
            x[h] = jnp.where(hit, NEG_TAKEN, x[h])
    for h in heads:
        biast_ref[0, h] = jnp.where(sel[h], 0.0, NEG_UNSELECTED).astype(BF16)


def _nsa_select(q, kc, vc, overlap, batch, seq, tq=NSA_QUERY_ROWS):
    ncmp = kc.shape[1]
    qw = NSA_KV_HEADS * NSA_GROUP * NSA_HEAD_DIM
    topk = min(SEL_TOPK, seq // SEL_BLOCK)
    kern = functools.partial(_nsa_select_kernel, tq=tq, topk=topk)
    return pl.pallas_call(
        kern,
        grid=(batch, seq // tq),
        in_specs=[
            pl.BlockSpec((1, tq, qw), lambda b, i: (b, i, 0)),
            pl.BlockSpec((NSA_KV_HEADS, ncmp, NSA_HEAD_DIM), lambda b, i: (b, 0, 0)),
            pl.BlockSpec((NSA_KV_HEADS, ncmp, NSA_HEAD_DIM), lambda b, i: (b, 0, 0)),
            pl.BlockSpec((SEL_LANES, ncmp), lambda b, i: (0, 0)),
        ],
        out_specs=[
            pl.BlockSpec((1, tq, qw), lambda b, i: (b, i, 0)),
            pl.BlockSpec((1, NSA_KV_HEADS, SEL_LANES, tq), lambda b, i: (b, 0, 0, i)),
        ],
        out_shape=[
            jax.ShapeDtypeStruct((batch, seq, qw), F32),
            jax.ShapeDtypeStruct((batch, NSA_KV_HEADS, SEL_LANES, seq), BF16),
        ],
        compiler_params=_params("parallel", "parallel"),
        name="nsa_compressed_select",
    )(q, kc, vc, overlap)


NSA_LANE_SPLIT = 2


def _nsa_attend_kernel(q_ref, biast_ref, ka_ref, vst_ref, kw_ref, vwt_ref, oc_ref, gl_ref, o_ref,
                       sa_scr, sb_scr, p_scr, w_scr, *, tq, tk):
    i = pl.program_id(2)
    s0 = i * tq
    n = NSA_GROUP * tq
    half = n // NSA_LANE_SPLIT
    d = NSA_HEAD_DIM
    qt = (q_ref[0] * LOG2_E).T
    qgt = jnp.concatenate([qt[g * d:(g + 1) * d] for g in range(NSA_GROUP)], axis=1)
    qgt = qgt.astype(BF16)
    qat = jnp.concatenate([jnp.concatenate([biast_ref[0, 0]] * NSA_GROUP, axis=1), qgt], axis=0)
    init = tuple((jnp.full((1, half), NEG_MASK, F32), jnp.zeros((1, half), F32),
                  jnp.zeros((d, half), F32)) for _ in range(NSA_LANE_SPLIT))

    def query_pos(shape):
        return s0 + (lax.broadcasted_iota(jnp.int32, shape, 1) & (tq - 1))

    halves = range(NSA_LANE_SPLIT)

    def scores(kt):
        k = ka_ref[0, pl.ds(pl.multiple_of(kt * tk, tk), tk), :]
        return tuple(jnp.dot(k, qat[:, hf * half:(hf + 1) * half], preferred_element_type=F32)
                     for hf in halves)

    def values(kt):
        vt = vst_ref[0, :, pl.ds(pl.multiple_of(kt * tk, tk), tk)]
        return tuple(jnp.dot(vt, p_scr[:, hf * half:(hf + 1) * half],
                             preferred_element_type=F32) for hf in halves)

    span = WINDOW + tq
    start = pl.multiple_of(jnp.maximum(s0 - WINDOW, 0), tq)

    def window_scores():
        kwin = kw_ref[0, pl.ds(start, span), :]
        return tuple(jnp.dot(kwin, qgt[:, hf * half:(hf + 1) * half],
                             preferred_element_type=F32) for hf in halves)

    def sel_step(kt, stats, src_scr, dst_scr, causal):
        s_next = window_scores() if causal else scores(kt + 1)
        pv = values(jnp.maximum(kt - 1, 0))
        new_stats = []
        for hf in halves:
            cols = slice(hf * half, (hf + 1) * half)
            m, l, acc = stats[hf]
            s = src_scr[:, cols]
            if causal:
                kp = kt * tk + lax.broadcasted_iota(jnp.int32, s.shape, 0)
                mask = kp <= query_pos(s.shape)
                s = jnp.where(mask, s, NEG_MASK)
            m_new = jnp.maximum(m, jnp.max(s, 0, keepdims=True))
            alpha = jnp.exp2(m - m_new)
            p = jnp.exp2(s - m_new)
            if causal:
                p = jnp.where(mask, p, 0.0)
            new_stats.append((m_new, alpha * l + jnp.sum(p, 0, keepdims=True),
                              alpha * (acc + pv[hf])))
            p_scr[:, cols] = p.astype(BF16)
        for hf in halves:
            (w_scr if causal else dst_scr)[:, hf * half:(hf + 1) * half] = s_next[hf]
        return tuple(new_stats)

    def by_parity(kt, stats, causal):
        return lax.cond((kt & 1) == 0,
                        lambda st: sel_step(kt, st, sa_scr, sb_scr, causal),
                        lambda st: sel_step(kt, st, sb_scr, sa_scr, causal), stats)

    n_full = s0 // tk
    first_scores = scores(0)
    for hf in halves:
        sa_scr[:, hf * half:(hf + 1) * half] = first_scores[hf]
    p_scr[...] = jnp.zeros_like(p_scr)
    stats = lax.fori_loop(0, n_full, lambda kt, c: by_parity(kt, c, False), init)
    stats = by_parity(n_full, stats, True)
    pv_last = values(n_full)

    vwt = vwt_ref[0, :, pl.ds(start, span)]
    p_w, l_w = [], []
    for hf in halves:
        s = w_scr[:, hf * half:(hf + 1) * half]
        kp = start + lax.broadcasted_iota(jnp.int32, s.shape, 0)
        t = query_pos(s.shape)
        mask = (kp <= t) & (kp > t - WINDOW)
        s = jnp.where(mask, s, NEG_MASK)
        p = jnp.where(mask, jnp.exp2(s - jnp.max(s, 0, keepdims=True)), 0.0)
        l_w.append(jnp.sum(p, 0, keepdims=True))
        p_w.append(p.astype(BF16))
    o_w = jnp.concatenate([jnp.dot(vwt, p_w[hf], preferred_element_type=F32)
                           / jnp.maximum(l_w[hf], 1e-30) for hf in halves], axis=1)
    o_s = jnp.concatenate([(stats[hf][2] + pv_last[hf]) / jnp.maximum(stats[hf][1], 1e-30)
                           for hf in halves], axis=1)

    def rows_layout(ot):
        return jnp.concatenate([ot[:, g * tq:(g + 1) * tq] for g in range(NSA_GROUP)], axis=0).T

    o_s = rows_layout(o_s)
    o_w = rows_layout(o_w)
    gates = jax.nn.sigmoid(gl_ref[0, 0])
    o_c = oc_ref[0]
    pieces = []
    for g in range(NSA_GROUP):
        c = slice(g * d, (g + 1) * d)
        pieces.append(gates[:, 3 * g:3 * g + 1] * o_c[:, c]
                      + gates[:, 3 * g + 1:3 * g + 2] * o_s[:, c]
                      + gates[:, 3 * g + 2:3 * g + 3] * o_w[:, c])
    o_ref[0] = jnp.concatenate(pieces, axis=1)


def _nsa_attend(q, biast, kaug, vst, kw, vwt, o_c, gl, batch, seq, tq=NSA_QUERY_ROWS,
                tk=NSA_KEY_ROWS):
    gw = NSA_GROUP * NSA_HEAD_DIM
    tk = min(tk, seq)
    assert tk % tq == 0 and WINDOW % tq == 0 and seq >= WINDOW + tq
    kern = functools.partial(_nsa_attend_kernel, tq=tq, tk=tk)
    bh = lambda b, h, i: (b * NSA_KV_HEADS + h, 0, 0)
    return pl.pallas_call(
        kern,
        grid=(batch, NSA_KV_HEADS, seq // tq),
        in_specs=[
            pl.BlockSpec((1, tq, gw), lambda b, h, i: (b, i, h)),
            pl.BlockSpec((1, 1, SEL_LANES, tq), lambda b, h, i: (b, h, 0, i)),
            pl.BlockSpec((1, seq, SEL_LANES + NSA_HEAD_DIM), bh),
            pl.BlockSpec((1, NSA_HEAD_DIM, seq), bh),
            pl.BlockSpec((1, seq, NSA_HEAD_DIM), bh),
            pl.BlockSpec((1, NSA_HEAD_DIM, seq), bh),
            pl.BlockSpec((1, tq, gw), lambda b, h, i: (b, i, h)),
            pl.BlockSpec((1, 1, tq, NSA_GROUP * 3), lambda b, h, i: (b, h, i, 0)),
        ],
        out_specs=pl.BlockSpec((1, tq, gw), lambda b, h, i: (b, i, h)),
        out_shape=jax.ShapeDtypeStruct((batch, seq, NSA_KV_HEADS * gw), F32),
        scratch_shapes=[pltpu.VMEM((tk, NSA_GROUP * tq), F32),
                        pltpu.VMEM((tk, NSA_GROUP * tq), F32),
                        pltpu.VMEM((tk, NSA_GROUP * tq), BF16),
                        pltpu.VMEM((WINDOW + tq, NSA_GROUP * tq), F32)],
        compiler_params=_params("parallel", "parallel", "arbitrary"),
        name="nsa_selected_window",
    )(q, biast, kaug, vst, kw, vwt, o_c, gl)


def _nsa(q, kv16, kaug, kwin, vst, vwt, misc, pe_k, w1_k, w2_k, pe_v, w1_v, w2_v, batch, seq):
    d = NSA_HEAD_DIM
    bh = batch * NSA_KV_HEADS
    n16 = seq // CMP_STRIDE
    kv16 = kv16.reshape(2, bh, n16, CMP_STRIDE * d)
    pe = jnp.stack([pe_k, pe_v]).reshape(2, 1, CMP_BLOCK * d)
    pe = jnp.broadcast_to(pe, (2, SUBLANES, CMP_BLOCK * d))
    cmp = _compress(kv16, pe, jnp.stack([w1_k, w1_v]), jnp.stack([w2_k, w2_v]))
    kc, vc = cmp[0], cmp[1]

    n_sel = seq // SEL_BLOCK
    cmp_start = np.arange(n16) * CMP_STRIDE
    sel_start = np.arange(SEL_LANES) * SEL_BLOCK
    overlap = ((cmp_start[:, None] < sel_start[None, :] + SEL_BLOCK)
               & (cmp_start[:, None] + CMP_BLOCK - 1 >= sel_start[None, :])
               & (np.arange(SEL_LANES)[None, :] < n_sel)
               & (np.arange(n16)[:, None] < (seq - CMP_BLOCK) // CMP_STRIDE + 1))
    overlap_t = jnp.asarray(overlap.T, BF16)

    q3 = q.reshape(batch, seq, NSA_Q_W)
    o_c, biast = _nsa_select(q3, kc, vc, overlap_t, batch, seq)

    gl = misc[:, :NSA_HEADS * 3].reshape(batch, seq, NSA_KV_HEADS, NSA_GROUP * 3)
    gl = jnp.moveaxis(gl, 2, 1)
    per_head = lambda a: a.reshape((bh,) + a.shape[2:])
    o = _nsa_attend(q3, biast, per_head(kaug), per_head(vst), per_head(kwin), per_head(vwt), o_c,
                    gl, batch, seq)
    return o.reshape(batch * seq, NSA_Q_W)


def _ssd_kernel(xbc_ref, halo_ref, z_ref, dt_ref, cw_ref, cb_ref, dtb_ref, alog_ref, dskip_ref,
                nw_ref, tril_ref, spread_ref, o_ref, state_scr, y_scr):
    c = pl.program_id(1)
    l = SSD_CHUNK

    @pl.when(c == 0)
    def _():
        state_scr[...] = jnp.zeros_like(state_scr)

    x = xbc_ref[0]
    halo = jnp.where(c == 0, 0.0, halo_ref[0])
    xx = jnp.concatenate([halo, x], axis=0)
    cw = cw_ref[...]
    conv = cb_ref[...]
    for k in range(SSD_CONV):
        off = SUBLANES - (SSD_CONV - 1) + k
        conv = conv + cw[k:k + 1] * xx[off:off + l]
    xbc = _silu(conv)
    xs = xbc[:, :SSD_D_INNER]
    gn = SSD_GROUPS * SSD_STATE
    bmat = xbc[:, SSD_D_INNER:SSD_D_INNER + gn]
    cmat = xbc[:, SSD_D_INNER + gn:]

    dt = _softplus(dt_ref[0] + dtb_ref[...])
    da = dt * (-jnp.exp(alog_ref[...]))
    a_cs = _dot_x3_left(tril_ref[...], da)
    a_cs_t = a_cs.T
    a_last = a_cs[l - 1:l]
    causal = (lax.broadcasted_iota(jnp.int32, (l, l), 0)
              >= lax.broadcasted_iota(jnp.int32, (l, l), 1))

    spread = spread_ref[...]
    dt_x = _dot_x3(dt, spread)
    grow_x = _dot_x3(jnp.exp(a_cs), spread)
    fade_x = _dot_x3(jnp.exp(a_last - a_cs), spread)
    chunk_x = _dot_x3(jnp.broadcast_to(jnp.exp(a_last), (SUBLANES, LANES)), spread)[0:1]
    xd = xs * dt_x
    xd16 = xd.astype(BF16)
    fxd16 = (xd * fade_x).astype(BF16)

    pairs = range(SSD_HEADS // 2)
    pairs_per_group = len(pairs) // SSD_GROUPS
    lanes = {c: slice(c * LANES, (c + 1) * LANES) for c in pairs}
    cb, y_off = {}, {}
    for g in range(SSD_GROUPS):
        bg = bmat[:, g * SSD_STATE:(g + 1) * SSD_STATE]
        cg16 = cmat[:, g * SSD_STATE:(g + 1) * SSD_STATE].astype(BF16)
        cb[g] = _dot_nt(cg16, bg.astype(BF16))
        bgt16 = bg.T.astype(BF16)
        for c in range(g * pairs_per_group, (g + 1) * pairs_per_group):
            st = state_scr[c]
            y_off[c] = jnp.dot(cg16, st.astype(BF16), preferred_element_type=F32)
            new = jnp.dot(bgt16, fxd16[:, lanes[c]], preferred_element_type=F32)
            state_scr[c] = st * chunk_x[:, lanes[c]] + new
    first_head = lax.broadcasted_iota(jnp.int32, (l, LANES), 1) < SSD_HEAD_DIM
    y_diag = {}
    for c in pairs:
        for hh in range(2):
            h = 2 * c + hh
            seg = jnp.where(causal, jnp.exp(a_cs[:, h:h + 1] - a_cs_t[h:h + 1, :]), 0.0)
            y_diag[h] = jnp.dot((cb[c // pairs_per_group] * seg).astype(BF16), xd16[:, lanes[c]],
                                preferred_element_type=F32)
    for c in pairs:
        y_scr[:, lanes[c]] = (jnp.where(first_head, y_diag[2 * c], y_diag[2 * c + 1])
                              + y_off[c] * grow_x[:, lanes[c]])

    y = (y_scr[...] + xs * dskip_ref[...]) * _silu(z_ref[0])
    gw = SSD_D_INNER // SSD_GROUPS
    outs = []
    for g in range(SSD_GROUPS):
        yg = y[:, g * gw:(g + 1) * gw]
        outs.append(yg * lax.rsqrt(jnp.mean(yg * yg, -1, keepdims=True) + SSD_NORM_EPS))
    o_ref[0] = jnp.concatenate(outs, axis=1) * nw_ref[...]


def _pad_lanes(v, width=LANES):
    v = v.reshape(1, -1).astype(F32)
    return jnp.pad(v, ((0, 0), (0, width - v.shape[1])))


def _ssd(z, xbc, misc, conv_w, conv_b, dt_bias, a_log, d_skip, norm_w, batch, seq):
    l = SSD_CHUNK
    nc = seq // l
    z3 = z.reshape(batch, seq, SSD_D_INNER)
    x3 = xbc.reshape(batch, seq, SSD_XBC)
    dt = misc[:, NSA_HEADS * 3:NSA_HEADS * 3 + SSD_HEADS]
    dt3 = jnp.pad(dt, ((0, 0), (0, LANES - SSD_HEADS))).reshape(batch, seq, LANES)
    tril = jnp.asarray(np.tril(np.ones((l, l))), BF16)
    spread = np.zeros((LANES, SSD_D_INNER), np.float32)
    spread[np.arange(SSD_D_INNER) // SSD_HEAD_DIM, np.arange(SSD_D_INNER)] = 1.0
    spread = jnp.asarray(spread, BF16)
    hb = l // SUBLANES
    const = lambda b, c: (0, 0)
    return pl.pallas_call(
        _ssd_kernel,
        grid=(batch, nc),
        in_specs=[
            pl.BlockSpec((1, l, SSD_XBC), lambda b, c: (b, c, 0)),
            pl.BlockSpec((1, SUBLANES, SSD_XBC), lambda b, c: (b, jnp.maximum(c * hb - 1, 0), 0)),
            pl.BlockSpec((1, l, SSD_D_INNER), lambda b, c: (b, c, 0)),
            pl.BlockSpec((1, l, LANES), lambda b, c: (b, c, 0)),
            pl.BlockSpec((SSD_CONV, SSD_XBC), const),
            pl.BlockSpec((1, SSD_XBC), const),
            pl.BlockSpec((1, LANES), const),
            pl.BlockSpec((1, LANES), const),
            pl.BlockSpec((1, SSD_D_INNER), const),
            pl.BlockSpec((1, SSD_D_INNER), const),
            pl.BlockSpec((l, l), const),
            pl.BlockSpec((LANES, SSD_D_INNER), const),
        ],
        out_specs=pl.BlockSpec((1, l, SSD_D_INNER), lambda b, c: (b, c, 0)),
        out_shape=jax.ShapeDtypeStruct((batch, seq, SSD_D_INNER), F32),
        scratch_shapes=[pltpu.VMEM((SSD_HEADS // 2, SSD_STATE, 2 * SSD_HEAD_DIM), F32),
                        pltpu.VMEM((l, SSD_D_INNER), F32)],
        compiler_params=_params("parallel", "arbitrary"),
        name="ssd_chunk_scan",
    )(x3, x3, z3, dt3, conv_w.reshape(SSD_CONV, SSD_XBC), conv_b.reshape(1, SSD_XBC),
      _pad_lanes(dt_bias), _pad_lanes(a_log),
      jnp.repeat(d_skip.astype(F32), SSD_HEAD_DIM).reshape(1, SSD_D_INNER),
      norm_w.reshape(1, SSD_D_INNER), tril, spread).reshape(batch * seq, SSD_D_INNER)


def _dot_x2(a, b):
    a1 = a.astype(BF16)
    a2 = (a - a1.astype(F32)).astype(BF16)
    return (jnp.dot(a1, b, preferred_element_type=F32)
            + jnp.dot(a2, b, preferred_element_type=F32))


def _head_sum(x, seg, seg_t):
    sums = _dot_x2(x, seg)
    return jnp.dot(sums.astype(BF16), seg_t, preferred_element_type=F32)


def _rwkv_pre_kernel(x_ref, halo_ref, g_ref, mu_ref, wr_ref, wk_ref, wv_ref, w0_ref, w1_ref,
                     w2_ref, a0_ref, a1_ref, a2_ref, g1_ref, g2_ref, kk_ref, ka_ref, seg_ref,
                     segt_ref, r_out, ld_out, k_out, v_out, kk_out, g_out, bt_out, kt_out, ldt_out,
                     *, tiles_per_seq):
    i = pl.program_id(0)
    h = _rms(x_ref[...], g_ref[...], NORM_EPS)
    prev_row = _rms(halo_ref[...], g_ref[...], NORM_EPS)[SUBLANES - 1:SUBLANES]
    prev_row = jnp.where(i % tiles_per_seq == 0, 0.0, prev_row)
    rowid = lax.broadcasted_iota(jnp.int32, h.shape, 0)
    prev = jnp.where(rowid == 0, prev_row, pltpu.roll(h, 1, 0))
    xx = prev - h
    mu = mu_ref[...]
    mix = lambda j: (h + xx * mu[j:j + 1]).astype(BF16)
    dot = lambda a, w_ref: jnp.dot(a, w_ref[...], preferred_element_type=F32)
    r = dot(mix(0), wr_ref)
    w = -_softplus(-(w0_ref[...] + dot(jnp.tanh(dot(mix(1), w1_ref)).astype(BF16), w2_ref))) - 0.5
    k = dot(mix(2), wk_ref)
    v = dot(mix(3), wv_ref)
    a = jax.nn.sigmoid(a0_ref[...] + dot(dot(mix(4), a1_ref).astype(BF16), a2_ref))
    g = dot(jax.nn.sigmoid(dot(mix(5), g1_ref)).astype(BF16), g2_ref)
    kk = k * kk_ref[...]
    norm = jnp.sqrt(_head_sum(kk * kk, seg_ref[...], segt_ref[...]))
    kk = kk / jnp.maximum(norm, 1e-12)
    k = k * (1.0 + (a - 1.0) * ka_ref[...])
    log_decay = -jnp.exp(w)
    r_out[...] = r
    ld_out[...] = log_decay
    k_out[...] = k
    v_out[...] = v
    kk_out[...] = kk
    g_out[...] = g
    bt_out[0] = (kk * a).T.astype(BF16)
    kt_out[0] = k.T.astype(BF16)
    ldt_out[0] = log_decay.T


def _pad_cols(w, width):
    return jnp.pad(w, ((0, 0), (0, width - w.shape[1])))


def _pad_rows(w, width):
    return jnp.pad(w, ((0, width - w.shape[0]), (0, 0)))


def _seg_matrices():
    seg = np.zeros((D_MODEL, LANES), np.float32)
    seg[np.arange(D_MODEL), np.arange(D_MODEL) // RWKV_HEAD_DIM] = 1.0
    return jnp.asarray(seg, BF16), jnp.asarray(seg.T, BF16)


def _rwkv_pre(x, g, mu, w_r, w_k, w_v, w0, w1, w2, a0, a1, a2, g1, g2, k_k, k_a, seq,
              tm=PROJ_ROWS):
    t, d = x.shape
    lora = lambda w: -(-w.shape[1] // LANES) * LANES
    w1p, w2p = _pad_cols(w1, lora(w1)), _pad_rows(w2, lora(w1))
    a1p, a2p = _pad_cols(a1, lora(a1)), _pad_rows(a2, lora(a1))
    g1p, g2p = _pad_cols(g1, lora(g1)), _pad_rows(g2, lora(g1))
    seg, seg_t = _seg_matrices()
    row = lambda i: (i, 0)
    const = lambda i: (0, 0)
    hb = tm // SUBLANES
    vec = lambda v: v.reshape(1, d)
    mats = [w.astype(BF16) for w in (w_r, w_k, w_v)]
    ins = [x, x, vec(g), mu, *mats, vec(w0), w1p.astype(BF16), w2p.astype(BF16), vec(a0),
           a1p.astype(BF16), a2p.astype(BF16), g1p.astype(BF16), g2p.astype(BF16), vec(k_k),
           vec(k_a), seg, seg_t]
    in_specs = [pl.BlockSpec((tm, d), row),
                pl.BlockSpec((SUBLANES, d), lambda i: (jnp.maximum(i * hb - 1, 0), 0))]
    in_specs += [pl.BlockSpec(a.shape, const) for a in ins[2:]]
    tps = seq // tm
    col = pl.BlockSpec((1, d, tm), lambda i: (i // tps, 0, i % tps))
    return pl.pallas_call(
        functools.partial(_rwkv_pre_kernel, tiles_per_seq=tps),
        grid=(t // tm,),
        in_specs=in_specs,
        out_specs=[pl.BlockSpec((tm, d), row)] * 6 + [col] * 3,
        out_shape=([jax.ShapeDtypeStruct((t, d), F32)] * 6
                   + [jax.ShapeDtypeStruct((t // seq, d, seq), BF16)] * 2
                   + [jax.ShapeDtypeStruct((t // seq, d, seq), F32)]),
        compiler_params=_params("parallel"),
        name="rwkv7_projections",
    )(*ins)


RWKV_CHUNK = 128


def _rwkv_chunk_kernel(r_ref, ld_ref, kk_ref, v_ref, bt_ref, kt_ref, ldt_ref, tril_ref, triu_ref,
                       y_ref, state_scr):
    @pl.when(pl.program_id(1) == 0)
    def _():
        state_scr[...] = jnp.zeros_like(state_scr)

    l = RWKV_CHUNK
    hd = RWKV_HEAD_DIM
    tril = tril_ref[...]
    ld = ld_ref[0]
    c_in = _dot_x3_left(tril, ld)
    a_bar = -kk_ref[0] * jnp.exp(c_in - ld)
    r_bar = r_ref[0] * jnp.exp(c_in)
    v = v_ref[0]
    ldt = ldt_ref[0]
    c_t = _dot_x3(ldt, triu_ref[...])
    scale_t = jnp.exp(-c_t)
    b_t = bt_ref[0].astype(F32) * scale_t
    k_t = kt_ref[0].astype(F32) * scale_t
    decay_col = jnp.exp(c_t[:, l - 1:l])

    row = lax.broadcasted_iota(jnp.int32, (l, l), 0)
    colx = lax.broadcasted_iota(jnp.int32, (l, l), 1)
    strict = row > colx
    incl = row >= colx
    lane = lax.broadcasted_iota(jnp.int32, (l, LANES), 1)
    first_head = lane < hd
    blockdiag = (lax.broadcasted_iota(jnp.int32, (LANES, LANES), 0) < hd) == (
        lax.broadcasted_iota(jnp.int32, (LANES, LANES), 1) < hd)
    mm = lambda a, b: jnp.dot(a, b, preferred_element_type=F32)
    b16 = lambda a: a.astype(BF16)

    pairs = range(D_MODEL // LANES)
    heads = [(c, hh) for c in pairs for hh in range(2)]
    lanes = {c: slice(c * LANES, (c + 1) * LANES) for c in pairs}
    v16 = {c: b16(v[:, lanes[c]]) for c in pairs}
    bk_t = {c: b16(jnp.concatenate([b_t[lanes[c]], k_t[lanes[c]]], axis=1)) for c in pairs}
    h2 = {c: state_scr[c] for c in pairs}
    gh = {}
    for c in pairs:
        rhs = jnp.concatenate([bk_t[c], b16(h2[c])], axis=1)
        a_p, r_p = a_bar[:, lanes[c]], r_bar[:, lanes[c]]
        for hh in range(2):
            keep = first_head if hh == 0 else ~first_head
            x = jnp.concatenate([jnp.where(keep, a_p, 0.0), jnp.where(keep, r_p, 0.0)], axis=0)
            gh[c, hh] = mm(b16(x), rhs)
    mp, u, p_r = {}, {}, {}
    for c, hh in heads:
        g = gh[c, hh]
        mp[c, hh] = b16(jnp.where(strict, g[:l, :l], 0.0))
        m_ak = b16(jnp.where(strict, g[:l, l:2 * l], 0.0))
        p_r[c, hh] = b16(jnp.concatenate([jnp.where(incl, g[l:, :l], 0.0),
                                          jnp.where(incl, g[l:, l:2 * l], 0.0)], axis=1))
        u[c, hh] = g[:l, 2 * l:] + mm(m_ak, v16[c])
    n_factors = l.bit_length() - 1
    for f in range(n_factors):
        du = {h: mm(mp[h], b16(u[h])) for h in heads}
        if f + 1 < n_factors:
            mp = {h: b16(mm(mp[h], mp[h])) for h in heads}
        u = {h: u[h] + du[h] for h in heads}
    ys = {h: gh[h][l:, 2 * l:] + mm(p_r[h], jnp.concatenate([b16(u[h]), v16[h[0]]], axis=0))
          for h in heads}
    for c in pairs:
        u_pair = jnp.where(first_head, u[c, 0], u[c, 1])
        y_ref[0, :, lanes[c]] = jnp.where(first_head, ys[c, 0], ys[c, 1])
        upd = h2[c] + mm(bk_t[c], jnp.concatenate([b16(u_pair), v16[c]], axis=0))
        state_scr[c] = jnp.where(blockdiag, upd * decay_col[lanes[c]], 0.0)


def _rwkv_scan(r, ld, kk, v, bt, kt, ldt, batch, seq):
    l = RWKV_CHUNK
    d = D_MODEL
    rows = lambda x: x.reshape(batch, seq, d)
    rblk = pl.BlockSpec((1, l, d), lambda b, c: (b, c, 0))
    cblk = pl.BlockSpec((1, d, l), lambda b, c: (b, 0, c))
    tril = jnp.asarray(np.tril(np.ones((l, l))), BF16)
    y = pl.pallas_call(
        _rwkv_chunk_kernel,
        grid=(batch, seq // l),
        in_specs=[rblk] * 4 + [cblk] * 3 + [pl.BlockSpec((l, l), lambda b, c: (0, 0))] * 2,
        out_specs=rblk,
        out_shape=jax.ShapeDtypeStruct((batch, seq, d), F32),
        scratch_shapes=[pltpu.VMEM((d // LANES, LANES, LANES), F32)],
        compiler_params=_params("parallel", "arbitrary"),
        name="rwkv7_recurrence",
    )(rows(r), rows(ld), rows(kk), rows(v), bt, kt, ldt, tril, tril.T)
    return y.reshape(batch * seq, d)


def _rwkv_post_kernel(x_ref, y_ref, r_ref, k_ref, v_ref, g_ref, lng_ref, lnb_ref, rk_ref, wo_ref,
                      gn_ref, seg_ref, segt_ref, o_ref):
    seg, seg_t = seg_ref[...], segt_ref[...]
    y = y_ref[...]
    inv = 1.0 / RWKV_HEAD_DIM
    mean = _head_sum(y, seg, seg_t) * inv
    yc = y - mean
    var = _head_sum(yc * yc, seg, seg_t) * inv
    yn = yc * lax.rsqrt(var + RWKV_GN_EPS) * lng_ref[...] + lnb_ref[...]
    bonus = _head_sum(r_ref[...] * k_ref[...] * rk_ref[...], seg, seg_t) * v_ref[...]
    out = ((yn + bonus) * g_ref[...]).astype(BF16)
    proj = jnp.dot(out, wo_ref[...], preferred_element_type=F32)
    o_ref[...] = x_ref[...] + _rms(proj, gn_ref[...], NORM_EPS)


def _rwkv_post(x, y, r, k, v, g, ln_g, ln_b, r_k, w_o, gn, tm=PROJ_ROWS):
    t, d = x.shape
    seg, seg_t = _seg_matrices()
    row = lambda i: (i, 0)
    const = lambda i: (0, 0)
    vec = lambda a: a.reshape(1, d)
    small = [vec(ln_g), vec(ln_b), vec(r_k), w_o.astype(BF16), vec(gn), seg, seg_t]
    return pl.pallas_call(
        _rwkv_post_kernel,
        grid=(t // tm,),
        in_specs=[pl.BlockSpec((tm, d), row)] * 6 + [pl.BlockSpec(a.shape, const) for a in small],
        out_specs=pl.BlockSpec((tm, d), row),
        out_shape=jax.ShapeDtypeStruct((t, d), F32),
        compiler_params=_params("parallel"),
        name="rwkv7_output",
    )(x, y, r, k, v, g, *small)


def _nsa_ssd_mixer(x, g_pre, g_post, cos, sin, w_in, pe_k, w1_k, w2_k, pe_v, w1_v, w2_v, conv_w,
                   conv_b, dt_bias, a_log, d_skip, norm_w, w_out, batch, seq):
    q, kv16, kaug, kwin, vst, vwt, z, xbc, misc = _inproj(x, g_pre, w_in, cos, sin, batch, seq)
    o_a = _nsa(q, kv16, kaug, kwin, vst, vwt, misc, pe_k, w1_k, w2_k, pe_v, w1_v, w2_v, batch, seq)
    o_b = _ssd(z, xbc, misc, conv_w, conv_b, dt_bias, a_log, d_skip, norm_w, batch, seq)
    return g_post, [o_a, o_b], [w_out[:NSA_Q_W], w_out[NSA_Q_W:]]


def _rwkv7_mixer(x, g_pre, g_post, mu, w_r, w_k, w_v, w_o, w0, w1, w2, a0, a1, a2, g1, g2, k_k,
                 k_a, r_k, ln_g, ln_b, batch, seq):
    r, ld, k, v, kk, g, bt, kt, ldt = _rwkv_pre(x, g_pre, mu, w_r, w_k, w_v, w0, w1, w2, a0, a1,
                                                a2, g1, g2, k_k, k_a, seq)
    y = _rwkv_scan(r, ld, kk, v, bt, kt, ldt, batch, seq)
    return _rwkv_post(x, y, r, k, v, g, ln_g, ln_b, r_k, w_o, g_post)


def kernel(x, norm_gains, ffn1_w_gate, ffn1_w_up, ffn1_w_down, ffn2_w_gate, ffn2_w_up, ffn2_w_down, ab_w_in, a_cmp_pe_k, a_cmp_w1_k, a_cmp_w2_k, a_cmp_pe_v, a_cmp_w1_v, a_cmp_w2_v, b_conv_w, b_conv_b, b_dt_bias, b_a_log, b_d_skip, b_norm_w, ab_w_out, c_mu, c_w_r, c_w_k, c_w_v, c_w_o, c_w0, c_w1, c_w2, c_a0, c_a1, c_a2, c_g1, c_g2, c_k_k, c_k_a, c_r_k, c_ln_g, c_ln_b):
    batch, seq, d = x.shape
    depth = norm_gains.shape[0]
    cos, sin = _rope_tables(seq)
    x = x.reshape(batch * seq, d)
    (ffn1_w_gate, ffn1_w_up, ffn1_w_down, ffn2_w_gate, ffn2_w_up, ffn2_w_down) = [
        w.astype(BF16) for w in (ffn1_w_gate, ffn1_w_up, ffn1_w_down, ffn2_w_gate, ffn2_w_up,
                                 ffn2_w_down)]
    for layer in range(depth):
        ng = norm_gains[layer]
        x = _ffn(x, ng[0], ng[1], ffn1_w_gate, ffn1_w_up, ffn1_w_down, layer)
        i = layer // 2
        mixer = None
        if layer % 2 == 0:
            mixer = _nsa_ssd_mixer(x, ng[2], ng[3], cos, sin, ab_w_in[i], a_cmp_pe_k[i],
                                   a_cmp_w1_k[i], a_cmp_w2_k[i], a_cmp_pe_v[i], a_cmp_w1_v[i],
                                   a_cmp_w2_v[i], b_conv_w[i], b_conv_b[i], b_dt_bias[i],
                                   b_a_log[i], b_d_skip[i], b_norm_w[i], ab_w_out[i], batch, seq)
        else:
            x = _rwkv7_mixer(x, ng[2], ng[3], c_mu[i], c_w_r[i], c_w_k[i], c_w_v[i], c_w_o[i],
                             c_w0[i], c_w1[i], c_w2[i], c_a0[i], c_a1[i], c_a2[i], c_g1[i],
                             c_g2[i], c_k_k[i], c_k_a[i], c_r_k[i], c_ln_g[i], c_ln_b[i],
                             batch, seq)
        x = _ffn(x, ng[4], ng[5], ffn2_w_gate, ffn2_w_up, ffn2_w_down, layer, mixer=mixer)
    return x.reshape(batch, seq, d)
```
